```python
import jax, jax.numpy as jnp
from jax import lax
import numpy as np

D_MODEL = 1024
BATCH = 16
SEQ = 2048
DEPTH = 1

GRID_W = 64
CTX_LEN = 256
EPS = 1e-6
N_FOURIER_GROUPS = 4
FOURIER_GROUP_DIM = 128
D_FOURIER = N_FOURIER_GROUPS * FOURIER_GROUP_DIM
N_LRU_HEADS = 8
LRU_HEAD_DIM = 64
D_LRU = N_LRU_HEADS * LRU_HEAD_DIM
CONV_WIDTH = 4
CONV_LEFT = 2
LRU_C = 8.0
D_IN = D_FOURIER + 2 * D_LRU + 2 * D_MODEL
N_GROUPS = 4
EXPERTS_PER_GROUP = 8
N_EXPERTS = N_GROUPS * EXPERTS_PER_GROUP
TOP_K = 2
D_EXPERT = 512
MOE_BLOCK = 128

kernel_name = "hybrid_fnet_rglru_hmoe_dit"


def rmsnorm(x, g):
    xf = x.astype(jnp.float32)
    y = xf * lax.rsqrt(jnp.mean(xf * xf, axis=-1, keepdims=True) + EPS)
    return (y * g.astype(jnp.float32)).astype(x.dtype)


def modulate(x, shift, scale):
    return x * (1 + scale) + shift


def fourier_mix(u, grid_w):
    b_, l_, _ = u.shape
    uf = u.astype(jnp.float32)
    if grid_w is not None:
        rows = l_ // grid_w
        uf = uf.reshape(b_, rows, grid_w, N_FOURIER_GROUPS, FOURIER_GROUP_DIM)
        out = jnp.real(jnp.fft.fftn(uf, axes=(1, 2, 4), norm="ortho"))
    else:
        uf = uf.reshape(b_, l_, N_FOURIER_GROUPS, FOURIER_GROUP_DIM)
        out = jnp.real(jnp.fft.fftn(uf, axes=(1, 3), norm="ortho"))
    return out.reshape(b_, l_, D_FOURIER).astype(u.dtype)


def short_conv(u, w, b):
    l_ = u.shape[1]
    up = jnp.pad(u, ((0, 0), (CONV_LEFT, CONV_WIDTH - 1 - CONV_LEFT), (0, 0)))
    return sum(up[:, k:k + l_] * w[k] for k in range(CONV_WIDTH)) + b


def lru_coeffs(xc, w_a, b_a, w_x, b_x, lam):
    b_, l_, _ = xc.shape
    xh = xc.reshape(b_, l_, N_LRU_HEADS, LRU_HEAD_DIM)
    r = jax.nn.sigmoid(jnp.einsum('blhi,hij->blhj', xh, w_a).reshape(b_, l_, D_LRU) + b_a)
    i = jax.nn.sigmoid(jnp.einsum('blhi,hij->blhj', xh, w_x).reshape(b_, l_, D_LRU) + b_x)
    log_a = -LRU_C * r.astype(jnp.float32) * jax.nn.softplus(-lam.astype(jnp.float32))
    a = jnp.exp(log_a)
    mult = jnp.sqrt(-jnp.expm1(2.0 * log_a))
    return a, mult * (i * xc).astype(jnp.float32)


def linear_scan(a, b, h0, reverse):
    if reverse:
        a, b = jnp.flip(a, 1), jnp.flip(b, 1)
    b = b.at[:, 0].add(a[:, 0] * h0)

    def combine(l, r):
        return (l[0] * r[0], r[0] * l[1] + r[1])

    _, h = lax.associative_scan(combine, (a, b), axis=1)
    return jnp.flip(h, 1) if reverse else h


def bidir_rglru(u, conv_w, conv_b, w_a, b_a, w_x, b_x, lam, h0):
    xc = short_conv(u, conv_w, conv_b)
    a_f, b_f = lru_coeffs(xc, w_a[0], b_a[0], w_x[0], b_x[0], lam[0])
    a_b, b_b = lru_coeffs(xc, w_a[1], b_a[1], w_x[1], b_x[1], lam[1])
    return linear_scan(a_f, b_f, h0[0], False), linear_scan(a_b, b_b, h0[1], True)


def token_mixer(h, w_in, conv_w, conv_b, w_a, b_a, w_x, b_x, lam, w_fo, w_lo, w_out, h0, grid_w):
    proj = h @ w_in
    u_f = proj[..., :D_FOURIER]
    u_r = proj[..., D_FOURIER:D_FOURIER + D_LRU]
    y_r = proj[..., D_FOURIER + D_LRU:D_FOURIER + 2 * D_LRU]
    gate_f, gate_r = jnp.split(jax.nn.sigmoid(proj[..., D_FOURIER + 2 * D_LRU:]), 2, axis=-1)
    branch_f = fourier_mix(u_f, grid_w) @ w_fo
    h_f, h_b = bidir_rglru(u_r, conv_w, conv_b, w_a, b_a, w_x, b_x, lam, h0)
    branch_r = ((h_f + h_b) * jax.nn.gelu(y_r.astype(jnp.float32))).astype(h.dtype) @ w_lo
    return (gate_f * branch_f + gate_r * branch_r) @ w_out, h_f, h_b


def hier_moe(h, w_group, b_group, w_er, b_er, w_g, w_u, w_d):
    b_, l_, d_ = h.shape
    t = h.reshape(-1, d_)
    n = t.shape[0]
    tf = t.astype(jnp.float32)
    g_logits = tf @ w_group.astype(jnp.float32) + b_group.astype(jnp.float32)
    g_prob = jax.nn.softmax(g_logits, axis=-1)
    grp = jnp.argmax(g_logits, axis=-1)
    p_grp = jnp.take_along_axis(g_prob, grp[:, None], axis=1)[:, 0]
    e_logits = (tf @ w_er.astype(jnp.float32) + b_er.astype(jnp.float32)).reshape(n, N_GROUPS, EXPERTS_PER_GROUP)
    e_sel = jnp.take_along_axis(e_logits, grp[:, None, None], axis=1)[:, 0]
    top_l, top_i = lax.top_k(e_sel, TOP_K)
    w = p_grp[:, None] * jax.nn.softmax(top_l, axis=-1)
    eid = grp[:, None] * EXPERTS_PER_GROUP + top_i

    flat_e = eid.reshape(-1)
    flat_w = w.reshape(-1)
    n_assign = n * TOP_K
    order = jnp.argsort(flat_e)
    se = flat_e[order]
    stok = (order // TOP_K).astype(jnp.int32)
    sw = flat_w[order]
    counts = jnp.bincount(flat_e, length=N_EXPERTS)
    padded = (counts + MOE_BLOCK - 1) // MOE_BLOCK * MOE_BLOCK
    start = jnp.cumsum(counts) - counts
    pend = jnp.cumsum(padded)
    pstart = pend - padded
    pos = pstart[se] + jnp.arange(n_assign) - start[se]
    n_rows = n_assign + N_EXPERTS * MOE_BLOCK
    n_blk = n_rows // MOE_BLOCK
    row_tok = jnp.full((n_rows,), n, jnp.int32).at[pos].set(stok)
    row_w = jnp.zeros((n_rows,), jnp.float32).at[pos].set(sw)
    blk_e = jnp.minimum(jnp.searchsorted(pend, jnp.arange(n_blk) * MOE_BLOCK, side='right'), N_EXPERTS - 1)
    t_pad = jnp.concatenate([t, jnp.zeros((1, d_), t.dtype)], axis=0)
    xs = t_pad[row_tok].reshape(n_blk, MOE_BLOCK, d_)

    def expert_block(args):
        xb, e = args
        hb = jax.nn.silu(xb @ w_g[e]) * (xb @ w_u[e])
        return hb @ w_d[e]

    ys = lax.map(expert_block, (xs, blk_e)).reshape(n_rows, d_)
    out = jnp.zeros((n + 1, d_), jnp.float32).at[row_tok].add(ys.astype(jnp.float32) * row_w[:, None])[:n]
    return out.reshape(b_, l_, d_).astype(h.dtype)


def setup_inputs(seed: int = 0) -> dict:
    key = jax.random.key(seed)
    ks = jax.random.split(key, 28)

    def nrm(k, shape, scale):
        return jax.random.normal(k, shape, jnp.float32) * scale

    s_d = D_MODEL ** -0.5
    a0 = jax.random.uniform(ks[14], (DEPTH, 2, D_LRU), jnp.float32, 0.9, 0.999)
    s = a0 ** (1.0 / LRU_C)
    lam = jnp.log(s) - jnp.log1p(-s)
    return {
        "x": nrm(ks[0], (BATCH, SEQ, D_MODEL), 1.0),
        "c": nrm(ks[1], (BATCH, D_MODEL), 1.0),
        "ctx": nrm(ks[2], (BATCH, CTX_LEN, D_MODEL), 1.0),
        "c_ctx": nrm(ks[3], (D_MODEL,), 1.0),
        "w_mod": nrm(ks[4], (DEPTH, D_MODEL, 6 * D_MODEL), 0.5 * s_d),
        "b_mod": nrm(ks[5], (DEPTH, 6 * D_MODEL), 0.01),
        "norm1_g": 1.0 + nrm(ks[6], (DEPTH, D_MODEL), 0.01),
        "w_in": nrm(ks[7], (DEPTH, D_MODEL, D_IN), s_d),
        "conv_w": nrm(ks[8], (DEPTH, CONV_WIDTH, D_LRU), CONV_WIDTH ** -0.5),
        "conv_b": nrm(ks[9], (DEPTH, D_LRU), 0.01),
        "lru_wa": nrm(ks[10], (DEPTH, 2, N_LRU_HEADS, LRU_HEAD_DIM, LRU_HEAD_DIM), LRU_HEAD_DIM ** -0.5),
        "lru_ba": nrm(ks[11], (DEPTH, 2, D_LRU), 0.01),
        "lru_wx": nrm(ks[12], (DEPTH, 2, N_LRU_HEADS, LRU_HEAD_DIM, LRU_HEAD_DIM), LRU_HEAD_DIM ** -0.5),
        "lru_bx": nrm(ks[13], (DEPTH, 2, D_LRU), 0.01),
        "lru_lam": lam,
        "w_fourier_out": nrm(ks[15], (DEPTH, D_FOURIER, D_MODEL), D_FOURIER ** -0.5),
        "w_lru_out": nrm(ks[16], (DEPTH, D_LRU, D_MODEL), D_LRU ** -0.5),
        "w_out": nrm(ks[17], (DEPTH, D_MODEL, D_MODEL), s_d),
        "norm2_g": 1.0 + nrm(ks[18], (DEPTH, D_MODEL), 0.01),
        "w_group": nrm(ks[19], (DEPTH, D_MODEL, N_GROUPS), s_d),
        "b_group": nrm(ks[20], (DEPTH, N_GROUPS), 0.01),
        "w_expert_router": nrm(ks[21], (DEPTH, D_MODEL, N_EXPERTS), s_d),
        "b_expert_router": nrm(ks[22], (DEPTH, N_EXPERTS), 0.01),
        "w_gate_e": nrm(ks[23], (DEPTH, N_EXPERTS, D_MODEL, D_EXPERT), s_d),
        "w_up_e": nrm(ks[24], (DEPTH, N_EXPERTS, D_MODEL, D_EXPERT), s_d),
        "w_down_e": nrm(ks[25], (DEPTH, N_EXPERTS, D_EXPERT, D_MODEL), D_EXPERT ** -0.5),
        "final_g": 1.0 + nrm(ks[26], (D_MODEL,), 0.01),
    }


def reference(x, c, ctx, c_ctx, w_mod, b_mod, norm1_g, w_in, conv_w, conv_b, lru_wa, lru_ba, lru_wx, lru_bx,
              lru_lam, w_fourier_out, w_lru_out, w_out, norm2_g, w_group, b_group, w_expert_router,
              b_expert_router, w_gate_e, w_up_e, w_down_e, final_g):
    for l in range(DEPTH):
        mod = (jax.nn.silu(c) @ w_mod[l] + b_mod[l])[:, None, :]
        sh1, sc1, g1, sh2, sc2, g2 = jnp.split(mod, 6, axis=-1)
        modc = jax.nn.silu(c_ctx) @ w_mod[l] + b_mod[l]
        csh1, csc1, cg1, csh2, csc2, cg2 = jnp.split(modc, 6, axis=-1)
        lru_p = (conv_w[l], conv_b[l], lru_wa[l], lru_ba[l], lru_wx[l], lru_bx[l], lru_lam[l])
        zeros_h = jnp.zeros((2, ctx.shape[0], D_LRU), jnp.float32)

        hc = modulate(rmsnorm(ctx, norm1_g[l]), csh1, csc1)
        if l < DEPTH - 1:
            mix_c, hcf, hcb = token_mixer(hc, w_in[l], *lru_p, w_fourier_out[l], w_lru_out[l], w_out[l],
                                          zeros_h, None)
        else:
            uc = hc @ w_in[l][:, D_FOURIER:D_FOURIER + D_LRU]
            hcf, hcb = bidir_rglru(uc, *lru_p, zeros_h)
        h0 = jnp.stack([hcf[:, -1], hcb[:, 0]])

        hx = modulate(rmsnorm(x, norm1_g[l]), sh1, sc1)
        mix_x, _, _ = token_mixer(hx, w_in[l], *lru_p, w_fourier_out[l], w_lru_out[l], w_out[l], h0, GRID_W)
        x = x + g1 * mix_x

        moe_p = (w_group[l], b_group[l], w_expert_router[l], b_expert_router[l], w_gate_e[l], w_up_e[l],
                 w_down_e[l])
        x = x + g2 * hier_moe(modulate(rmsnorm(x, norm2_g[l]), sh2, sc2), *moe_p)

        if l < DEPTH - 1:
            ctx = ctx + cg1 * mix_c
            ctx = ctx + cg2 * hier_moe(modulate(rmsnorm(ctx, norm2_g[l]), csh2, csc2), *moe_p)
    return rmsnorm(x, final_g)
```

```python
import functools

import numpy as np
import jax
import jax.numpy as jnp
from jax import lax
from jax.experimental import pallas as pl
from jax.experimental.pallas import tpu as pltpu

F32 = jnp.float32
BF16 = jnp.bfloat16

LANES = 128
SUBLANES = 8
VMEM_LIMIT_BYTES = 56 * 1024 * 1024

D_MODEL = 1024
GRID_W = 64
EPS = 1e-6
N_FOURIER_GROUPS = 4
FOURIER_GROUP_DIM = 128
D_FOURIER = N_FOURIER_GROUPS * FOURIER_GROUP_DIM
N_LRU_HEADS = 8
LRU_HEAD_DIM = 64
D_LRU = N_LRU_HEADS * LRU_HEAD_DIM
CONV_WIDTH = 4
LRU_C = 8.0
N_GROUPS = 4
EXPERTS_PER_GROUP = 8
N_EXPERTS = N_GROUPS * EXPERTS_PER_GROUP
TOP_K = 2
D_EXPERT = 512

MOD_ROWS = 24
MOD_BLOCK_N = 1536
TOKEN_TILE = 512
SCAN_BATCH = SUBLANES
SCAN_CHUNK = 256
SCAN_PITCH = SCAN_CHUNK + SUBLANES
N_SLABS = D_LRU // LANES
FOURIER_TILE = 512
MOE_BLOCK = 256
ROUTE_LANES = LANES
N_DUMMY_ROWS = 4 * MOE_BLOCK


def _params(semantics):
    return pltpu.CompilerParams(dimension_semantics=semantics, vmem_limit_bytes=VMEM_LIMIT_BYTES)


def _rms_modulate(x, g, shift, scale):
    y = x * lax.rsqrt(jnp.mean(x * x, axis=-1, keepdims=True) + EPS) * g
    return y * (1.0 + scale) + shift


def _mod_kernel(c_ref, w_ref, b_ref, o_ref):
    c = c_ref[...]
    s = c * jax.nn.sigmoid(c)
    o_ref[...] = jnp.dot(s, w_ref[...], preferred_element_type=F32,
                         precision=lax.Precision.HIGHEST) + b_ref[...]


def _mod_call(c_all, w_mod, b_mod):
    n_out = w_mod.shape[1]
    return pl.pallas_call(
        _mod_kernel,
        out_shape=jax.ShapeDtypeStruct((MOD_ROWS, n_out), F32),
        grid=(n_out // MOD_BLOCK_N,),
        in_specs=[
            pl.BlockSpec((MOD_ROWS, D_MODEL), lambda j: (0, 0)),
            pl.BlockSpec((D_MODEL, MOD_BLOCK_N), lambda j: (0, j)),
            pl.BlockSpec((1, MOD_BLOCK_N), lambda j: (0, j)),
        ],
        out_specs=pl.BlockSpec((MOD_ROWS, MOD_BLOCK_N), lambda j: (0, j)),
        compiler_params=_params(("arbitrary",)),
        name="mod",
    )(c_all, w_mod, b_mod)


def _inproj_kernel(x_ref, sh_ref, sc_ref, g_ref, w_ref, dft_ref, ab_ref, ur_ref, yr_ref):
    hx = _rms_modulate(x_ref[...], g_ref[...], sh_ref[0], sc_ref[0]).astype(BF16)
    proj = jnp.dot(hx, w_ref[...], preferred_element_type=F32)
    uf = proj[:, :D_FOURIER].astype(BF16)
    for g in range(N_FOURIER_GROUPS):
        lo, hi = g * FOURIER_GROUP_DIM, (g + 1) * FOURIER_GROUP_DIM
        cs = jnp.dot(uf[:, lo:hi], dft_ref[...], preferred_element_type=F32)
        ab_ref[0, 0, :, lo:hi] = cs[:, :FOURIER_GROUP_DIM].astype(BF16)
        ab_ref[0, 1, :, lo:hi] = cs[:, FOURIER_GROUP_DIM:].astype(BF16)
    ur_ref[...] = proj[:, D_FOURIER:D_FOURIER + D_LRU]
    yr_ref[...] = proj[:, D_FOURIER + D_LRU:].astype(BF16)


def _inproj_call(x2, sh, sc, g, w, dft_ch, batch, seq):
    n = batch * seq
    tpb = seq // TOKEN_TILE
    mod_spec = pl.BlockSpec((1, 1, D_MODEL), lambda i: (i // tpb, 0, 0))
    return pl.pallas_call(
        _inproj_kernel,
        out_shape=(
            jax.ShapeDtypeStruct((batch, 2, seq, D_FOURIER), BF16),
            jax.ShapeDtypeStruct((n, D_LRU), F32),
            jax.ShapeDtypeStruct((n, D_LRU), BF16),
        ),
        grid=(n // TOKEN_TILE,),
        in_specs=[
            pl.BlockSpec((TOKEN_TILE, D_MODEL), lambda i: (i, 0)),
            mod_spec, mod_spec,
            pl.BlockSpec((1, D_MODEL), lambda i: (0, 0)),
            pl.BlockSpec(w.shape, lambda i: (0, 0)),
            pl.BlockSpec(dft_ch.shape, lambda i: (0, 0)),
        ],
        out_specs=(
            pl.BlockSpec((1, 2, TOKEN_TILE, D_FOURIER), lambda i: (i // tpb, 0, i % tpb, 0)),
            pl.BlockSpec((TOKEN_TILE, D_LRU), lambda i: (i, 0)),
            pl.BlockSpec((TOKEN_TILE, D_LRU), lambda i: (i, 0)),
        ),
        compiler_params=_params(("parallel",)),
        name="inproj",
    )(x2, sh, sc, g, w, dft_ch)


def _inproj_ctx_kernel(x_ref, sh_ref, sc_ref, g_ref, w_ref, ur_ref):
    hx = _rms_modulate(x_ref[...], g_ref[...], sh_ref[...], sc_ref[...]).astype(BF16)
    ur_ref[...] = jnp.dot(hx, w_ref[...], preferred_element_type=F32)


def _inproj_ctx_call(ctx2, sh, sc, g, w):
    n = ctx2.shape[0]
    vec = pl.BlockSpec((1, D_MODEL), lambda i: (0, 0))
    return pl.pallas_call(
        _inproj_ctx_kernel,
        out_shape=jax.ShapeDtypeStruct((n, D_LRU), F32),
        grid=(n // TOKEN_TILE,),
        in_specs=[pl.BlockSpec((TOKEN_TILE, D_MODEL), lambda i: (i, 0)), vec, vec, vec,
                  pl.BlockSpec(w.shape, lambda i: (0, 0))],
        out_specs=pl.BlockSpec((TOKEN_TILE, D_LRU), lambda i: (i, 0)),
        compiler_params=_params(("parallel",)),
        name="inproj_ctx",
    )(ctx2, sh, sc, g, w)


def _rglru_kernel(n_chunks, u_ref, up_ref, un_ref, cw_ref, cb_ref, wg_ref, bg_ref, lam_ref, h0_ref,
                  h_ref, a_s, b_s, h_s, state):
    tc, pitch = SCAN_CHUNK, SCAN_PITCH
    d = pl.program_id(0)
    k = pl.program_id(2)
    kk = k + d * (n_chunks - 1 - 2 * k)

    @pl.when(k == 0)
    def _():
        state[...] = h0_ref[0]

    lam = lam_ref[0]
    neg_lam = -lam
    softplus = jnp.maximum(neg_lam, 0.0) + jnp.log1p(jnp.exp(-jnp.abs(neg_lam)))
    has_prev = kk > 0
    has_next = kk < n_chunks - 1
    row = lax.broadcasted_iota(jnp.int32, (tc, D_LRU), 0)
    cw = cw_ref[...]

    for i in range(SCAN_BATCH):
        u = u_ref[i]
        prev = jnp.where(has_prev, up_ref[i], 0.0)
        nxt = jnp.where(has_next, un_ref[i], 0.0)
        p2, p1, n1 = prev[SUBLANES - 2:SUBLANES - 1], prev[SUBLANES - 1:SUBLANES], nxt[0:1]
        um1 = jnp.where(row == 0, p1, pltpu.roll(u, 1, 0))
        um2 = jnp.where(row == 0, p2, jnp.where(row == 1, p1, pltpu.roll(u, 2, 0)))
        up1 = jnp.where(row == tc - 1, n1, pltpu.roll(u, tc - 1, 0))
        xc = um2 * cw[0:1] + um1 * cw[1:2] + u * cw[2:3] + up1 * cw[3:4] + cb_ref[...]
        gz = jnp.dot(xc.astype(BF16), wg_ref[0], preferred_element_type=F32) + bg_ref[0]
        r = jax.nn.sigmoid(gz[:, :D_LRU])
        ig = jax.nn.sigmoid(gz[:, D_LRU:])
        log_a = -LRU_C * r * softplus
        a = jnp.exp(log_a)
        b = jnp.sqrt((1.0 - a) * (1.0 + a)) * (ig * xc)
        for j in range(N_SLABS):
            a_s[j, pl.ds(i * pitch, tc), :] = a[:, j * LANES:(j + 1) * LANES]
            b_s[j, pl.ds(i * pitch, tc), :] = b[:, j * LANES:(j + 1) * LANES]

    def step(s, h):
        t = s + d * (tc - 1 - 2 * s)
        out = []
        for j in range(N_SLABS):
            rows = pl.ds(t, SCAN_BATCH, stride=pitch)
            hj = a_s[j, rows, :] * h[j] + b_s[j, rows, :]
            h_s[j, rows, :] = hj
            out.append(hj)
        return tuple(out)

    st = state[...]
    h = lax.fori_loop(0, tc, step, tuple(st[:, j * LANES:(j + 1) * LANES] for j in range(N_SLABS)),
                      unroll=4)
    for j in range(N_SLABS):
        state[:, j * LANES:(j + 1) * LANES] = h[j]
    for i in range(SCAN_BATCH):
        for j in range(N_SLABS):
            h_ref[0, i, :, j * LANES:(j + 1) * LANES] = h_s[j, pl.ds(i * pitch, tc), :]


def _rglru_call(u3, conv_w, conv_b, wg, bg, lam, h0):
    batch, seq, _ = u3.shape
    n_chunks = seq // SCAN_CHUNK
    halo_blocks = SCAN_CHUNK // SUBLANES
    last_halo = seq // SUBLANES - 1

    def chunk(d, k):
        return k + d * (n_chunks - 1 - 2 * k)

    scratch = pltpu.VMEM((N_SLABS, SCAN_BATCH * SCAN_PITCH, LANES), F32)
    return pl.pallas_call(
        functools.partial(_rglru_kernel, n_chunks),
        out_shape=jax.ShapeDtypeStruct((2, batch, seq, D_LRU), F32),
        grid=(2, batch // SCAN_BATCH, n_chunks),
        in_specs=[
            pl.BlockSpec((SCAN_BATCH, SCAN_CHUNK, D_LRU), lambda d, g, k: (g, chunk(d, k), 0)),
            pl.BlockSpec((SCAN_BATCH, SUBLANES, D_LRU),
                         lambda d, g, k: (g, jnp.maximum(chunk(d, k) * halo_blocks - 1, 0), 0)),
            pl.BlockSpec((SCAN_BATCH, SUBLANES, D_LRU),
                         lambda d, g, k: (g, jnp.minimum((chunk(d, k) + 1) * halo_blocks, last_halo), 0)),
            pl.BlockSpec((CONV_WIDTH, D_LRU), lambda d, g, k: (0, 0)),
            pl.BlockSpec((1, D_LRU), lambda d, g, k: (0, 0)),
            pl.BlockSpec((1, D_LRU, 2 * D_LRU), lambda d, g, k: (d, 0, 0)),
            pl.BlockSpec((1, 1, 2 * D_LRU), lambda d, g, k: (d, 0, 0)),
            pl.BlockSpec((1, 1, D_LRU), lambda d, g, k: (d, 0, 0)),
            pl.BlockSpec((1, SCAN_BATCH, D_LRU), lambda d, g, k: (d, g, 0)),
        ],
        out_specs=pl.BlockSpec((1, SCAN_BATCH, SCAN_CHUNK, D_LRU), lambda d, g, k: (d, g, chunk(d, k), 0)),
        scratch_shapes=[scratch, scratch, scratch, pltpu.VMEM((SCAN_BATCH, D_LRU), F32)],
        compiler_params=_params(("arbitrary", "arbitrary", "arbitrary")),
        name="rglru",
    )(u3, u3, u3, conv_w, conv_b, wg, bg, lam, h0)


def _fourier_kernel(m_ref, ab_ref, o_ref):
    seq2 = m_ref.shape[1]
    rhs = ab_ref[0].reshape(seq2, D_FOURIER)
    o_ref[0] = jnp.dot(m_ref[...], rhs, preferred_element_type=F32).astype(BF16)


def _fourier_call(m_pos, ab):
    batch, _, seq, _ = ab.shape
    return pl.pallas_call(
        _fourier_kernel,
        out_shape=jax.ShapeDtypeStruct((batch, seq, D_FOURIER), BF16),
        grid=(batch, seq // FOURIER_TILE),
        in_specs=[
            pl.BlockSpec((FOURIER_TILE, 2 * seq), lambda b, m: (m, 0)),
            pl.BlockSpec((1, 2, seq, D_FOURIER), lambda b, m: (b, 0, 0, 0)),
        ],
        out_specs=pl.BlockSpec((1, FOURIER_TILE, D_FOURIER), lambda b, m: (b, m, 0)),
        compiler_params=_params(("parallel", "parallel")),
        name="fourier",
    )(m_pos, ab)


def _route(logits):
    lane = lax.broadcasted_iota(jnp.int32, logits.shape, 1)
    neg = -jnp.inf
    gl = jnp.where(lane < N_GROUPS, logits, neg)
    gmax = jnp.max(gl, axis=1, keepdims=True)
    grp = jnp.min(jnp.where(gl == gmax, lane, ROUTE_LANES), axis=1, keepdims=True)
    p_grp = 1.0 / jnp.sum(jnp.exp(gl - gmax), axis=1, keepdims=True)
    e_lane = lane - N_GROUPS
    in_grp = (e_lane >= 0) & (e_lane < N_EXPERTS) & ((e_lane // EXPERTS_PER_GROUP) == grp)
    el = jnp.where(in_grp, logits, neg)
    t1 = jnp.max(el, axis=1, keepdims=True)
    i1 = jnp.min(jnp.where(el == t1, lane, ROUTE_LANES), axis=1, keepdims=True)
    el2 = jnp.where(lane == i1, neg, el)
    t2 = jnp.max(el2, axis=1, keepdims=True)
    i2 = jnp.min(jnp.where(el2 == t2, lane, ROUTE_LANES), axis=1, keepdims=True)
    e = jnp.exp(t2 - t1)
    w1 = p_grp / (1.0 + e)
    w2 = p_grp * e / (1.0 + e)
    out = jnp.where(lane == 0, (i1 - N_GROUPS).astype(F32), 0.0)
    out = jnp.where(lane == 1, (i2 - N_GROUPS).astype(F32), out)
    out = jnp.where(lane == 2, w1, out)
    return jnp.where(lane == 3, w2, out)


def _merge_kernel(x_ref, sh1_ref, sc1_ref, g1_ref, sh2_ref, sc2_ref, n1_ref, n2_ref, wgate_ref,
                  fm_ref, hf_ref, hb_ref, yr_ref, wfo_ref, wlo_ref, wout_ref, wr_ref, br_ref,
                  x1_ref, h2_ref, route_ref):
    x = x_ref[...]
    hx = _rms_modulate(x, n1_ref[...], sh1_ref[0], sc1_ref[0]).astype(BF16)
    gates = jax.nn.sigmoid(jnp.dot(hx, wgate_ref[...], preferred_element_type=F32))
    branch_f = jnp.dot(fm_ref[0], wfo_ref[...], preferred_element_type=F32)
    y = yr_ref[...].astype(F32)
    gelu = 0.5 * y * (1.0 + jnp.tanh(np.sqrt(2.0 / np.pi).astype(np.float32) * (y + 0.044715 * (y * y * y))))
    lr = ((hf_ref[0, 0] + hb_ref[0, 0]) * gelu).astype(BF16)
    branch_r = jnp.dot(lr, wlo_ref[...], preferred_element_type=F32)
    mixed = gates[:, :D_MODEL] * branch_f + gates[:, D_MODEL:] * branch_r
    mix = jnp.dot(mixed.astype(BF16), wout_ref[...], preferred_element_type=F32)
    x1 = x + g1_ref[0] * mix
    x1_ref[...] = x1
    h2 = _rms_modulate(x1, n2_ref[...], sh2_ref[0], sc2_ref[0])
    h2_ref[...] = h2
    logits = jnp.dot(h2, wr_ref[...], preferred_element_type=F32,
                     precision=lax.Precision.HIGHEST) + br_ref[...]
    route_ref[...] = _route(logits)


def _merge_call(x2, mods, n1, n2, wgate, fm, h, yr, wfo, wlo, wout, wr, br, batch, seq):
    n = batch * seq
    tm = TOKEN_TILE
    tpb = seq // tm
    mod_spec = pl.BlockSpec((1, 1, D_MODEL), lambda i: (i // tpb, 0, 0))
    vec = pl.BlockSpec((1, D_MODEL), lambda i: (0, 0))
    tile = pl.BlockSpec((tm, D_MODEL), lambda i: (i, 0))

    def full(a):
        return pl.BlockSpec(a.shape, lambda i: (0,) * a.ndim)

    return pl.pallas_call(
        _merge_kernel,
        out_shape=(
            jax.ShapeDtypeStruct((n, D_MODEL), F32),
            jax.ShapeDtypeStruct((n, D_MODEL), F32),
            jax.ShapeDtypeStruct((n, ROUTE_LANES), F32),
        ),
        grid=(n // tm,),
        in_specs=[
            tile, mod_spec, mod_spec, mod_spec, mod_spec, mod_spec, vec, vec, full(wgate),
            pl.BlockSpec((1, tm, D_FOURIER), lambda i: (i // tpb, i % tpb, 0)),
            pl.BlockSpec((1, 1, tm, D_LRU), lambda i: (0, i // tpb, i % tpb, 0)),
            pl.BlockSpec((1, 1, tm, D_LRU), lambda i: (1, i // tpb, i % tpb, 0)),
            pl.BlockSpec((tm, D_LRU), lambda i: (i, 0)),
            full(wfo), full(wlo), full(wout), full(wr), full(br),
        ],
        out_specs=(tile, tile, pl.BlockSpec((tm, ROUTE_LANES), lambda i: (i, 0))),
        compiler_params=_params(("parallel",)),
        name="merge",
    )(x2, *mods, n1, n2, wgate, fm, h, h, yr, wfo, wlo, wout, wr, br)


def _row_copy(src_ref, dst_ref, sem, src_row, dst_row):
    return pltpu.make_async_copy(src_ref.at[pl.ds(src_row, 1)], dst_ref.at[pl.ds(dst_row, 1)], sem)


def _moe_kernel(be_ref, src_hbm, dst_hbm, h2_hbm, wg_ref, wu_ref, wd_ref, out_hbm,
                src_s, dst_s, xbuf, ybuf, wg_b, wu_b, wd_b, idx_sem, gat_sem, sct_sem):
    i = pl.program_id(0)

    @pl.when(i == 0)
    def _():
        ybuf[...] = jnp.zeros_like(ybuf)
        n_assign = out_hbm.shape[0] - N_DUMMY_ROWS
        fills = [pltpu.make_async_copy(ybuf, out_hbm.at[pl.ds(n_assign + q * MOE_BLOCK, MOE_BLOCK)], sct_sem)
                 for q in range(N_DUMMY_ROWS // MOE_BLOCK)]
        for cp in fills:
            cp.start()
        for cp in fills:
            cp.wait()

    src_cp = pltpu.make_async_copy(src_hbm.at[i], src_s, idx_sem.at[0])
    dst_cp = pltpu.make_async_copy(dst_hbm.at[i], dst_s, idx_sem.at[1])
    src_cp.start()
    dst_cp.start()

    prev_e = be_ref[jnp.maximum(i - 1, 0)]

    @pl.when((i == 0) | (be_ref[i] != prev_e))
    def _():
        wg_b[...] = wg_ref[0].astype(BF16)
        wu_b[...] = wu_ref[0].astype(BF16)
        wd_b[...] = wd_ref[0].astype(BF16)

    src_cp.wait()
    dst_cp.wait()

    def gather(r, c):
        _row_copy(h2_hbm, xbuf, gat_sem, src_s[0, r], r).start()
        return c

    lax.fori_loop(0, MOE_BLOCK, gather, 0)
    pltpu.make_async_copy(h2_hbm.at[pl.ds(0, MOE_BLOCK)], xbuf, gat_sem).wait()

    xb = xbuf[...].astype(BF16)
    hg = jnp.dot(xb, wg_b[...], preferred_element_type=F32)
    hu = jnp.dot(xb, wu_b[...], preferred_element_type=F32)
    hb = (hg * jax.nn.sigmoid(hg) * hu).astype(BF16)
    ybuf[...] = jnp.dot(hb, wd_b[...], preferred_element_type=F32)

    def scatter(r, c):
        _row_copy(ybuf, out_hbm, sct_sem, r, dst_s[0, r]).start()
        return c

    lax.fori_loop(0, MOE_BLOCK, scatter, 0)
    pltpu.make_async_copy(ybuf, out_hbm.at[pl.ds(0, MOE_BLOCK)], sct_sem).wait()


def _moe_call(blk_e, row_src, row_dst, h2, w_g, w_u, w_d, n_assign):
    n_blk = blk_e.shape[0]
    any_spec = pl.BlockSpec(memory_space=pl.ANY)
    grid_spec = pltpu.PrefetchScalarGridSpec(
        num_scalar_prefetch=1,
        grid=(n_blk,),
        in_specs=[
            any_spec, any_spec, any_spec,
            pl.BlockSpec((1, D_MODEL, D_EXPERT), lambda i, be: (be[i], 0, 0)),
            pl.BlockSpec((1, D_MODEL, D_EXPERT), lambda i, be: (be[i], 0, 0)),
            pl.BlockSpec((1, D_EXPERT, D_MODEL), lambda i, be: (be[i], 0, 0)),
        ],
        out_specs=any_spec,
        scratch_shapes=[
            pltpu.SMEM((1, MOE_BLOCK), jnp.int32),
            pltpu.SMEM((1, MOE_BLOCK), jnp.int32),
            pltpu.VMEM((MOE_BLOCK, D_MODEL), F32),
            pltpu.VMEM((MOE_BLOCK, D_MODEL), F32),
            pltpu.VMEM((D_MODEL, D_EXPERT), BF16),
            pltpu.VMEM((D_MODEL, D_EXPERT), BF16),
            pltpu.VMEM((D_EXPERT, D_MODEL), BF16),
            pltpu.SemaphoreType.DMA((2,)),
            pltpu.SemaphoreType.DMA,
            pltpu.SemaphoreType.DMA,
        ],
    )
    return pl.pallas_call(
        _moe_kernel,
        out_shape=jax.ShapeDtypeStruct((n_assign + N_DUMMY_ROWS, D_MODEL), F32),
        grid_spec=grid_spec,
        compiler_params=_params(("arbitrary",)),
        name="moe",
    )(blk_e, row_src, row_dst, h2, w_g, w_u, w_d)


def _final_kernel(x1_ref, y0_ref, y1_ref, route_ref, g2_ref, fg_ref, o_ref):
    route = route_ref[...]
    moe = route[:, 2:3] * y0_ref[...] + route[:, 3:4] * y1_ref[...]
    x2 = x1_ref[...] + g2_ref[0] * moe
    o_ref[...] = x2 * lax.rsqrt(jnp.mean(x2 * x2, axis=-1, keepdims=True) + EPS) * fg_ref[...]


def _final_call(x1, out2, route, g2, fg, batch, seq):
    n = batch * seq
    tm = TOKEN_TILE
    tpb = seq // tm
    tile = pl.BlockSpec((tm, D_MODEL), lambda i: (i, 0))
    return pl.pallas_call(
        _final_kernel,
        out_shape=jax.ShapeDtypeStruct((n, D_MODEL), F32),
        grid=(n // tm,),
        in_specs=[
            tile,
            pl.BlockSpec((tm, D_MODEL), lambda i: (i, 0)),
            pl.BlockSpec((tm, D_MODEL), lambda i: (i + n // tm, 0)),
            pl.BlockSpec((tm, ROUTE_LANES), lambda i: (i, 0)),
            pl.BlockSpec((1, 1, D_MODEL), lambda i: (i // tpb, 0, 0)),
            pl.BlockSpec((1, D_MODEL), lambda i: (0, 0)),
        ],
        out_specs=tile,
        compiler_params=_params(("parallel",)),
        name="final",
    )(x1, out2, out2, route, g2, fg)


def _channel_dft():
    j = np.arange(FOURIER_GROUP_DIM)
    ang = 2.0 * np.pi * np.outer(j, j) / FOURIER_GROUP_DIM
    return np.concatenate([np.cos(ang), np.sin(ang)], axis=1).astype(np.float32)


def _position_dft(seq):
    rows = seq // GRID_W
    assert GRID_W % rows == 0
    r, c = np.divmod(np.arange(seq), GRID_W)
    phase = (np.outer(r, r) * (GRID_W // rows) + np.outer(c, c)) % GRID_W
    ang = 2.0 * np.pi * phase / GRID_W
    scale = 1.0 / np.sqrt(float(seq) * FOURIER_GROUP_DIM)
    return np.concatenate([np.cos(ang), -np.sin(ang)], axis=1) * scale


def _block_diag(w):
    heads, hd, _ = w.shape
    eye = jnp.eye(heads, dtype=w.dtype)
    return jnp.einsum('hij,hg->higj', w, eye).reshape(heads * hd, heads * hd)


def _gate_weights(w_a, w_x):
    return jnp.stack([jnp.concatenate([_block_diag(w_a[d]), _block_diag(w_x[d])], axis=1)
                      for d in range(2)]).astype(BF16)


def _dispatch(eid, n_tokens):
    n_assign = n_tokens * TOP_K
    n_rows = n_assign + N_EXPERTS * MOE_BLOCK
    n_blk = n_rows // MOE_BLOCK
    flat_e = eid.reshape(-1)
    order = jnp.argsort(flat_e, stable=True).astype(jnp.int32)
    se = flat_e[order]
    start = jnp.searchsorted(se, jnp.arange(N_EXPERTS, dtype=jnp.int32), side='left').astype(jnp.int32)
    counts = jnp.diff(jnp.concatenate([start, jnp.array([n_assign], jnp.int32)]))
    padded = (counts + MOE_BLOCK - 1) // MOE_BLOCK * MOE_BLOCK
    pend = jnp.cumsum(padded)
    pstart = pend - padded
    blk_e = jnp.minimum(jnp.searchsorted(pend, jnp.arange(n_blk, dtype=jnp.int32) * MOE_BLOCK, side='right'),
                        N_EXPERTS - 1).astype(jnp.int32)
    r = jnp.arange(n_rows, dtype=jnp.int32)
    e_row = blk_e[r // MOE_BLOCK]
    j = r - pstart[e_row]
    valid = (j >= 0) & (j < counts[e_row])
    a_idx = order[jnp.clip(start[e_row] + j, 0, n_assign - 1)]
    row_src = jnp.where(valid, a_idx // TOP_K, 0)
    row_dst = jnp.where(valid, (a_idx % TOP_K) * n_tokens + a_idx // TOP_K, n_assign + r % N_DUMMY_ROWS)
    return blk_e, row_src.reshape(n_blk, 1, MOE_BLOCK), row_dst.reshape(n_blk, 1, MOE_BLOCK)


def kernel(x, c, ctx, c_ctx, w_mod, b_mod, norm1_g, w_in, conv_w, conv_b, lru_wa, lru_ba, lru_wx, lru_bx,
           lru_lam, w_fourier_out, w_lru_out, w_out, norm2_g, w_group, b_group, w_expert_router,
           b_expert_router, w_gate_e, w_up_e, w_down_e, final_g):
    batch, seq, _ = x.shape
    ctx_len = ctx.shape[1]
    n = batch * seq
    assert w_mod.shape[0] == 1, "single-layer stack only: the context stream is not carried across layers"
    x2 = x.reshape(n, D_MODEL)
    dft_ch = jnp.asarray(_channel_dft().astype(BF16))
    m_pos = jnp.asarray(_position_dft(seq).astype(BF16))

    for l in range(1):
        c_all = jnp.concatenate([c, c_ctx[None], jnp.zeros((MOD_ROWS - batch - 1, D_MODEL), F32)], axis=0)
        mod = _mod_call(c_all, w_mod[l], b_mod[l][None])
        sh1, sc1, g1, sh2, sc2, g2 = [m[:batch, None, :] for m in jnp.split(mod, 6, axis=-1)]
        csh1, csc1 = mod[batch:batch + 1, :D_MODEL], mod[batch:batch + 1, D_MODEL:2 * D_MODEL]

        w_in_b = w_in[l].astype(BF16)
        n1 = norm1_g[l][None]
        wg = _gate_weights(lru_wa[l], lru_wx[l])
        bg = jnp.concatenate([lru_ba[l], lru_bx[l]], axis=-1)[:, None, :]
        lam = lru_lam[l][:, None, :]
        cb = conv_b[l][None]

        uc = _inproj_ctx_call(ctx.reshape(batch * ctx_len, D_MODEL), csh1, csc1, n1,
                              w_in_b[:, D_FOURIER:D_FOURIER + D_LRU])
        hc = _rglru_call(uc.reshape(batch, ctx_len, D_LRU), conv_w[l], cb, wg, bg, lam,
                         jnp.zeros((2, batch, D_LRU), F32))
        h0 = jnp.stack([hc[0, :, -1], hc[1, :, 0]])

        ab, ur, yr = _inproj_call(x2, sh1, sc1, n1, w_in_b[:, :D_FOURIER + 2 * D_LRU], dft_ch, batch, seq)
        h = _rglru_call(ur.reshape(batch, seq, D_LRU), conv_w[l], cb, wg, bg, lam, h0)
        fm = _fourier_call(m_pos, ab)

        w_route = jnp.zeros((D_MODEL, ROUTE_LANES), F32)
        w_route = w_route.at[:, :N_GROUPS].set(w_group[l]).at[:, N_GROUPS:N_GROUPS + N_EXPERTS].set(
            w_expert_router[l])
        b_route = jnp.zeros((1, ROUTE_LANES), F32)
        b_route = b_route.at[0, :N_GROUPS].set(b_group[l]).at[0, N_GROUPS:N_GROUPS + N_EXPERTS].set(
            b_expert_router[l])
        x1, h2, route = _merge_call(
            x2, (sh1, sc1, g1, sh2, sc2), n1, norm2_g[l][None], w_in_b[:, D_FOURIER + 2 * D_LRU:],
            fm, h, yr, w_fourier_out[l].astype(BF16), w_lru_out[l].astype(BF16), w_out[l].astype(BF16),
            w_route, b_route, batch, seq)

        eid = route[:, :TOP_K].astype(jnp.int32)
        blk_e, row_src, row_dst = _dispatch(eid, n)
        out2 = _moe_call(blk_e, row_src, row_dst, h2, w_gate_e[l], w_up_e[l], w_down_e[l], TOP_K * n)
        x2 = _final_call(x1, out2, route, g2, final_g[None], batch, seq)
    return x2.reshape(batch, seq, D_MODEL)
```

```python
import functools

import numpy as np
import jax
import jax.numpy as jnp
from jax import lax
from jax.experimental import pallas as pl
from jax.experimental.pallas import tpu as pltpu

F32 = jnp.float32
BF16 = jnp.bfloat16

LANES = 128
SUBLANES = 8
VMEM_LIMIT_BYTES = 56 * 1024 * 1024

D_MODEL = 1024
GRID_W = 64
EPS = 1e-6
N_FOURIER_GROUPS = 4
FOURIER_GROUP_DIM = 128
D_FOURIER = N_FOURIER_GROUPS * FOURIER_GROUP_DIM
N_LRU_HEADS = 8
LRU_HEAD_DIM = 64
D_LRU = N_LRU_HEADS * LRU_HEAD_DIM
CONV_WIDTH = 4
LRU_C = 8.0
N_GROUPS = 4
EXPERTS_PER_GROUP = 8
N_EXPERTS = N_GROUPS * EXPERTS_PER_GROUP
TOP_K = 2
D_EXPERT = 512

MOD_ROWS = 24
MOD_BLOCK_N = 1536
TOKEN_TILE = 512
SCAN_BATCH = SUBLANES
SCAN_CHUNK = 256
SCAN_PITCH = SCAN_CHUNK + SUBLANES
N_SLABS = D_LRU // LANES
FOURIER_TILE = 512
MOE_BLOCK = 256
ROUTE_LANES = LANES
N_DUMMY_ROWS = 4 * MOE_BLOCK


def _params(semantics):
    return pltpu.CompilerParams(dimension_semantics=semantics, vmem_limit_bytes=VMEM_LIMIT_BYTES)


def _rms_modulate(x, g, shift, scale):
    y = x * lax.rsqrt(jnp.mean(x * x, axis=-1, keepdims=True) + EPS) * g
    return y * (1.0 + scale) + shift


def _mod_kernel(c_ref, w_ref, b_ref, o_ref):
    c = c_ref[...]
    s = c * jax.nn.sigmoid(c)
    o_ref[...] = jnp.dot(s, w_ref[...], preferred_element_type=F32,
                         precision=lax.Precision.HIGHEST) + b_ref[...]


def _mod_call(c_all, w_mod, b_mod):
    n_out = w_mod.shape[1]
    return pl.pallas_call(
        _mod_kernel,
        out_shape=jax.ShapeDtypeStruct((MOD_ROWS, n_out), F32),
        grid=(n_out // MOD_BLOCK_N,),
        in_specs=[
            pl.BlockSpec((MOD_ROWS, D_MODEL), lambda j: (0, 0)),
            pl.BlockSpec((D_MODEL, MOD_BLOCK_N), lambda j: (0, j)),
            pl.BlockSpec((1, MOD_BLOCK_N), lambda j: (0, j)),
        ],
        out_specs=pl.BlockSpec((MOD_ROWS, MOD_BLOCK_N), lambda j: (0, j)),
        compiler_params=_params(("arbitrary",)),
        name="mod",
    )(c_all, w_mod, b_mod)


def _inproj_kernel(x_ref, sh_ref, sc_ref, g_ref, w_ref, dft_ref, ab_ref, ur_ref, yr_ref):
    hx = _rms_modulate(x_ref[...], g_ref[...], sh_ref[0], sc_ref[0]).astype(BF16)
    proj = jnp.dot(hx, w_ref[...], preferred_element_type=F32)
    uf = proj[:, :D_FOURIER].astype(BF16)
    for g in range(N_FOURIER_GROUPS):
        lo, hi = g * FOURIER_GROUP_DIM, (g + 1) * FOURIER_GROUP_DIM
        cs = jnp.dot(uf[:, lo:hi], dft_ref[...], preferred_element_type=F32)
        ab_ref[0, 0, :, lo:hi] = cs[:, :FOURIER_GROUP_DIM].astype(BF16)
        ab_ref[0, 1, :, lo:hi] = cs[:, FOURIER_GROUP_DIM:].astype(BF16)
    ur_ref[...] = proj[:, D_FOURIER:D_FOURIER + D_LRU]
    yr_ref[...] = proj[:, D_FOURIER + D_LRU:].astype(BF16)


def _inproj_call(x2, sh, sc, g, w, dft_ch, batch, seq):
    n = batch * seq
    tpb = seq // TOKEN_TILE
    mod_spec = pl.BlockSpec((1, 1, D_MODEL), lambda i: (i // tpb, 0, 0))
    return pl.pallas_call(
        _inproj_kernel,
        out_shape=(
            jax.ShapeDtypeStruct((batch, 2, seq, D_FOURIER), BF16),
            jax.ShapeDtypeStruct((n, D_LRU), F32),
            jax.ShapeDtypeStruct((n, D_LRU), BF16),
        ),
        grid=(n // TOKEN_TILE,),
        in_specs=[
            pl.BlockSpec((TOKEN_TILE, D_MODEL), lambda i: (i, 0)),
            mod_spec, mod_spec,
            pl.BlockSpec((1, D_MODEL), lambda i: (0, 0)),
            pl.BlockSpec(w.shape, lambda i: (0, 0)),
            pl.BlockSpec(dft_ch.shape, lambda i: (0, 0)),
        ],
        out_specs=(
            pl.BlockSpec((1, 2, TOKEN_TILE, D_FOURIER), lambda i: (i // tpb, 0, i % tpb, 0)),
            pl.BlockSpec((TOKEN_TILE, D_LRU), lambda i: (i, 0)),
            pl.BlockSpec((TOKEN_TILE, D_LRU), lambda i: (i, 0)),
        ),
        compiler_params=_params(("parallel",)),
        name="inproj",
    )(x2, sh, sc, g, w, dft_ch)


def _inproj_ctx_kernel(x_ref, sh_ref, sc_ref, g_ref, w_ref, ur_ref):
    hx = _rms_modulate(x_ref[...], g_ref[...], sh_ref[...], sc_ref[...]).astype(BF16)
    ur_ref[...] = jnp.dot(hx, w_ref[...], preferred_element_type=F32)


def _inproj_ctx_call(ctx2, sh, sc, g, w):
    n = ctx2.shape[0]
    vec = pl.BlockSpec((1, D_MODEL), lambda i: (0, 0))
    return pl.pallas_call(
        _inproj_ctx_kernel,
        out_shape=jax.ShapeDtypeStruct((n, D_LRU), F32),
        grid=(n // TOKEN_TILE,),
        in_specs=[pl.BlockSpec((TOKEN_TILE, D_MODEL), lambda i: (i, 0)), vec, vec, vec,
                  pl.BlockSpec(w.shape, lambda i: (0, 0))],
        out_specs=pl.BlockSpec((TOKEN_TILE, D_LRU), lambda i: (i, 0)),
        compiler_params=_params(("parallel",)),
        name="inproj_ctx",
    )(ctx2, sh, sc, g, w)


def _rglru_kernel(n_chunks, u_ref, up_ref, un_ref, cw_ref, cb_ref, wg_ref, bg_ref, lam_ref, h0_ref,
                  h_ref, a_s, b_s, h_s, state):
    tc, pitch = SCAN_CHUNK, SCAN_PITCH
    d = pl.program_id(0)
    k = pl.program_id(2)
    kk = k + d * (n_chunks - 1 - 2 * k)

    @pl.when(k == 0)
    def _():
        state[...] = h0_ref[0]

    lam = lam_ref[0]
    neg_lam = -lam
    softplus = jnp.maximum(neg_lam, 0.0) + jnp.log1p(jnp.exp(-jnp.abs(neg_lam)))
    has_prev = kk > 0
    has_next = kk < n_chunks - 1
    row = lax.broadcasted_iota(jnp.int32, (tc, D_LRU), 0)
    cw = cw_ref[...]

    for i in range(SCAN_BATCH):
        u = u_ref[i]
        prev = jnp.where(has_prev, up_ref[i], 0.0)
        nxt = jnp.where(has_next, un_ref[i], 0.0)
        p2, p1, n1 = prev[SUBLANES - 2:SUBLANES - 1], prev[SUBLANES - 1:SUBLANES], nxt[0:1]
        um1 = jnp.where(row == 0, p1, pltpu.roll(u, 1, 0))
        um2 = jnp.where(row == 0, p2, jnp.where(row == 1, p1, pltpu.roll(u, 2, 0)))
        up1 = jnp.where(row == tc - 1, n1, pltpu.roll(u, tc - 1, 0))
        xc = um2 * cw[0:1] + um1 * cw[1:2] + u * cw[2:3] + up1 * cw[3:4] + cb_ref[...]
        gz = jnp.dot(xc.astype(BF16), wg_ref[0], preferred_element_type=F32) + bg_ref[0]
        r = jax.nn.sigmoid(gz[:, :D_LRU])
        ig = jax.nn.sigmoid(gz[:, D_LRU:])
        log_a = -LRU_C * r * softplus
        a = jnp.exp(log_a)
        b = jnp.sqrt((1.0 - a) * (1.0 + a)) * (ig * xc)
        for j in range(N_SLABS):
            a_s[j, pl.ds(i * pitch, tc), :] = a[:, j * LANES:(j + 1) * LANES]
            b_s[j, pl.ds(i * pitch, tc), :] = b[:, j * LANES:(j + 1) * LANES]

    def step(s, h):
        t = s + d * (tc - 1 - 2 * s)
        out = []
        for j in range(N_SLABS):
            rows = pl.ds(t, SCAN_BATCH, stride=pitch)
            hj = a_s[j, rows, :] * h[j] + b_s[j, rows, :]
            h_s[j, rows, :] = hj
            out.append(hj)
        return tuple(out)

    st = state[...]
    h = lax.fori_loop(0, tc, step, tuple(st[:, j * LANES:(j + 1) * LANES] for j in range(N_SLABS)),
                      unroll=4)
    for j in range(N_SLABS):
        state[:, j * LANES:(j + 1) * LANES] = h[j]
    for i in range(SCAN_BATCH):
        for j in range(N_SLABS):
            h_ref[0, i, :, j * LANES:(j + 1) * LANES] = h_s[j, pl.ds(i * pitch, tc), :]


def _rglru_call(u3, conv_w, conv_b, wg, bg, lam, h0):
    batch, seq, _ = u3.shape
    n_chunks = seq // SCAN_CHUNK
    halo_blocks = SCAN_CHUNK // SUBLANES
    last_halo = seq // SUBLANES - 1

    def chunk(d, k):
        return k + d * (n_chunks - 1 - 2 * k)

    scratch = pltpu.VMEM((N_SLABS, SCAN_BATCH * SCAN_PITCH, LANES), F32)
    return pl.pallas_call(
        functools.partial(_rglru_kernel, n_chunks),
        out_shape=jax.ShapeDtypeStruct((2, batch, seq, D_LRU), F32),
        grid=(2, batch // SCAN_BATCH, n_chunks),
        in_specs=[
            pl.BlockSpec((SCAN_BATCH, SCAN_CHUNK, D_LRU), lambda d, g, k: (g, chunk(d, k), 0)),
            pl.BlockSpec((SCAN_BATCH, SUBLANES, D_LRU),
                         lambda d, g, k: (g, jnp.maximum(chunk(d, k) * halo_blocks - 1, 0), 0)),
            pl.BlockSpec((SCAN_BATCH, SUBLANES, D_LRU),
                         lambda d, g, k: (g, jnp.minimum((chunk(d, k) + 1) * halo_blocks, last_halo), 0)),
            pl.BlockSpec((CONV_WIDTH, D_LRU), lambda d, g, k: (0, 0)),
            pl.BlockSpec((1, D_LRU), lambda d, g, k: (0, 0)),
            pl.BlockSpec((1, D_LRU, 2 * D_LRU), lambda d, g, k: (d, 0, 0)),
            pl.BlockSpec((1, 1, 2 * D_LRU), lambda d, g, k: (d, 0, 0)),
            pl.BlockSpec((1, 1, D_LRU), lambda d, g, k: (d, 0, 0)),
            pl.BlockSpec((1, SCAN_BATCH, D_LRU), lambda d, g, k: (d, g, 0)),
        ],
        out_specs=pl.BlockSpec((1, SCAN_BATCH, SCAN_CHUNK, D_LRU), lambda d, g, k: (d, g, chunk(d, k), 0)),
        scratch_shapes=[scratch, scratch, scratch, pltpu.VMEM((SCAN_BATCH, D_LRU), F32)],
        compiler_params=_params(("arbitrary", "arbitrary", "arbitrary")),
        name="rglru",
    )(u3, u3, u3, conv_w, conv_b, wg, bg, lam, h0)


def _fourier_kernel(m_ref, ab_ref, o_ref):
    seq2 = m_ref.shape[1]
    rhs = ab_ref[0].reshape(seq2, D_FOURIER)
    o_ref[0] = jnp.dot(m_ref[...], rhs, preferred_element_type=F32).astype(BF16)


def _fourier_call(m_pos, ab):
    batch, _, seq, _ = ab.shape
    return pl.pallas_call(
        _fourier_kernel,
        out_shape=jax.ShapeDtypeStruct((batch, seq, D_FOURIER), BF16),
        grid=(batch, seq // FOURIER_TILE),
        in_specs=[
            pl.BlockSpec((FOURIER_TILE, 2 * seq), lambda b, m: (m, 0)),
            pl.BlockSpec((1, 2, seq, D_FOURIER), lambda b, m: (b, 0, 0, 0)),
        ],
        out_specs=pl.BlockSpec((1, FOURIER_TILE, D_FOURIER), lambda b, m: (b, m, 0)),
        compiler_params=_params(("parallel", "parallel")),
        name="fourier",
    )(m_pos, ab)


def _route(logits):
    lane = lax.broadcasted_iota(jnp.int32, logits.shape, 1)
    neg = -jnp.inf
    gl = jnp.where(lane < N_GROUPS, logits, neg)
    gmax = jnp.max(gl, axis=1, keepdims=True)
    grp = jnp.min(jnp.where(gl == gmax, lane, ROUTE_LANES), axis=1, keepdims=True)
    p_grp = 1.0 / jnp.sum(jnp.exp(gl - gmax), axis=1, keepdims=True)
    e_lane = lane - N_GROUPS
    in_grp = (e_lane >= 0) & (e_lane < N_EXPERTS) & ((e_lane // EXPERTS_PER_GROUP) == grp)
    el = jnp.where(in_grp, logits, neg)
    t1 = jnp.max(el, axis=1, keepdims=True)
    i1 = jnp.min(jnp.where(el == t1, lane, ROUTE_LANES), axis=1, keepdims=True)
    el2 = jnp.where(lane == i1, neg, el)
    t2 = jnp.max(el2, axis=1, keepdims=True)
    i2 = jnp.min(jnp.where(el2 == t2, lane, ROUTE_LANES), axis=1, keepdims=True)
    e = jnp.exp(t2 - t1)
    w1 = p_grp / (1.0 + e)
    w2 = p_grp * e / (1.0 + e)
    out = jnp.where(lane == 0, (i1 - N_GROUPS).astype(F32), 0.0)
    out = jnp.where(lane == 1, (i2 - N_GROUPS).astype(F32), out)
    out = jnp.where(lane == 2, w1, out)
    return jnp.where(lane == 3, w2, out)


def _merge_kernel(x_ref, sh1_ref, sc1_ref, g1_ref, sh2_ref, sc2_ref, n1_ref, n2_ref, wgate_ref,
                  fm_ref, hf_ref, hb_ref, yr_ref, wfo_ref, wlo_ref, wout_ref, wr_ref, br_ref,
                  x1_ref, h2_ref, route_ref):
    x = x_ref[...]
    hx = _rms_modulate(x, n1_ref[...], sh1_ref[0], sc1_ref[0]).astype(BF16)
    gates = jax.nn.sigmoid(jnp.dot(hx, wgate_ref[...], preferred_element_type=F32))
    branch_f = jnp.dot(fm_ref[0], wfo_ref[...], preferred_element_type=F32)
    y = yr_ref[...].astype(F32)
    gelu = 0.5 * y * (1.0 + jnp.tanh(np.sqrt(2.0 / np.pi).astype(np.float32) * (y + 0.044715 * (y * y * y))))
    lr = ((hf_ref[0, 0] + hb_ref[0, 0]) * gelu).astype(BF16)
    branch_r = jnp.dot(lr, wlo_ref[...], preferred_element_type=F32)
    mixed = gates[:, :D_MODEL] * branch_f + gates[:, D_MODEL:] * branch_r
    mix = jnp.dot(mixed.astype(BF16), wout_ref[...], preferred_element_type=F32)
    x1 = x + g1_ref[0] * mix
    x1_ref[...] = x1
    h2 = _rms_modulate(x1, n2_ref[...], sh2_ref[0], sc2_ref[0])
    h2_ref[...] = h2
    h2_hi = h2.astype(BF16)
    h2_lo = (h2 - h2_hi.astype(F32)).astype(BF16)
    parts = (jnp.dot(h2_hi, wr_ref[...], preferred_element_type=F32)
             + jnp.dot(h2_lo, wr_ref[...], preferred_element_type=F32))
    route_ref[...] = _route(parts[:, :ROUTE_LANES] + parts[:, ROUTE_LANES:] + br_ref[...])


def _merge_call(x2, mods, n1, n2, wgate, fm, h, yr, wfo, wlo, wout, wr, br, batch, seq):
    n = batch * seq
    tm = TOKEN_TILE
    tpb = seq // tm
    mod_spec = pl.BlockSpec((1, 1, D_MODEL), lambda i: (i // tpb, 0, 0))
    vec = pl.BlockSpec((1, D_MODEL), lambda i: (0, 0))
    tile = pl.BlockSpec((tm, D_MODEL), lambda i: (i, 0))

    def full(a):
        return pl.BlockSpec(a.shape, lambda i: (0,) * a.ndim)

    return pl.pallas_call(
        _merge_kernel,
        out_shape=(
            jax.ShapeDtypeStruct((n, D_MODEL), F32),
            jax.ShapeDtypeStruct((n, D_MODEL), F32),
            jax.ShapeDtypeStruct((n, ROUTE_LANES), F32),
        ),
        grid=(n // tm,),
        in_specs=[
            tile, mod_spec, mod_spec, mod_spec, mod_spec, mod_spec, vec, vec, full(wgate),
            pl.BlockSpec((1, tm, D_FOURIER), lambda i: (i // tpb, i % tpb, 0)),
            pl.BlockSpec((1, 1, tm, D_LRU), lambda i: (0, i // tpb, i % tpb, 0)),
            pl.BlockSpec((1, 1, tm, D_LRU), lambda i: (1, i // tpb, i % tpb, 0)),
            pl.BlockSpec((tm, D_LRU), lambda i: (i, 0)),
            full(wfo), full(wlo), full(wout), full(wr), full(br),
        ],
        out_specs=(tile, tile, pl.BlockSpec((tm, ROUTE_LANES), lambda i: (i, 0))),
        compiler_params=_params(("parallel",)),
        name="merge",
    )(x2, *mods, n1, n2, wgate, fm, h, h, yr, wfo, wlo, wout, wr, br)


def _row_copy(src_ref, dst_ref, sem, src_row, dst_row):
    return pltpu.make_async_copy(src_ref.at[pl.ds(src_row, 1)], dst_ref.at[pl.ds(dst_row, 1)], sem)


def _moe_kernel(n_tokens, be_ref, off_ref, nv_ref, order_ref, h2_hbm, wg_ref, wu_ref, wd_ref, out_hbm,
                xbuf, ybuf, wg_b, wu_b, wd_b, gat_sem, sct_sem):
    i = pl.program_id(0)
    n_blk = pl.num_programs(0)
    n_assign = n_tokens * TOP_K
    slot = i % 2

    def block_copy_for_wait(sem, to_hbm):
        if to_hbm:
            return pltpu.make_async_copy(ybuf.at[0], out_hbm.at[pl.ds(0, MOE_BLOCK)], sem)
        return pltpu.make_async_copy(h2_hbm.at[pl.ds(0, MOE_BLOCK)], xbuf.at[0], sem)

    def assignment(b, r):
        valid = r < nv_ref[b]
        a = order_ref[jnp.minimum(off_ref[b] + r, n_assign - 1)]
        return valid, a // TOP_K, a % TOP_K

    def issue_gather(b, s):
        def body(r, c):
            valid, tok, _ = assignment(b, r)
            _row_copy(h2_hbm, xbuf.at[s], gat_sem.at[s], jnp.where(valid, tok, 0), r).start()
            return c
        lax.fori_loop(0, MOE_BLOCK, body, 0, unroll=8)

    def issue_scatter(b, s):
        dummy0 = n_assign + (b % (N_DUMMY_ROWS // MOE_BLOCK)) * MOE_BLOCK

        def body(r, c):
            valid, tok, k = assignment(b, r)
            _row_copy(ybuf.at[s], out_hbm, sct_sem.at[s], r, jnp.where(valid, k * n_tokens + tok, dummy0 + r)).start()
            return c
        lax.fori_loop(0, MOE_BLOCK, body, 0, unroll=8)

    @pl.when(i == 0)
    def _():
        ybuf[0] = jnp.zeros((MOE_BLOCK, D_MODEL), F32)
        fills = [pltpu.make_async_copy(ybuf.at[0], out_hbm.at[pl.ds(n_assign + q * MOE_BLOCK, MOE_BLOCK)],
                                       sct_sem.at[0]) for q in range(N_DUMMY_ROWS // MOE_BLOCK)]
        for cp in fills:
            cp.start()
        for cp in fills:
            cp.wait()

        @pl.when(nv_ref[0] > 0)
        def _():
            issue_gather(0, 0)

    nxt = jnp.minimum(i + 1, n_blk - 1)

    @pl.when((i + 1 < n_blk) & (nv_ref[nxt] > 0))
    def _():
        issue_gather(nxt, 1 - slot)

    @pl.when((i == 0) | (be_ref[i] != be_ref[jnp.maximum(i - 1, 0)]))
    def _():
        wg_b[...] = wg_ref[0].astype(BF16)
        wu_b[...] = wu_ref[0].astype(BF16)
        wd_b[...] = wd_ref[0].astype(BF16)

    @pl.when((i >= 2) & (nv_ref[jnp.maximum(i - 2, 0)] > 0))
    def _():
        block_copy_for_wait(sct_sem.at[slot], True).wait()

    @pl.when(nv_ref[i] > 0)
    def _():
        block_copy_for_wait(gat_sem.at[slot], False).wait()
        xb = xbuf[slot].astype(BF16)
        hg = jnp.dot(xb, wg_b[...], preferred_element_type=F32)
        hu = jnp.dot(xb, wu_b[...], preferred_element_type=F32)
        hb = (hg * jax.nn.sigmoid(hg) * hu).astype(BF16)
        ybuf[slot] = jnp.dot(hb, wd_b[...], preferred_element_type=F32)
        issue_scatter(i, slot)

    @pl.when(i == n_blk - 1)
    def _():
        @pl.when((i >= 1) & (nv_ref[jnp.maximum(i - 1, 0)] > 0))
        def _():
            block_copy_for_wait(sct_sem.at[1 - slot], True).wait()

        @pl.when(nv_ref[i] > 0)
        def _():
            block_copy_for_wait(sct_sem.at[slot], True).wait()


def _moe_call(blk_e, blk_off, blk_nv, order, h2, w_g, w_u, w_d):
    n_blk = blk_e.shape[0]
    n_tokens = h2.shape[0]
    any_spec = pl.BlockSpec(memory_space=pl.ANY)

    def expert(i, be, off, nv, order):
        return (be[i], 0, 0)

    grid_spec = pltpu.PrefetchScalarGridSpec(
        num_scalar_prefetch=4,
        grid=(n_blk,),
        in_specs=[
            any_spec,
            pl.BlockSpec((1, D_MODEL, D_EXPERT), expert),
            pl.BlockSpec((1, D_MODEL, D_EXPERT), expert),
            pl.BlockSpec((1, D_EXPERT, D_MODEL), expert),
        ],
        out_specs=any_spec,
        scratch_shapes=[
            pltpu.VMEM((2, MOE_BLOCK, D_MODEL), F32),
            pltpu.VMEM((2, MOE_BLOCK, D_MODEL), F32),
            pltpu.VMEM((D_MODEL, D_EXPERT), BF16),
            pltpu.VMEM((D_MODEL, D_EXPERT), BF16),
            pltpu.VMEM((D_EXPERT, D_MODEL), BF16),
            pltpu.SemaphoreType.DMA((2,)),
            pltpu.SemaphoreType.DMA((2,)),
        ],
    )
    return pl.pallas_call(
        functools.partial(_moe_kernel, n_tokens),
        out_shape=jax.ShapeDtypeStruct((n_tokens * TOP_K + N_DUMMY_ROWS, D_MODEL), F32),
        grid_spec=grid_spec,
        compiler_params=_params(("arbitrary",)),
        name="moe",
    )(blk_e, blk_off, blk_nv, order, h2, w_g, w_u, w_d)


def _final_kernel(x1_ref, y0_ref, y1_ref, route_ref, g2_ref, fg_ref, o_ref):
    route = route_ref[...]
    moe = route[:, 2:3] * y0_ref[...] + route[:, 3:4] * y1_ref[...]
    x2 = x1_ref[...] + g2_ref[0] * moe
    o_ref[...] = x2 * lax.rsqrt(jnp.mean(x2 * x2, axis=-1, keepdims=True) + EPS) * fg_ref[...]


def _final_call(x1, out2, route, g2, fg, batch, seq):
    n = batch * seq
    tm = TOKEN_TILE
    tpb = seq // tm
    tile = pl.BlockSpec((tm, D_MODEL), lambda i: (i, 0))
    return pl.pallas_call(
        _final_kernel,
        out_shape=jax.ShapeDtypeStruct((n, D_MODEL), F32),
        grid=(n // tm,),
        in_specs=[
            tile,
            pl.BlockSpec((tm, D_MODEL), lambda i: (i, 0)),
            pl.BlockSpec((tm, D_MODEL), lambda i: (i + n // tm, 0)),
            pl.BlockSpec((tm, ROUTE_LANES), lambda i: (i, 0)),
            pl.BlockSpec((1, 1, D_MODEL), lambda i: (i // tpb, 0, 0)),
            pl.BlockSpec((1, D_MODEL), lambda i: (0, 0)),
        ],
        out_specs=tile,
        compiler_params=_params(("parallel",)),
        name="final",
    )(x1, out2, out2, route, g2, fg)


def _channel_dft():
    j = np.arange(FOURIER_GROUP_DIM)
    ang = 2.0 * np.pi * np.outer(j, j) / FOURIER_GROUP_DIM
    return np.concatenate([np.cos(ang), np.sin(ang)], axis=1).astype(np.float32)


def _position_dft(seq):
    rows = seq // GRID_W
    assert GRID_W % rows == 0
    r, c = np.divmod(np.arange(seq), GRID_W)
    phase = (np.outer(r, r) * (GRID_W // rows) + np.outer(c, c)) % GRID_W
    ang = 2.0 * np.pi * phase / GRID_W
    scale = 1.0 / np.sqrt(float(seq) * FOURIER_GROUP_DIM)
    return np.concatenate([np.cos(ang), -np.sin(ang)], axis=1) * scale


def _block_diag(w):
    heads, hd, _ = w.shape
    eye = jnp.eye(heads, dtype=w.dtype)
    return jnp.einsum('hij,hg->higj', w, eye).reshape(heads * hd, heads * hd)


def _gate_weights(w_a, w_x):
    return jnp.stack([jnp.concatenate([_block_diag(w_a[d]), _block_diag(w_x[d])], axis=1)
                      for d in range(2)]).astype(BF16)


def _dispatch(eid, n_tokens):
    n_assign = n_tokens * TOP_K
    n_blk = n_assign // MOE_BLOCK + N_EXPERTS
    flat_e = eid.reshape(-1)
    _, order = lax.sort((flat_e, jnp.arange(n_assign, dtype=jnp.int32)), num_keys=1)
    experts = jnp.arange(N_EXPERTS, dtype=jnp.int32)
    counts = jnp.sum((flat_e[:, None] == experts[None, :]).astype(jnp.int32), axis=0)
    start = jnp.cumsum(counts) - counts
    blocks = (counts + MOE_BLOCK - 1) // MOE_BLOCK
    bend = jnp.cumsum(blocks)
    b = jnp.arange(n_blk, dtype=jnp.int32)
    owner = (b[:, None] >= bend[None, :]).astype(jnp.int32)
    blk_e = jnp.minimum(jnp.sum(owner, axis=1), N_EXPERTS - 1)
    onehot = (blk_e[:, None] == experts[None, :]).astype(jnp.int32)
    first = jnp.sum(onehot * (bend - blocks)[None, :], axis=1)
    in_expert = (b - first) * MOE_BLOCK
    blk_off = jnp.sum(onehot * start[None, :], axis=1) + in_expert
    blk_nv = jnp.clip(jnp.sum(onehot * counts[None, :], axis=1) - in_expert, 0, MOE_BLOCK)
    return blk_e, blk_off, blk_nv, order


def kernel(x, c, ctx, c_ctx, w_mod, b_mod, norm1_g, w_in, conv_w, conv_b, lru_wa, lru_ba, lru_wx, lru_bx,
           lru_lam, w_fourier_out, w_lru_out, w_out, norm2_g, w_group, b_group, w_expert_router,
           b_expert_router, w_gate_e, w_up_e, w_down_e, final_g):
    batch, seq, _ = x.shape
    ctx_len = ctx.shape[1]
    n = batch * seq
    assert w_mod.shape[0] == 1, "single-layer stack only: the context stream is not carried across layers"
    x2 = x.reshape(n, D_MODEL)
    dft_ch = jnp.asarray(_channel_dft().astype(BF16))
    m_pos = jnp.asarray(_position_dft(seq).astype(BF16))

    for l in range(1):
        c_all = jnp.concatenate([c, c_ctx[None], jnp.zeros((MOD_ROWS - batch - 1, D_MODEL), F32)], axis=0)
        mod = _mod_call(c_all, w_mod[l], b_mod[l][None])
        sh1, sc1, g1, sh2, sc2, g2 = [m[:batch, None, :] for m in jnp.split(mod, 6, axis=-1)]
        csh1, csc1 = mod[batch:batch + 1, :D_MODEL], mod[batch:batch + 1, D_MODEL:2 * D_MODEL]

        w_in_b = w_in[l].astype(BF16)
        n1 = norm1_g[l][None]
        wg = _gate_weights(lru_wa[l], lru_wx[l])
        bg = jnp.concatenate([lru_ba[l], lru_bx[l]], axis=-1)[:, None, :]
        lam = lru_lam[l][:, None, :]
        cb = conv_b[l][None]

        uc = _inproj_ctx_call(ctx.reshape(batch * ctx_len, D_MODEL), csh1, csc1, n1,
                              w_in_b[:, D_FOURIER:D_FOURIER + D_LRU])
        hc = _rglru_call(uc.reshape(batch, ctx_len, D_LRU), conv_w[l], cb, wg, bg, lam,
                         jnp.zeros((2, batch, D_LRU), F32))
        h0 = jnp.stack([hc[0, :, -1], hc[1, :, 0]])

        ab, ur, yr = _inproj_call(x2, sh1, sc1, n1, w_in_b[:, :D_FOURIER + 2 * D_LRU], dft_ch, batch, seq)
        h = _rglru_call(ur.reshape(batch, seq, D_LRU), conv_w[l], cb, wg, bg, lam, h0)
        fm = _fourier_call(m_pos, ab)

        w_route = jnp.zeros((D_MODEL, ROUTE_LANES), F32)
        w_route = w_route.at[:, :N_GROUPS].set(w_group[l]).at[:, N_GROUPS:N_GROUPS + N_EXPERTS].set(
            w_expert_router[l])
        b_route = jnp.zeros((1, ROUTE_LANES), F32)
        b_route = b_route.at[0, :N_GROUPS].set(b_group[l]).at[0, N_GROUPS:N_GROUPS + N_EXPERTS].set(
            b_expert_router[l])
        w_route_hi = w_route.astype(BF16)
        w_route_lo = (w_route - w_route_hi.astype(F32)).astype(BF16)
        w_route = jnp.concatenate([w_route_hi, w_route_lo], axis=1)
        x1, h2, route = _merge_call(
            x2, (sh1, sc1, g1, sh2, sc2), n1, norm2_g[l][None], w_in_b[:, D_FOURIER + 2 * D_LRU:],
            fm, h, yr, w_fourier_out[l].astype(BF16), w_lru_out[l].astype(BF16), w_out[l].astype(BF16),
            w_route, b_route, batch, seq)

        eid = route[:, :TOP_K].astype(jnp.int32)
        blk_e, blk_off, blk_nv, order = _dispatch(eid, n)
        out2 = _moe_call(blk_e, blk_off, blk_nv, order, h2, w_gate_e[l], w_up_e[l], w_down_e[l])
        x2 = _final_call(x1, out2, route, g2, final_g[None], batch, seq)
    return x2.reshape(batch, seq, D_MODEL)
```

```python
import functools

import numpy as np
import jax
import jax.numpy as jnp
from jax import lax
from jax.experimental import pallas as pl
from jax.experimental.pallas import tpu as pltpu

F32 = jnp.float32
BF16 = jnp.bfloat16

LANES = 128
SUBLANES = 8
VMEM_LIMIT_BYTES = 56 * 1024 * 1024

D_MODEL = 1024
GRID_W = 64
EPS = 1e-6
N_FOURIER_GROUPS = 4
FOURIER_GROUP_DIM = 128
D_FOURIER = N_FOURIER_GROUPS * FOURIER_GROUP_DIM
N_LRU_HEADS = 8
LRU_HEAD_DIM = 64
D_LRU = N_LRU_HEADS * LRU_HEAD_DIM
CONV_WIDTH = 4
LRU_C = 8.0
N_GROUPS = 4
EXPERTS_PER_GROUP = 8
N_EXPERTS = N_GROUPS * EXPERTS_PER_GROUP
TOP_K = 2
D_EXPERT = 512

MOD_ROWS = 24
MOD_BLOCK_N = 1536
TOKEN_TILE = 512
SCAN_BATCH = SUBLANES
SCAN_CHUNK = 256
SCAN_PITCH = SCAN_CHUNK + SUBLANES
N_SLABS = D_LRU // LANES
FOURIER_TILE = 512
MOE_BLOCK = 256
ROUTE_LANES = LANES
N_DUMMY_ROWS = 4 * MOE_BLOCK


def _params(semantics):
    return pltpu.CompilerParams(dimension_semantics=semantics, vmem_limit_bytes=VMEM_LIMIT_BYTES)


TOKEN_ROWS = D_MODEL // LANES


def _store_token_tiled(ref, x):
    rows = x.shape[0]
    for s in range(TOKEN_ROWS):
        ref[pl.ds(s, rows, stride=TOKEN_ROWS), :] = x[:, s * LANES:(s + 1) * LANES]


def _load_token_tiled(ref, rows):
    return jnp.concatenate([ref[pl.ds(s, rows, stride=TOKEN_ROWS), :] for s in range(TOKEN_ROWS)], axis=1)


def _rms_modulate(x, g, shift, scale):
    y = x * lax.rsqrt(jnp.mean(x * x, axis=-1, keepdims=True) + EPS) * g
    return y * (1.0 + scale) + shift


def _mod_kernel(c_ref, w_ref, b_ref, o_ref):
    c = c_ref[...]
    s = c * jax.nn.sigmoid(c)
    o_ref[...] = jnp.dot(s, w_ref[...], preferred_element_type=F32,
                         precision=lax.Precision.HIGHEST) + b_ref[...]


def _mod_call(c_all, w_mod, b_mod):
    n_out = w_mod.shape[1]
    return pl.pallas_call(
        _mod_kernel,
        out_shape=jax.ShapeDtypeStruct((MOD_ROWS, n_out), F32),
        grid=(n_out // MOD_BLOCK_N,),
        in_specs=[
            pl.BlockSpec((MOD_ROWS, D_MODEL), lambda j: (0, 0)),
            pl.BlockSpec((D_MODEL, MOD_BLOCK_N), lambda j: (0, j)),
            pl.BlockSpec((1, MOD_BLOCK_N), lambda j: (0, j)),
        ],
        out_specs=pl.BlockSpec((MOD_ROWS, MOD_BLOCK_N), lambda j: (0, j)),
        compiler_params=_params(("arbitrary",)),
        name="mod",
    )(c_all, w_mod, b_mod)


def _inproj_kernel(x_ref, sh_ref, sc_ref, g_ref, w_ref, dft_ref, ab_ref, ur_ref, yr_ref):
    hx = _rms_modulate(x_ref[...], g_ref[...], sh_ref[0], sc_ref[0]).astype(BF16)
    proj = jnp.dot(hx, w_ref[...], preferred_element_type=F32)
    uf = proj[:, :D_FOURIER].astype(BF16)
    for g in range(N_FOURIER_GROUPS):
        lo, hi = g * FOURIER_GROUP_DIM, (g + 1) * FOURIER_GROUP_DIM
        cs = jnp.dot(uf[:, lo:hi], dft_ref[...], preferred_element_type=F32)
        ab_ref[0, 0, :, lo:hi] = cs[:, :FOURIER_GROUP_DIM].astype(BF16)
        ab_ref[0, 1, :, lo:hi] = cs[:, FOURIER_GROUP_DIM:].astype(BF16)
    ur_ref[...] = proj[:, D_FOURIER:D_FOURIER + D_LRU]
    yr_ref[...] = proj[:, D_FOURIER + D_LRU:].astype(BF16)


def _inproj_call(x2, sh, sc, g, w, dft_ch, batch, seq):
    n = batch * seq
    tpb = seq // TOKEN_TILE
    mod_spec = pl.BlockSpec((1, 1, D_MODEL), lambda i: (i // tpb, 0, 0))
    return pl.pallas_call(
        _inproj_kernel,
        out_shape=(
            jax.ShapeDtypeStruct((batch, 2, seq, D_FOURIER), BF16),
            jax.ShapeDtypeStruct((n, D_LRU), F32),
            jax.ShapeDtypeStruct((n, D_LRU), BF16),
        ),
        grid=(n // TOKEN_TILE,),
        in_specs=[
            pl.BlockSpec((TOKEN_TILE, D_MODEL), lambda i: (i, 0)),
            mod_spec, mod_spec,
            pl.BlockSpec((1, D_MODEL), lambda i: (0, 0)),
            pl.BlockSpec(w.shape, lambda i: (0, 0)),
            pl.BlockSpec(dft_ch.shape, lambda i: (0, 0)),
        ],
        out_specs=(
            pl.BlockSpec((1, 2, TOKEN_TILE, D_FOURIER), lambda i: (i // tpb, 0, i % tpb, 0)),
            pl.BlockSpec((TOKEN_TILE, D_LRU), lambda i: (i, 0)),
            pl.BlockSpec((TOKEN_TILE, D_LRU), lambda i: (i, 0)),
        ),
        compiler_params=_params(("parallel",)),
        name="inproj",
    )(x2, sh, sc, g, w, dft_ch)


def _inproj_ctx_kernel(x_ref, sh_ref, sc_ref, g_ref, w_ref, ur_ref):
    hx = _rms_modulate(x_ref[...], g_ref[...], sh_ref[...], sc_ref[...]).astype(BF16)
    ur_ref[...] = jnp.dot(hx, w_ref[...], preferred_element_type=F32)


def _inproj_ctx_call(ctx2, sh, sc, g, w):
    n = ctx2.shape[0]
    vec = pl.BlockSpec((1, D_MODEL), lambda i: (0, 0))
    return pl.pallas_call(
        _inproj_ctx_kernel,
        out_shape=jax.ShapeDtypeStruct((n, D_LRU), F32),
        grid=(n // TOKEN_TILE,),
        in_specs=[pl.BlockSpec((TOKEN_TILE, D_MODEL), lambda i: (i, 0)), vec, vec, vec,
                  pl.BlockSpec(w.shape, lambda i: (0, 0))],
        out_specs=pl.BlockSpec((TOKEN_TILE, D_LRU), lambda i: (i, 0)),
        compiler_params=_params(("parallel",)),
        name="inproj_ctx",
    )(ctx2, sh, sc, g, w)


def _rglru_kernel(n_chunks, u_ref, up_ref, un_ref, cw_ref, cb_ref, wg_ref, bg_ref, lam_ref, h0_ref,
                  h_ref, a_s, b_s, h_s, state):
    tc, pitch = SCAN_CHUNK, SCAN_PITCH
    d = pl.program_id(0)
    k = pl.program_id(2)
    kk = k + d * (n_chunks - 1 - 2 * k)

    @pl.when(k == 0)
    def _():
        state[...] = h0_ref[0]

    lam = lam_ref[0]
    neg_lam = -lam
    softplus = jnp.maximum(neg_lam, 0.0) + jnp.log1p(jnp.exp(-jnp.abs(neg_lam)))
    has_prev = kk > 0
    has_next = kk < n_chunks - 1
    row = lax.broadcasted_iota(jnp.int32, (tc, D_LRU), 0)
    cw = cw_ref[...]

    for i in range(SCAN_BATCH):
        u = u_ref[i]
        prev = jnp.where(has_prev, up_ref[i], 0.0)
        nxt = jnp.where(has_next, un_ref[i], 0.0)
        p2, p1, n1 = prev[SUBLANES - 2:SUBLANES - 1], prev[SUBLANES - 1:SUBLANES], nxt[0:1]
        um1 = jnp.where(row == 0, p1, pltpu.roll(u, 1, 0))
        um2 = jnp.where(row == 0, p2, jnp.where(row == 1, p1, pltpu.roll(u, 2, 0)))
        up1 = jnp.where(row == tc - 1, n1, pltpu.roll(u, tc - 1, 0))
        xc = um2 * cw[0:1] + um1 * cw[1:2] + u * cw[2:3] + up1 * cw[3:4] + cb_ref[...]
        gz = jnp.dot(xc.astype(BF16), wg_ref[0], preferred_element_type=F32) + bg_ref[0]
        r = jax.nn.sigmoid(gz[:, :D_LRU])
        ig = jax.nn.sigmoid(gz[:, D_LRU:])
        log_a = -LRU_C * r * softplus
        a = jnp.exp(log_a)
        b = jnp.sqrt((1.0 - a) * (1.0 + a)) * (ig * xc)
        for j in range(N_SLABS):
            a_s[j, pl.ds(i * pitch, tc), :] = a[:, j * LANES:(j + 1) * LANES]
            b_s[j, pl.ds(i * pitch, tc), :] = b[:, j * LANES:(j + 1) * LANES]

    def step(s, h):
        t = s + d * (tc - 1 - 2 * s)
        out = []
        for j in range(N_SLABS):
            rows = pl.ds(t, SCAN_BATCH, stride=pitch)
            hj = a_s[j, rows, :] * h[j] + b_s[j, rows, :]
            h_s[j, rows, :] = hj
            out.append(hj)
        return tuple(out)

    st = state[...]
    h = lax.fori_loop(0, tc, step, tuple(st[:, j * LANES:(j + 1) * LANES] for j in range(N_SLABS)),
                      unroll=4)
    for j in range(N_SLABS):
        state[:, j * LANES:(j + 1) * LANES] = h[j]
    for i in range(SCAN_BATCH):
        for j in range(N_SLABS):
            h_ref[0, i, :, j * LANES:(j + 1) * LANES] = h_s[j, pl.ds(i * pitch, tc), :]


def _rglru_call(u3, conv_w, conv_b, wg, bg, lam, h0):
    batch, seq, _ = u3.shape
    n_chunks = seq // SCAN_CHUNK
    halo_blocks = SCAN_CHUNK // SUBLANES
    last_halo = seq // SUBLANES - 1

    def chunk(d, k):
        return k + d * (n_chunks - 1 - 2 * k)

    scratch = pltpu.VMEM((N_SLABS, SCAN_BATCH * SCAN_PITCH, LANES), F32)
    return pl.pallas_call(
        functools.partial(_rglru_kernel, n_chunks),
        out_shape=jax.ShapeDtypeStruct((2, batch, seq, D_LRU), F32),
        grid=(2, batch // SCAN_BATCH, n_chunks),
        in_specs=[
            pl.BlockSpec((SCAN_BATCH, SCAN_CHUNK, D_LRU), lambda d, g, k: (g, chunk(d, k), 0)),
            pl.BlockSpec((SCAN_BATCH, SUBLANES, D_LRU),
                         lambda d, g, k: (g, jnp.maximum(chunk(d, k) * halo_blocks - 1, 0), 0)),
            pl.BlockSpec((SCAN_BATCH, SUBLANES, D_LRU),
                         lambda d, g, k: (g, jnp.minimum((chunk(d, k) + 1) * halo_blocks, last_halo), 0)),
            pl.BlockSpec((CONV_WIDTH, D_LRU), lambda d, g, k: (0, 0)),
            pl.BlockSpec((1, D_LRU), lambda d, g, k: (0, 0)),
            pl.BlockSpec((1, D_LRU, 2 * D_LRU), lambda d, g, k: (d, 0, 0)),
            pl.BlockSpec((1, 1, 2 * D_LRU), lambda d, g, k: (d, 0, 0)),
            pl.BlockSpec((1, 1, D_LRU), lambda d, g, k: (d, 0, 0)),
            pl.BlockSpec((1, SCAN_BATCH, D_LRU), lambda d, g, k: (d, g, 0)),
        ],
        out_specs=pl.BlockSpec((1, SCAN_BATCH, SCAN_CHUNK, D_LRU), lambda d, g, k: (d, g, chunk(d, k), 0)),
        scratch_shapes=[scratch, scratch, scratch, pltpu.VMEM((SCAN_BATCH, D_LRU), F32)],
        compiler_params=_params(("arbitrary", "arbitrary", "arbitrary")),
        name="rglru",
    )(u3, u3, u3, conv_w, conv_b, wg, bg, lam, h0)


def _fourier_kernel(m_ref, ab_ref, o_ref):
    seq2 = m_ref.shape[1]
    rhs = ab_ref[0].reshape(seq2, D_FOURIER)
    o_ref[0] = jnp.dot(m_ref[...], rhs, preferred_element_type=F32).astype(BF16)


def _fourier_call(m_pos, ab):
    batch, _, seq, _ = ab.shape
    return pl.pallas_call(
        _fourier_kernel,
        out_shape=jax.ShapeDtypeStruct((batch, seq, D_FOURIER), BF16),
        grid=(batch, seq // FOURIER_TILE),
        in_specs=[
            pl.BlockSpec((FOURIER_TILE, 2 * seq), lambda b, m: (m, 0)),
            pl.BlockSpec((1, 2, seq, D_FOURIER), lambda b, m: (b, 0, 0, 0)),
        ],
        out_specs=pl.BlockSpec((1, FOURIER_TILE, D_FOURIER), lambda b, m: (b, m, 0)),
        compiler_params=_params(("parallel", "parallel")),
        name="fourier",
    )(m_pos, ab)


def _route(logits):
    lane = lax.broadcasted_iota(jnp.int32, logits.shape, 1)
    neg = -jnp.inf
    gl = jnp.where(lane < N_GROUPS, logits, neg)
    gmax = jnp.max(gl, axis=1, keepdims=True)
    grp = jnp.min(jnp.where(gl == gmax, lane, ROUTE_LANES), axis=1, keepdims=True)
    p_grp = 1.0 / jnp.sum(jnp.exp(gl - gmax), axis=1, keepdims=True)
    e_lane = lane - N_GROUPS
    in_grp = (e_lane >= 0) & (e_lane < N_EXPERTS) & ((e_lane // EXPERTS_PER_GROUP) == grp)
    el = jnp.where(in_grp, logits, neg)
    t1 = jnp.max(el, axis=1, keepdims=True)
    i1 = jnp.min(jnp.where(el == t1, lane, ROUTE_LANES), axis=1, keepdims=True)
    el2 = jnp.where(lane == i1, neg, el)
    t2 = jnp.max(el2, axis=1, keepdims=True)
    i2 = jnp.min(jnp.where(el2 == t2, lane, ROUTE_LANES), axis=1, keepdims=True)
    e = jnp.exp(t2 - t1)
    w1 = p_grp / (1.0 + e)
    w2 = p_grp * e / (1.0 + e)
    out = jnp.where(lane == 0, (i1 - N_GROUPS).astype(F32), 0.0)
    out = jnp.where(lane == 1, (i2 - N_GROUPS).astype(F32), out)
    out = jnp.where(lane == 2, w1, out)
    return jnp.where(lane == 3, w2, out)


def _merge_kernel(x_ref, sh1_ref, sc1_ref, g1_ref, sh2_ref, sc2_ref, n1_ref, n2_ref, wgate_ref,
                  fm_ref, hf_ref, hb_ref, yr_ref, wfo_ref, wlo_ref, wout_ref, wr_ref, br_ref,
                  x1_ref, h2_ref, route_ref):
    x = x_ref[...]
    hx = _rms_modulate(x, n1_ref[...], sh1_ref[0], sc1_ref[0]).astype(BF16)
    gates = jax.nn.sigmoid(jnp.dot(hx, wgate_ref[...], preferred_element_type=F32))
    branch_f = jnp.dot(fm_ref[0], wfo_ref[...], preferred_element_type=F32)
    y = yr_ref[...].astype(F32)
    gelu = 0.5 * y * (1.0 + jnp.tanh(np.sqrt(2.0 / np.pi).astype(np.float32) * (y + 0.044715 * (y * y * y))))
    lr = ((hf_ref[0, 0] + hb_ref[0, 0]) * gelu).astype(BF16)
    branch_r = jnp.dot(lr, wlo_ref[...], preferred_element_type=F32)
    mixed = gates[:, :D_MODEL] * branch_f + gates[:, D_MODEL:] * branch_r
    mix = jnp.dot(mixed.astype(BF16), wout_ref[...], preferred_element_type=F32)
    x1 = x + g1_ref[0] * mix
    x1_ref[...] = x1
    h2 = _rms_modulate(x1, n2_ref[...], sh2_ref[0], sc2_ref[0])
    _store_token_tiled(h2_ref, h2)
    h2_hi = h2.astype(BF16)
    h2_lo = (h2 - h2_hi.astype(F32)).astype(BF16)
    parts = (jnp.dot(h2_hi, wr_ref[...], preferred_element_type=F32)
             + jnp.dot(h2_lo, wr_ref[...], preferred_element_type=F32))
    route_ref[...] = _route(parts[:, :ROUTE_LANES] + parts[:, ROUTE_LANES:] + br_ref[...])


def _merge_call(x2, mods, n1, n2, wgate, fm, h, yr, wfo, wlo, wout, wr, br, batch, seq):
    n = batch * seq
    tm = TOKEN_TILE
    tpb = seq // tm
    mod_spec = pl.BlockSpec((1, 1, D_MODEL), lambda i: (i // tpb, 0, 0))
    vec = pl.BlockSpec((1, D_MODEL), lambda i: (0, 0))
    tile = pl.BlockSpec((tm, D_MODEL), lambda i: (i, 0))

    def full(a):
        return pl.BlockSpec(a.shape, lambda i: (0,) * a.ndim)

    return pl.pallas_call(
        _merge_kernel,
        out_shape=(
            jax.ShapeDtypeStruct((n, D_MODEL), F32),
            jax.ShapeDtypeStruct((n * TOKEN_ROWS, LANES), F32),
            jax.ShapeDtypeStruct((n, ROUTE_LANES), F32),
        ),
        grid=(n // tm,),
        in_specs=[
            tile, mod_spec, mod_spec, mod_spec, mod_spec, mod_spec, vec, vec, full(wgate),
            pl.BlockSpec((1, tm, D_FOURIER), lambda i: (i // tpb, i % tpb, 0)),
            pl.BlockSpec((1, 1, tm, D_LRU), lambda i: (0, i // tpb, i % tpb, 0)),
            pl.BlockSpec((1, 1, tm, D_LRU), lambda i: (1, i // tpb, i % tpb, 0)),
            pl.BlockSpec((tm, D_LRU), lambda i: (i, 0)),
            full(wfo), full(wlo), full(wout), full(wr), full(br),
        ],
        out_specs=(tile, pl.BlockSpec((tm * TOKEN_ROWS, LANES), lambda i: (i, 0)),
                   pl.BlockSpec((tm, ROUTE_LANES), lambda i: (i, 0))),
        compiler_params=_params(("parallel",)),
        name="merge",
    )(x2, *mods, n1, n2, wgate, fm, h, h, yr, wfo, wlo, wout, wr, br)


def _token_rows(t, count=1):
    start = t * TOKEN_ROWS
    if not isinstance(start, int):
        start = pl.multiple_of(start, TOKEN_ROWS)
    return pl.ds(start, count * TOKEN_ROWS)


def _row_copy(src_ref, dst_ref, sem, src_tok, dst_tok):
    return pltpu.make_async_copy(src_ref.at[_token_rows(src_tok)], dst_ref.at[_token_rows(dst_tok)], sem)


def _moe_kernel(n_tokens, be_ref, off_ref, nv_ref, order_ref, h2_hbm, wg_ref, wu_ref, wd_ref, out_hbm,
                xbuf, ybuf, wg_b, wu_b, wd_b, gat_sem, sct_sem):
    i = pl.program_id(0)
    n_blk = pl.num_programs(0)
    n_assign = n_tokens * TOP_K
    slot = i % 2

    def block_copy_for_wait(sem, to_hbm):
        if to_hbm:
            return pltpu.make_async_copy(ybuf.at[0], out_hbm.at[_token_rows(0, MOE_BLOCK)], sem)
        return pltpu.make_async_copy(h2_hbm.at[_token_rows(0, MOE_BLOCK)], xbuf.at[0], sem)

    def issue_gather(b, s):
        off, nv = off_ref[b], nv_ref[b]

        def body(r, c):
            a = order_ref[jnp.minimum(off + r, n_assign - 1)]
            tok = jnp.where(r < nv, lax.shift_right_logical(a, 1), 0)
            _row_copy(h2_hbm, xbuf.at[s], gat_sem.at[s], tok, r).start()
            return c
        lax.fori_loop(0, MOE_BLOCK, body, 0, unroll=8)

    def issue_scatter(b, s):
        off, nv = off_ref[b], nv_ref[b]
        dummy0 = n_assign + (b % (N_DUMMY_ROWS // MOE_BLOCK)) * MOE_BLOCK

        def body(r, c):
            a = order_ref[jnp.minimum(off + r, n_assign - 1)]
            dst = (a & 1) * n_tokens + lax.shift_right_logical(a, 1)
            _row_copy(ybuf.at[s], out_hbm, sct_sem.at[s], r, jnp.where(r < nv, dst, dummy0 + r)).start()
            return c
        lax.fori_loop(0, MOE_BLOCK, body, 0, unroll=8)

    @pl.when(i == 0)
    def _():
        ybuf[0] = jnp.zeros(ybuf.shape[1:], F32)
        fills = [pltpu.make_async_copy(ybuf.at[0], out_hbm.at[_token_rows(n_assign + q * MOE_BLOCK, MOE_BLOCK)],
                                       sct_sem.at[0]) for q in range(N_DUMMY_ROWS // MOE_BLOCK)]
        for cp in fills:
            cp.start()
        for cp in fills:
            cp.wait()

        @pl.when(nv_ref[0] > 0)
        def _():
            issue_gather(0, 0)

    nxt = jnp.minimum(i + 1, n_blk - 1)

    @pl.when((i + 1 < n_blk) & (nv_ref[nxt] > 0))
    def _():
        issue_gather(nxt, 1 - slot)

    @pl.when((i == 0) | (be_ref[i] != be_ref[jnp.maximum(i - 1, 0)]))
    def _():
        wg_b[...] = wg_ref[0].astype(BF16)
        wu_b[...] = wu_ref[0].astype(BF16)
        wd_b[...] = wd_ref[0].astype(BF16)

    @pl.when((i >= 2) & (nv_ref[jnp.maximum(i - 2, 0)] > 0))
    def _():
        block_copy_for_wait(sct_sem.at[slot], True).wait()

    @pl.when(nv_ref[i] > 0)
    def _():
        block_copy_for_wait(gat_sem.at[slot], False).wait()
        xb = _load_token_tiled(xbuf.at[slot], MOE_BLOCK).astype(BF16)
        hg = jnp.dot(xb, wg_b[...], preferred_element_type=F32)
        hu = jnp.dot(xb, wu_b[...], preferred_element_type=F32)
        hb = (hg * jax.nn.sigmoid(hg) * hu).astype(BF16)
        _store_token_tiled(ybuf.at[slot], jnp.dot(hb, wd_b[...], preferred_element_type=F32))
        issue_scatter(i, slot)

    @pl.when(i == n_blk - 1)
    def _():
        @pl.when((i >= 1) & (nv_ref[jnp.maximum(i - 1, 0)] > 0))
        def _():
            block_copy_for_wait(sct_sem.at[1 - slot], True).wait()

        @pl.when(nv_ref[i] > 0)
        def _():
            block_copy_for_wait(sct_sem.at[slot], True).wait()


def _moe_call(blk_e, blk_off, blk_nv, order, h2, w_g, w_u, w_d):
    n_blk = blk_e.shape[0]
    n_tokens = h2.shape[0] // TOKEN_ROWS
    any_spec = pl.BlockSpec(memory_space=pl.ANY)

    def expert(i, be, off, nv, order):
        return (be[i], 0, 0)

    grid_spec = pltpu.PrefetchScalarGridSpec(
        num_scalar_prefetch=4,
        grid=(n_blk,),
        in_specs=[
            any_spec,
            pl.BlockSpec((1, D_MODEL, D_EXPERT), expert),
            pl.BlockSpec((1, D_MODEL, D_EXPERT), expert),
            pl.BlockSpec((1, D_EXPERT, D_MODEL), expert),
        ],
        out_specs=any_spec,
        scratch_shapes=[
            pltpu.VMEM((2, MOE_BLOCK * TOKEN_ROWS, LANES), F32),
            pltpu.VMEM((2, MOE_BLOCK * TOKEN_ROWS, LANES), F32),
            pltpu.VMEM((D_MODEL, D_EXPERT), BF16),
            pltpu.VMEM((D_MODEL, D_EXPERT), BF16),
            pltpu.VMEM((D_EXPERT, D_MODEL), BF16),
            pltpu.SemaphoreType.DMA((2,)),
            pltpu.SemaphoreType.DMA((2,)),
        ],
    )
    return pl.pallas_call(
        functools.partial(_moe_kernel, n_tokens),
        out_shape=jax.ShapeDtypeStruct(((n_tokens * TOP_K + N_DUMMY_ROWS) * TOKEN_ROWS, LANES), F32),
        grid_spec=grid_spec,
        compiler_params=_params(("arbitrary",)),
        name="moe",
    )(blk_e, blk_off, blk_nv, order, h2, w_g, w_u, w_d)


def _final_kernel(x1_ref, y0_ref, y1_ref, route_ref, g2_ref, fg_ref, o_ref):
    route = route_ref[...]
    rows = route.shape[0]
    moe = route[:, 2:3] * _load_token_tiled(y0_ref, rows) + route[:, 3:4] * _load_token_tiled(y1_ref, rows)
    x2 = x1_ref[...] + g2_ref[0] * moe
    o_ref[...] = x2 * lax.rsqrt(jnp.mean(x2 * x2, axis=-1, keepdims=True) + EPS) * fg_ref[...]


def _final_call(x1, out2, route, g2, fg, batch, seq):
    n = batch * seq
    tm = TOKEN_TILE
    tpb = seq // tm
    tile = pl.BlockSpec((tm, D_MODEL), lambda i: (i, 0))
    return pl.pallas_call(
        _final_kernel,
        out_shape=jax.ShapeDtypeStruct((n, D_MODEL), F32),
        grid=(n // tm,),
        in_specs=[
            tile,
            pl.BlockSpec((tm * TOKEN_ROWS, LANES), lambda i: (i, 0)),
            pl.BlockSpec((tm * TOKEN_ROWS, LANES), lambda i: (i + n // tm, 0)),
            pl.BlockSpec((tm, ROUTE_LANES), lambda i: (i, 0)),
            pl.BlockSpec((1, 1, D_MODEL), lambda i: (i // tpb, 0, 0)),
            pl.BlockSpec((1, D_MODEL), lambda i: (0, 0)),
        ],
        out_specs=tile,
        compiler_params=_params(("parallel",)),
        name="final",
    )(x1, out2, out2, route, g2, fg)


def _channel_dft():
    j = np.arange(FOURIER_GROUP_DIM)
    ang = 2.0 * np.pi * np.outer(j, j) / FOURIER_GROUP_DIM
    return np.concatenate([np.cos(ang), np.sin(ang)], axis=1).astype(np.float32)


def _position_dft(seq):
    rows = seq // GRID_W
    assert GRID_W % rows == 0
    r, c = np.divmod(np.arange(seq), GRID_W)
    phase = (np.outer(r, r) * (GRID_W // rows) + np.outer(c, c)) % GRID_W
    ang = 2.0 * np.pi * phase / GRID_W
    scale = 1.0 / np.sqrt(float(seq) * FOURIER_GROUP_DIM)
    return np.concatenate([np.cos(ang), -np.sin(ang)], axis=1) * scale


def _block_diag(w):
    heads, hd, _ = w.shape
    eye = jnp.eye(heads, dtype=w.dtype)
    return jnp.einsum('hij,hg->higj', w, eye).reshape(heads * hd, heads * hd)


def _gate_weights(w_a, w_x):
    return jnp.stack([jnp.concatenate([_block_diag(w_a[d]), _block_diag(w_x[d])], axis=1)
                      for d in range(2)]).astype(BF16)


def _dispatch(eid, n_tokens):
    n_assign = n_tokens * TOP_K
    n_blk = n_assign // MOE_BLOCK + N_EXPERTS
    flat_e = eid.reshape(-1)
    _, order = lax.sort((flat_e, jnp.arange(n_assign, dtype=jnp.int32)), num_keys=1)
    experts = jnp.arange(N_EXPERTS, dtype=jnp.int32)
    counts = jnp.sum((flat_e[:, None] == experts[None, :]).astype(jnp.int32), axis=0)
    start = jnp.cumsum(counts) - counts
    blocks = (counts + MOE_BLOCK - 1) // MOE_BLOCK
    bend = jnp.cumsum(blocks)
    b = jnp.arange(n_blk, dtype=jnp.int32)
    owner = (b[:, None] >= bend[None, :]).astype(jnp.int32)
    blk_e = jnp.minimum(jnp.sum(owner, axis=1), N_EXPERTS - 1)
    onehot = (blk_e[:, None] == experts[None, :]).astype(jnp.int32)
    first = jnp.sum(onehot * (bend - blocks)[None, :], axis=1)
    in_expert = (b - first) * MOE_BLOCK
    blk_off = jnp.sum(onehot * start[None, :], axis=1) + in_expert
    blk_nv = jnp.clip(jnp.sum(onehot * counts[None, :], axis=1) - in_expert, 0, MOE_BLOCK)
    return blk_e, blk_off, blk_nv, order


def kernel(x, c, ctx, c_ctx, w_mod, b_mod, norm1_g, w_in, conv_w, conv_b, lru_wa, lru_ba, lru_wx, lru_bx,
           lru_lam, w_fourier_out, w_lru_out, w_out, norm2_g, w_group, b_group, w_expert_router,
           b_expert_router, w_gate_e, w_up_e, w_down_e, final_g):
    batch, seq, _ = x.shape
    ctx_len = ctx.shape[1]
    n = batch * seq
    assert w_mod.shape[0] == 1, "single-layer stack only: the context stream is not carried across layers"
    x2 = x.reshape(n, D_MODEL)
    dft_ch = jnp.asarray(_channel_dft().astype(BF16))
    m_pos = jnp.asarray(_position_dft(seq).astype(BF16))

    for l in range(1):
        c_all = jnp.concatenate([c, c_ctx[None], jnp.zeros((MOD_ROWS - batch - 1, D_MODEL), F32)], axis=0)
        mod = _mod_call(c_all, w_mod[l], b_mod[l][None])
        sh1, sc1, g1, sh2, sc2, g2 = [m[:batch, None, :] for m in jnp.split(mod, 6, axis=-1)]
        csh1, csc1 = mod[batch:batch + 1, :D_MODEL], mod[batch:batch + 1, D_MODEL:2 * D_MODEL]

        w_in_b = w_in[l].astype(BF16)
        n1 = norm1_g[l][None]
        wg = _gate_weights(lru_wa[l], lru_wx[l])
        bg = jnp.concatenate([lru_ba[l], lru_bx[l]], axis=-1)[:, None, :]
        lam = lru_lam[l][:, None, :]
        cb = conv_b[l][None]

        uc = _inproj_ctx_call(ctx.reshape(batch * ctx_len, D_MODEL), csh1, csc1, n1,
                              w_in_b[:, D_FOURIER:D_FOURIER + D_LRU])
        hc = _rglru_call(uc.reshape(batch, ctx_len, D_LRU), conv_w[l], cb, wg, bg, lam,
                         jnp.zeros((2, batch, D_LRU), F32))
        h0 = jnp.stack([hc[0, :, -1], hc[1, :, 0]])

        ab, ur, yr = _inproj_call(x2, sh1, sc1, n1, w_in_b[:, :D_FOURIER + 2 * D_LRU], dft_ch, batch, seq)
        h = _rglru_call(ur.reshape(batch, seq, D_LRU), conv_w[l], cb, wg, bg, lam, h0)
        fm = _fourier_call(m_pos, ab)

        w_route = jnp.zeros((D_MODEL, ROUTE_LANES), F32)
        w_route = w_route.at[:, :N_GROUPS].set(w_group[l]).at[:, N_GROUPS:N_GROUPS + N_EXPERTS].set(
            w_expert_router[l])
        b_route = jnp.zeros((1, ROUTE_LANES), F32)
        b_route = b_route.at[0, :N_GROUPS].set(b_group[l]).at[0, N_GROUPS:N_GROUPS + N_EXPERTS].set(
            b_expert_router[l])
        w_route_hi = w_route.astype(BF16)
        w_route_lo = (w_route - w_route_hi.astype(F32)).astype(BF16)
        w_route = jnp.concatenate([w_route_hi, w_route_lo], axis=1)
        x1, h2, route = _merge_call(
            x2, (sh1, sc1, g1, sh2, sc2), n1, norm2_g[l][None], w_in_b[:, D_FOURIER + 2 * D_LRU:],
            fm, h, yr, w_fourier_out[l].astype(BF16), w_lru_out[l].astype(BF16), w_out[l].astype(BF16),
            w_route, b_route, batch, seq)

        eid = route[:, :TOP_K].astype(jnp.int32)
        blk_e, blk_off, blk_nv, order = _dispatch(eid, n)
        out2 = _moe_call(blk_e, blk_off, blk_nv, order, h2, w_gate_e[l], w_up_e[l], w_down_e[l])
        x2 = _final_call(x1, out2, route, g2, final_g[None], batch, seq)
    return x2.reshape(batch, seq, D_MODEL)
```

```python
import functools

import numpy as np
import jax
import jax.numpy as jnp
from jax import lax
from jax.experimental import pallas as pl
from jax.experimental.pallas import tpu as pltpu

F32 = jnp.float32
BF16 = jnp.bfloat16

LANES = 128
SUBLANES = 8
VMEM_LIMIT_BYTES = 56 * 1024 * 1024

D_MODEL = 1024
GRID_W = 64
EPS = 1e-6
N_FOURIER_GROUPS = 4
FOURIER_GROUP_DIM = 128
D_FOURIER = N_FOURIER_GROUPS * FOURIER_GROUP_DIM
N_LRU_HEADS = 8
LRU_HEAD_DIM = 64
D_LRU = N_LRU_HEADS * LRU_HEAD_DIM
CONV_WIDTH = 4
LRU_C = 8.0
N_GROUPS = 4
EXPERTS_PER_GROUP = 8
N_EXPERTS = N_GROUPS * EXPERTS_PER_GROUP
TOP_K = 2
D_EXPERT = 512

MOD_ROWS = 24
MOD_BLOCK_N = 1536
TOKEN_TILE = 512
SCAN_BATCH = SUBLANES
SCAN_CHUNK = 256
SCAN_PITCH = SCAN_CHUNK + SUBLANES
N_SLABS = D_LRU // LANES
FOURIER_TILE = 512
MOE_BLOCK = 256
ROUTE_LANES = LANES
N_DUMMY_ROWS = 4 * MOE_BLOCK
SRC_BITS = 15
SRC_MASK = (1 << SRC_BITS) - 1


def _params(semantics):
    return pltpu.CompilerParams(dimension_semantics=semantics, vmem_limit_bytes=VMEM_LIMIT_BYTES)


TOKEN_ROWS = D_MODEL // LANES


def _store_token_tiled(ref, x):
    rows = x.shape[0]
    for s in range(TOKEN_ROWS):
        ref[pl.ds(s, rows, stride=TOKEN_ROWS), :] = x[:, s * LANES:(s + 1) * LANES]


def _load_token_tiled(ref, rows):
    return jnp.concatenate([ref[pl.ds(s, rows, stride=TOKEN_ROWS), :] for s in range(TOKEN_ROWS)], axis=1)


def _rms_modulate(x, g, shift, scale):
    y = x * lax.rsqrt(jnp.mean(x * x, axis=-1, keepdims=True) + EPS) * g
    return y * (1.0 + scale) + shift


def _mod_kernel(c_ref, w_ref, b_ref, o_ref):
    c = c_ref[...]
    s = c * jax.nn.sigmoid(c)
    o_ref[...] = jnp.dot(s, w_ref[...], preferred_element_type=F32,
                         precision=lax.Precision.HIGHEST) + b_ref[...]


def _mod_call(c_all, w_mod, b_mod):
    n_out = w_mod.shape[1]
    return pl.pallas_call(
        _mod_kernel,
        out_shape=jax.ShapeDtypeStruct((MOD_ROWS, n_out), F32),
        grid=(n_out // MOD_BLOCK_N,),
        in_specs=[
            pl.BlockSpec((MOD_ROWS, D_MODEL), lambda j: (0, 0)),
            pl.BlockSpec((D_MODEL, MOD_BLOCK_N), lambda j: (0, j)),
            pl.BlockSpec((1, MOD_BLOCK_N), lambda j: (0, j)),
        ],
        out_specs=pl.BlockSpec((MOD_ROWS, MOD_BLOCK_N), lambda j: (0, j)),
        compiler_params=_params(("arbitrary",)),
        name="mod",
    )(c_all, w_mod, b_mod)


def _inproj_kernel(x_ref, sh_ref, sc_ref, g_ref, w_ref, dft_ref, ab_ref, ur_ref, yr_ref):
    hx = _rms_modulate(x_ref[...], g_ref[...], sh_ref[0], sc_ref[0]).astype(BF16)
    proj = jnp.dot(hx, w_ref[...], preferred_element_type=F32)
    uf = proj[:, :D_FOURIER].astype(BF16)
    for g in range(N_FOURIER_GROUPS):
        lo, hi = g * FOURIER_GROUP_DIM, (g + 1) * FOURIER_GROUP_DIM
        cs = jnp.dot(uf[:, lo:hi], dft_ref[...], preferred_element_type=F32)
        ab_ref[0, 0, :, lo:hi] = cs[:, :FOURIER_GROUP_DIM].astype(BF16)
        ab_ref[0, 1, :, lo:hi] = cs[:, FOURIER_GROUP_DIM:].astype(BF16)
    ur_ref[...] = proj[:, D_FOURIER:D_FOURIER + D_LRU]
    yr_ref[...] = proj[:, D_FOURIER + D_LRU:].astype(BF16)


def _inproj_call(x2, sh, sc, g, w, dft_ch, batch, seq):
    n = batch * seq
    tpb = seq // TOKEN_TILE
    mod_spec = pl.BlockSpec((1, 1, D_MODEL), lambda i: (i // tpb, 0, 0))
    return pl.pallas_call(
        _inproj_kernel,
        out_shape=(
            jax.ShapeDtypeStruct((batch, 2, seq, D_FOURIER), BF16),
            jax.ShapeDtypeStruct((n, D_LRU), F32),
            jax.ShapeDtypeStruct((n, D_LRU), BF16),
        ),
        grid=(n // TOKEN_TILE,),
        in_specs=[
            pl.BlockSpec((TOKEN_TILE, D_MODEL), lambda i: (i, 0)),
            mod_spec, mod_spec,
            pl.BlockSpec((1, D_MODEL), lambda i: (0, 0)),
            pl.BlockSpec(w.shape, lambda i: (0, 0)),
            pl.BlockSpec(dft_ch.shape, lambda i: (0, 0)),
        ],
        out_specs=(
            pl.BlockSpec((1, 2, TOKEN_TILE, D_FOURIER), lambda i: (i // tpb, 0, i % tpb, 0)),
            pl.BlockSpec((TOKEN_TILE, D_LRU), lambda i: (i, 0)),
            pl.BlockSpec((TOKEN_TILE, D_LRU), lambda i: (i, 0)),
        ),
        compiler_params=_params(("parallel",)),
        name="inproj",
    )(x2, sh, sc, g, w, dft_ch)


def _inproj_ctx_kernel(x_ref, sh_ref, sc_ref, g_ref, w_ref, ur_ref):
    hx = _rms_modulate(x_ref[...], g_ref[...], sh_ref[...], sc_ref[...]).astype(BF16)
    ur_ref[...] = jnp.dot(hx, w_ref[...], preferred_element_type=F32)


def _inproj_ctx_call(ctx2, sh, sc, g, w):
    n = ctx2.shape[0]
    vec = pl.BlockSpec((1, D_MODEL), lambda i: (0, 0))
    return pl.pallas_call(
        _inproj_ctx_kernel,
        out_shape=jax.ShapeDtypeStruct((n, D_LRU), F32),
        grid=(n // TOKEN_TILE,),
        in_specs=[pl.BlockSpec((TOKEN_TILE, D_MODEL), lambda i: (i, 0)), vec, vec, vec,
                  pl.BlockSpec(w.shape, lambda i: (0, 0))],
        out_specs=pl.BlockSpec((TOKEN_TILE, D_LRU), lambda i: (i, 0)),
        compiler_params=_params(("parallel",)),
        name="inproj_ctx",
    )(ctx2, sh, sc, g, w)


def _rglru_kernel(n_chunks, u_ref, up_ref, un_ref, cw_ref, cb_ref, wg_ref, bg_ref, lam_ref, h0_ref,
                  h_ref, a_s, b_s, h_s, state):
    tc, pitch = SCAN_CHUNK, SCAN_PITCH
    d = pl.program_id(0)
    k = pl.program_id(2)
    kk = k + d * (n_chunks - 1 - 2 * k)

    @pl.when(k == 0)
    def _():
        state[...] = h0_ref[0]

    lam = lam_ref[0]
    neg_lam = -lam
    softplus = jnp.maximum(neg_lam, 0.0) + jnp.log1p(jnp.exp(-jnp.abs(neg_lam)))
    has_prev = kk > 0
    has_next = kk < n_chunks - 1
    row = lax.broadcasted_iota(jnp.int32, (tc, D_LRU), 0)
    cw = cw_ref[...]

    for i in range(SCAN_BATCH):
        u = u_ref[i]
        prev = jnp.where(has_prev, up_ref[i], 0.0)
        nxt = jnp.where(has_next, un_ref[i], 0.0)
        p2, p1, n1 = prev[SUBLANES - 2:SUBLANES - 1], prev[SUBLANES - 1:SUBLANES], nxt[0:1]
        um1 = jnp.where(row == 0, p1, pltpu.roll(u, 1, 0))
        um2 = jnp.where(row == 0, p2, jnp.where(row == 1, p1, pltpu.roll(u, 2, 0)))
        up1 = jnp.where(row == tc - 1, n1, pltpu.roll(u, tc - 1, 0))
        xc = um2 * cw[0:1] + um1 * cw[1:2] + u * cw[2:3] + up1 * cw[3:4] + cb_ref[...]
        gz = jnp.dot(xc.astype(BF16), wg_ref[0], preferred_element_type=F32) + bg_ref[0]
        r = jax.nn.sigmoid(gz[:, :D_LRU])
        ig = jax.nn.sigmoid(gz[:, D_LRU:])
        log_a = -LRU_C * r * softplus
        a = jnp.exp(log_a)
        b = jnp.sqrt((1.0 - a) * (1.0 + a)) * (ig * xc)
        for j in range(N_SLABS):
            a_s[j, pl.ds(i * pitch, tc), :] = a[:, j * LANES:(j + 1) * LANES]
            b_s[j, pl.ds(i * pitch, tc), :] = b[:, j * LANES:(j + 1) * LANES]

    def step(s, h):
        t = s + d * (tc - 1 - 2 * s)
        out = []
        for j in range(N_SLABS):
            rows = pl.ds(t, SCAN_BATCH, stride=pitch)
            hj = a_s[j, rows, :] * h[j] + b_s[j, rows, :]
            h_s[j, rows, :] = hj
            out.append(hj)
        return tuple(out)

    st = state[...]
    h = lax.fori_loop(0, tc, step, tuple(st[:, j * LANES:(j + 1) * LANES] for j in range(N_SLABS)),
                      unroll=4)
    for j in range(N_SLABS):
        state[:, j * LANES:(j + 1) * LANES] = h[j]
    for i in range(SCAN_BATCH):
        for j in range(N_SLABS):
            h_ref[0, i, :, j * LANES:(j + 1) * LANES] = h_s[j, pl.ds(i * pitch, tc), :]


def _rglru_call(u3, conv_w, conv_b, wg, bg, lam, h0):
    batch, seq, _ = u3.shape
    n_chunks = seq // SCAN_CHUNK
    halo_blocks = SCAN_CHUNK // SUBLANES
    last_halo = seq // SUBLANES - 1

    def chunk(d, k):
        return k + d * (n_chunks - 1 - 2 * k)

    scratch = pltpu.VMEM((N_SLABS, SCAN_BATCH * SCAN_PITCH, LANES), F32)
    return pl.pallas_call(
        functools.partial(_rglru_kernel, n_chunks),
        out_shape=jax.ShapeDtypeStruct((2, batch, seq, D_LRU), F32),
        grid=(2, batch // SCAN_BATCH, n_chunks),
        in_specs=[
            pl.BlockSpec((SCAN_BATCH, SCAN_CHUNK, D_LRU), lambda d, g, k: (g, chunk(d, k), 0)),
            pl.BlockSpec((SCAN_BATCH, SUBLANES, D_LRU),
                         lambda d, g, k: (g, jnp.maximum(chunk(d, k) * halo_blocks - 1, 0), 0)),
            pl.BlockSpec((SCAN_BATCH, SUBLANES, D_LRU),
                         lambda d, g, k: (g, jnp.minimum((chunk(d, k) + 1) * halo_blocks, last_halo), 0)),
            pl.BlockSpec((CONV_WIDTH, D_LRU), lambda d, g, k: (0, 0)),
            pl.BlockSpec((1, D_LRU), lambda d, g, k: (0, 0)),
            pl.BlockSpec((1, D_LRU, 2 * D_LRU), lambda d, g, k: (d, 0, 0)),
            pl.BlockSpec((1, 1, 2 * D_LRU), lambda d, g, k: (d, 0, 0)),
            pl.BlockSpec((1, 1, D_LRU), lambda d, g, k: (d, 0, 0)),
            pl.BlockSpec((1, SCAN_BATCH, D_LRU), lambda d, g, k: (d, g, 0)),
        ],
        out_specs=pl.BlockSpec((1, SCAN_BATCH, SCAN_CHUNK, D_LRU), lambda d, g, k: (d, g, chunk(d, k), 0)),
        scratch_shapes=[scratch, scratch, scratch, pltpu.VMEM((SCAN_BATCH, D_LRU), F32)],
        compiler_params=_params(("arbitrary", "arbitrary", "arbitrary")),
        name="rglru",
    )(u3, u3, u3, conv_w, conv_b, wg, bg, lam, h0)


def _fourier_kernel(m_ref, ab_ref, o_ref):
    seq2 = m_ref.shape[1]
    rhs = ab_ref[0].reshape(seq2, D_FOURIER)
    o_ref[0] = jnp.dot(m_ref[...], rhs, preferred_element_type=F32).astype(BF16)


def _fourier_call(m_pos, ab):
    batch, _, seq, _ = ab.shape
    return pl.pallas_call(
        _fourier_kernel,
        out_shape=jax.ShapeDtypeStruct((batch, seq, D_FOURIER), BF16),
        grid=(batch, seq // FOURIER_TILE),
        in_specs=[
            pl.BlockSpec((FOURIER_TILE, 2 * seq), lambda b, m: (m, 0)),
            pl.BlockSpec((1, 2, seq, D_FOURIER), lambda b, m: (b, 0, 0, 0)),
        ],
        out_specs=pl.BlockSpec((1, FOURIER_TILE, D_FOURIER), lambda b, m: (b, m, 0)),
        compiler_params=_params(("parallel", "parallel")),
        name="fourier",
    )(m_pos, ab)


def _route(logits):
    lane = lax.broadcasted_iota(jnp.int32, logits.shape, 1)
    neg = -jnp.inf
    gl = jnp.where(lane < N_GROUPS, logits, neg)
    gmax = jnp.max(gl, axis=1, keepdims=True)
    grp = jnp.min(jnp.where(gl == gmax, lane, ROUTE_LANES), axis=1, keepdims=True)
    p_grp = 1.0 / jnp.sum(jnp.exp(gl - gmax), axis=1, keepdims=True)
    e_lane = lane - N_GROUPS
    in_grp = (e_lane >= 0) & (e_lane < N_EXPERTS) & ((e_lane // EXPERTS_PER_GROUP) == grp)
    el = jnp.where(in_grp, logits, neg)
    t1 = jnp.max(el, axis=1, keepdims=True)
    i1 = jnp.min(jnp.where(el == t1, lane, ROUTE_LANES), axis=1, keepdims=True)
    el2 = jnp.where(lane == i1, neg, el)
    t2 = jnp.max(el2, axis=1, keepdims=True)
    i2 = jnp.min(jnp.where(el2 == t2, lane, ROUTE_LANES), axis=1, keepdims=True)
    e = jnp.exp(t2 - t1)
    w1 = p_grp / (1.0 + e)
    w2 = p_grp * e / (1.0 + e)
    out = jnp.where(lane == 0, (i1 - N_GROUPS).astype(F32), 0.0)
    out = jnp.where(lane == 1, (i2 - N_GROUPS).astype(F32), out)
    out = jnp.where(lane == 2, w1, out)
    return jnp.where(lane == 3, w2, out)


def _merge_kernel(x_ref, sh1_ref, sc1_ref, g1_ref, sh2_ref, sc2_ref, n1_ref, n2_ref, wgate_ref,
                  fm_ref, hf_ref, hb_ref, yr_ref, wfo_ref, wlo_ref, wout_ref, wr_ref, br_ref,
                  x1_ref, h2_ref, route_ref):
    x = x_ref[...]
    hx = _rms_modulate(x, n1_ref[...], sh1_ref[0], sc1_ref[0]).astype(BF16)
    gates = jax.nn.sigmoid(jnp.dot(hx, wgate_ref[...], preferred_element_type=F32))
    branch_f = jnp.dot(fm_ref[0], wfo_ref[...], preferred_element_type=F32)
    y = yr_ref[...].astype(F32)
    gelu = 0.5 * y * (1.0 + jnp.tanh(np.sqrt(2.0 / np.pi).astype(np.float32) * (y + 0.044715 * (y * y * y))))
    lr = ((hf_ref[0, 0] + hb_ref[0, 0]) * gelu).astype(BF16)
    branch_r = jnp.dot(lr, wlo_ref[...], preferred_element_type=F32)
    mixed = gates[:, :D_MODEL] * branch_f + gates[:, D_MODEL:] * branch_r
    mix = jnp.dot(mixed.astype(BF16), wout_ref[...], preferred_element_type=F32)
    x1 = x + g1_ref[0] * mix
    x1_ref[...] = x1
    h2 = _rms_modulate(x1, n2_ref[...], sh2_ref[0], sc2_ref[0])
    _store_token_tiled(h2_ref, h2)
    h2_hi = h2.astype(BF16)
    h2_lo = (h2 - h2_hi.astype(F32)).astype(BF16)
    parts = (jnp.dot(h2_hi, wr_ref[...], preferred_element_type=F32)
             + jnp.dot(h2_lo, wr_ref[...], preferred_element_type=F32))
    route_ref[...] = _route(parts[:, :ROUTE_LANES] + parts[:, ROUTE_LANES:] + br_ref[...])


def _merge_call(x2, mods, n1, n2, wgate, fm, h, yr, wfo, wlo, wout, wr, br, batch, seq):
    n = batch * seq
    tm = TOKEN_TILE
    tpb = seq // tm
    mod_spec = pl.BlockSpec((1, 1, D_MODEL), lambda i: (i // tpb, 0, 0))
    vec = pl.BlockSpec((1, D_MODEL), lambda i: (0, 0))
    tile = pl.BlockSpec((tm, D_MODEL), lambda i: (i, 0))

    def full(a):
        return pl.BlockSpec(a.shape, lambda i: (0,) * a.ndim)

    return pl.pallas_call(
        _merge_kernel,
        out_shape=(
            jax.ShapeDtypeStruct((n, D_MODEL), F32),
            jax.ShapeDtypeStruct((n * TOKEN_ROWS, LANES), F32),
            jax.ShapeDtypeStruct((n, ROUTE_LANES), F32),
        ),
        grid=(n // tm,),
        in_specs=[
            tile, mod_spec, mod_spec, mod_spec, mod_spec, mod_spec, vec, vec, full(wgate),
            pl.BlockSpec((1, tm, D_FOURIER), lambda i: (i // tpb, i % tpb, 0)),
            pl.BlockSpec((1, 1, tm, D_LRU), lambda i: (0, i // tpb, i % tpb, 0)),
            pl.BlockSpec((1, 1, tm, D_LRU), lambda i: (1, i // tpb, i % tpb, 0)),
            pl.BlockSpec((tm, D_LRU), lambda i: (i, 0)),
            full(wfo), full(wlo), full(wout), full(wr), full(br),
        ],
        out_specs=(tile, pl.BlockSpec((tm * TOKEN_ROWS, LANES), lambda i: (i, 0)),
                   pl.BlockSpec((tm, ROUTE_LANES), lambda i: (i, 0))),
        compiler_params=_params(("parallel",)),
        name="merge",
    )(x2, *mods, n1, n2, wgate, fm, h, h, yr, wfo, wlo, wout, wr, br)


def _token_rows(t, count=1):
    start = t * TOKEN_ROWS
    if not isinstance(start, int):
        start = pl.multiple_of(start, TOKEN_ROWS)
    return pl.ds(start, count * TOKEN_ROWS)


def _row_copy(src_ref, dst_ref, sem, src_tok, dst_tok):
    return pltpu.make_async_copy(src_ref.at[_token_rows(src_tok)], dst_ref.at[_token_rows(dst_tok)], sem)


def _moe_kernel(n_tokens, be_ref, nv_ref, rows_ref, h2_hbm, wg_ref, wu_ref, wd_ref, out_hbm,
                xbuf, ybuf, wg_b, wu_b, wd_b, gat_sem, sct_sem):
    i = pl.program_id(0)
    n_blk = pl.num_programs(0)
    n_assign = n_tokens * TOP_K
    slot = i % 2

    def block_copy_for_wait(sem, to_hbm):
        if to_hbm:
            return pltpu.make_async_copy(ybuf.at[0], out_hbm.at[_token_rows(0, MOE_BLOCK)], sem)
        return pltpu.make_async_copy(h2_hbm.at[_token_rows(0, MOE_BLOCK)], xbuf.at[0], sem)

    def issue_gather(b, s):
        def body(j, c):
            for p in range(2):
                r = 2 * j + p
                tok = rows_ref[b * MOE_BLOCK + r] & SRC_MASK
                _row_copy(h2_hbm, xbuf.at[s], gat_sem.at[s], tok, r).start(priority=p)
            return c
        lax.fori_loop(0, MOE_BLOCK // 2, body, 0, unroll=8)

    def issue_scatter(b, s):
        def body(j, c):
            for p in range(2):
                r = 2 * j + p
                dst = lax.shift_right_logical(rows_ref[b * MOE_BLOCK + r], SRC_BITS)
                _row_copy(ybuf.at[s], out_hbm, sct_sem.at[s], r, dst).start(priority=p)
            return c
        lax.fori_loop(0, MOE_BLOCK // 2, body, 0, unroll=8)

    @pl.when(i == 0)
    def _():
        ybuf[0] = jnp.zeros(ybuf.shape[1:], F32)
        fills = [pltpu.make_async_copy(ybuf.at[0], out_hbm.at[_token_rows(n_assign + q * MOE_BLOCK, MOE_BLOCK)],
                                       sct_sem.at[0]) for q in range(N_DUMMY_ROWS // MOE_BLOCK)]
        for cp in fills:
            cp.start()
        for cp in fills:
            cp.wait()

        @pl.when(nv_ref[0] > 0)
        def _():
            issue_gather(0, 0)

    nxt = jnp.minimum(i + 1, n_blk - 1)

    @pl.when((i + 1 < n_blk) & (nv_ref[nxt] > 0))
    def _():
        issue_gather(nxt, 1 - slot)

    @pl.when((i == 0) | (be_ref[i] != be_ref[jnp.maximum(i - 1, 0)]))
    def _():
        wg_b[...] = wg_ref[0].astype(BF16)
        wu_b[...] = wu_ref[0].astype(BF16)
        wd_b[...] = wd_ref[0].astype(BF16)

    @pl.when((i >= 2) & (nv_ref[jnp.maximum(i - 2, 0)] > 0))
    def _():
        block_copy_for_wait(sct_sem.at[slot], True).wait()

    @pl.when(nv_ref[i] > 0)
    def _():
        block_copy_for_wait(gat_sem.at[slot], False).wait()
        xb = _load_token_tiled(xbuf.at[slot], MOE_BLOCK).astype(BF16)
        hg = jnp.dot(xb, wg_b[...], preferred_element_type=F32)
        hu = jnp.dot(xb, wu_b[...], preferred_element_type=F32)
        hb = (hg * jax.nn.sigmoid(hg) * hu).astype(BF16)
        _store_token_tiled(ybuf.at[slot], jnp.dot(hb, wd_b[...], preferred_element_type=F32))
        issue_scatter(i, slot)

    @pl.when(i == n_blk - 1)
    def _():
        @pl.when((i >= 1) & (nv_ref[jnp.maximum(i - 1, 0)] > 0))
        def _():
            block_copy_for_wait(sct_sem.at[1 - slot], True).wait()

        @pl.when(nv_ref[i] > 0)
        def _():
            block_copy_for_wait(sct_sem.at[slot], True).wait()


def _moe_call(blk_e, blk_nv, rows, h2, w_g, w_u, w_d):
    n_blk = blk_e.shape[0]
    n_tokens = h2.shape[0] // TOKEN_ROWS
    any_spec = pl.BlockSpec(memory_space=pl.ANY)

    def expert(i, be, nv, rows):
        return (be[i], 0, 0)

    grid_spec = pltpu.PrefetchScalarGridSpec(
        num_scalar_prefetch=3,
        grid=(n_blk,),
        in_specs=[
            any_spec,
            pl.BlockSpec((1, D_MODEL, D_EXPERT), expert),
            pl.BlockSpec((1, D_MODEL, D_EXPERT), expert),
            pl.BlockSpec((1, D_EXPERT, D_MODEL), expert),
        ],
        out_specs=any_spec,
        scratch_shapes=[
            pltpu.VMEM((2, MOE_BLOCK * TOKEN_ROWS, LANES), F32),
            pltpu.VMEM((2, MOE_BLOCK * TOKEN_ROWS, LANES), F32),
            pltpu.VMEM((D_MODEL, D_EXPERT), BF16),
            pltpu.VMEM((D_MODEL, D_EXPERT), BF16),
            pltpu.VMEM((D_EXPERT, D_MODEL), BF16),
            pltpu.SemaphoreType.DMA((2,)),
            pltpu.SemaphoreType.DMA((2,)),
        ],
    )
    return pl.pallas_call(
        functools.partial(_moe_kernel, n_tokens),
        out_shape=jax.ShapeDtypeStruct(((n_tokens * TOP_K + N_DUMMY_ROWS) * TOKEN_ROWS, LANES), F32),
        grid_spec=grid_spec,
        compiler_params=_params(("arbitrary",)),
        name="moe",
    )(blk_e, blk_nv, rows, h2, w_g, w_u, w_d)


def _final_kernel(x1_ref, y0_ref, y1_ref, route_ref, g2_ref, fg_ref, o_ref):
    route = route_ref[...]
    rows = route.shape[0]
    moe = route[:, 2:3] * _load_token_tiled(y0_ref, rows) + route[:, 3:4] * _load_token_tiled(y1_ref, rows)
    x2 = x1_ref[...] + g2_ref[0] * moe
    o_ref[...] = x2 * lax.rsqrt(jnp.mean(x2 * x2, axis=-1, keepdims=True) + EPS) * fg_ref[...]


def _final_call(x1, out2, route, g2, fg, batch, seq):
    n = batch * seq
    tm = TOKEN_TILE
    tpb = seq // tm
    tile = pl.BlockSpec((tm, D_MODEL), lambda i: (i, 0))
    return pl.pallas_call(
        _final_kernel,
        out_shape=jax.ShapeDtypeStruct((n, D_MODEL), F32),
        grid=(n // tm,),
        in_specs=[
            tile,
            pl.BlockSpec((tm * TOKEN_ROWS, LANES), lambda i: (i, 0)),
            pl.BlockSpec((tm * TOKEN_ROWS, LANES), lambda i: (i + n // tm, 0)),
            pl.BlockSpec((tm, ROUTE_LANES), lambda i: (i, 0)),
            pl.BlockSpec((1, 1, D_MODEL), lambda i: (i // tpb, 0, 0)),
            pl.BlockSpec((1, D_MODEL), lambda i: (0, 0)),
        ],
        out_specs=tile,
        compiler_params=_params(("parallel",)),
        name="final",
    )(x1, out2, out2, route, g2, fg)


def _channel_dft():
    j = np.arange(FOURIER_GROUP_DIM)
    ang = 2.0 * np.pi * np.outer(j, j) / FOURIER_GROUP_DIM
    return np.concatenate([np.cos(ang), np.sin(ang)], axis=1).astype(np.float32)


def _position_dft(seq):
    rows = seq // GRID_W
    assert GRID_W % rows == 0
    r, c = np.divmod(np.arange(seq), GRID_W)
    phase = (np.outer(r, r) * (GRID_W // rows) + np.outer(c, c)) % GRID_W
    ang = 2.0 * np.pi * phase / GRID_W
    scale = 1.0 / np.sqrt(float(seq) * FOURIER_GROUP_DIM)
    return np.concatenate([np.cos(ang), -np.sin(ang)], axis=1) * scale


def _block_diag(w):
    heads, hd, _ = w.shape
    eye = jnp.eye(heads, dtype=w.dtype)
    return jnp.einsum('hij,hg->higj', w, eye).reshape(heads * hd, heads * hd)


def _gate_weights(w_a, w_x):
    return jnp.stack([jnp.concatenate([_block_diag(w_a[d]), _block_diag(w_x[d])], axis=1)
                      for d in range(2)]).astype(BF16)


def _dispatch(eid, n_tokens):
    n_assign = n_tokens * TOP_K
    n_blk = n_assign // MOE_BLOCK + N_EXPERTS
    n_rows = n_blk * MOE_BLOCK
    assert n_tokens <= SRC_MASK + 1 and (n_assign + N_DUMMY_ROWS) << SRC_BITS < 2 ** 32
    flat_e = eid.reshape(-1)
    experts = jnp.arange(N_EXPERTS, dtype=jnp.int32)
    counts = jnp.sum((flat_e[:, None] == experts[None, :]).astype(jnp.int32), axis=0)
    blocks = (counts + MOE_BLOCK - 1) // MOE_BLOCK
    n_pad = blocks * MOE_BLOCK - counts
    pad_j = jnp.arange(MOE_BLOCK, dtype=jnp.int32)
    pad_key = jnp.where(pad_j[None, :] < n_pad[:, None], 2 * experts[:, None] + 1, 2 * N_EXPERTS)
    keys = jnp.concatenate([2 * flat_e, pad_key.reshape(-1)])
    vals = jnp.concatenate([jnp.arange(n_assign, dtype=jnp.int32),
                            jnp.full((N_EXPERTS * MOE_BLOCK,), -1, jnp.int32)])
    _, a = lax.sort((keys, vals), num_keys=1)
    r = jnp.arange(n_rows, dtype=jnp.int32)
    tok = lax.shift_right_logical(a, 1)
    dst = jnp.where(a >= 0, (a & 1) * n_tokens + tok, n_assign + r % N_DUMMY_ROWS)
    rows = lax.shift_left(dst, SRC_BITS) | jnp.where(a >= 0, tok, 0)

    bend = jnp.cumsum(blocks)
    b = jnp.arange(n_blk, dtype=jnp.int32)
    owner = (b[:, None] >= bend[None, :]).astype(jnp.int32)
    blk_e = jnp.minimum(jnp.sum(owner, axis=1), N_EXPERTS - 1)
    onehot = (blk_e[:, None] == experts[None, :]).astype(jnp.int32)
    first = jnp.sum(onehot * (bend - blocks)[None, :], axis=1)
    blk_nv = jnp.clip(jnp.sum(onehot * counts[None, :], axis=1) - (b - first) * MOE_BLOCK, 0, MOE_BLOCK)
    return blk_e, blk_nv, rows


def kernel(x, c, ctx, c_ctx, w_mod, b_mod, norm1_g, w_in, conv_w, conv_b, lru_wa, lru_ba, lru_wx, lru_bx,
           lru_lam, w_fourier_out, w_lru_out, w_out, norm2_g, w_group, b_group, w_expert_router,
           b_expert_router, w_gate_e, w_up_e, w_down_e, final_g):
    batch, seq, _ = x.shape
    ctx_len = ctx.shape[1]
    n = batch * seq
    assert w_mod.shape[0] == 1, "single-layer stack only: the context stream is not carried across layers"
    x2 = x.reshape(n, D_MODEL)
    dft_ch = jnp.asarray(_channel_dft().astype(BF16))
    m_pos = jnp.asarray(_position_dft(seq).astype(BF16))

    for l in range(1):
        c_all = jnp.concatenate([c, c_ctx[None], jnp.zeros((MOD_ROWS - batch - 1, D_MODEL), F32)], axis=0)
        mod = _mod_call(c_all, w_mod[l], b_mod[l][None])
        sh1, sc1, g1, sh2, sc2, g2 = [m[:batch, None, :] for m in jnp.split(mod, 6, axis=-1)]
        csh1, csc1 = mod[batch:batch + 1, :D_MODEL], mod[batch:batch + 1, D_MODEL:2 * D_MODEL]

        w_in_b = w_in[l].astype(BF16)
        n1 = norm1_g[l][None]
        wg = _gate_weights(lru_wa[l], lru_wx[l])
        bg = jnp.concatenate([lru_ba[l], lru_bx[l]], axis=-1)[:, None, :]
        lam = lru_lam[l][:, None, :]
        cb = conv_b[l][None]

        uc = _inproj_ctx_call(ctx.reshape(batch * ctx_len, D_MODEL), csh1, csc1, n1,
                              w_in_b[:, D_FOURIER:D_FOURIER + D_LRU])
        hc = _rglru_call(uc.reshape(batch, ctx_len, D_LRU), conv_w[l], cb, wg, bg, lam,
                         jnp.zeros((2, batch, D_LRU), F32))
        h0 = jnp.stack([hc[0, :, -1], hc[1, :, 0]])

        ab, ur, yr = _inproj_call(x2, sh1, sc1, n1, w_in_b[:, :D_FOURIER + 2 * D_LRU], dft_ch, batch, seq)
        h = _rglru_call(ur.reshape(batch, seq, D_LRU), conv_w[l], cb, wg, bg, lam, h0)
        fm = _fourier_call(m_pos, ab)

        w_route = jnp.zeros((D_MODEL, ROUTE_LANES), F32)
        w_route = w_route.at[:, :N_GROUPS].set(w_group[l]).at[:, N_GROUPS:N_GROUPS + N_EXPERTS].set(
            w_expert_router[l])
        b_route = jnp.zeros((1, ROUTE_LANES), F32)
        b_route = b_route.at[0, :N_GROUPS].set(b_group[l]).at[0, N_GROUPS:N_GROUPS + N_EXPERTS].set(
            b_expert_router[l])
        w_route_hi = w_route.astype(BF16)
        w_route_lo = (w_route - w_route_hi.astype(F32)).astype(BF16)
        w_route = jnp.concatenate([w_route_hi, w_route_lo], axis=1)
        x1, h2, route = _merge_call(
            x2, (sh1, sc1, g1, sh2, sc2), n1, norm2_g[l][None], w_in_b[:, D_FOURIER + 2 * D_LRU:],
            fm, h, yr, w_fourier_out[l].astype(BF16), w_lru_out[l].astype(BF16), w_out[l].astype(BF16),
            w_route, b_route, batch, seq)

        eid = route[:, :TOP_K].astype(jnp.int32)
        blk_e, blk_nv, rows = _dispatch(eid, n)
        out2 = _moe_call(blk_e, blk_nv, rows, h2, w_gate_e[l], w_up_e[l], w_down_e[l])
        x2 = _final_call(x1, out2, route, g2, final_g[None], batch, seq)
    return x2.reshape(batch, seq, D_MODEL)
```

```python
import functools

import numpy as np
import jax
import jax.numpy as jnp
from jax import lax
from jax.experimental import pallas as pl
from jax.experimental.pallas import tpu as pltpu

F32 = jnp.float32
BF16 = jnp.bfloat16

LANES = 128
SUBLANES = 8
VMEM_LIMIT_BYTES = 56 * 1024 * 1024

D_MODEL = 1024
GRID_W = 64
EPS = 1e-6
N_FOURIER_GROUPS = 4
FOURIER_GROUP_DIM = 128
D_FOURIER = N_FOURIER_GROUPS * FOURIER_GROUP_DIM
N_LRU_HEADS = 8
LRU_HEAD_DIM = 64
D_LRU = N_LRU_HEADS * LRU_HEAD_DIM
CONV_WIDTH = 4
LRU_C = 8.0
N_GROUPS = 4
EXPERTS_PER_GROUP = 8
N_EXPERTS = N_GROUPS * EXPERTS_PER_GROUP
TOP_K = 2
D_EXPERT = 512

MOD_ROWS = 24
MOD_BLOCK_N = 1536
TOKEN_TILE = 512
SCAN_BATCH = SUBLANES
SCAN_CHUNK = 256
SCAN_PITCH = SCAN_CHUNK + SUBLANES
N_SLABS = D_LRU // LANES
FOURIER_TILE = 512
MOE_BLOCK = 256
ROUTE_LANES = LANES
N_DUMMY_ROWS = 4 * MOE_BLOCK
SRC_BITS = 15
SRC_MASK = (1 << SRC_BITS) - 1


def _params(semantics):
    return pltpu.CompilerParams(dimension_semantics=semantics, vmem_limit_bytes=VMEM_LIMIT_BYTES)


TOKEN_ROWS = D_MODEL // LANES


def _store_token_tiled(ref, x):
    rows = x.shape[0]
    for s in range(TOKEN_ROWS):
        ref[pl.ds(s, rows, stride=TOKEN_ROWS), :] = x[:, s * LANES:(s + 1) * LANES]


def _load_token_tiled(ref, rows):
    return jnp.concatenate([ref[pl.ds(s, rows, stride=TOKEN_ROWS), :] for s in range(TOKEN_ROWS)], axis=1)


def _rms_modulate(x, g, shift, scale):
    y = x * lax.rsqrt(jnp.mean(x * x, axis=-1, keepdims=True) + EPS) * g
    return y * (1.0 + scale) + shift


def _mod_kernel(c_ref, w_ref, b_ref, o_ref):
    c = c_ref[...]
    s = c * jax.nn.sigmoid(c)
    o_ref[...] = jnp.dot(s, w_ref[...], preferred_element_type=F32,
                         precision=lax.Precision.HIGHEST) + b_ref[...]


def _mod_call(c_all, w_mod, b_mod):
    n_out = w_mod.shape[1]
    return pl.pallas_call(
        _mod_kernel,
        out_shape=jax.ShapeDtypeStruct((MOD_ROWS, n_out), F32),
        grid=(n_out // MOD_BLOCK_N,),
        in_specs=[
            pl.BlockSpec((MOD_ROWS, D_MODEL), lambda j: (0, 0)),
            pl.BlockSpec((D_MODEL, MOD_BLOCK_N), lambda j: (0, j)),
            pl.BlockSpec((1, MOD_BLOCK_N), lambda j: (0, j)),
        ],
        out_specs=pl.BlockSpec((MOD_ROWS, MOD_BLOCK_N), lambda j: (0, j)),
        compiler_params=_params(("arbitrary",)),
        name="mod",
    )(c_all, w_mod, b_mod)


def _inproj_kernel(x_ref, sh_ref, sc_ref, g_ref, w_ref, dft_ref, ab_ref, ur_ref, yr_ref):
    hx = _rms_modulate(x_ref[...], g_ref[...], sh_ref[0], sc_ref[0]).astype(BF16)
    proj = jnp.dot(hx, w_ref[...], preferred_element_type=F32)
    uf = proj[:, :D_FOURIER].astype(BF16)
    for g in range(N_FOURIER_GROUPS):
        lo, hi = g * FOURIER_GROUP_DIM, (g + 1) * FOURIER_GROUP_DIM
        cs = jnp.dot(uf[:, lo:hi], dft_ref[...], preferred_element_type=F32)
        ab_ref[0, 0, :, lo:hi] = cs[:, :FOURIER_GROUP_DIM].astype(BF16)
        ab_ref[0, 1, :, lo:hi] = cs[:, FOURIER_GROUP_DIM:].astype(BF16)
    ur_ref[...] = proj[:, D_FOURIER:D_FOURIER + D_LRU]
    yr_ref[...] = proj[:, D_FOURIER + D_LRU:].astype(BF16)


def _inproj_call(x2, sh, sc, g, w, dft_ch, batch, seq):
    n = batch * seq
    tpb = seq // TOKEN_TILE
    mod_spec = pl.BlockSpec((1, 1, D_MODEL), lambda i: (i // tpb, 0, 0))
    return pl.pallas_call(
        _inproj_kernel,
        out_shape=(
            jax.ShapeDtypeStruct((batch, 2, seq, D_FOURIER), BF16),
            jax.ShapeDtypeStruct((n, D_LRU), F32),
            jax.ShapeDtypeStruct((n, D_LRU), BF16),
        ),
        grid=(n // TOKEN_TILE,),
        in_specs=[
            pl.BlockSpec((TOKEN_TILE, D_MODEL), lambda i: (i, 0)),
            mod_spec, mod_spec,
            pl.BlockSpec((1, D_MODEL), lambda i: (0, 0)),
            pl.BlockSpec(w.shape, lambda i: (0, 0)),
            pl.BlockSpec(dft_ch.shape, lambda i: (0, 0)),
        ],
        out_specs=(
            pl.BlockSpec((1, 2, TOKEN_TILE, D_FOURIER), lambda i: (i // tpb, 0, i % tpb, 0)),
            pl.BlockSpec((TOKEN_TILE, D_LRU), lambda i: (i, 0)),
            pl.BlockSpec((TOKEN_TILE, D_LRU), lambda i: (i, 0)),
        ),
        compiler_params=_params(("parallel",)),
        name="inproj",
    )(x2, sh, sc, g, w, dft_ch)


def _inproj_ctx_kernel(x_ref, sh_ref, sc_ref, g_ref, w_ref, ur_ref):
    hx = _rms_modulate(x_ref[...], g_ref[...], sh_ref[...], sc_ref[...]).astype(BF16)
    ur_ref[...] = jnp.dot(hx, w_ref[...], preferred_element_type=F32)


def _inproj_ctx_call(ctx2, sh, sc, g, w):
    n = ctx2.shape[0]
    vec = pl.BlockSpec((1, D_MODEL), lambda i: (0, 0))
    return pl.pallas_call(
        _inproj_ctx_kernel,
        out_shape=jax.ShapeDtypeStruct((n, D_LRU), F32),
        grid=(n // TOKEN_TILE,),
        in_specs=[pl.BlockSpec((TOKEN_TILE, D_MODEL), lambda i: (i, 0)), vec, vec, vec,
                  pl.BlockSpec(w.shape, lambda i: (0, 0))],
        out_specs=pl.BlockSpec((TOKEN_TILE, D_LRU), lambda i: (i, 0)),
        compiler_params=_params(("parallel",)),
        name="inproj_ctx",
    )(ctx2, sh, sc, g, w)


def _rglru_kernel(n_chunks, u_ref, up_ref, un_ref, cw_ref, cb_ref, wg_ref, bg_ref, lam_ref, h0_ref,
                  h_ref, a_s, b_s, h_s, state):
    tc, pitch = SCAN_CHUNK, SCAN_PITCH
    d = pl.program_id(0)
    k = pl.program_id(2)
    kk = k + d * (n_chunks - 1 - 2 * k)

    @pl.when(k == 0)
    def _():
        state[...] = h0_ref[0]

    lam = lam_ref[0]
    neg_lam = -lam
    softplus = jnp.maximum(neg_lam, 0.0) + jnp.log1p(jnp.exp(-jnp.abs(neg_lam)))
    has_prev = kk > 0
    has_next = kk < n_chunks - 1
    row = lax.broadcasted_iota(jnp.int32, (tc, D_LRU), 0)
    cw = cw_ref[...]

    for i in range(SCAN_BATCH):
        u = u_ref[i]
        prev = jnp.where(has_prev, up_ref[i], 0.0)
        nxt = jnp.where(has_next, un_ref[i], 0.0)
        p2, p1, n1 = prev[SUBLANES - 2:SUBLANES - 1], prev[SUBLANES - 1:SUBLANES], nxt[0:1]
        um1 = jnp.where(row == 0, p1, pltpu.roll(u, 1, 0))
        um2 = jnp.where(row == 0, p2, jnp.where(row == 1, p1, pltpu.roll(u, 2, 0)))
        up1 = jnp.where(row == tc - 1, n1, pltpu.roll(u, tc - 1, 0))
        xc = um2 * cw[0:1] + um1 * cw[1:2] + u * cw[2:3] + up1 * cw[3:4] + cb_ref[...]
        gz = jnp.dot(xc.astype(BF16), wg_ref[0], preferred_element_type=F32) + bg_ref[0]
        r = jax.nn.sigmoid(gz[:, :D_LRU])
        ig = jax.nn.sigmoid(gz[:, D_LRU:])
        log_a = -LRU_C * r * softplus
        a = jnp.exp(log_a)
        b = jnp.sqrt((1.0 - a) * (1.0 + a)) * (ig * xc)
        for j in range(N_SLABS):
            a_s[j, pl.ds(i * pitch, tc), :] = a[:, j * LANES:(j + 1) * LANES]
            b_s[j, pl.ds(i * pitch, tc), :] = b[:, j * LANES:(j + 1) * LANES]

    def step(s, h):
        t = s + d * (tc - 1 - 2 * s)
        out = []
        for j in range(N_SLABS):
            rows = pl.ds(t, SCAN_BATCH, stride=pitch)
            hj = a_s[j, rows, :] * h[j] + b_s[j, rows, :]
            h_s[j, rows, :] = hj
            out.append(hj)
        return tuple(out)

    st = state[...]
    h = lax.fori_loop(0, tc, step, tuple(st[:, j * LANES:(j + 1) * LANES] for j in range(N_SLABS)),
                      unroll=4)
    for j in range(N_SLABS):
        state[:, j * LANES:(j + 1) * LANES] = h[j]
    for i in range(SCAN_BATCH):
        for j in range(N_SLABS):
            h_ref[0, i, :, j * LANES:(j + 1) * LANES] = h_s[j, pl.ds(i * pitch, tc), :]


def _rglru_call(u3, conv_w, conv_b, wg, bg, lam, h0):
    batch, seq, _ = u3.shape
    n_chunks = seq // SCAN_CHUNK
    halo_blocks = SCAN_CHUNK // SUBLANES
    last_halo = seq // SUBLANES - 1

    def chunk(d, k):
        return k + d * (n_chunks - 1 - 2 * k)

    scratch = pltpu.VMEM((N_SLABS, SCAN_BATCH * SCAN_PITCH, LANES), F32)
    return pl.pallas_call(
        functools.partial(_rglru_kernel, n_chunks),
        out_shape=jax.ShapeDtypeStruct((2, batch, seq, D_LRU), F32),
        grid=(2, batch // SCAN_BATCH, n_chunks),
        in_specs=[
            pl.BlockSpec((SCAN_BATCH, SCAN_CHUNK, D_LRU), lambda d, g, k: (g, chunk(d, k), 0)),
            pl.BlockSpec((SCAN_BATCH, SUBLANES, D_LRU),
                         lambda d, g, k: (g, jnp.maximum(chunk(d, k) * halo_blocks - 1, 0), 0)),
            pl.BlockSpec((SCAN_BATCH, SUBLANES, D_LRU),
                         lambda d, g, k: (g, jnp.minimum((chunk(d, k) + 1) * halo_blocks, last_halo), 0)),
            pl.BlockSpec((CONV_WIDTH, D_LRU), lambda d, g, k: (0, 0)),
            pl.BlockSpec((1, D_LRU), lambda d, g, k: (0, 0)),
            pl.BlockSpec((1, D_LRU, 2 * D_LRU), lambda d, g, k: (d, 0, 0)),
            pl.BlockSpec((1, 1, 2 * D_LRU), lambda d, g, k: (d, 0, 0)),
            pl.BlockSpec((1, 1, D_LRU), lambda d, g, k: (d, 0, 0)),
            pl.BlockSpec((1, SCAN_BATCH, D_LRU), lambda d, g, k: (d, g, 0)),
        ],
        out_specs=pl.BlockSpec((1, SCAN_BATCH, SCAN_CHUNK, D_LRU), lambda d, g, k: (d, g, chunk(d, k), 0)),
        scratch_shapes=[scratch, scratch, scratch, pltpu.VMEM((SCAN_BATCH, D_LRU), F32)],
        compiler_params=_params(("arbitrary", "arbitrary", "arbitrary")),
        name="rglru",
    )(u3, u3, u3, conv_w, conv_b, wg, bg, lam, h0)


def _fourier_kernel(m_ref, ab_ref, o_ref):
    seq2 = m_ref.shape[1]
    rhs = ab_ref[0].reshape(seq2, D_FOURIER)
    o_ref[0] = jnp.dot(m_ref[...], rhs, preferred_element_type=F32).astype(BF16)


def _fourier_call(m_pos, ab):
    batch, _, seq, _ = ab.shape
    return pl.pallas_call(
        _fourier_kernel,
        out_shape=jax.ShapeDtypeStruct((batch, seq, D_FOURIER), BF16),
        grid=(batch, seq // FOURIER_TILE),
        in_specs=[
            pl.BlockSpec((FOURIER_TILE, 2 * seq), lambda b, m: (m, 0)),
            pl.BlockSpec((1, 2, seq, D_FOURIER), lambda b, m: (b, 0, 0, 0)),
        ],
        out_specs=pl.BlockSpec((1, FOURIER_TILE, D_FOURIER), lambda b, m: (b, m, 0)),
        compiler_params=_params(("parallel", "parallel")),
        name="fourier",
    )(m_pos, ab)


def _route(logits):
    lane = lax.broadcasted_iota(jnp.int32, logits.shape, 1)
    neg = -jnp.inf
    gl = jnp.where(lane < N_GROUPS, logits, neg)
    gmax = jnp.max(gl, axis=1, keepdims=True)
    grp = jnp.min(jnp.where(gl == gmax, lane, ROUTE_LANES), axis=1, keepdims=True)
    p_grp = 1.0 / jnp.sum(jnp.exp(gl - gmax), axis=1, keepdims=True)
    e_lane = lane - N_GROUPS
    in_grp = (e_lane >= 0) & (e_lane < N_EXPERTS) & ((e_lane // EXPERTS_PER_GROUP) == grp)
    el = jnp.where(in_grp, logits, neg)
    t1 = jnp.max(el, axis=1, keepdims=True)
    i1 = jnp.min(jnp.where(el == t1, lane, ROUTE_LANES), axis=1, keepdims=True)
    el2 = jnp.where(lane == i1, neg, el)
    t2 = jnp.max(el2, axis=1, keepdims=True)
    i2 = jnp.min(jnp.where(el2 == t2, lane, ROUTE_LANES), axis=1, keepdims=True)
    e = jnp.exp(t2 - t1)
    w1 = p_grp / (1.0 + e)
    w2 = p_grp * e / (1.0 + e)
    out = jnp.where(lane == 0, (i1 - N_GROUPS).astype(F32), 0.0)
    out = jnp.where(lane == 1, (i2 - N_GROUPS).astype(F32), out)
    out = jnp.where(lane == 2, w1, out)
    return jnp.where(lane == 3, w2, out)


def _merge_kernel(x_ref, sh1_ref, sc1_ref, g1_ref, sh2_ref, sc2_ref, n1_ref, n2_ref, wgate_ref,
                  fm_ref, hf_ref, hb_ref, yr_ref, wfo_ref, wlo_ref, wout_ref, wr_ref, br_ref,
                  x1_ref, h2_ref, route_ref):
    x = x_ref[...]
    hx = _rms_modulate(x, n1_ref[...], sh1_ref[0], sc1_ref[0]).astype(BF16)
    gates = jax.nn.sigmoid(jnp.dot(hx, wgate_ref[...], preferred_element_type=F32))
    branch_f = jnp.dot(fm_ref[0], wfo_ref[...], preferred_element_type=F32)
    y = yr_ref[...].astype(F32)
    gelu = 0.5 * y * (1.0 + jnp.tanh(np.sqrt(2.0 / np.pi).astype(np.float32) * (y + 0.044715 * (y * y * y))))
    lr = ((hf_ref[0, 0] + hb_ref[0, 0]) * gelu).astype(BF16)
    branch_r = jnp.dot(lr, wlo_ref[...], preferred_element_type=F32)
    mixed = gates[:, :D_MODEL] * branch_f + gates[:, D_MODEL:] * branch_r
    mix = jnp.dot(mixed.astype(BF16), wout_ref[...], preferred_element_type=F32)
    x1 = x + g1_ref[0] * mix
    x1_ref[...] = x1
    h2 = _rms_modulate(x1, n2_ref[...], sh2_ref[0], sc2_ref[0])
    _store_token_tiled(h2_ref, h2)
    h2_hi = h2.astype(BF16)
    h2_lo = (h2 - h2_hi.astype(F32)).astype(BF16)
    parts = (jnp.dot(h2_hi, wr_ref[...], preferred_element_type=F32)
             + jnp.dot(h2_lo, wr_ref[...], preferred_element_type=F32))
    route_ref[...] = _route(parts[:, :ROUTE_LANES] + parts[:, ROUTE_LANES:] + br_ref[...])


def _merge_call(x2, mods, n1, n2, wgate, fm, h, yr, wfo, wlo, wout, wr, br, batch, seq):
    n = batch * seq
    tm = TOKEN_TILE
    tpb = seq // tm
    mod_spec = pl.BlockSpec((1, 1, D_MODEL), lambda i: (i // tpb, 0, 0))
    vec = pl.BlockSpec((1, D_MODEL), lambda i: (0, 0))
    tile = pl.BlockSpec((tm, D_MODEL), lambda i: (i, 0))

    def full(a):
        return pl.BlockSpec(a.shape, lambda i: (0,) * a.ndim)

    return pl.pallas_call(
        _merge_kernel,
        out_shape=(
            jax.ShapeDtypeStruct((n, D_MODEL), F32),
            jax.ShapeDtypeStruct((n * TOKEN_ROWS, LANES), F32),
            jax.ShapeDtypeStruct((n, ROUTE_LANES), F32),
        ),
        grid=(n // tm,),
        in_specs=[
            tile, mod_spec, mod_spec, mod_spec, mod_spec, mod_spec, vec, vec, full(wgate),
            pl.BlockSpec((1, tm, D_FOURIER), lambda i: (i // tpb, i % tpb, 0)),
            pl.BlockSpec((1, 1, tm, D_LRU), lambda i: (0, i // tpb, i % tpb, 0)),
            pl.BlockSpec((1, 1, tm, D_LRU), lambda i: (1, i // tpb, i % tpb, 0)),
            pl.BlockSpec((tm, D_LRU), lambda i: (i, 0)),
            full(wfo), full(wlo), full(wout), full(wr), full(br),
        ],
        out_specs=(tile, pl.BlockSpec((tm * TOKEN_ROWS, LANES), lambda i: (i, 0)),
                   pl.BlockSpec((tm, ROUTE_LANES), lambda i: (i, 0))),
        compiler_params=_params(("parallel",)),
        name="merge",
    )(x2, *mods, n1, n2, wgate, fm, h, h, yr, wfo, wlo, wout, wr, br)


def _token_rows(t, count=1):
    start = t * TOKEN_ROWS
    if not isinstance(start, int):
        start = pl.multiple_of(start, TOKEN_ROWS)
    return pl.ds(start, count * TOKEN_ROWS)


def _row_copy(src_ref, dst_ref, sem, src_tok, dst_tok):
    return pltpu.make_async_copy(src_ref.at[_token_rows(src_tok)], dst_ref.at[_token_rows(dst_tok)], sem)


def _moe_kernel(n_tokens, be_ref, nv_ref, rows_ref, h2_hbm, wg_ref, wu_ref, wd_ref, out_hbm,
                x0, x1, y0, y1, wg_b, wu_b, wd_b, gat_sem, sct_sem):
    i = pl.program_id(0)
    n_assign = n_tokens * TOP_K
    xs, ys = (x0, x1), (y0, y1)

    def gather_wait(s):
        pltpu.make_async_copy(h2_hbm.at[_token_rows(0, MOE_BLOCK)], xs[s], gat_sem.at[s]).wait()

    def scatter_wait(s):
        pltpu.make_async_copy(ys[s], out_hbm.at[_token_rows(0, MOE_BLOCK)], sct_sem.at[s]).wait()

    def issue_gather(b, s):
        base = (b + 1) * MOE_BLOCK
        for r in range(MOE_BLOCK):
            tok = rows_ref[base + r] & SRC_MASK
            _row_copy(h2_hbm, xs[s], gat_sem.at[s], tok, r).start(priority=r % 2)

    def issue_scatter(b, s):
        base = (b + 1) * MOE_BLOCK
        for r in range(MOE_BLOCK):
            dst = lax.shift_right_logical(rows_ref[base + r], SRC_BITS)
            _row_copy(ys[s], out_hbm, sct_sem.at[s], r, dst).start(priority=r % 2)

    active = nv_ref[i] > 0
    prev_active = (i > 0) & (nv_ref[jnp.maximum(i - 1, 0)] > 0)

    @pl.when(i == 0)
    def _():
        y0[...] = jnp.zeros(y0.shape, F32)
        y1[...] = jnp.zeros(y1.shape, F32)
        fills = [pltpu.make_async_copy(y0, out_hbm.at[_token_rows(n_assign + q * MOE_BLOCK, MOE_BLOCK)],
                                       sct_sem.at[min(q, 1)]) for q in range(N_DUMMY_ROWS // MOE_BLOCK)]
        for cp in fills:
            cp.start()
        for cp in fills[1:]:
            cp.wait()
        issue_gather(0, 0)

    @pl.when(active & ((i == 0) | (be_ref[i] != be_ref[jnp.maximum(i - 1, 0)])))
    def _():
        wg_b[...] = wg_ref[0].astype(BF16)
        wu_b[...] = wu_ref[0].astype(BF16)
        wd_b[...] = wd_ref[0].astype(BF16)

    for s in range(2):
        @pl.when(active & (i % 2 == s))
        def _(s=s):
            scatter_wait(s)
            gather_wait(s)
            issue_gather(i + 1, 1 - s)
            issue_scatter(i - 1, 1 - s)
            xb = _load_token_tiled(xs[s], MOE_BLOCK).astype(BF16)
            hg = jnp.dot(xb, wg_b[...], preferred_element_type=F32)
            hu = jnp.dot(xb, wu_b[...], preferred_element_type=F32)
            hb = (hg * jax.nn.sigmoid(hg) * hu).astype(BF16)
            _store_token_tiled(ys[s], jnp.dot(hb, wd_b[...], preferred_element_type=F32))

        @pl.when(jnp.logical_not(active) & prev_active & (i % 2 == s))
        def _(s=s):
            gather_wait(s)
            issue_scatter(i - 1, 1 - s)
            scatter_wait(s)
            scatter_wait(1 - s)


def _moe_call(blk_e, blk_nv, rows, h2, w_g, w_u, w_d):
    n_steps = blk_e.shape[0]
    n_tokens = h2.shape[0] // TOKEN_ROWS
    any_spec = pl.BlockSpec(memory_space=pl.ANY)

    def expert(i, be, nv, rows):
        return (be[i], 0, 0)

    block_buf = pltpu.VMEM((MOE_BLOCK * TOKEN_ROWS, LANES), F32)
    grid_spec = pltpu.PrefetchScalarGridSpec(
        num_scalar_prefetch=3,
        grid=(n_steps,),
        in_specs=[
            any_spec,
            pl.BlockSpec((1, D_MODEL, D_EXPERT), expert),
            pl.BlockSpec((1, D_MODEL, D_EXPERT), expert),
            pl.BlockSpec((1, D_EXPERT, D_MODEL), expert),
        ],
        out_specs=any_spec,
        scratch_shapes=[
            block_buf, block_buf, block_buf, block_buf,
            pltpu.VMEM((D_MODEL, D_EXPERT), BF16),
            pltpu.VMEM((D_MODEL, D_EXPERT), BF16),
            pltpu.VMEM((D_EXPERT, D_MODEL), BF16),
            pltpu.SemaphoreType.DMA((2,)),
            pltpu.SemaphoreType.DMA((2,)),
        ],
    )
    return pl.pallas_call(
        functools.partial(_moe_kernel, n_tokens),
        out_shape=jax.ShapeDtypeStruct(((n_tokens * TOP_K + N_DUMMY_ROWS) * TOKEN_ROWS, LANES), F32),
        grid_spec=grid_spec,
        compiler_params=_params(("arbitrary",)),
        name="moe",
    )(blk_e, blk_nv, rows, h2, w_g, w_u, w_d)


def _final_kernel(x1_ref, y0_ref, y1_ref, route_ref, g2_ref, fg_ref, o_ref):
    route = route_ref[...]
    rows = route.shape[0]
    moe = route[:, 2:3] * _load_token_tiled(y0_ref, rows) + route[:, 3:4] * _load_token_tiled(y1_ref, rows)
    x2 = x1_ref[...] + g2_ref[0] * moe
    o_ref[...] = x2 * lax.rsqrt(jnp.mean(x2 * x2, axis=-1, keepdims=True) + EPS) * fg_ref[...]


def _final_call(x1, out2, route, g2, fg, batch, seq):
    n = batch * seq
    tm = TOKEN_TILE
    tpb = seq // tm
    tile = pl.BlockSpec((tm, D_MODEL), lambda i: (i, 0))
    return pl.pallas_call(
        _final_kernel,
        out_shape=jax.ShapeDtypeStruct((n, D_MODEL), F32),
        grid=(n // tm,),
        in_specs=[
            tile,
            pl.BlockSpec((tm * TOKEN_ROWS, LANES), lambda i: (i, 0)),
            pl.BlockSpec((tm * TOKEN_ROWS, LANES), lambda i: (i + n // tm, 0)),
            pl.BlockSpec((tm, ROUTE_LANES), lambda i: (i, 0)),
            pl.BlockSpec((1, 1, D_MODEL), lambda i: (i // tpb, 0, 0)),
            pl.BlockSpec((1, D_MODEL), lambda i: (0, 0)),
        ],
        out_specs=tile,
        compiler_params=_params(("parallel",)),
        name="final",
    )(x1, out2, out2, route, g2, fg)


def _channel_dft():
    j = np.arange(FOURIER_GROUP_DIM)
    ang = 2.0 * np.pi * np.outer(j, j) / FOURIER_GROUP_DIM
    return np.concatenate([np.cos(ang), np.sin(ang)], axis=1).astype(np.float32)


def _position_dft(seq):
    rows = seq // GRID_W
    assert GRID_W % rows == 0
    r, c = np.divmod(np.arange(seq), GRID_W)
    phase = (np.outer(r, r) * (GRID_W // rows) + np.outer(c, c)) % GRID_W
    ang = 2.0 * np.pi * phase / GRID_W
    scale = 1.0 / np.sqrt(float(seq) * FOURIER_GROUP_DIM)
    return np.concatenate([np.cos(ang), -np.sin(ang)], axis=1) * scale


def _block_diag(w):
    heads, hd, _ = w.shape
    eye = jnp.eye(heads, dtype=w.dtype)
    return jnp.einsum('hij,hg->higj', w, eye).reshape(heads * hd, heads * hd)


def _gate_weights(w_a, w_x):
    return jnp.stack([jnp.concatenate([_block_diag(w_a[d]), _block_diag(w_x[d])], axis=1)
                      for d in range(2)]).astype(BF16)


def _dispatch(eid, n_tokens):
    n_assign = n_tokens * TOP_K
    n_blk = n_assign // MOE_BLOCK + N_EXPERTS
    n_rows = n_blk * MOE_BLOCK
    assert n_tokens <= SRC_MASK + 1 and (n_assign + N_DUMMY_ROWS) << SRC_BITS < 2 ** 32
    flat_e = eid.reshape(-1)
    experts = jnp.arange(N_EXPERTS, dtype=jnp.int32)
    counts = jnp.sum((flat_e[:, None] == experts[None, :]).astype(jnp.int32), axis=0)
    blocks = (counts + MOE_BLOCK - 1) // MOE_BLOCK
    n_pad = blocks * MOE_BLOCK - counts
    pad_j = jnp.arange(MOE_BLOCK, dtype=jnp.int32)
    pad_key = jnp.where(pad_j[None, :] < n_pad[:, None], 2 * experts[:, None] + 1, 2 * N_EXPERTS)
    keys = jnp.concatenate([2 * flat_e, pad_key.reshape(-1)])
    vals = jnp.concatenate([jnp.arange(n_assign, dtype=jnp.int32),
                            jnp.full((N_EXPERTS * MOE_BLOCK,), -1, jnp.int32)])
    _, a = lax.sort((keys, vals), num_keys=1)
    edge = jnp.full((MOE_BLOCK,), -1, jnp.int32)
    a = jnp.concatenate([edge, a, edge])
    r = jnp.arange(n_rows + 2 * MOE_BLOCK, dtype=jnp.int32)
    tok = lax.shift_right_logical(a, 1)
    dst = jnp.where(a >= 0, (a & 1) * n_tokens + tok, n_assign + r % N_DUMMY_ROWS)
    rows = lax.shift_left(dst, SRC_BITS) | jnp.where(a >= 0, tok, 0)

    bend = jnp.cumsum(blocks)
    b = jnp.arange(n_blk + 1, dtype=jnp.int32)
    owner = (b[:, None] >= bend[None, :]).astype(jnp.int32)
    blk_e = jnp.minimum(jnp.sum(owner, axis=1), N_EXPERTS - 1)
    onehot = (blk_e[:, None] == experts[None, :]).astype(jnp.int32)
    first = jnp.sum(onehot * (bend - blocks)[None, :], axis=1)
    blk_nv = jnp.clip(jnp.sum(onehot * counts[None, :], axis=1) - (b - first) * MOE_BLOCK, 0, MOE_BLOCK)
    return blk_e, blk_nv, rows


def kernel(x, c, ctx, c_ctx, w_mod, b_mod, norm1_g, w_in, conv_w, conv_b, lru_wa, lru_ba, lru_wx, lru_bx,
           lru_lam, w_fourier_out, w_lru_out, w_out, norm2_g, w_group, b_group, w_expert_router,
           b_expert_router, w_gate_e, w_up_e, w_down_e, final_g):
    batch, seq, _ = x.shape
    ctx_len = ctx.shape[1]
    n = batch * seq
    assert w_mod.shape[0] == 1, "single-layer stack only: the context stream is not carried across layers"
    x2 = x.reshape(n, D_MODEL)
    dft_ch = jnp.asarray(_channel_dft().astype(BF16))
    m_pos = jnp.asarray(_position_dft(seq).astype(BF16))

    for l in range(1):
        c_all = jnp.concatenate([c, c_ctx[None], jnp.zeros((MOD_ROWS - batch - 1, D_MODEL), F32)], axis=0)
        mod = _mod_call(c_all, w_mod[l], b_mod[l][None])
        sh1, sc1, g1, sh2, sc2, g2 = [m[:batch, None, :] for m in jnp.split(mod, 6, axis=-1)]
        csh1, csc1 = mod[batch:batch + 1, :D_MODEL], mod[batch:batch + 1, D_MODEL:2 * D_MODEL]

        w_in_b = w_in[l].astype(BF16)
        n1 = norm1_g[l][None]
        wg = _gate_weights(lru_wa[l], lru_wx[l])
        bg = jnp.concatenate([lru_ba[l], lru_bx[l]], axis=-1)[:, None, :]
        lam = lru_lam[l][:, None, :]
        cb = conv_b[l][None]

        uc = _inproj_ctx_call(ctx.reshape(batch * ctx_len, D_MODEL), csh1, csc1, n1,
                              w_in_b[:, D_FOURIER:D_FOURIER + D_LRU])
        hc = _rglru_call(uc.reshape(batch, ctx_len, D_LRU), conv_w[l], cb, wg, bg, lam,
                         jnp.zeros((2, batch, D_LRU), F32))
        h0 = jnp.stack([hc[0, :, -1], hc[1, :, 0]])

        ab, ur, yr = _inproj_call(x2, sh1, sc1, n1, w_in_b[:, :D_FOURIER + 2 * D_LRU], dft_ch, batch, seq)
        h = _rglru_call(ur.reshape(batch, seq, D_LRU), conv_w[l], cb, wg, bg, lam, h0)
        fm = _fourier_call(m_pos, ab)

        w_route = jnp.zeros((D_MODEL, ROUTE_LANES), F32)
        w_route = w_route.at[:, :N_GROUPS].set(w_group[l]).at[:, N_GROUPS:N_GROUPS + N_EXPERTS].set(
            w_expert_router[l])
        b_route = jnp.zeros((1, ROUTE_LANES), F32)
        b_route = b_route.at[0, :N_GROUPS].set(b_group[l]).at[0, N_GROUPS:N_GROUPS + N_EXPERTS].set(
            b_expert_router[l])
        w_route_hi = w_route.astype(BF16)
        w_route_lo = (w_route - w_route_hi.astype(F32)).astype(BF16)
        w_route = jnp.concatenate([w_route_hi, w_route_lo], axis=1)
        x1, h2, route = _merge_call(
            x2, (sh1, sc1, g1, sh2, sc2), n1, norm2_g[l][None], w_in_b[:, D_FOURIER + 2 * D_LRU:],
            fm, h, yr, w_fourier_out[l].astype(BF16), w_lru_out[l].astype(BF16), w_out[l].astype(BF16),
            w_route, b_route, batch, seq)

        eid = route[:, :TOP_K].astype(jnp.int32)
        blk_e, blk_nv, rows = _dispatch(eid, n)
        out2 = _moe_call(blk_e, blk_nv, rows, h2, w_gate_e[l], w_up_e[l], w_down_e[l])
        x2 = _final_call(x1, out2, route, g2, final_g[None], batch, seq)
    return x2.reshape(batch, seq, D_MODEL)
```

```python
import functools

import numpy as np
import jax
import jax.numpy as jnp
from jax import lax
from jax.experimental import pallas as pl
from jax.experimental.pallas import tpu as pltpu

F32 = jnp.float32
BF16 = jnp.bfloat16

LANES = 128
SUBLANES = 8
VMEM_LIMIT_BYTES = 56 * 1024 * 1024

D_MODEL = 1024
GRID_W = 64
EPS = 1e-6
N_FOURIER_GROUPS = 4
FOURIER_GROUP_DIM = 128
D_FOURIER = N_FOURIER_GROUPS * FOURIER_GROUP_DIM
N_LRU_HEADS = 8
LRU_HEAD_DIM = 64
D_LRU = N_LRU_HEADS * LRU_HEAD_DIM
CONV_WIDTH = 4
LRU_C = 8.0
N_GROUPS = 4
EXPERTS_PER_GROUP = 8
N_EXPERTS = N_GROUPS * EXPERTS_PER_GROUP
TOP_K = 2
D_EXPERT = 512

MOD_ROWS = 24
MOD_BLOCK_N = 1536
TOKEN_TILE = 512
SCAN_BATCH = SUBLANES
SCAN_CHUNK = 256
SCAN_PITCH = SCAN_CHUNK + SUBLANES
N_SLABS = D_LRU // LANES
FOURIER_TILE = 512
MOE_BLOCK = 256
ROUTE_LANES = LANES
N_DUMMY_ROWS = 4 * MOE_BLOCK
SRC_BITS = 15
SRC_MASK = (1 << SRC_BITS) - 1


def _params(semantics):
    return pltpu.CompilerParams(dimension_semantics=semantics, vmem_limit_bytes=VMEM_LIMIT_BYTES)


U32 = jnp.uint32
HALF = D_MODEL // 2
TOKEN_ROWS = HALF // LANES
HIGH_HALF_WORD = np.uint32(0xFFFF0000)


def _pack_bf16_pairs(x):
    bits = pltpu.bitcast(x.astype(BF16).astype(F32), U32)
    return (bits[:, :HALF] >> 16) | (bits[:, HALF:] & HIGH_HALF_WORD)


def _unpack_bf16_pairs(u, dtype):
    lo = pltpu.bitcast(u << 16, F32)
    hi = pltpu.bitcast(u & HIGH_HALF_WORD, F32)
    return jnp.concatenate([lo, hi], axis=1).astype(dtype)


def _store_token_tiled(ref, x):
    rows = x.shape[0]
    packed = _pack_bf16_pairs(x)
    for s in range(TOKEN_ROWS):
        ref[pl.ds(s, rows, stride=TOKEN_ROWS), :] = packed[:, s * LANES:(s + 1) * LANES]


def _load_token_tiled(ref, rows, dtype):
    packed = jnp.concatenate([ref[pl.ds(s, rows, stride=TOKEN_ROWS), :] for s in range(TOKEN_ROWS)], axis=1)
    return _unpack_bf16_pairs(packed, dtype)


def _rms_modulate(x, g, shift, scale):
    y = x * lax.rsqrt(jnp.mean(x * x, axis=-1, keepdims=True) + EPS) * g
    return y * (1.0 + scale) + shift


def _mod_kernel(c_ref, w_ref, b_ref, o_ref):
    c = c_ref[...]
    s = c * jax.nn.sigmoid(c)
    o_ref[...] = jnp.dot(s, w_ref[...], preferred_element_type=F32,
                         precision=lax.Precision.HIGHEST) + b_ref[...]


def _mod_call(c_all, w_mod, b_mod):
    n_out = w_mod.shape[1]
    return pl.pallas_call(
        _mod_kernel,
        out_shape=jax.ShapeDtypeStruct((MOD_ROWS, n_out), F32),
        grid=(n_out // MOD_BLOCK_N,),
        in_specs=[
            pl.BlockSpec((MOD_ROWS, D_MODEL), lambda j: (0, 0)),
            pl.BlockSpec((D_MODEL, MOD_BLOCK_N), lambda j: (0, j)),
            pl.BlockSpec((1, MOD_BLOCK_N), lambda j: (0, j)),
        ],
        out_specs=pl.BlockSpec((MOD_ROWS, MOD_BLOCK_N), lambda j: (0, j)),
        compiler_params=_params(("arbitrary",)),
        name="mod",
    )(c_all, w_mod, b_mod)


def _inproj_kernel(x_ref, sh_ref, sc_ref, g_ref, w_ref, dft_ref, ab_ref, ur_ref, yr_ref):
    hx = _rms_modulate(x_ref[...], g_ref[...], sh_ref[0], sc_ref[0]).astype(BF16)
    proj = jnp.dot(hx, w_ref[...], preferred_element_type=F32)
    uf = proj[:, :D_FOURIER].astype(BF16)
    for g in range(N_FOURIER_GROUPS):
        lo, hi = g * FOURIER_GROUP_DIM, (g + 1) * FOURIER_GROUP_DIM
        cs = jnp.dot(uf[:, lo:hi], dft_ref[...], preferred_element_type=F32)
        ab_ref[0, 0, :, lo:hi] = cs[:, :FOURIER_GROUP_DIM].astype(BF16)
        ab_ref[0, 1, :, lo:hi] = cs[:, FOURIER_GROUP_DIM:].astype(BF16)
    ur_ref[...] = proj[:, D_FOURIER:D_FOURIER + D_LRU]
    yr_ref[...] = proj[:, D_FOURIER + D_LRU:].astype(BF16)


def _inproj_call(x2, sh, sc, g, w, dft_ch, batch, seq):
    n = batch * seq
    tpb = seq // TOKEN_TILE
    mod_spec = pl.BlockSpec((1, 1, D_MODEL), lambda i: (i // tpb, 0, 0))
    return pl.pallas_call(
        _inproj_kernel,
        out_shape=(
            jax.ShapeDtypeStruct((batch, 2, seq, D_FOURIER), BF16),
            jax.ShapeDtypeStruct((n, D_LRU), F32),
            jax.ShapeDtypeStruct((n, D_LRU), BF16),
        ),
        grid=(n // TOKEN_TILE,),
        in_specs=[
            pl.BlockSpec((TOKEN_TILE, D_MODEL), lambda i: (i, 0)),
            mod_spec, mod_spec,
            pl.BlockSpec((1, D_MODEL), lambda i: (0, 0)),
            pl.BlockSpec(w.shape, lambda i: (0, 0)),
            pl.BlockSpec(dft_ch.shape, lambda i: (0, 0)),
        ],
        out_specs=(
            pl.BlockSpec((1, 2, TOKEN_TILE, D_FOURIER), lambda i: (i // tpb, 0, i % tpb, 0)),
            pl.BlockSpec((TOKEN_TILE, D_LRU), lambda i: (i, 0)),
            pl.BlockSpec((TOKEN_TILE, D_LRU), lambda i: (i, 0)),
        ),
        compiler_params=_params(("parallel",)),
        name="inproj",
    )(x2, sh, sc, g, w, dft_ch)


def _inproj_ctx_kernel(x_ref, sh_ref, sc_ref, g_ref, w_ref, ur_ref):
    hx = _rms_modulate(x_ref[...], g_ref[...], sh_ref[...], sc_ref[...]).astype(BF16)
    ur_ref[...] = jnp.dot(hx, w_ref[...], preferred_element_type=F32)


def _inproj_ctx_call(ctx2, sh, sc, g, w):
    n = ctx2.shape[0]
    vec = pl.BlockSpec((1, D_MODEL), lambda i: (0, 0))
    return pl.pallas_call(
        _inproj_ctx_kernel,
        out_shape=jax.ShapeDtypeStruct((n, D_LRU), F32),
        grid=(n // TOKEN_TILE,),
        in_specs=[pl.BlockSpec((TOKEN_TILE, D_MODEL), lambda i: (i, 0)), vec, vec, vec,
                  pl.BlockSpec(w.shape, lambda i: (0, 0))],
        out_specs=pl.BlockSpec((TOKEN_TILE, D_LRU), lambda i: (i, 0)),
        compiler_params=_params(("parallel",)),
        name="inproj_ctx",
    )(ctx2, sh, sc, g, w)


def _rglru_kernel(n_chunks, u_ref, up_ref, un_ref, cw_ref, cb_ref, wg_ref, bg_ref, lam_ref, h0_ref,
                  h_ref, a_s, b_s, h_s, state):
    tc, pitch = SCAN_CHUNK, SCAN_PITCH
    d = pl.program_id(0)
    k = pl.program_id(2)
    kk = k + d * (n_chunks - 1 - 2 * k)

    @pl.when(k == 0)
    def _():
        state[...] = h0_ref[0]

    lam = lam_ref[0]
    neg_lam = -lam
    softplus = jnp.maximum(neg_lam, 0.0) + jnp.log1p(jnp.exp(-jnp.abs(neg_lam)))
    has_prev = kk > 0
    has_next = kk < n_chunks - 1
    row = lax.broadcasted_iota(jnp.int32, (tc, D_LRU), 0)
    cw = cw_ref[...]

    for i in range(SCAN_BATCH):
        u = u_ref[i]
        prev = jnp.where(has_prev, up_ref[i], 0.0)
        nxt = jnp.where(has_next, un_ref[i], 0.0)
        p2, p1, n1 = prev[SUBLANES - 2:SUBLANES - 1], prev[SUBLANES - 1:SUBLANES], nxt[0:1]
        um1 = jnp.where(row == 0, p1, pltpu.roll(u, 1, 0))
        um2 = jnp.where(row == 0, p2, jnp.where(row == 1, p1, pltpu.roll(u, 2, 0)))
        up1 = jnp.where(row == tc - 1, n1, pltpu.roll(u, tc - 1, 0))
        xc = um2 * cw[0:1] + um1 * cw[1:2] + u * cw[2:3] + up1 * cw[3:4] + cb_ref[...]
        gz = jnp.dot(xc.astype(BF16), wg_ref[0], preferred_element_type=F32) + bg_ref[0]
        r = jax.nn.sigmoid(gz[:, :D_LRU])
        ig = jax.nn.sigmoid(gz[:, D_LRU:])
        log_a = -LRU_C * r * softplus
        a = jnp.exp(log_a)
        b = jnp.sqrt((1.0 - a) * (1.0 + a)) * (ig * xc)
        for j in range(N_SLABS):
            a_s[j, pl.ds(i * pitch, tc), :] = a[:, j * LANES:(j + 1) * LANES]
            b_s[j, pl.ds(i * pitch, tc), :] = b[:, j * LANES:(j + 1) * LANES]

    def step(s, h):
        t = s + d * (tc - 1 - 2 * s)
        out = []
        for j in range(N_SLABS):
            rows = pl.ds(t, SCAN_BATCH, stride=pitch)
            hj = a_s[j, rows, :] * h[j] + b_s[j, rows, :]
            h_s[j, rows, :] = hj
            out.append(hj)
        return tuple(out)

    st = state[...]
    h = lax.fori_loop(0, tc, step, tuple(st[:, j * LANES:(j + 1) * LANES] for j in range(N_SLABS)),
                      unroll=4)
    for j in range(N_SLABS):
        state[:, j * LANES:(j + 1) * LANES] = h[j]
    for i in range(SCAN_BATCH):
        for j in range(N_SLABS):
            h_ref[0, i, :, j * LANES:(j + 1) * LANES] = h_s[j, pl.ds(i * pitch, tc), :]


def _rglru_call(u3, conv_w, conv_b, wg, bg, lam, h0):
    batch, seq, _ = u3.shape
    n_chunks = seq // SCAN_CHUNK
    halo_blocks = SCAN_CHUNK // SUBLANES
    last_halo = seq // SUBLANES - 1

    def chunk(d, k):
        return k + d * (n_chunks - 1 - 2 * k)

    scratch = pltpu.VMEM((N_SLABS, SCAN_BATCH * SCAN_PITCH, LANES), F32)
    return pl.pallas_call(
        functools.partial(_rglru_kernel, n_chunks),
        out_shape=jax.ShapeDtypeStruct((2, batch, seq, D_LRU), F32),
        grid=(2, batch // SCAN_BATCH, n_chunks),
        in_specs=[
            pl.BlockSpec((SCAN_BATCH, SCAN_CHUNK, D_LRU), lambda d, g, k: (g, chunk(d, k), 0)),
            pl.BlockSpec((SCAN_BATCH, SUBLANES, D_LRU),
                         lambda d, g, k: (g, jnp.maximum(chunk(d, k) * halo_blocks - 1, 0), 0)),
            pl.BlockSpec((SCAN_BATCH, SUBLANES, D_LRU),
                         lambda d, g, k: (g, jnp.minimum((chunk(d, k) + 1) * halo_blocks, last_halo), 0)),
            pl.BlockSpec((CONV_WIDTH, D_LRU), lambda d, g, k: (0, 0)),
            pl.BlockSpec((1, D_LRU), lambda d, g, k: (0, 0)),
            pl.BlockSpec((1, D_LRU, 2 * D_LRU), lambda d, g, k: (d, 0, 0)),
            pl.BlockSpec((1, 1, 2 * D_LRU), lambda d, g, k: (d, 0, 0)),
            pl.BlockSpec((1, 1, D_LRU), lambda d, g, k: (d, 0, 0)),
            pl.BlockSpec((1, SCAN_BATCH, D_LRU), lambda d, g, k: (d, g, 0)),
        ],
        out_specs=pl.BlockSpec((1, SCAN_BATCH, SCAN_CHUNK, D_LRU), lambda d, g, k: (d, g, chunk(d, k), 0)),
        scratch_shapes=[scratch, scratch, scratch, pltpu.VMEM((SCAN_BATCH, D_LRU), F32)],
        compiler_params=_params(("arbitrary", "arbitrary", "arbitrary")),
        name="rglru",
    )(u3, u3, u3, conv_w, conv_b, wg, bg, lam, h0)


def _fourier_kernel(m_ref, ab_ref, o_ref):
    seq2 = m_ref.shape[1]
    rhs = ab_ref[0].reshape(seq2, D_FOURIER)
    o_ref[0] = jnp.dot(m_ref[...], rhs, preferred_element_type=F32).astype(BF16)


def _fourier_call(m_pos, ab):
    batch, _, seq, _ = ab.shape
    return pl.pallas_call(
        _fourier_kernel,
        out_shape=jax.ShapeDtypeStruct((batch, seq, D_FOURIER), BF16),
        grid=(batch, seq // FOURIER_TILE),
        in_specs=[
            pl.BlockSpec((FOURIER_TILE, 2 * seq), lambda b, m: (m, 0)),
            pl.BlockSpec((1, 2, seq, D_FOURIER), lambda b, m: (b, 0, 0, 0)),
        ],
        out_specs=pl.BlockSpec((1, FOURIER_TILE, D_FOURIER), lambda b, m: (b, m, 0)),
        compiler_params=_params(("parallel", "parallel")),
        name="fourier",
    )(m_pos, ab)


def _route(logits):
    lane = lax.broadcasted_iota(jnp.int32, logits.shape, 1)
    neg = -jnp.inf
    gl = jnp.where(lane < N_GROUPS, logits, neg)
    gmax = jnp.max(gl, axis=1, keepdims=True)
    grp = jnp.min(jnp.where(gl == gmax, lane, ROUTE_LANES), axis=1, keepdims=True)
    p_grp = 1.0 / jnp.sum(jnp.exp(gl - gmax), axis=1, keepdims=True)
    e_lane = lane - N_GROUPS
    in_grp = (e_lane >= 0) & (e_lane < N_EXPERTS) & ((e_lane // EXPERTS_PER_GROUP) == grp)
    el = jnp.where(in_grp, logits, neg)
    t1 = jnp.max(el, axis=1, keepdims=True)
    i1 = jnp.min(jnp.where(el == t1, lane, ROUTE_LANES), axis=1, keepdims=True)
    el2 = jnp.where(lane == i1, neg, el)
    t2 = jnp.max(el2, axis=1, keepdims=True)
    i2 = jnp.min(jnp.where(el2 == t2, lane, ROUTE_LANES), axis=1, keepdims=True)
    e = jnp.exp(t2 - t1)
    w1 = p_grp / (1.0 + e)
    w2 = p_grp * e / (1.0 + e)
    out = jnp.where(lane == 0, (i1 - N_GROUPS).astype(F32), 0.0)
    out = jnp.where(lane == 1, (i2 - N_GROUPS).astype(F32), out)
    out = jnp.where(lane == 2, w1, out)
    return jnp.where(lane == 3, w2, out)


def _merge_kernel(x_ref, sh1_ref, sc1_ref, g1_ref, sh2_ref, sc2_ref, n1_ref, n2_ref, wgate_ref,
                  fm_ref, hf_ref, hb_ref, yr_ref, wfo_ref, wlo_ref, wout_ref, wr_ref, br_ref,
                  x1_ref, h2_ref, route_ref):
    x = x_ref[...]
    hx = _rms_modulate(x, n1_ref[...], sh1_ref[0], sc1_ref[0]).astype(BF16)
    gates = jax.nn.sigmoid(jnp.dot(hx, wgate_ref[...], preferred_element_type=F32))
    branch_f = jnp.dot(fm_ref[0], wfo_ref[...], preferred_element_type=F32)
    y = yr_ref[...].astype(F32)
    gelu = 0.5 * y * (1.0 + jnp.tanh(np.sqrt(2.0 / np.pi).astype(np.float32) * (y + 0.044715 * (y * y * y))))
    lr = ((hf_ref[0, 0] + hb_ref[0, 0]) * gelu).astype(BF16)
    branch_r = jnp.dot(lr, wlo_ref[...], preferred_element_type=F32)
    mixed = gates[:, :D_MODEL] * branch_f + gates[:, D_MODEL:] * branch_r
    mix = jnp.dot(mixed.astype(BF16), wout_ref[...], preferred_element_type=F32)
    x1 = x + g1_ref[0] * mix
    x1_ref[...] = x1
    h2 = _rms_modulate(x1, n2_ref[...], sh2_ref[0], sc2_ref[0])
    _store_token_tiled(h2_ref, h2)
    h2_hi = h2.astype(BF16)
    h2_lo = (h2 - h2_hi.astype(F32)).astype(BF16)
    parts = (jnp.dot(h2_hi, wr_ref[...], preferred_element_type=F32)
             + jnp.dot(h2_lo, wr_ref[...], preferred_element_type=F32))
    route_ref[...] = _route(parts[:, :ROUTE_LANES] + parts[:, ROUTE_LANES:] + br_ref[...])


def _merge_call(x2, mods, n1, n2, wgate, fm, h, yr, wfo, wlo, wout, wr, br, batch, seq):
    n = batch * seq
    tm = TOKEN_TILE
    tpb = seq // tm
    mod_spec = pl.BlockSpec((1, 1, D_MODEL), lambda i: (i // tpb, 0, 0))
    vec = pl.BlockSpec((1, D_MODEL), lambda i: (0, 0))
    tile = pl.BlockSpec((tm, D_MODEL), lambda i: (i, 0))

    def full(a):
        return pl.BlockSpec(a.shape, lambda i: (0,) * a.ndim)

    return pl.pallas_call(
        _merge_kernel,
        out_shape=(
            jax.ShapeDtypeStruct((n, D_MODEL), F32),
            jax.ShapeDtypeStruct((n * TOKEN_ROWS, LANES), U32),
            jax.ShapeDtypeStruct((n, ROUTE_LANES), F32),
        ),
        grid=(n // tm,),
        in_specs=[
            tile, mod_spec, mod_spec, mod_spec, mod_spec, mod_spec, vec, vec, full(wgate),
            pl.BlockSpec((1, tm, D_FOURIER), lambda i: (i // tpb, i % tpb, 0)),
            pl.BlockSpec((1, 1, tm, D_LRU), lambda i: (0, i // tpb, i % tpb, 0)),
            pl.BlockSpec((1, 1, tm, D_LRU), lambda i: (1, i // tpb, i % tpb, 0)),
            pl.BlockSpec((tm, D_LRU), lambda i: (i, 0)),
            full(wfo), full(wlo), full(wout), full(wr), full(br),
        ],
        out_specs=(tile, pl.BlockSpec((tm * TOKEN_ROWS, LANES), lambda i: (i, 0)),
                   pl.BlockSpec((tm, ROUTE_LANES), lambda i: (i, 0))),
        compiler_params=_params(("parallel",)),
        name="merge",
    )(x2, *mods, n1, n2, wgate, fm, h, h, yr, wfo, wlo, wout, wr, br)


def _token_rows(t, count=1):
    start = t * TOKEN_ROWS
    if not isinstance(start, int):
        start = pl.multiple_of(start, TOKEN_ROWS)
    return pl.ds(start, count * TOKEN_ROWS)


def _row_copy(src_ref, dst_ref, sem, src_tok, dst_tok):
    return pltpu.make_async_copy(src_ref.at[_token_rows(src_tok)], dst_ref.at[_token_rows(dst_tok)], sem)


def _moe_kernel(n_tokens, be_ref, nv_ref, rows_ref, h2_hbm, wg_ref, wu_ref, wd_ref, out_hbm,
                x0, x1, y0, y1, wg_b, wu_b, wd_b, gat_sem, sct_sem):
    i = pl.program_id(0)
    n_assign = n_tokens * TOP_K
    xs, ys = (x0, x1), (y0, y1)

    def gather_wait(s):
        pltpu.make_async_copy(h2_hbm.at[_token_rows(0, MOE_BLOCK)], xs[s], gat_sem.at[s]).wait()

    def scatter_wait(s):
        pltpu.make_async_copy(ys[s], out_hbm.at[_token_rows(0, MOE_BLOCK)], sct_sem.at[s]).wait()

    def issue_gather(b, s):
        base = (b + 1) * MOE_BLOCK
        for r in range(MOE_BLOCK):
            tok = rows_ref[base + r] & SRC_MASK
            _row_copy(h2_hbm, xs[s], gat_sem.at[s], tok, r).start(priority=r % 2)

    def issue_scatter(b, s):
        base = (b + 1) * MOE_BLOCK
        for r in range(MOE_BLOCK):
            dst = lax.shift_right_logical(rows_ref[base + r], SRC_BITS)
            _row_copy(ys[s], out_hbm, sct_sem.at[s], r, dst).start(priority=r % 2)

    active = nv_ref[i] > 0
    prev_active = (i > 0) & (nv_ref[jnp.maximum(i - 1, 0)] > 0)

    @pl.when(i == 0)
    def _():
        y0[...] = jnp.zeros(y0.shape, U32)
        y1[...] = jnp.zeros(y1.shape, U32)
        fills = [pltpu.make_async_copy(y0, out_hbm.at[_token_rows(n_assign + q * MOE_BLOCK, MOE_BLOCK)],
                                       sct_sem.at[min(q, 1)]) for q in range(N_DUMMY_ROWS // MOE_BLOCK)]
        for cp in fills:
            cp.start()
        for cp in fills[1:]:
            cp.wait()
        issue_gather(0, 0)

    @pl.when(active & ((i == 0) | (be_ref[i] != be_ref[jnp.maximum(i - 1, 0)])))
    def _():
        wg_b[...] = wg_ref[0].astype(BF16)
        wu_b[...] = wu_ref[0].astype(BF16)
        wd_b[...] = wd_ref[0].astype(BF16)

    for s in range(2):
        @pl.when(active & (i % 2 == s))
        def _(s=s):
            scatter_wait(s)
            gather_wait(s)
            issue_gather(i + 1, 1 - s)
            issue_scatter(i - 1, 1 - s)
            xb = _load_token_tiled(xs[s], MOE_BLOCK, BF16)
            hg = jnp.dot(xb, wg_b[...], preferred_element_type=F32)
            hu = jnp.dot(xb, wu_b[...], preferred_element_type=F32)
            hb = (hg * jax.nn.sigmoid(hg) * hu).astype(BF16)
            _store_token_tiled(ys[s], jnp.dot(hb, wd_b[...], preferred_element_type=F32))

        @pl.when(jnp.logical_not(active) & prev_active & (i % 2 == s))
        def _(s=s):
            gather_wait(s)
            issue_scatter(i - 1, 1 - s)
            scatter_wait(s)
            scatter_wait(1 - s)


def _moe_call(blk_e, blk_nv, rows, h2, w_g, w_u, w_d):
    n_steps = blk_e.shape[0]
    n_tokens = h2.shape[0] // TOKEN_ROWS
    any_spec = pl.BlockSpec(memory_space=pl.ANY)

    def expert(i, be, nv, rows):
        return (be[i], 0, 0)

    block_buf = pltpu.VMEM((MOE_BLOCK * TOKEN_ROWS, LANES), U32)
    grid_spec = pltpu.PrefetchScalarGridSpec(
        num_scalar_prefetch=3,
        grid=(n_steps,),
        in_specs=[
            any_spec,
            pl.BlockSpec((1, D_MODEL, D_EXPERT), expert),
            pl.BlockSpec((1, D_MODEL, D_EXPERT), expert),
            pl.BlockSpec((1, D_EXPERT, D_MODEL), expert),
        ],
        out_specs=any_spec,
        scratch_shapes=[
            block_buf, block_buf, block_buf, block_buf,
            pltpu.VMEM((D_MODEL, D_EXPERT), BF16),
            pltpu.VMEM((D_MODEL, D_EXPERT), BF16),
            pltpu.VMEM((D_EXPERT, D_MODEL), BF16),
            pltpu.SemaphoreType.DMA((2,)),
            pltpu.SemaphoreType.DMA((2,)),
        ],
    )
    return pl.pallas_call(
        functools.partial(_moe_kernel, n_tokens),
        out_shape=jax.ShapeDtypeStruct(((n_tokens * TOP_K + N_DUMMY_ROWS) * TOKEN_ROWS, LANES), U32),
        grid_spec=grid_spec,
        compiler_params=_params(("arbitrary",)),
        name="moe",
    )(blk_e, blk_nv, rows, h2, w_g, w_u, w_d)


def _final_kernel(x1_ref, y0_ref, y1_ref, route_ref, g2_ref, fg_ref, o_ref):
    route = route_ref[...]
    rows = route.shape[0]
    moe = (route[:, 2:3] * _load_token_tiled(y0_ref, rows, F32)
           + route[:, 3:4] * _load_token_tiled(y1_ref, rows, F32))
    x2 = x1_ref[...] + g2_ref[0] * moe
    o_ref[...] = x2 * lax.rsqrt(jnp.mean(x2 * x2, axis=-1, keepdims=True) + EPS) * fg_ref[...]


def _final_call(x1, out2, route, g2, fg, batch, seq):
    n = batch * seq
    tm = TOKEN_TILE
    tpb = seq // tm
    tile = pl.BlockSpec((tm, D_MODEL), lambda i: (i, 0))
    return pl.pallas_call(
        _final_kernel,
        out_shape=jax.ShapeDtypeStruct((n, D_MODEL), F32),
        grid=(n // tm,),
        in_specs=[
            tile,
            pl.BlockSpec((tm * TOKEN_ROWS, LANES), lambda i: (i, 0)),
            pl.BlockSpec((tm * TOKEN_ROWS, LANES), lambda i: (i + n // tm, 0)),
            pl.BlockSpec((tm, ROUTE_LANES), lambda i: (i, 0)),
            pl.BlockSpec((1, 1, D_MODEL), lambda i: (i // tpb, 0, 0)),
            pl.BlockSpec((1, D_MODEL), lambda i: (0, 0)),
        ],
        out_specs=tile,
        compiler_params=_params(("parallel",)),
        name="final",
    )(x1, out2, out2, route, g2, fg)


def _channel_dft():
    j = np.arange(FOURIER_GROUP_DIM)
    ang = 2.0 * np.pi * np.outer(j, j) / FOURIER_GROUP_DIM
    return np.concatenate([np.cos(ang), np.sin(ang)], axis=1).astype(np.float32)


def _position_dft(seq):
    rows = seq // GRID_W
    assert GRID_W % rows == 0
    r, c = np.divmod(np.arange(seq), GRID_W)
    phase = (np.outer(r, r) * (GRID_W // rows) + np.outer(c, c)) % GRID_W
    ang = 2.0 * np.pi * phase / GRID_W
    scale = 1.0 / np.sqrt(float(seq) * FOURIER_GROUP_DIM)
    return np.concatenate([np.cos(ang), -np.sin(ang)], axis=1) * scale


def _block_diag(w):
    heads, hd, _ = w.shape
    eye = jnp.eye(heads, dtype=w.dtype)
    return jnp.einsum('hij,hg->higj', w, eye).reshape(heads * hd, heads * hd)


def _gate_weights(w_a, w_x):
    return jnp.stack([jnp.concatenate([_block_diag(w_a[d]), _block_diag(w_x[d])], axis=1)
                      for d in range(2)]).astype(BF16)


def _dispatch(eid, n_tokens):
    n_assign = n_tokens * TOP_K
    n_blk = n_assign // MOE_BLOCK + N_EXPERTS
    n_rows = n_blk * MOE_BLOCK
    assert n_tokens <= SRC_MASK + 1 and (n_assign + N_DUMMY_ROWS) << SRC_BITS < 2 ** 32
    flat_e = eid.reshape(-1)
    experts = jnp.arange(N_EXPERTS, dtype=jnp.int32)
    counts = jnp.sum((flat_e[:, None] == experts[None, :]).astype(jnp.int32), axis=0)
    blocks = (counts + MOE_BLOCK - 1) // MOE_BLOCK
    n_pad = blocks * MOE_BLOCK - counts
    pad_j = jnp.arange(MOE_BLOCK, dtype=jnp.int32)
    pad_key = jnp.where(pad_j[None, :] < n_pad[:, None], 2 * experts[:, None] + 1, 2 * N_EXPERTS)
    keys = jnp.concatenate([2 * flat_e, pad_key.reshape(-1)])
    vals = jnp.concatenate([jnp.arange(n_assign, dtype=jnp.int32),
                            jnp.full((N_EXPERTS * MOE_BLOCK,), -1, jnp.int32)])
    _, a = lax.sort((keys, vals), num_keys=1)
    edge = jnp.full((MOE_BLOCK,), -1, jnp.int32)
    a = jnp.concatenate([edge, a, edge])
    r = jnp.arange(n_rows + 2 * MOE_BLOCK, dtype=jnp.int32)
    tok = lax.shift_right_logical(a, 1)
    dst = jnp.where(a >= 0, (a & 1) * n_tokens + tok, n_assign + r % N_DUMMY_ROWS)
    rows = lax.shift_left(dst, SRC_BITS) | jnp.where(a >= 0, tok, 0)

    bend = jnp.cumsum(blocks)
    b = jnp.arange(n_blk + 1, dtype=jnp.int32)
    owner = (b[:, None] >= bend[None, :]).astype(jnp.int32)
    blk_e = jnp.minimum(jnp.sum(owner, axis=1), N_EXPERTS - 1)
    onehot = (blk_e[:, None] == experts[None, :]).astype(jnp.int32)
    first = jnp.sum(onehot * (bend - blocks)[None, :], axis=1)
    blk_nv = jnp.clip(jnp.sum(onehot * counts[None, :], axis=1) - (b - first) * MOE_BLOCK, 0, MOE_BLOCK)
    return blk_e, blk_nv, rows


def kernel(x, c, ctx, c_ctx, w_mod, b_mod, norm1_g, w_in, conv_w, conv_b, lru_wa, lru_ba, lru_wx, lru_bx,
           lru_lam, w_fourier_out, w_lru_out, w_out, norm2_g, w_group, b_group, w_expert_router,
           b_expert_router, w_gate_e, w_up_e, w_down_e, final_g):
    batch, seq, _ = x.shape
    ctx_len = ctx.shape[1]
    n = batch * seq
    assert w_mod.shape[0] == 1, "single-layer stack only: the context stream is not carried across layers"
    x2 = x.reshape(n, D_MODEL)
    dft_ch = jnp.asarray(_channel_dft().astype(BF16))
    m_pos = jnp.asarray(_position_dft(seq).astype(BF16))

    for l in range(1):
        c_all = jnp.concatenate([c, c_ctx[None], jnp.zeros((MOD_ROWS - batch - 1, D_MODEL), F32)], axis=0)
        mod = _mod_call(c_all, w_mod[l], b_mod[l][None])
        sh1, sc1, g1, sh2, sc2, g2 = [m[:batch, None, :] for m in jnp.split(mod, 6, axis=-1)]
        csh1, csc1 = mod[batch:batch + 1, :D_MODEL], mod[batch:batch + 1, D_MODEL:2 * D_MODEL]

        w_in_b = w_in[l].astype(BF16)
        n1 = norm1_g[l][None]
        wg = _gate_weights(lru_wa[l], lru_wx[l])
        bg = jnp.concatenate([lru_ba[l], lru_bx[l]], axis=-1)[:, None, :]
        lam = lru_lam[l][:, None, :]
        cb = conv_b[l][None]

        uc = _inproj_ctx_call(ctx.reshape(batch * ctx_len, D_MODEL), csh1, csc1, n1,
                              w_in_b[:, D_FOURIER:D_FOURIER + D_LRU])
        hc = _rglru_call(uc.reshape(batch, ctx_len, D_LRU), conv_w[l], cb, wg, bg, lam,
                         jnp.zeros((2, batch, D_LRU), F32))
        h0 = jnp.stack([hc[0, :, -1], hc[1, :, 0]])

        ab, ur, yr = _inproj_call(x2, sh1, sc1, n1, w_in_b[:, :D_FOURIER + 2 * D_LRU], dft_ch, batch, seq)
        h = _rglru_call(ur.reshape(batch, seq, D_LRU), conv_w[l], cb, wg, bg, lam, h0)
        fm = _fourier_call(m_pos, ab)

        w_route = jnp.zeros((D_MODEL, ROUTE_LANES), F32)
        w_route = w_route.at[:, :N_GROUPS].set(w_group[l]).at[:, N_GROUPS:N_GROUPS + N_EXPERTS].set(
            w_expert_router[l])
        b_route = jnp.zeros((1, ROUTE_LANES), F32)
        b_route = b_route.at[0, :N_GROUPS].set(b_group[l]).at[0, N_GROUPS:N_GROUPS + N_EXPERTS].set(
            b_expert_router[l])
        w_route_hi = w_route.astype(BF16)
        w_route_lo = (w_route - w_route_hi.astype(F32)).astype(BF16)
        w_route = jnp.concatenate([w_route_hi, w_route_lo], axis=1)
        x1, h2, route = _merge_call(
            x2, (sh1, sc1, g1, sh2, sc2), n1, norm2_g[l][None], w_in_b[:, D_FOURIER + 2 * D_LRU:],
            fm, h, yr, w_fourier_out[l].astype(BF16), w_lru_out[l].astype(BF16), w_out[l].astype(BF16),
            w_route, b_route, batch, seq)

        eid = route[:, :TOP_K].astype(jnp.int32)
        blk_e, blk_nv, rows = _dispatch(eid, n)
        out2 = _moe_call(blk_e, blk_nv, rows, h2, w_gate_e[l], w_up_e[l], w_down_e[l])
        x2 = _final_call(x1, out2, route, g2, final_g[None], batch, seq)
    return x2.reshape(batch, seq, D_MODEL)
```

```python
import functools

import numpy as np
import jax
import jax.numpy as jnp
from jax import lax
from jax.experimental import pallas as pl
from jax.experimental.pallas import tpu as pltpu

F32 = jnp.float32
BF16 = jnp.bfloat16

LANES = 128
SUBLANES = 8
VMEM_LIMIT_BYTES = 56 * 1024 * 1024

D_MODEL = 1024
GRID_W = 64
EPS = 1e-6
N_FOURIER_GROUPS = 4
FOURIER_GROUP_DIM = 128
D_FOURIER = N_FOURIER_GROUPS * FOURIER_GROUP_DIM
N_LRU_HEADS = 8
LRU_HEAD_DIM = 64
D_LRU = N_LRU_HEADS * LRU_HEAD_DIM
CONV_WIDTH = 4
LRU_C = 8.0
N_GROUPS = 4
EXPERTS_PER_GROUP = 8
N_EXPERTS = N_GROUPS * EXPERTS_PER_GROUP
TOP_K = 2
D_EXPERT = 512

MOD_ROWS = 24
MOD_BLOCK_N = 1536
TOKEN_TILE = 512
SCAN_BATCH = SUBLANES
SCAN_CHUNK = 256
SCAN_PITCH = SCAN_CHUNK + SUBLANES
N_SLABS = D_LRU // LANES
FOURIER_TILE = 512
MOE_BLOCK = 256
ROUTE_LANES = LANES
N_DUMMY_ROWS = 4 * MOE_BLOCK
MOE_WINDOW = 8192


def _params(semantics):
    return pltpu.CompilerParams(dimension_semantics=semantics, vmem_limit_bytes=VMEM_LIMIT_BYTES)


U32 = jnp.uint32
HALF = D_MODEL // 2
TOKEN_ROWS = HALF // LANES
HIGH_HALF_WORD = np.uint32(0xFFFF0000)


def _pack_bf16_pairs(x):
    bits = pltpu.bitcast(x.astype(BF16).astype(F32), U32)
    return (bits[:, :HALF] >> 16) | (bits[:, HALF:] & HIGH_HALF_WORD)


def _unpack_bf16_pairs(u, dtype):
    lo = pltpu.bitcast(u << 16, F32)
    hi = pltpu.bitcast(u & HIGH_HALF_WORD, F32)
    return jnp.concatenate([lo, hi], axis=1).astype(dtype)


def _store_token_tiled(ref, x):
    rows = x.shape[0]
    packed = _pack_bf16_pairs(x)
    for s in range(TOKEN_ROWS):
        ref[pl.ds(s, rows, stride=TOKEN_ROWS), :] = packed[:, s * LANES:(s + 1) * LANES]


def _load_token_tiled(ref, rows, dtype):
    packed = jnp.concatenate([ref[pl.ds(s, rows, stride=TOKEN_ROWS), :] for s in range(TOKEN_ROWS)], axis=1)
    return _unpack_bf16_pairs(packed, dtype)


def _rms_modulate(x, g, shift, scale):
    y = x * lax.rsqrt(jnp.mean(x * x, axis=-1, keepdims=True) + EPS) * g
    return y * (1.0 + scale) + shift


def _mod_kernel(c_ref, w_ref, b_ref, o_ref):
    c = c_ref[...]
    s = c * jax.nn.sigmoid(c)
    o_ref[...] = jnp.dot(s, w_ref[...], preferred_element_type=F32,
                         precision=lax.Precision.HIGHEST) + b_ref[...]


def _mod_call(c_all, w_mod, b_mod):
    n_out = w_mod.shape[1]
    return pl.pallas_call(
        _mod_kernel,
        out_shape=jax.ShapeDtypeStruct((MOD_ROWS, n_out), F32),
        grid=(n_out // MOD_BLOCK_N,),
        in_specs=[
            pl.BlockSpec((MOD_ROWS, D_MODEL), lambda j: (0, 0)),
            pl.BlockSpec((D_MODEL, MOD_BLOCK_N), lambda j: (0, j)),
            pl.BlockSpec((1, MOD_BLOCK_N), lambda j: (0, j)),
        ],
        out_specs=pl.BlockSpec((MOD_ROWS, MOD_BLOCK_N), lambda j: (0, j)),
        compiler_params=_params(("arbitrary",)),
        name="mod",
    )(c_all, w_mod, b_mod)


def _inproj_kernel(x_ref, sh_ref, sc_ref, g_ref, w_ref, dft_ref, ab_ref, ur_ref, yr_ref):
    hx = _rms_modulate(x_ref[...], g_ref[...], sh_ref[0], sc_ref[0]).astype(BF16)
    proj = jnp.dot(hx, w_ref[...], preferred_element_type=F32)
    uf = proj[:, :D_FOURIER].astype(BF16)
    for g in range(N_FOURIER_GROUPS):
        lo, hi = g * FOURIER_GROUP_DIM, (g + 1) * FOURIER_GROUP_DIM
        cs = jnp.dot(uf[:, lo:hi], dft_ref[...], preferred_element_type=F32)
        ab_ref[0, 0, :, lo:hi] = cs[:, :FOURIER_GROUP_DIM].astype(BF16)
        ab_ref[0, 1, :, lo:hi] = cs[:, FOURIER_GROUP_DIM:].astype(BF16)
    ur_ref[...] = proj[:, D_FOURIER:D_FOURIER + D_LRU]
    yr_ref[...] = proj[:, D_FOURIER + D_LRU:].astype(BF16)


def _inproj_call(x2, sh, sc, g, w, dft_ch, batch, seq):
    n = batch * seq
    tpb = seq // TOKEN_TILE
    mod_spec = pl.BlockSpec((1, 1, D_MODEL), lambda i: (i // tpb, 0, 0))
    return pl.pallas_call(
        _inproj_kernel,
        out_shape=(
            jax.ShapeDtypeStruct((batch, 2, seq, D_FOURIER), BF16),
            jax.ShapeDtypeStruct((n, D_LRU), F32),
            jax.ShapeDtypeStruct((n, D_LRU), BF16),
        ),
        grid=(n // TOKEN_TILE,),
        in_specs=[
            pl.BlockSpec((TOKEN_TILE, D_MODEL), lambda i: (i, 0)),
            mod_spec, mod_spec,
            pl.BlockSpec((1, D_MODEL), lambda i: (0, 0)),
            pl.BlockSpec(w.shape, lambda i: (0, 0)),
            pl.BlockSpec(dft_ch.shape, lambda i: (0, 0)),
        ],
        out_specs=(
            pl.BlockSpec((1, 2, TOKEN_TILE, D_FOURIER), lambda i: (i // tpb, 0, i % tpb, 0)),
            pl.BlockSpec((TOKEN_TILE, D_LRU), lambda i: (i, 0)),
            pl.BlockSpec((TOKEN_TILE, D_LRU), lambda i: (i, 0)),
        ),
        compiler_params=_params(("parallel",)),
        name="inproj",
    )(x2, sh, sc, g, w, dft_ch)


def _inproj_ctx_kernel(x_ref, sh_ref, sc_ref, g_ref, w_ref, ur_ref):
    hx = _rms_modulate(x_ref[...], g_ref[...], sh_ref[...], sc_ref[...]).astype(BF16)
    ur_ref[...] = jnp.dot(hx, w_ref[...], preferred_element_type=F32)


def _inproj_ctx_call(ctx2, sh, sc, g, w):
    n = ctx2.shape[0]
    vec = pl.BlockSpec((1, D_MODEL), lambda i: (0, 0))
    return pl.pallas_call(
        _inproj_ctx_kernel,
        out_shape=jax.ShapeDtypeStruct((n, D_LRU), F32),
        grid=(n // TOKEN_TILE,),
        in_specs=[pl.BlockSpec((TOKEN_TILE, D_MODEL), lambda i: (i, 0)), vec, vec, vec,
                  pl.BlockSpec(w.shape, lambda i: (0, 0))],
        out_specs=pl.BlockSpec((TOKEN_TILE, D_LRU), lambda i: (i, 0)),
        compiler_params=_params(("parallel",)),
        name="inproj_ctx",
    )(ctx2, sh, sc, g, w)


def _rglru_kernel(n_chunks, u_ref, up_ref, un_ref, cw_ref, cb_ref, wg_ref, bg_ref, lam_ref, h0_ref,
                  h_ref, a_s, b_s, h_s, state):
    tc, pitch = SCAN_CHUNK, SCAN_PITCH
    d = pl.program_id(0)
    k = pl.program_id(2)
    kk = k + d * (n_chunks - 1 - 2 * k)

    @pl.when(k == 0)
    def _():
        state[...] = h0_ref[0]

    lam = lam_ref[0]
    neg_lam = -lam
    softplus = jnp.maximum(neg_lam, 0.0) + jnp.log1p(jnp.exp(-jnp.abs(neg_lam)))
    has_prev = kk > 0
    has_next = kk < n_chunks - 1
    row = lax.broadcasted_iota(jnp.int32, (tc, D_LRU), 0)
    cw = cw_ref[...]

    for i in range(SCAN_BATCH):
        u = u_ref[i]
        prev = jnp.where(has_prev, up_ref[i], 0.0)
        nxt = jnp.where(has_next, un_ref[i], 0.0)
        p2, p1, n1 = prev[SUBLANES - 2:SUBLANES - 1], prev[SUBLANES - 1:SUBLANES], nxt[0:1]
        um1 = jnp.where(row == 0, p1, pltpu.roll(u, 1, 0))
        um2 = jnp.where(row == 0, p2, jnp.where(row == 1, p1, pltpu.roll(u, 2, 0)))
        up1 = jnp.where(row == tc - 1, n1, pltpu.roll(u, tc - 1, 0))
        xc = um2 * cw[0:1] + um1 * cw[1:2] + u * cw[2:3] + up1 * cw[3:4] + cb_ref[...]
        gz = jnp.dot(xc.astype(BF16), wg_ref[0], preferred_element_type=F32) + bg_ref[0]
        r = jax.nn.sigmoid(gz[:, :D_LRU])
        ig = jax.nn.sigmoid(gz[:, D_LRU:])
        log_a = -LRU_C * r * softplus
        a = jnp.exp(log_a)
        b = jnp.sqrt((1.0 - a) * (1.0 + a)) * (ig * xc)
        for j in range(N_SLABS):
            a_s[j, pl.ds(i * pitch, tc), :] = a[:, j * LANES:(j + 1) * LANES]
            b_s[j, pl.ds(i * pitch, tc), :] = b[:, j * LANES:(j + 1) * LANES]

    def step(s, h):
        t = s + d * (tc - 1 - 2 * s)
        out = []
        for j in range(N_SLABS):
            rows = pl.ds(t, SCAN_BATCH, stride=pitch)
            hj = a_s[j, rows, :] * h[j] + b_s[j, rows, :]
            h_s[j, rows, :] = hj
            out.append(hj)
        return tuple(out)

    st = state[...]
    h = lax.fori_loop(0, tc, step, tuple(st[:, j * LANES:(j + 1) * LANES] for j in range(N_SLABS)),
                      unroll=4)
    for j in range(N_SLABS):
        state[:, j * LANES:(j + 1) * LANES] = h[j]
    for i in range(SCAN_BATCH):
        for j in range(N_SLABS):
            h_ref[0, i, :, j * LANES:(j + 1) * LANES] = h_s[j, pl.ds(i * pitch, tc), :]


def _rglru_call(u3, conv_w, conv_b, wg, bg, lam, h0):
    batch, seq, _ = u3.shape
    n_chunks = seq // SCAN_CHUNK
    halo_blocks = SCAN_CHUNK // SUBLANES
    last_halo = seq // SUBLANES - 1

    def chunk(d, k):
        return k + d * (n_chunks - 1 - 2 * k)

    scratch = pltpu.VMEM((N_SLABS, SCAN_BATCH * SCAN_PITCH, LANES), F32)
    return pl.pallas_call(
        functools.partial(_rglru_kernel, n_chunks),
        out_shape=jax.ShapeDtypeStruct((2, batch, seq, D_LRU), F32),
        grid=(2, batch // SCAN_BATCH, n_chunks),
        in_specs=[
            pl.BlockSpec((SCAN_BATCH, SCAN_CHUNK, D_LRU), lambda d, g, k: (g, chunk(d, k), 0)),
            pl.BlockSpec((SCAN_BATCH, SUBLANES, D_LRU),
                         lambda d, g, k: (g, jnp.maximum(chunk(d, k) * halo_blocks - 1, 0), 0)),
            pl.BlockSpec((SCAN_BATCH, SUBLANES, D_LRU),
                         lambda d, g, k: (g, jnp.minimum((chunk(d, k) + 1) * halo_blocks, last_halo), 0)),
            pl.BlockSpec((CONV_WIDTH, D_LRU), lambda d, g, k: (0, 0)),
            pl.BlockSpec((1, D_LRU), lambda d, g, k: (0, 0)),
            pl.BlockSpec((1, D_LRU, 2 * D_LRU), lambda d, g, k: (d, 0, 0)),
            pl.BlockSpec((1, 1, 2 * D_LRU), lambda d, g, k: (d, 0, 0)),
            pl.BlockSpec((1, 1, D_LRU), lambda d, g, k: (d, 0, 0)),
            pl.BlockSpec((1, SCAN_BATCH, D_LRU), lambda d, g, k: (d, g, 0)),
        ],
        out_specs=pl.BlockSpec((1, SCAN_BATCH, SCAN_CHUNK, D_LRU), lambda d, g, k: (d, g, chunk(d, k), 0)),
        scratch_shapes=[scratch, scratch, scratch, pltpu.VMEM((SCAN_BATCH, D_LRU), F32)],
        compiler_params=_params(("arbitrary", "arbitrary", "arbitrary")),
        name="rglru",
    )(u3, u3, u3, conv_w, conv_b, wg, bg, lam, h0)


def _fourier_kernel(m_ref, ab_ref, o_ref):
    seq2 = m_ref.shape[1]
    rhs = ab_ref[0].reshape(seq2, D_FOURIER)
    o_ref[0] = jnp.dot(m_ref[...], rhs, preferred_element_type=F32).astype(BF16)


def _fourier_call(m_pos, ab):
    batch, _, seq, _ = ab.shape
    return pl.pallas_call(
        _fourier_kernel,
        out_shape=jax.ShapeDtypeStruct((batch, seq, D_FOURIER), BF16),
        grid=(batch, seq // FOURIER_TILE),
        in_specs=[
            pl.BlockSpec((FOURIER_TILE, 2 * seq), lambda b, m: (m, 0)),
            pl.BlockSpec((1, 2, seq, D_FOURIER), lambda b, m: (b, 0, 0, 0)),
        ],
        out_specs=pl.BlockSpec((1, FOURIER_TILE, D_FOURIER), lambda b, m: (b, m, 0)),
        compiler_params=_params(("parallel", "parallel")),
        name="fourier",
    )(m_pos, ab)


def _route(logits):
    lane = lax.broadcasted_iota(jnp.int32, logits.shape, 1)
    neg = -jnp.inf
    gl = jnp.where(lane < N_GROUPS, logits, neg)
    gmax = jnp.max(gl, axis=1, keepdims=True)
    grp = jnp.min(jnp.where(gl == gmax, lane, ROUTE_LANES), axis=1, keepdims=True)
    p_grp = 1.0 / jnp.sum(jnp.exp(gl - gmax), axis=1, keepdims=True)
    e_lane = lane - N_GROUPS
    in_grp = (e_lane >= 0) & (e_lane < N_EXPERTS) & ((e_lane // EXPERTS_PER_GROUP) == grp)
    el = jnp.where(in_grp, logits, neg)
    t1 = jnp.max(el, axis=1, keepdims=True)
    i1 = jnp.min(jnp.where(el == t1, lane, ROUTE_LANES), axis=1, keepdims=True)
    el2 = jnp.where(lane == i1, neg, el)
    t2 = jnp.max(el2, axis=1, keepdims=True)
    i2 = jnp.min(jnp.where(el2 == t2, lane, ROUTE_LANES), axis=1, keepdims=True)
    e = jnp.exp(t2 - t1)
    w1 = p_grp / (1.0 + e)
    w2 = p_grp * e / (1.0 + e)
    out = jnp.where(lane == 0, (i1 - N_GROUPS).astype(F32), 0.0)
    out = jnp.where(lane == 1, (i2 - N_GROUPS).astype(F32), out)
    out = jnp.where(lane == 2, w1, out)
    return jnp.where(lane == 3, w2, out)


def _merge_kernel(x_ref, sh1_ref, sc1_ref, g1_ref, sh2_ref, sc2_ref, n1_ref, n2_ref, wgate_ref,
                  fm_ref, hf_ref, hb_ref, yr_ref, wfo_ref, wlo_ref, wout_ref, wr_ref, br_ref,
                  x1_ref, h2_ref, route_ref):
    x = x_ref[...]
    hx = _rms_modulate(x, n1_ref[...], sh1_ref[0], sc1_ref[0]).astype(BF16)
    gates = jax.nn.sigmoid(jnp.dot(hx, wgate_ref[...], preferred_element_type=F32))
    branch_f = jnp.dot(fm_ref[0], wfo_ref[...], preferred_element_type=F32)
    y = yr_ref[...].astype(F32)
    gelu = 0.5 * y * (1.0 + jnp.tanh(np.sqrt(2.0 / np.pi).astype(np.float32) * (y + 0.044715 * (y * y * y))))
    lr = ((hf_ref[0, 0] + hb_ref[0, 0]) * gelu).astype(BF16)
    branch_r = jnp.dot(lr, wlo_ref[...], preferred_element_type=F32)
    mixed = gates[:, :D_MODEL] * branch_f + gates[:, D_MODEL:] * branch_r
    mix = jnp.dot(mixed.astype(BF16), wout_ref[...], preferred_element_type=F32)
    x1 = x + g1_ref[0] * mix
    x1_ref[...] = x1
    h2 = _rms_modulate(x1, n2_ref[...], sh2_ref[0], sc2_ref[0])
    _store_token_tiled(h2_ref, h2)
    h2_hi = h2.astype(BF16)
    h2_lo = (h2 - h2_hi.astype(F32)).astype(BF16)
    parts = (jnp.dot(h2_hi, wr_ref[...], preferred_element_type=F32)
             + jnp.dot(h2_lo, wr_ref[...], preferred_element_type=F32))
    route_ref[...] = _route(parts[:, :ROUTE_LANES] + parts[:, ROUTE_LANES:] + br_ref[...])


def _merge_call(x2, mods, n1, n2, wgate, fm, h, yr, wfo, wlo, wout, wr, br, batch, seq):
    n = batch * seq
    tm = TOKEN_TILE
    tpb = seq // tm
    mod_spec = pl.BlockSpec((1, 1, D_MODEL), lambda i: (i // tpb, 0, 0))
    vec = pl.BlockSpec((1, D_MODEL), lambda i: (0, 0))
    tile = pl.BlockSpec((tm, D_MODEL), lambda i: (i, 0))

    def full(a):
        return pl.BlockSpec(a.shape, lambda i: (0,) * a.ndim)

    return pl.pallas_call(
        _merge_kernel,
        out_shape=(
            jax.ShapeDtypeStruct((n, D_MODEL), F32),
            jax.ShapeDtypeStruct((n * TOKEN_ROWS, LANES), U32),
            jax.ShapeDtypeStruct((n, ROUTE_LANES), F32),
        ),
        grid=(n // tm,),
        in_specs=[
            tile, mod_spec, mod_spec, mod_spec, mod_spec, mod_spec, vec, vec, full(wgate),
            pl.BlockSpec((1, tm, D_FOURIER), lambda i: (i // tpb, i % tpb, 0)),
            pl.BlockSpec((1, 1, tm, D_LRU), lambda i: (0, i // tpb, i % tpb, 0)),
            pl.BlockSpec((1, 1, tm, D_LRU), lambda i: (1, i // tpb, i % tpb, 0)),
            pl.BlockSpec((tm, D_LRU), lambda i: (i, 0)),
            full(wfo), full(wlo), full(wout), full(wr), full(br),
        ],
        out_specs=(tile, pl.BlockSpec((tm * TOKEN_ROWS, LANES), lambda i: (i, 0)),
                   pl.BlockSpec((tm, ROUTE_LANES), lambda i: (i, 0))),
        compiler_params=_params(("parallel",)),
        name="merge",
    )(x2, *mods, n1, n2, wgate, fm, h, h, yr, wfo, wlo, wout, wr, br)


def _token_rows(t, count=1):
    start = t * TOKEN_ROWS
    if not isinstance(start, int):
        start = pl.multiple_of(start, TOKEN_ROWS)
    return pl.ds(start, count * TOKEN_ROWS)


def _row_copy(src_ref, dst_ref, sem, src_tok, dst_tok):
    return pltpu.make_async_copy(src_ref.at[_token_rows(src_tok)], dst_ref.at[_token_rows(dst_tok)], sem)


def _moe_kernel(n_tokens, be_ref, bw_ref, off_ref, nv_ref, order_ref, h2_hbm, wg_ref, wu_ref, wd_ref, out_hbm,
                win, xbuf, y0, y1, wg_b, wu_b, wd_b, win_sem, sct_sem):
    i = pl.program_id(0)
    n_assign = n_tokens * TOP_K
    ys = (y0, y1)
    prev = jnp.maximum(i - 1, 0)

    def scatter_wait(s):
        pltpu.make_async_copy(ys[s], out_hbm.at[_token_rows(0, MOE_BLOCK)], sct_sem.at[s]).wait()

    def assignment(off, r):
        return order_ref[jnp.minimum(off + r, n_assign - 1)]

    def gather_rows():
        off, nv, tok0 = off_ref[i], nv_ref[i], bw_ref[i] * MOE_WINDOW
        for r in range(MOE_BLOCK):
            tok = lax.shift_right_logical(assignment(off, r), 1)
            local = jnp.where(r < nv, tok - tok0, 0)
            xbuf[_token_rows(r)] = win[_token_rows(local)]

    def issue_scatter_prev(s):
        off = off_ref[prev]
        nv = jnp.where(i > 0, nv_ref[prev], 0)
        dummy0 = n_assign + (prev % (N_DUMMY_ROWS // MOE_BLOCK)) * MOE_BLOCK
        for r in range(MOE_BLOCK):
            a = assignment(off, r)
            dst = (a & 1) * n_tokens + lax.shift_right_logical(a, 1)
            _row_copy(ys[s], out_hbm, sct_sem.at[s], r, jnp.where(r < nv, dst, dummy0 + r)).start(priority=r % 2)

    active = nv_ref[i] > 0
    prev_active = (i > 0) & (nv_ref[prev] > 0)

    @pl.when(i == 0)
    def _():
        y0[...] = jnp.zeros(y0.shape, U32)
        y1[...] = jnp.zeros(y1.shape, U32)
        fills = [pltpu.make_async_copy(y0, out_hbm.at[_token_rows(n_assign + q * MOE_BLOCK, MOE_BLOCK)],
                                       sct_sem.at[min(q, 1)]) for q in range(N_DUMMY_ROWS // MOE_BLOCK)]
        for cp in fills:
            cp.start()
        for cp in fills[1:]:
            cp.wait()

    @pl.when(active & ((i == 0) | (bw_ref[i] != bw_ref[prev])))
    def _():
        cp = pltpu.make_async_copy(h2_hbm.at[_token_rows(bw_ref[i] * MOE_WINDOW, MOE_WINDOW)], win, win_sem)
        cp.start()
        cp.wait()

    @pl.when(active & ((i == 0) | (be_ref[i] != be_ref[prev])))
    def _():
        wg_b[...] = wg_ref[0].astype(BF16)
        wu_b[...] = wu_ref[0].astype(BF16)
        wd_b[...] = wd_ref[0].astype(BF16)

    for s in range(2):
        @pl.when(active & (i % 2 == s))
        def _(s=s):
            scatter_wait(s)
            gather_rows()
            issue_scatter_prev(1 - s)
            xb = _load_token_tiled(xbuf, MOE_BLOCK, BF16)
            hg = jnp.dot(xb, wg_b[...], preferred_element_type=F32)
            hu = jnp.dot(xb, wu_b[...], preferred_element_type=F32)
            hb = (hg * jax.nn.sigmoid(hg) * hu).astype(BF16)
            _store_token_tiled(ys[s], jnp.dot(hb, wd_b[...], preferred_element_type=F32))

        @pl.when(jnp.logical_not(active) & prev_active & (i % 2 == s))
        def _(s=s):
            issue_scatter_prev(1 - s)
            scatter_wait(s)
            scatter_wait(1 - s)


def _moe_call(blk_e, blk_w, blk_off, blk_nv, order, h2, w_g, w_u, w_d):
    n_steps = blk_e.shape[0]
    n_tokens = h2.shape[0] // TOKEN_ROWS
    any_spec = pl.BlockSpec(memory_space=pl.ANY)

    def expert(i, be, bw, off, nv, order):
        return (be[i], 0, 0)

    block_buf = pltpu.VMEM((MOE_BLOCK * TOKEN_ROWS, LANES), U32)
    grid_spec = pltpu.PrefetchScalarGridSpec(
        num_scalar_prefetch=5,
        grid=(n_steps,),
        in_specs=[
            any_spec,
            pl.BlockSpec((1, D_MODEL, D_EXPERT), expert),
            pl.BlockSpec((1, D_MODEL, D_EXPERT), expert),
            pl.BlockSpec((1, D_EXPERT, D_MODEL), expert),
        ],
        out_specs=any_spec,
        scratch_shapes=[
            pltpu.VMEM((MOE_WINDOW * TOKEN_ROWS, LANES), U32),
            block_buf, block_buf, block_buf,
            pltpu.VMEM((D_MODEL, D_EXPERT), BF16),
            pltpu.VMEM((D_MODEL, D_EXPERT), BF16),
            pltpu.VMEM((D_EXPERT, D_MODEL), BF16),
            pltpu.SemaphoreType.DMA,
            pltpu.SemaphoreType.DMA((2,)),
        ],
    )
    return pl.pallas_call(
        functools.partial(_moe_kernel, n_tokens),
        out_shape=jax.ShapeDtypeStruct(((n_tokens * TOP_K + N_DUMMY_ROWS) * TOKEN_ROWS, LANES), U32),
        grid_spec=grid_spec,
        compiler_params=_params(("arbitrary",)),
        name="moe",
    )(blk_e, blk_w, blk_off, blk_nv, order, h2, w_g, w_u, w_d)


def _final_kernel(x1_ref, y0_ref, y1_ref, route_ref, g2_ref, fg_ref, o_ref):
    route = route_ref[...]
    rows = route.shape[0]
    moe = (route[:, 2:3] * _load_token_tiled(y0_ref, rows, F32)
           + route[:, 3:4] * _load_token_tiled(y1_ref, rows, F32))
    x2 = x1_ref[...] + g2_ref[0] * moe
    o_ref[...] = x2 * lax.rsqrt(jnp.mean(x2 * x2, axis=-1, keepdims=True) + EPS) * fg_ref[...]


def _final_call(x1, out2, route, g2, fg, batch, seq):
    n = batch * seq
    tm = TOKEN_TILE
    tpb = seq // tm
    tile = pl.BlockSpec((tm, D_MODEL), lambda i: (i, 0))
    return pl.pallas_call(
        _final_kernel,
        out_shape=jax.ShapeDtypeStruct((n, D_MODEL), F32),
        grid=(n // tm,),
        in_specs=[
            tile,
            pl.BlockSpec((tm * TOKEN_ROWS, LANES), lambda i: (i, 0)),
            pl.BlockSpec((tm * TOKEN_ROWS, LANES), lambda i: (i + n // tm, 0)),
            pl.BlockSpec((tm, ROUTE_LANES), lambda i: (i, 0)),
            pl.BlockSpec((1, 1, D_MODEL), lambda i: (i // tpb, 0, 0)),
            pl.BlockSpec((1, D_MODEL), lambda i: (0, 0)),
        ],
        out_specs=tile,
        compiler_params=_params(("parallel",)),
        name="final",
    )(x1, out2, out2, route, g2, fg)


def _channel_dft():
    j = np.arange(FOURIER_GROUP_DIM)
    ang = 2.0 * np.pi * np.outer(j, j) / FOURIER_GROUP_DIM
    return np.concatenate([np.cos(ang), np.sin(ang)], axis=1).astype(np.float32)


def _position_dft(seq):
    rows = seq // GRID_W
    assert GRID_W % rows == 0
    r, c = np.divmod(np.arange(seq), GRID_W)
    phase = (np.outer(r, r) * (GRID_W // rows) + np.outer(c, c)) % GRID_W
    ang = 2.0 * np.pi * phase / GRID_W
    scale = 1.0 / np.sqrt(float(seq) * FOURIER_GROUP_DIM)
    return np.concatenate([np.cos(ang), -np.sin(ang)], axis=1) * scale


def _block_diag(w):
    heads, hd, _ = w.shape
    eye = jnp.eye(heads, dtype=w.dtype)
    return jnp.einsum('hij,hg->higj', w, eye).reshape(heads * hd, heads * hd)


def _gate_weights(w_a, w_x):
    return jnp.stack([jnp.concatenate([_block_diag(w_a[d]), _block_diag(w_x[d])], axis=1)
                      for d in range(2)]).astype(BF16)


def _dispatch(eid, n_tokens):
    n_assign = n_tokens * TOP_K
    n_seg = (n_tokens // MOE_WINDOW) * N_EXPERTS
    n_blk = n_assign // MOE_BLOCK + n_seg
    a = jnp.arange(n_assign, dtype=jnp.int32)
    seg_of = (a // (TOP_K * MOE_WINDOW)) * N_EXPERTS + eid.reshape(-1)
    _, order = lax.sort((seg_of, a), num_keys=1)
    segs = jnp.arange(n_seg, dtype=jnp.int32)
    counts = jnp.sum((seg_of[:, None] == segs[None, :]).astype(jnp.int32), axis=0)
    start = jnp.cumsum(counts) - counts
    blocks = (counts + MOE_BLOCK - 1) // MOE_BLOCK
    bend = jnp.cumsum(blocks)
    b = jnp.arange(n_blk + 1, dtype=jnp.int32)
    ended = (b[:, None] >= bend[None, :]).astype(jnp.int32)
    seg = jnp.minimum(jnp.sum(ended, axis=1), n_seg - 1)
    onehot = (seg[:, None] == segs[None, :]).astype(jnp.int32)
    in_seg = (b - jnp.sum(onehot * (bend - blocks)[None, :], axis=1)) * MOE_BLOCK
    blk_off = jnp.sum(onehot * start[None, :], axis=1) + in_seg
    blk_nv = jnp.clip(jnp.sum(onehot * counts[None, :], axis=1) - in_seg, 0, MOE_BLOCK)
    return seg % N_EXPERTS, seg // N_EXPERTS, blk_off, blk_nv, order


def kernel(x, c, ctx, c_ctx, w_mod, b_mod, norm1_g, w_in, conv_w, conv_b, lru_wa, lru_ba, lru_wx, lru_bx,
           lru_lam, w_fourier_out, w_lru_out, w_out, norm2_g, w_group, b_group, w_expert_router,
           b_expert_router, w_gate_e, w_up_e, w_down_e, final_g):
    batch, seq, _ = x.shape
    ctx_len = ctx.shape[1]
    n = batch * seq
    assert w_mod.shape[0] == 1, "single-layer stack only: the context stream is not carried across layers"
    x2 = x.reshape(n, D_MODEL)
    dft_ch = jnp.asarray(_channel_dft().astype(BF16))
    m_pos = jnp.asarray(_position_dft(seq).astype(BF16))

    for l in range(1):
        c_all = jnp.concatenate([c, c_ctx[None], jnp.zeros((MOD_ROWS - batch - 1, D_MODEL), F32)], axis=0)
        mod = _mod_call(c_all, w_mod[l], b_mod[l][None])
        sh1, sc1, g1, sh2, sc2, g2 = [m[:batch, None, :] for m in jnp.split(mod, 6, axis=-1)]
        csh1, csc1 = mod[batch:batch + 1, :D_MODEL], mod[batch:batch + 1, D_MODEL:2 * D_MODEL]

        w_in_b = w_in[l].astype(BF16)
        n1 = norm1_g[l][None]
        wg = _gate_weights(lru_wa[l], lru_wx[l])
        bg = jnp.concatenate([lru_ba[l], lru_bx[l]], axis=-1)[:, None, :]
        lam = lru_lam[l][:, None, :]
        cb = conv_b[l][None]

        uc = _inproj_ctx_call(ctx.reshape(batch * ctx_len, D_MODEL), csh1, csc1, n1,
                              w_in_b[:, D_FOURIER:D_FOURIER + D_LRU])
        hc = _rglru_call(uc.reshape(batch, ctx_len, D_LRU), conv_w[l], cb, wg, bg, lam,
                         jnp.zeros((2, batch, D_LRU), F32))
        h0 = jnp.stack([hc[0, :, -1], hc[1, :, 0]])

        ab, ur, yr = _inproj_call(x2, sh1, sc1, n1, w_in_b[:, :D_FOURIER + 2 * D_LRU], dft_ch, batch, seq)
        h = _rglru_call(ur.reshape(batch, seq, D_LRU), conv_w[l], cb, wg, bg, lam, h0)
        fm = _fourier_call(m_pos, ab)

        w_route = jnp.zeros((D_MODEL, ROUTE_LANES), F32)
        w_route = w_route.at[:, :N_GROUPS].set(w_group[l]).at[:, N_GROUPS:N_GROUPS + N_EXPERTS].set(
            w_expert_router[l])
        b_route = jnp.zeros((1, ROUTE_LANES), F32)
        b_route = b_route.at[0, :N_GROUPS].set(b_group[l]).at[0, N_GROUPS:N_GROUPS + N_EXPERTS].set(
            b_expert_router[l])
        w_route_hi = w_route.astype(BF16)
        w_route_lo = (w_route - w_route_hi.astype(F32)).astype(BF16)
        w_route = jnp.concatenate([w_route_hi, w_route_lo], axis=1)
        x1, h2, route = _merge_call(
            x2, (sh1, sc1, g1, sh2, sc2), n1, norm2_g[l][None], w_in_b[:, D_FOURIER + 2 * D_LRU:],
            fm, h, yr, w_fourier_out[l].astype(BF16), w_lru_out[l].astype(BF16), w_out[l].astype(BF16),
            w_route, b_route, batch, seq)

        eid = route[:, :TOP_K].astype(jnp.int32)
        out2 = _moe_call(*_dispatch(eid, n), h2, w_gate_e[l], w_up_e[l], w_down_e[l])
        x2 = _final_call(x1, out2, route, g2, final_g[None], batch, seq)
    return x2.reshape(batch, seq, D_MODEL)
```

```python
import functools

import numpy as np
import jax
import jax.numpy as jnp
from jax import lax
from jax.experimental import pallas as pl
from jax.experimental.pallas import tpu as pltpu

F32 = jnp.float32
BF16 = jnp.bfloat16

LANES = 128
SUBLANES = 8
VMEM_LIMIT_BYTES = 56 * 1024 * 1024

D_MODEL = 1024
GRID_W = 64
EPS = 1e-6
N_FOURIER_GROUPS = 4
FOURIER_GROUP_DIM = 128
D_FOURIER = N_FOURIER_GROUPS * FOURIER_GROUP_DIM
N_LRU_HEADS = 8
LRU_HEAD_DIM = 64
D_LRU = N_LRU_HEADS * LRU_HEAD_DIM
CONV_WIDTH = 4
LRU_C = 8.0
N_GROUPS = 4
EXPERTS_PER_GROUP = 8
N_EXPERTS = N_GROUPS * EXPERTS_PER_GROUP
TOP_K = 2
D_EXPERT = 512

MOD_ROWS = 24
MOD_BLOCK_N = 1536
TOKEN_TILE = 512
SCAN_BATCH = SUBLANES
SCAN_CHUNK = 256
SCAN_PITCH = SCAN_CHUNK + SUBLANES
N_SLABS = D_LRU // LANES
FOURIER_TILE = 512
MOE_BLOCK = 256
ROUTE_LANES = LANES
N_DUMMY_ROWS = MOE_BLOCK
MOE_WINDOW = 8192
LOCAL_BITS = 15
LOCAL_MASK = (1 << LOCAL_BITS) - 1


def _params(semantics):
    return pltpu.CompilerParams(dimension_semantics=semantics, vmem_limit_bytes=VMEM_LIMIT_BYTES)


U32 = jnp.uint32
HALF = D_MODEL // 2
TOKEN_ROWS = HALF // LANES
HIGH_HALF_WORD = np.uint32(0xFFFF0000)


def _pack_bf16_pairs(x):
    bits = pltpu.bitcast(x.astype(BF16).astype(F32), U32)
    return (bits[:, :HALF] >> 16) | (bits[:, HALF:] & HIGH_HALF_WORD)


def _unpack_bf16_pairs(u, dtype):
    lo = pltpu.bitcast(u << 16, F32)
    hi = pltpu.bitcast(u & HIGH_HALF_WORD, F32)
    return jnp.concatenate([lo, hi], axis=1).astype(dtype)


def _store_token_tiled(ref, x):
    rows = x.shape[0]
    packed = _pack_bf16_pairs(x)
    for s in range(TOKEN_ROWS):
        ref[pl.ds(s, rows, stride=TOKEN_ROWS), :] = packed[:, s * LANES:(s + 1) * LANES]


def _load_token_tiled(ref, rows, dtype):
    packed = jnp.concatenate([ref[pl.ds(s, rows, stride=TOKEN_ROWS), :] for s in range(TOKEN_ROWS)], axis=1)
    return _unpack_bf16_pairs(packed, dtype)


def _rms_modulate(x, g, shift, scale):
    y = x * lax.rsqrt(jnp.mean(x * x, axis=-1, keepdims=True) + EPS) * g
    return y * (1.0 + scale) + shift


def _mod_kernel(c_ref, w_ref, b_ref, o_ref):
    c = c_ref[...]
    s = c * jax.nn.sigmoid(c)
    o_ref[...] = jnp.dot(s, w_ref[...], preferred_element_type=F32,
                         precision=lax.Precision.HIGHEST) + b_ref[...]


def _mod_call(c_all, w_mod, b_mod):
    n_out = w_mod.shape[1]
    return pl.pallas_call(
        _mod_kernel,
        out_shape=jax.ShapeDtypeStruct((MOD_ROWS, n_out), F32),
        grid=(n_out // MOD_BLOCK_N,),
        in_specs=[
            pl.BlockSpec((MOD_ROWS, D_MODEL), lambda j: (0, 0)),
            pl.BlockSpec((D_MODEL, MOD_BLOCK_N), lambda j: (0, j)),
            pl.BlockSpec((1, MOD_BLOCK_N), lambda j: (0, j)),
        ],
        out_specs=pl.BlockSpec((MOD_ROWS, MOD_BLOCK_N), lambda j: (0, j)),
        compiler_params=_params(("arbitrary",)),
        name="mod",
    )(c_all, w_mod, b_mod)


def _inproj_kernel(x_ref, sh_ref, sc_ref, g_ref, w_ref, dft_ref, ab_ref, ur_ref, yr_ref):
    hx = _rms_modulate(x_ref[...], g_ref[...], sh_ref[0], sc_ref[0]).astype(BF16)
    proj = jnp.dot(hx, w_ref[...], preferred_element_type=F32)
    uf = proj[:, :D_FOURIER].astype(BF16)
    for g in range(N_FOURIER_GROUPS):
        lo, hi = g * FOURIER_GROUP_DIM, (g + 1) * FOURIER_GROUP_DIM
        cs = jnp.dot(uf[:, lo:hi], dft_ref[...], preferred_element_type=F32)
        ab_ref[0, 0, :, lo:hi] = cs[:, :FOURIER_GROUP_DIM].astype(BF16)
        ab_ref[0, 1, :, lo:hi] = cs[:, FOURIER_GROUP_DIM:].astype(BF16)
    ur_ref[...] = proj[:, D_FOURIER:D_FOURIER + D_LRU]
    yr_ref[...] = proj[:, D_FOURIER + D_LRU:].astype(BF16)


def _inproj_call(x2, sh, sc, g, w, dft_ch, batch, seq):
    n = batch * seq
    tpb = seq // TOKEN_TILE
    mod_spec = pl.BlockSpec((1, 1, D_MODEL), lambda i: (i // tpb, 0, 0))
    return pl.pallas_call(
        _inproj_kernel,
        out_shape=(
            jax.ShapeDtypeStruct((batch, 2, seq, D_FOURIER), BF16),
            jax.ShapeDtypeStruct((n, D_LRU), F32),
            jax.ShapeDtypeStruct((n, D_LRU), BF16),
        ),
        grid=(n // TOKEN_TILE,),
        in_specs=[
            pl.BlockSpec((TOKEN_TILE, D_MODEL), lambda i: (i, 0)),
            mod_spec, mod_spec,
            pl.BlockSpec((1, D_MODEL), lambda i: (0, 0)),
            pl.BlockSpec(w.shape, lambda i: (0, 0)),
            pl.BlockSpec(dft_ch.shape, lambda i: (0, 0)),
        ],
        out_specs=(
            pl.BlockSpec((1, 2, TOKEN_TILE, D_FOURIER), lambda i: (i // tpb, 0, i % tpb, 0)),
            pl.BlockSpec((TOKEN_TILE, D_LRU), lambda i: (i, 0)),
            pl.BlockSpec((TOKEN_TILE, D_LRU), lambda i: (i, 0)),
        ),
        compiler_params=_params(("parallel",)),
        name="inproj",
    )(x2, sh, sc, g, w, dft_ch)


def _inproj_ctx_kernel(x_ref, sh_ref, sc_ref, g_ref, w_ref, ur_ref):
    hx = _rms_modulate(x_ref[...], g_ref[...], sh_ref[...], sc_ref[...]).astype(BF16)
    ur_ref[...] = jnp.dot(hx, w_ref[...], preferred_element_type=F32)


def _inproj_ctx_call(ctx2, sh, sc, g, w):
    n = ctx2.shape[0]
    vec = pl.BlockSpec((1, D_MODEL), lambda i: (0, 0))
    return pl.pallas_call(
        _inproj_ctx_kernel,
        out_shape=jax.ShapeDtypeStruct((n, D_LRU), F32),
        grid=(n // TOKEN_TILE,),
        in_specs=[pl.BlockSpec((TOKEN_TILE, D_MODEL), lambda i: (i, 0)), vec, vec, vec,
                  pl.BlockSpec(w.shape, lambda i: (0, 0))],
        out_specs=pl.BlockSpec((TOKEN_TILE, D_LRU), lambda i: (i, 0)),
        compiler_params=_params(("parallel",)),
        name="inproj_ctx",
    )(ctx2, sh, sc, g, w)


def _rglru_kernel(n_chunks, u_ref, up_ref, un_ref, cw_ref, cb_ref, wg_ref, bg_ref, lam_ref, h0_ref,
                  h_ref, a_s, b_s, h_s, state):
    tc, pitch = SCAN_CHUNK, SCAN_PITCH
    d = pl.program_id(0)
    k = pl.program_id(2)
    kk = k + d * (n_chunks - 1 - 2 * k)

    @pl.when(k == 0)
    def _():
        state[...] = h0_ref[0]

    lam = lam_ref[0]
    neg_lam = -lam
    softplus = jnp.maximum(neg_lam, 0.0) + jnp.log1p(jnp.exp(-jnp.abs(neg_lam)))
    has_prev = kk > 0
    has_next = kk < n_chunks - 1
    row = lax.broadcasted_iota(jnp.int32, (tc, D_LRU), 0)
    cw = cw_ref[...]

    for i in range(SCAN_BATCH):
        u = u_ref[i]
        prev = jnp.where(has_prev, up_ref[i], 0.0)
        nxt = jnp.where(has_next, un_ref[i], 0.0)
        p2, p1, n1 = prev[SUBLANES - 2:SUBLANES - 1], prev[SUBLANES - 1:SUBLANES], nxt[0:1]
        um1 = jnp.where(row == 0, p1, pltpu.roll(u, 1, 0))
        um2 = jnp.where(row == 0, p2, jnp.where(row == 1, p1, pltpu.roll(u, 2, 0)))
        up1 = jnp.where(row == tc - 1, n1, pltpu.roll(u, tc - 1, 0))
        xc = um2 * cw[0:1] + um1 * cw[1:2] + u * cw[2:3] + up1 * cw[3:4] + cb_ref[...]
        gz = jnp.dot(xc.astype(BF16), wg_ref[0], preferred_element_type=F32) + bg_ref[0]
        r = jax.nn.sigmoid(gz[:, :D_LRU])
        ig = jax.nn.sigmoid(gz[:, D_LRU:])
        log_a = -LRU_C * r * softplus
        a = jnp.exp(log_a)
        b = jnp.sqrt((1.0 - a) * (1.0 + a)) * (ig * xc)
        for j in range(N_SLABS):
            a_s[j, pl.ds(i * pitch, tc), :] = a[:, j * LANES:(j + 1) * LANES]
            b_s[j, pl.ds(i * pitch, tc), :] = b[:, j * LANES:(j + 1) * LANES]

    def step(s, h):
        t = s + d * (tc - 1 - 2 * s)
        out = []
        for j in range(N_SLABS):
            rows = pl.ds(t, SCAN_BATCH, stride=pitch)
            hj = a_s[j, rows, :] * h[j] + b_s[j, rows, :]
            h_s[j, rows, :] = hj
            out.append(hj)
        return tuple(out)

    st = state[...]
    h = lax.fori_loop(0, tc, step, tuple(st[:, j * LANES:(j + 1) * LANES] for j in range(N_SLABS)),
                      unroll=4)
    for j in range(N_SLABS):
        state[:, j * LANES:(j + 1) * LANES] = h[j]
    for i in range(SCAN_BATCH):
        for j in range(N_SLABS):
            h_ref[0, i, :, j * LANES:(j + 1) * LANES] = h_s[j, pl.ds(i * pitch, tc), :]


def _rglru_call(u3, conv_w, conv_b, wg, bg, lam, h0):
    batch, seq, _ = u3.shape
    n_chunks = seq // SCAN_CHUNK
    halo_blocks = SCAN_CHUNK // SUBLANES
    last_halo = seq // SUBLANES - 1

    def chunk(d, k):
        return k + d * (n_chunks - 1 - 2 * k)

    scratch = pltpu.VMEM((N_SLABS, SCAN_BATCH * SCAN_PITCH, LANES), F32)
    return pl.pallas_call(
        functools.partial(_rglru_kernel, n_chunks),
        out_shape=jax.ShapeDtypeStruct((2, batch, seq, D_LRU), F32),
        grid=(2, batch // SCAN_BATCH, n_chunks),
        in_specs=[
            pl.BlockSpec((SCAN_BATCH, SCAN_CHUNK, D_LRU), lambda d, g, k: (g, chunk(d, k), 0)),
            pl.BlockSpec((SCAN_BATCH, SUBLANES, D_LRU),
                         lambda d, g, k: (g, jnp.maximum(chunk(d, k) * halo_blocks - 1, 0), 0)),
            pl.BlockSpec((SCAN_BATCH, SUBLANES, D_LRU),
                         lambda d, g, k: (g, jnp.minimum((chunk(d, k) + 1) * halo_blocks, last_halo), 0)),
            pl.BlockSpec((CONV_WIDTH, D_LRU), lambda d, g, k: (0, 0)),
            pl.BlockSpec((1, D_LRU), lambda d, g, k: (0, 0)),
            pl.BlockSpec((1, D_LRU, 2 * D_LRU), lambda d, g, k: (d, 0, 0)),
            pl.BlockSpec((1, 1, 2 * D_LRU), lambda d, g, k: (d, 0, 0)),
            pl.BlockSpec((1, 1, D_LRU), lambda d, g, k: (d, 0, 0)),
            pl.BlockSpec((1, SCAN_BATCH, D_LRU), lambda d, g, k: (d, g, 0)),
        ],
        out_specs=pl.BlockSpec((1, SCAN_BATCH, SCAN_CHUNK, D_LRU), lambda d, g, k: (d, g, chunk(d, k), 0)),
        scratch_shapes=[scratch, scratch, scratch, pltpu.VMEM((SCAN_BATCH, D_LRU), F32)],
        compiler_params=_params(("arbitrary", "arbitrary", "arbitrary")),
        name="rglru",
    )(u3, u3, u3, conv_w, conv_b, wg, bg, lam, h0)


def _fourier_kernel(m_ref, ab_ref, o_ref):
    seq2 = m_ref.shape[1]
    rhs = ab_ref[0].reshape(seq2, D_FOURIER)
    o_ref[0] = jnp.dot(m_ref[...], rhs, preferred_element_type=F32).astype(BF16)


def _fourier_call(m_pos, ab):
    batch, _, seq, _ = ab.shape
    return pl.pallas_call(
        _fourier_kernel,
        out_shape=jax.ShapeDtypeStruct((batch, seq, D_FOURIER), BF16),
        grid=(batch, seq // FOURIER_TILE),
        in_specs=[
            pl.BlockSpec((FOURIER_TILE, 2 * seq), lambda b, m: (m, 0)),
            pl.BlockSpec((1, 2, seq, D_FOURIER), lambda b, m: (b, 0, 0, 0)),
        ],
        out_specs=pl.BlockSpec((1, FOURIER_TILE, D_FOURIER), lambda b, m: (b, m, 0)),
        compiler_params=_params(("parallel", "parallel")),
        name="fourier",
    )(m_pos, ab)


def _route(logits):
    lane = lax.broadcasted_iota(jnp.int32, logits.shape, 1)
    neg = -jnp.inf
    gl = jnp.where(lane < N_GROUPS, logits, neg)
    gmax = jnp.max(gl, axis=1, keepdims=True)
    grp = jnp.min(jnp.where(gl == gmax, lane, ROUTE_LANES), axis=1, keepdims=True)
    p_grp = 1.0 / jnp.sum(jnp.exp(gl - gmax), axis=1, keepdims=True)
    e_lane = lane - N_GROUPS
    in_grp = (e_lane >= 0) & (e_lane < N_EXPERTS) & ((e_lane // EXPERTS_PER_GROUP) == grp)
    el = jnp.where(in_grp, logits, neg)
    t1 = jnp.max(el, axis=1, keepdims=True)
    i1 = jnp.min(jnp.where(el == t1, lane, ROUTE_LANES), axis=1, keepdims=True)
    el2 = jnp.where(lane == i1, neg, el)
    t2 = jnp.max(el2, axis=1, keepdims=True)
    i2 = jnp.min(jnp.where(el2 == t2, lane, ROUTE_LANES), axis=1, keepdims=True)
    e = jnp.exp(t2 - t1)
    w1 = p_grp / (1.0 + e)
    w2 = p_grp * e / (1.0 + e)
    out = jnp.where(lane == 0, (i1 - N_GROUPS).astype(F32), 0.0)
    out = jnp.where(lane == 1, (i2 - N_GROUPS).astype(F32), out)
    out = jnp.where(lane == 2, w1, out)
    return jnp.where(lane == 3, w2, out)


def _merge_kernel(x_ref, sh1_ref, sc1_ref, g1_ref, sh2_ref, sc2_ref, n1_ref, n2_ref, wgate_ref,
                  fm_ref, hf_ref, hb_ref, yr_ref, wfo_ref, wlo_ref, wout_ref, wr_ref, br_ref,
                  x1_ref, h2_ref, route_ref):
    x = x_ref[...]
    hx = _rms_modulate(x, n1_ref[...], sh1_ref[0], sc1_ref[0]).astype(BF16)
    gates = jax.nn.sigmoid(jnp.dot(hx, wgate_ref[...], preferred_element_type=F32))
    branch_f = jnp.dot(fm_ref[0], wfo_ref[...], preferred_element_type=F32)
    y = yr_ref[...].astype(F32)
    gelu = 0.5 * y * (1.0 + jnp.tanh(np.sqrt(2.0 / np.pi).astype(np.float32) * (y + 0.044715 * (y * y * y))))
    lr = ((hf_ref[0, 0] + hb_ref[0, 0]) * gelu).astype(BF16)
    branch_r = jnp.dot(lr, wlo_ref[...], preferred_element_type=F32)
    mixed = gates[:, :D_MODEL] * branch_f + gates[:, D_MODEL:] * branch_r
    mix = jnp.dot(mixed.astype(BF16), wout_ref[...], preferred_element_type=F32)
    x1 = x + g1_ref[0] * mix
    x1_ref[...] = x1
    h2 = _rms_modulate(x1, n2_ref[...], sh2_ref[0], sc2_ref[0])
    _store_token_tiled(h2_ref, h2)
    h2_hi = h2.astype(BF16)
    h2_lo = (h2 - h2_hi.astype(F32)).astype(BF16)
    parts = (jnp.dot(h2_hi, wr_ref[...], preferred_element_type=F32)
             + jnp.dot(h2_lo, wr_ref[...], preferred_element_type=F32))
    route_ref[...] = _route(parts[:, :ROUTE_LANES] + parts[:, ROUTE_LANES:] + br_ref[...])


def _merge_call(x2, mods, n1, n2, wgate, fm, h, yr, wfo, wlo, wout, wr, br, batch, seq):
    n = batch * seq
    tm = TOKEN_TILE
    tpb = seq // tm
    mod_spec = pl.BlockSpec((1, 1, D_MODEL), lambda i: (i // tpb, 0, 0))
    vec = pl.BlockSpec((1, D_MODEL), lambda i: (0, 0))
    tile = pl.BlockSpec((tm, D_MODEL), lambda i: (i, 0))

    def full(a):
        return pl.BlockSpec(a.shape, lambda i: (0,) * a.ndim)

    return pl.pallas_call(
        _merge_kernel,
        out_shape=(
            jax.ShapeDtypeStruct((n, D_MODEL), F32),
            jax.ShapeDtypeStruct((n * TOKEN_ROWS, LANES), U32),
            jax.ShapeDtypeStruct((n, ROUTE_LANES), F32),
        ),
        grid=(n // tm,),
        in_specs=[
            tile, mod_spec, mod_spec, mod_spec, mod_spec, mod_spec, vec, vec, full(wgate),
            pl.BlockSpec((1, tm, D_FOURIER), lambda i: (i // tpb, i % tpb, 0)),
            pl.BlockSpec((1, 1, tm, D_LRU), lambda i: (0, i // tpb, i % tpb, 0)),
            pl.BlockSpec((1, 1, tm, D_LRU), lambda i: (1, i // tpb, i % tpb, 0)),
            pl.BlockSpec((tm, D_LRU), lambda i: (i, 0)),
            full(wfo), full(wlo), full(wout), full(wr), full(br),
        ],
        out_specs=(tile, pl.BlockSpec((tm * TOKEN_ROWS, LANES), lambda i: (i, 0)),
                   pl.BlockSpec((tm, ROUTE_LANES), lambda i: (i, 0))),
        compiler_params=_params(("parallel",)),
        name="merge",
    )(x2, *mods, n1, n2, wgate, fm, h, h, yr, wfo, wlo, wout, wr, br)


def _token_rows(t, count=1):
    start = t * TOKEN_ROWS
    if not isinstance(start, int):
        start = pl.multiple_of(start, TOKEN_ROWS)
    return pl.ds(start, count * TOKEN_ROWS)


def _row_copy(src_ref, dst_ref, sem, src_tok, dst_tok):
    return pltpu.make_async_copy(src_ref.at[_token_rows(src_tok)], dst_ref.at[_token_rows(dst_tok)], sem)


def _moe_kernel(n_tokens, be_ref, bw_ref, off_ref, nv_ref, nxt_ref, so_ref, rows_ref,
                h2_hbm, wg_hbm, wu_hbm, wd_hbm, out_hbm,
                win, xbuf, y0, y1, wg_f, wu_f, wd_f, wg_b, wu_b, wd_b, win_sem, w_sem, sct_sem):
    i = pl.program_id(0)
    n_assign = n_tokens * TOP_K
    ys = (y0, y1)
    prev = jnp.maximum(i - 1, 0)

    def scatter_wait(s):
        pltpu.make_async_copy(ys[s], out_hbm.at[_token_rows(0, MOE_BLOCK)], sct_sem.at[s]).wait()

    def weight_copies(e, slot):
        return [pltpu.make_async_copy(w.at[e], f.at[slot], w_sem.at[slot])
                for w, f in ((wg_hbm, wg_f), (wu_hbm, wu_f), (wd_hbm, wd_f))]

    def gather_rows():
        base = off_ref[i]
        for r in range(MOE_BLOCK):
            local = pl.multiple_of(rows_ref[base + r] & LOCAL_MASK, TOKEN_ROWS)
            xbuf[_token_rows(r)] = win[pl.ds(local, TOKEN_ROWS)]

    def issue_scatter_prev(s):
        base = jnp.where(i > 0, off_ref[prev], n_assign)
        for r in range(MOE_BLOCK):
            dst = lax.shift_right_logical(rows_ref[base + r], LOCAL_BITS)
            _row_copy(ys[s], out_hbm, sct_sem.at[s], r, dst).start(priority=r % 2)

    active = nv_ref[i] > 0
    prev_active = (i > 0) & (nv_ref[prev] > 0)
    w_slot = so_ref[i] % 2

    @pl.when(i == 0)
    def _():
        y0[...] = jnp.zeros(y0.shape, U32)
        y1[...] = jnp.zeros(y1.shape, U32)
        pltpu.make_async_copy(y0, out_hbm.at[_token_rows(n_assign, MOE_BLOCK)], sct_sem.at[0]).start()
        for cp in weight_copies(be_ref[0], 0):
            cp.start()

    @pl.when(active & ((i == 0) | (bw_ref[i] != bw_ref[prev])))
    def _():
        cp = pltpu.make_async_copy(h2_hbm.at[_token_rows(bw_ref[i] * MOE_WINDOW, MOE_WINDOW)], win, win_sem)
        cp.start()
        cp.wait()

    @pl.when(active & ((i == 0) | (so_ref[i] != so_ref[prev])))
    def _():
        for cp in weight_copies(be_ref[i], w_slot):
            cp.wait()
        wg_b[...] = wg_f[w_slot].astype(BF16)
        wu_b[...] = wu_f[w_slot].astype(BF16)
        wd_b[...] = wd_f[w_slot].astype(BF16)

        @pl.when(nxt_ref[i] >= 0)
        def _():
            for cp in weight_copies(nxt_ref[i], 1 - w_slot):
                cp.start()

    for s in range(2):
        @pl.when(active & (i % 2 == s))
        def _(s=s):
            scatter_wait(s)
            gather_rows()
            issue_scatter_prev(1 - s)
            xb = _load_token_tiled(xbuf, MOE_BLOCK, BF16)
            hg = jnp.dot(xb, wg_b[...], preferred_element_type=F32)
            hu = jnp.dot(xb, wu_b[...], preferred_element_type=F32)
            hb = (hg * jax.nn.sigmoid(hg) * hu).astype(BF16)
            _store_token_tiled(ys[s], jnp.dot(hb, wd_b[...], preferred_element_type=F32))

        @pl.when(jnp.logical_not(active) & prev_active & (i % 2 == s))
        def _(s=s):
            issue_scatter_prev(1 - s)
            scatter_wait(s)
            scatter_wait(1 - s)


def _moe_call(blk_e, blk_w, blk_off, blk_nv, blk_next_e, blk_seg, rows, h2, w_g, w_u, w_d):
    n_steps = blk_e.shape[0]
    n_tokens = h2.shape[0] // TOKEN_ROWS
    any_spec = pl.BlockSpec(memory_space=pl.ANY)
    block_buf = pltpu.VMEM((MOE_BLOCK * TOKEN_ROWS, LANES), U32)
    grid_spec = pltpu.PrefetchScalarGridSpec(
        num_scalar_prefetch=7,
        grid=(n_steps,),
        in_specs=[any_spec, any_spec, any_spec, any_spec],
        out_specs=any_spec,
        scratch_shapes=[
            pltpu.VMEM((MOE_WINDOW * TOKEN_ROWS, LANES), U32),
            block_buf, block_buf, block_buf,
            pltpu.VMEM((2, D_MODEL, D_EXPERT), F32),
            pltpu.VMEM((2, D_MODEL, D_EXPERT), F32),
            pltpu.VMEM((2, D_EXPERT, D_MODEL), F32),
            pltpu.VMEM((D_MODEL, D_EXPERT), BF16),
            pltpu.VMEM((D_MODEL, D_EXPERT), BF16),
            pltpu.VMEM((D_EXPERT, D_MODEL), BF16),
            pltpu.SemaphoreType.DMA,
            pltpu.SemaphoreType.DMA((2,)),
            pltpu.SemaphoreType.DMA((2,)),
        ],
    )
    return pl.pallas_call(
        functools.partial(_moe_kernel, n_tokens),
        out_shape=jax.ShapeDtypeStruct(((n_tokens * TOP_K + N_DUMMY_ROWS) * TOKEN_ROWS, LANES), U32),
        grid_spec=grid_spec,
        compiler_params=_params(("arbitrary",)),
        name="moe",
    )(blk_e, blk_w, blk_off, blk_nv, blk_next_e, blk_seg, rows, h2, w_g, w_u, w_d)


def _final_kernel(x1_ref, y0_ref, y1_ref, route_ref, g2_ref, fg_ref, o_ref):
    route = route_ref[...]
    rows = route.shape[0]
    moe = (route[:, 2:3] * _load_token_tiled(y0_ref, rows, F32)
           + route[:, 3:4] * _load_token_tiled(y1_ref, rows, F32))
    x2 = x1_ref[...] + g2_ref[0] * moe
    o_ref[...] = x2 * lax.rsqrt(jnp.mean(x2 * x2, axis=-1, keepdims=True) + EPS) * fg_ref[...]


def _final_call(x1, out2, route, g2, fg, batch, seq):
    n = batch * seq
    tm = TOKEN_TILE
    tpb = seq // tm
    tile = pl.BlockSpec((tm, D_MODEL), lambda i: (i, 0))
    return pl.pallas_call(
        _final_kernel,
        out_shape=jax.ShapeDtypeStruct((n, D_MODEL), F32),
        grid=(n // tm,),
        in_specs=[
            tile,
            pl.BlockSpec((tm * TOKEN_ROWS, LANES), lambda i: (i, 0)),
            pl.BlockSpec((tm * TOKEN_ROWS, LANES), lambda i: (i + n // tm, 0)),
            pl.BlockSpec((tm, ROUTE_LANES), lambda i: (i, 0)),
            pl.BlockSpec((1, 1, D_MODEL), lambda i: (i // tpb, 0, 0)),
            pl.BlockSpec((1, D_MODEL), lambda i: (0, 0)),
        ],
        out_specs=tile,
        compiler_params=_params(("parallel",)),
        name="final",
    )(x1, out2, out2, route, g2, fg)


def _channel_dft():
    j = np.arange(FOURIER_GROUP_DIM)
    ang = 2.0 * np.pi * np.outer(j, j) / FOURIER_GROUP_DIM
    return np.concatenate([np.cos(ang), np.sin(ang)], axis=1).astype(np.float32)


def _position_dft(seq):
    rows = seq // GRID_W
    assert GRID_W % rows == 0
    r, c = np.divmod(np.arange(seq), GRID_W)
    phase = (np.outer(r, r) * (GRID_W // rows) + np.outer(c, c)) % GRID_W
    ang = 2.0 * np.pi * phase / GRID_W
    scale = 1.0 / np.sqrt(float(seq) * FOURIER_GROUP_DIM)
    return np.concatenate([np.cos(ang), -np.sin(ang)], axis=1) * scale


def _block_diag(w):
    heads, hd, _ = w.shape
    eye = jnp.eye(heads, dtype=w.dtype)
    return jnp.einsum('hij,hg->higj', w, eye).reshape(heads * hd, heads * hd)


def _gate_weights(w_a, w_x):
    return jnp.stack([jnp.concatenate([_block_diag(w_a[d]), _block_diag(w_x[d])], axis=1)
                      for d in range(2)]).astype(BF16)


def _dispatch(eid, n_tokens):
    n_assign = n_tokens * TOP_K
    n_seg = (n_tokens // MOE_WINDOW) * N_EXPERTS
    n_blk = n_assign // MOE_BLOCK + n_seg
    a = jnp.arange(n_assign, dtype=jnp.int32)
    seg_of = (a // (TOP_K * MOE_WINDOW)) * N_EXPERTS + eid.reshape(-1)
    _, order = lax.sort((seg_of, a), num_keys=1)
    segs = jnp.arange(n_seg, dtype=jnp.int32)
    counts = jnp.sum((seg_of[:, None] == segs[None, :]).astype(jnp.int32), axis=0)
    start = jnp.cumsum(counts) - counts
    blocks = (counts + MOE_BLOCK - 1) // MOE_BLOCK
    bend = jnp.cumsum(blocks)
    b = jnp.arange(n_blk + 1, dtype=jnp.int32)
    ended = (b[:, None] >= bend[None, :]).astype(jnp.int32)
    seg = jnp.minimum(jnp.sum(ended, axis=1), n_seg - 1)
    onehot = (seg[:, None] == segs[None, :]).astype(jnp.int32)
    in_seg = (b - jnp.sum(onehot * (bend - blocks)[None, :], axis=1)) * MOE_BLOCK
    blk_off = jnp.sum(onehot * start[None, :], axis=1) + in_seg
    blk_nv = jnp.clip(jnp.sum(onehot * counts[None, :], axis=1) - in_seg, 0, MOE_BLOCK)
    seg_end = jnp.sum(onehot * bend[None, :], axis=1)
    follows = (b[None, :] == seg_end[:, None]).astype(jnp.int32)
    next_e = jnp.sum(follows * (seg % N_EXPERTS)[None, :], axis=1)
    next_nv = jnp.sum(follows * blk_nv[None, :], axis=1)
    blk_next_e = jnp.where(next_nv > 0, next_e, -1)

    tok = lax.shift_right_logical(order, 1)
    dst = (order & 1) * n_tokens + tok
    rows = lax.shift_left(dst, LOCAL_BITS) | ((tok % MOE_WINDOW) * TOKEN_ROWS)
    dummy = lax.shift_left(n_assign + jnp.arange(N_DUMMY_ROWS, dtype=jnp.int32), LOCAL_BITS)
    seg_ordinal = jnp.cumsum((blocks > 0).astype(jnp.int32)) - 1
    blk_seg = jnp.sum(onehot * seg_ordinal[None, :], axis=1)
    assert MOE_WINDOW * TOKEN_ROWS <= LOCAL_MASK + 1 and (n_assign + N_DUMMY_ROWS) << LOCAL_BITS <= 2 ** 32
    return seg % N_EXPERTS, seg // N_EXPERTS, blk_off, blk_nv, blk_next_e, blk_seg, jnp.concatenate([rows, dummy])


def kernel(x, c, ctx, c_ctx, w_mod, b_mod, norm1_g, w_in, conv_w, conv_b, lru_wa, lru_ba, lru_wx, lru_bx,
           lru_lam, w_fourier_out, w_lru_out, w_out, norm2_g, w_group, b_group, w_expert_router,
           b_expert_router, w_gate_e, w_up_e, w_down_e, final_g):
    batch, seq, _ = x.shape
    ctx_len = ctx.shape[1]
    n = batch * seq
    assert w_mod.shape[0] == 1, "single-layer stack only: the context stream is not carried across layers"
    x2 = x.reshape(n, D_MODEL)
    dft_ch = jnp.asarray(_channel_dft().astype(BF16))
    m_pos = jnp.asarray(_position_dft(seq).astype(BF16))

    for l in range(1):
        c_all = jnp.concatenate([c, c_ctx[None], jnp.zeros((MOD_ROWS - batch - 1, D_MODEL), F32)], axis=0)
        mod = _mod_call(c_all, w_mod[l], b_mod[l][None])
        sh1, sc1, g1, sh2, sc2, g2 = [m[:batch, None, :] for m in jnp.split(mod, 6, axis=-1)]
        csh1, csc1 = mod[batch:batch + 1, :D_MODEL], mod[batch:batch + 1, D_MODEL:2 * D_MODEL]

        w_in_b = w_in[l].astype(BF16)
        n1 = norm1_g[l][None]
        wg = _gate_weights(lru_wa[l], lru_wx[l])
        bg = jnp.concatenate([lru_ba[l], lru_bx[l]], axis=-1)[:, None, :]
        lam = lru_lam[l][:, None, :]
        cb = conv_b[l][None]

        uc = _inproj_ctx_call(ctx.reshape(batch * ctx_len, D_MODEL), csh1, csc1, n1,
                              w_in_b[:, D_FOURIER:D_FOURIER + D_LRU])
        hc = _rglru_call(uc.reshape(batch, ctx_len, D_LRU), conv_w[l], cb, wg, bg, lam,
                         jnp.zeros((2, batch, D_LRU), F32))
        h0 = jnp.stack([hc[0, :, -1], hc[1, :, 0]])

        ab, ur, yr = _inproj_call(x2, sh1, sc1, n1, w_in_b[:, :D_FOURIER + 2 * D_LRU], dft_ch, batch, seq)
        h = _rglru_call(ur.reshape(batch, seq, D_LRU), conv_w[l], cb, wg, bg, lam, h0)
        fm = _fourier_call(m_pos, ab)

        w_route = jnp.zeros((D_MODEL, ROUTE_LANES), F32)
        w_route = w_route.at[:, :N_GROUPS].set(w_group[l]).at[:, N_GROUPS:N_GROUPS + N_EXPERTS].set(
            w_expert_router[l])
        b_route = jnp.zeros((1, ROUTE_LANES), F32)
        b_route = b_route.at[0, :N_GROUPS].set(b_group[l]).at[0, N_GROUPS:N_GROUPS + N_EXPERTS].set(
            b_expert_router[l])
        w_route_hi = w_route.astype(BF16)
        w_route_lo = (w_route - w_route_hi.astype(F32)).astype(BF16)
        w_route = jnp.concatenate([w_route_hi, w_route_lo], axis=1)
        x1, h2, route = _merge_call(
            x2, (sh1, sc1, g1, sh2, sc2), n1, norm2_g[l][None], w_in_b[:, D_FOURIER + 2 * D_LRU:],
            fm, h, yr, w_fourier_out[l].astype(BF16), w_lru_out[l].astype(BF16), w_out[l].astype(BF16),
            w_route, b_route, batch, seq)

        eid = route[:, :TOP_K].astype(jnp.int32)
        out2 = _moe_call(*_dispatch(eid, n), h2, w_gate_e[l], w_up_e[l], w_down_e[l])
        x2 = _final_call(x1, out2, route, g2, final_g[None], batch, seq)
    return x2.reshape(batch, seq, D_MODEL)
```

```python
import functools

import numpy as np
import jax
import jax.numpy as jnp
from jax import lax
from jax.experimental import pallas as pl
from jax.experimental.pallas import tpu as pltpu

F32 = jnp.float32
BF16 = jnp.bfloat16

LANES = 128
SUBLANES = 8
VMEM_LIMIT_BYTES = 56 * 1024 * 1024

D_MODEL = 1024
GRID_W = 64
EPS = 1e-6
N_FOURIER_GROUPS = 4
FOURIER_GROUP_DIM = 128
D_FOURIER = N_FOURIER_GROUPS * FOURIER_GROUP_DIM
N_LRU_HEADS = 8
LRU_HEAD_DIM = 64
D_LRU = N_LRU_HEADS * LRU_HEAD_DIM
CONV_WIDTH = 4
LRU_C = 8.0
N_GROUPS = 4
EXPERTS_PER_GROUP = 8
N_EXPERTS = N_GROUPS * EXPERTS_PER_GROUP
TOP_K = 2
D_EXPERT = 512

MOD_ROWS = 24
MOD_BLOCK_N = 1536
TOKEN_TILE = 512
MERGE_ROW_GROUPS = 2
SCAN_BATCH = SUBLANES
SCAN_CHUNK = 256
SCAN_PITCH = SCAN_CHUNK + SUBLANES
N_SLABS = D_LRU // LANES
FOURIER_TILE = 512
MOE_BLOCK = 256
ROUTE_LANES = LANES
N_DUMMY_ROWS = MOE_BLOCK
MOE_WINDOW = 8192
LOCAL_BITS = 15
LOCAL_MASK = (1 << LOCAL_BITS) - 1


def _params(semantics):
    return pltpu.CompilerParams(dimension_semantics=semantics, vmem_limit_bytes=VMEM_LIMIT_BYTES)


U32 = jnp.uint32
HALF = D_MODEL // 2
TOKEN_ROWS = HALF // LANES
HIGH_HALF_WORD = np.uint32(0xFFFF0000)


def _pack_bf16_pairs(x):
    bits = pltpu.bitcast(x.astype(BF16).astype(F32), U32)
    return (bits[:, :HALF] >> 16) | (bits[:, HALF:] & HIGH_HALF_WORD)


def _unpack_bf16_pairs(u, dtype):
    lo = pltpu.bitcast(u << 16, F32)
    hi = pltpu.bitcast(u & HIGH_HALF_WORD, F32)
    return jnp.concatenate([lo, hi], axis=1).astype(dtype)


def _store_token_tiled(ref, x):
    rows = x.shape[0]
    packed = _pack_bf16_pairs(x)
    for s in range(TOKEN_ROWS):
        ref[pl.ds(s, rows, stride=TOKEN_ROWS), :] = packed[:, s * LANES:(s + 1) * LANES]


def _load_token_tiled(ref, rows, dtype):
    packed = jnp.concatenate([ref[pl.ds(s, rows, stride=TOKEN_ROWS), :] for s in range(TOKEN_ROWS)], axis=1)
    return _unpack_bf16_pairs(packed, dtype)


def _rms_modulate(x, g, shift, scale):
    y = x * lax.rsqrt(jnp.mean(x * x, axis=-1, keepdims=True) + EPS) * g
    return y * (1.0 + scale) + shift


def _mod_kernel(c_ref, w_ref, b_ref, o_ref):
    c = c_ref[...]
    s = c * jax.nn.sigmoid(c)
    o_ref[...] = jnp.dot(s, w_ref[...], preferred_element_type=F32,
                         precision=lax.Precision.HIGHEST) + b_ref[...]


def _mod_call(c_all, w_mod, b_mod):
    n_out = w_mod.shape[1]
    return pl.pallas_call(
        _mod_kernel,
        out_shape=jax.ShapeDtypeStruct((MOD_ROWS, n_out), F32),
        grid=(n_out // MOD_BLOCK_N,),
        in_specs=[
            pl.BlockSpec((MOD_ROWS, D_MODEL), lambda j: (0, 0)),
            pl.BlockSpec((D_MODEL, MOD_BLOCK_N), lambda j: (0, j)),
            pl.BlockSpec((1, MOD_BLOCK_N), lambda j: (0, j)),
        ],
        out_specs=pl.BlockSpec((MOD_ROWS, MOD_BLOCK_N), lambda j: (0, j)),
        compiler_params=_params(("arbitrary",)),
        name="mod",
    )(c_all, w_mod, b_mod)


def _inproj_kernel(x_ref, sh_ref, sc_ref, g_ref, w_ref, dft_ref, ab_ref, ur_ref, yr_ref):
    hx = _rms_modulate(x_ref[...], g_ref[...], sh_ref[0], sc_ref[0]).astype(BF16)
    proj = jnp.dot(hx, w_ref[...], preferred_element_type=F32)
    uf = proj[:, :D_FOURIER].astype(BF16)
    for g in range(N_FOURIER_GROUPS):
        lo, hi = g * FOURIER_GROUP_DIM, (g + 1) * FOURIER_GROUP_DIM
        cs = jnp.dot(uf[:, lo:hi], dft_ref[...], preferred_element_type=F32)
        ab_ref[0, 0, :, lo:hi] = cs[:, :FOURIER_GROUP_DIM].astype(BF16)
        ab_ref[0, 1, :, lo:hi] = cs[:, FOURIER_GROUP_DIM:].astype(BF16)
    ur_ref[...] = proj[:, D_FOURIER:D_FOURIER + D_LRU]
    yr_ref[...] = proj[:, D_FOURIER + D_LRU:].astype(BF16)


def _inproj_call(x2, sh, sc, g, w, dft_ch, batch, seq):
    n = batch * seq
    tpb = seq // TOKEN_TILE
    mod_spec = pl.BlockSpec((1, 1, D_MODEL), lambda i: (i // tpb, 0, 0))
    return pl.pallas_call(
        _inproj_kernel,
        out_shape=(
            jax.ShapeDtypeStruct((batch, 2, seq, D_FOURIER), BF16),
            jax.ShapeDtypeStruct((n, D_LRU), F32),
            jax.ShapeDtypeStruct((n, D_LRU), BF16),
        ),
        grid=(n // TOKEN_TILE,),
        in_specs=[
            pl.BlockSpec((TOKEN_TILE, D_MODEL), lambda i: (i, 0)),
            mod_spec, mod_spec,
            pl.BlockSpec((1, D_MODEL), lambda i: (0, 0)),
            pl.BlockSpec(w.shape, lambda i: (0, 0)),
            pl.BlockSpec(dft_ch.shape, lambda i: (0, 0)),
        ],
        out_specs=(
            pl.BlockSpec((1, 2, TOKEN_TILE, D_FOURIER), lambda i: (i // tpb, 0, i % tpb, 0)),
            pl.BlockSpec((TOKEN_TILE, D_LRU), lambda i: (i, 0)),
            pl.BlockSpec((TOKEN_TILE, D_LRU), lambda i: (i, 0)),
        ),
        compiler_params=_params(("parallel",)),
        name="inproj",
    )(x2, sh, sc, g, w, dft_ch)


def _inproj_ctx_kernel(x_ref, sh_ref, sc_ref, g_ref, w_ref, ur_ref):
    hx = _rms_modulate(x_ref[...], g_ref[...], sh_ref[...], sc_ref[...]).astype(BF16)
    ur_ref[...] = jnp.dot(hx, w_ref[...], preferred_element_type=F32)


def _inproj_ctx_call(ctx2, sh, sc, g, w):
    n = ctx2.shape[0]
    vec = pl.BlockSpec((1, D_MODEL), lambda i: (0, 0))
    return pl.pallas_call(
        _inproj_ctx_kernel,
        out_shape=jax.ShapeDtypeStruct((n, D_LRU), F32),
        grid=(n // TOKEN_TILE,),
        in_specs=[pl.BlockSpec((TOKEN_TILE, D_MODEL), lambda i: (i, 0)), vec, vec, vec,
                  pl.BlockSpec(w.shape, lambda i: (0, 0))],
        out_specs=pl.BlockSpec((TOKEN_TILE, D_LRU), lambda i: (i, 0)),
        compiler_params=_params(("parallel",)),
        name="inproj_ctx",
    )(ctx2, sh, sc, g, w)


def _rglru_kernel(n_chunks, u_ref, up_ref, un_ref, cw_ref, cb_ref, wg_ref, bg_ref, lam_ref, h0_ref,
                  h_ref, a_s, b_s, h_s, state):
    tc, pitch = SCAN_CHUNK, SCAN_PITCH
    d = pl.program_id(0)
    k = pl.program_id(2)
    kk = k + d * (n_chunks - 1 - 2 * k)

    @pl.when(k == 0)
    def _():
        state[...] = h0_ref[0]

    lam = lam_ref[0]
    neg_lam = -lam
    softplus = jnp.maximum(neg_lam, 0.0) + jnp.log1p(jnp.exp(-jnp.abs(neg_lam)))
    rate = (-0.5 * LRU_C * np.log2(np.e).astype(np.float32)) * softplus
    has_prev = kk > 0
    has_next = kk < n_chunks - 1
    row = lax.broadcasted_iota(jnp.int32, (SUBLANES, D_LRU), 0)
    cw = cw_ref[...]

    def patch(rolled, first_rows, fix):
        if first_rows:
            return jnp.concatenate([fix(rolled[:SUBLANES]), rolled[SUBLANES:]], axis=0)
        return jnp.concatenate([rolled[:-SUBLANES], fix(rolled[-SUBLANES:])], axis=0)

    for i in range(SCAN_BATCH):
        u = u_ref[i]
        prev = jnp.where(has_prev, up_ref[i], 0.0)
        nxt = jnp.where(has_next, un_ref[i], 0.0)
        p2, p1, n1 = prev[SUBLANES - 2:SUBLANES - 1], prev[SUBLANES - 1:SUBLANES], nxt[0:1]
        um1 = patch(pltpu.roll(u, 1, 0), True, lambda t: jnp.where(row == 0, p1, t))
        um2 = patch(pltpu.roll(u, 2, 0), True, lambda t: jnp.where(row == 0, p2, jnp.where(row == 1, p1, t)))
        up1 = patch(pltpu.roll(u, tc - 1, 0), False, lambda t: jnp.where(row == SUBLANES - 1, n1, t))
        xc = um2 * cw[0:1] + um1 * cw[1:2] + u * cw[2:3] + up1 * cw[3:4] + cb_ref[...]
        gz = jnp.dot(xc.astype(BF16), wg_ref[0], preferred_element_type=F32) + bg_ref[0]
        t_a = jnp.tanh(gz[:, :D_LRU])
        t_x = jnp.tanh(gz[:, D_LRU:])
        a = jnp.exp2(rate + rate * t_a)
        half_xc = 0.5 * xc
        b = jnp.sqrt(1.0 - a * a) * (half_xc + half_xc * t_x)
        for j in range(N_SLABS):
            a_s[j, pl.ds(i * pitch, tc), :] = a[:, j * LANES:(j + 1) * LANES]
            b_s[j, pl.ds(i * pitch, tc), :] = b[:, j * LANES:(j + 1) * LANES]

    def step(s, h):
        t = s + d * (tc - 1 - 2 * s)
        out = []
        for j in range(N_SLABS):
            rows = pl.ds(t, SCAN_BATCH, stride=pitch)
            hj = a_s[j, rows, :] * h[j] + b_s[j, rows, :]
            h_s[j, rows, :] = hj
            out.append(hj)
        return tuple(out)

    st = state[...]
    h = lax.fori_loop(0, tc, step, tuple(st[:, j * LANES:(j + 1) * LANES] for j in range(N_SLABS)),
                      unroll=4)
    for j in range(N_SLABS):
        state[:, j * LANES:(j + 1) * LANES] = h[j]
    for i in range(SCAN_BATCH):
        for j in range(N_SLABS):
            h_ref[0, i, :, j * LANES:(j + 1) * LANES] = h_s[j, pl.ds(i * pitch, tc), :]


def _rglru_call(u3, conv_w, conv_b, wg, bg, lam, h0):
    batch, seq, _ = u3.shape
    n_chunks = seq // SCAN_CHUNK
    halo_blocks = SCAN_CHUNK // SUBLANES
    last_halo = seq // SUBLANES - 1

    def chunk(d, k):
        return k + d * (n_chunks - 1 - 2 * k)

    scratch = pltpu.VMEM((N_SLABS, SCAN_BATCH * SCAN_PITCH, LANES), F32)
    return pl.pallas_call(
        functools.partial(_rglru_kernel, n_chunks),
        out_shape=jax.ShapeDtypeStruct((2, batch, seq, D_LRU), F32),
        grid=(2, batch // SCAN_BATCH, n_chunks),
        in_specs=[
            pl.BlockSpec((SCAN_BATCH, SCAN_CHUNK, D_LRU), lambda d, g, k: (g, chunk(d, k), 0)),
            pl.BlockSpec((SCAN_BATCH, SUBLANES, D_LRU),
                         lambda d, g, k: (g, jnp.maximum(chunk(d, k) * halo_blocks - 1, 0), 0)),
            pl.BlockSpec((SCAN_BATCH, SUBLANES, D_LRU),
                         lambda d, g, k: (g, jnp.minimum((chunk(d, k) + 1) * halo_blocks, last_halo), 0)),
            pl.BlockSpec((CONV_WIDTH, D_LRU), lambda d, g, k: (0, 0)),
            pl.BlockSpec((1, D_LRU), lambda d, g, k: (0, 0)),
            pl.BlockSpec((1, D_LRU, 2 * D_LRU), lambda d, g, k: (d, 0, 0)),
            pl.BlockSpec((1, 1, 2 * D_LRU), lambda d, g, k: (d, 0, 0)),
            pl.BlockSpec((1, 1, D_LRU), lambda d, g, k: (d, 0, 0)),
            pl.BlockSpec((1, SCAN_BATCH, D_LRU), lambda d, g, k: (d, g, 0)),
        ],
        out_specs=pl.BlockSpec((1, SCAN_BATCH, SCAN_CHUNK, D_LRU), lambda d, g, k: (d, g, chunk(d, k), 0)),
        scratch_shapes=[scratch, scratch, scratch, pltpu.VMEM((SCAN_BATCH, D_LRU), F32)],
        compiler_params=_params(("arbitrary", "arbitrary", "arbitrary")),
        name="rglru",
    )(u3, u3, u3, conv_w, conv_b, wg, bg, lam, h0)


def _fourier_kernel(m_ref, ab_ref, o_ref):
    seq2 = m_ref.shape[1]
    rhs = ab_ref[0].reshape(seq2, D_FOURIER)
    o_ref[0] = jnp.dot(m_ref[...], rhs, preferred_element_type=F32).astype(BF16)


def _fourier_call(m_pos, ab):
    batch, _, seq, _ = ab.shape
    return pl.pallas_call(
        _fourier_kernel,
        out_shape=jax.ShapeDtypeStruct((batch, seq, D_FOURIER), BF16),
        grid=(batch, seq // FOURIER_TILE),
        in_specs=[
            pl.BlockSpec((FOURIER_TILE, 2 * seq), lambda b, m: (m, 0)),
            pl.BlockSpec((1, 2, seq, D_FOURIER), lambda b, m: (b, 0, 0, 0)),
        ],
        out_specs=pl.BlockSpec((1, FOURIER_TILE, D_FOURIER), lambda b, m: (b, m, 0)),
        compiler_params=_params(("parallel", "parallel")),
        name="fourier",
    )(m_pos, ab)


def _route(logits):
    lane = lax.broadcasted_iota(jnp.int32, logits.shape, 1)
    neg = -jnp.inf
    gl = jnp.where(lane < N_GROUPS, logits, neg)
    gmax = jnp.max(gl, axis=1, keepdims=True)
    grp = jnp.min(jnp.where(gl == gmax, lane, ROUTE_LANES), axis=1, keepdims=True)
    p_grp = 1.0 / jnp.sum(jnp.exp(gl - gmax), axis=1, keepdims=True)
    e_lane = lane - N_GROUPS
    in_grp = (e_lane >= 0) & (e_lane < N_EXPERTS) & ((e_lane // EXPERTS_PER_GROUP) == grp)
    el = jnp.where(in_grp, logits, neg)
    t1 = jnp.max(el, axis=1, keepdims=True)
    i1 = jnp.min(jnp.where(el == t1, lane, ROUTE_LANES), axis=1, keepdims=True)
    el2 = jnp.where(lane == i1, neg, el)
    t2 = jnp.max(el2, axis=1, keepdims=True)
    i2 = jnp.min(jnp.where(el2 == t2, lane, ROUTE_LANES), axis=1, keepdims=True)
    e = jnp.exp(t2 - t1)
    w1 = p_grp / (1.0 + e)
    w2 = p_grp * e / (1.0 + e)
    out = jnp.where(lane == 0, (i1 - N_GROUPS).astype(F32), 0.0)
    out = jnp.where(lane == 1, (i2 - N_GROUPS).astype(F32), out)
    out = jnp.where(lane == 2, w1, out)
    return jnp.where(lane == 3, w2, out)


def _merge_kernel(x_ref, sh1_ref, sc1_ref, g1_ref, sh2_ref, sc2_ref, n1_ref, n2_ref, wgate_ref,
                  fm_ref, hf_ref, hb_ref, yr_ref, wfo_ref, wlo_ref, wout_ref, wr_ref, br_ref,
                  x1_ref, h2_ref, route_ref):
    group = x_ref.shape[0] // MERGE_ROW_GROUPS
    h2_groups, route_groups = [], []
    for q in range(MERGE_ROW_GROUPS):
        rows = pl.ds(q * group, group)
        x = x_ref[rows, :]
        hx = _rms_modulate(x, n1_ref[...], sh1_ref[0], sc1_ref[0]).astype(BF16)
        t = jnp.tanh(jnp.dot(hx, wgate_ref[...], preferred_element_type=F32))
        branch_f = jnp.dot(fm_ref[0, rows, :], wfo_ref[...], preferred_element_type=F32)
        y = yr_ref[rows, :].astype(F32)
        gelu = 0.5 * y * (1.0 + jnp.tanh(np.sqrt(2.0 / np.pi).astype(np.float32) * (y + 0.044715 * (y * y * y))))
        lr = ((hf_ref[0, 0, rows, :] + hb_ref[0, 0, rows, :]) * gelu).astype(BF16)
        branch_r = jnp.dot(lr, wlo_ref[...], preferred_element_type=F32)
        mixed = (branch_f + branch_r) + (t[:, :D_MODEL] * branch_f + t[:, D_MODEL:] * branch_r)
        mix = jnp.dot(mixed.astype(BF16), wout_ref[...], preferred_element_type=F32)
        x1 = x + g1_ref[0] * mix
        x1_ref[rows, :] = x1
        h2 = _rms_modulate(x1, n2_ref[...], sh2_ref[0], sc2_ref[0])
        h2_groups.append(h2)
        h2_hi = h2.astype(BF16)
        h2_lo = (h2 - h2_hi.astype(F32)).astype(BF16)
        parts = (jnp.dot(h2_hi, wr_ref[...], preferred_element_type=F32)
                 + jnp.dot(h2_lo, wr_ref[...], preferred_element_type=F32))
        route_groups.append(_route(parts[:, :ROUTE_LANES] + parts[:, ROUTE_LANES:] + br_ref[...]))
    _store_token_tiled(h2_ref, jnp.concatenate(h2_groups, axis=0))
    route_ref[...] = jnp.concatenate(route_groups, axis=0)


def _merge_call(x2, mods, n1, n2, wgate, fm, h, yr, wfo, wlo, wout, wr, br, batch, seq):
    n = batch * seq
    tm = TOKEN_TILE
    tpb = seq // tm
    mod_spec = pl.BlockSpec((1, 1, D_MODEL), lambda i: (i // tpb, 0, 0))
    vec = pl.BlockSpec((1, D_MODEL), lambda i: (0, 0))
    tile = pl.BlockSpec((tm, D_MODEL), lambda i: (i, 0))

    def full(a):
        return pl.BlockSpec(a.shape, lambda i: (0,) * a.ndim)

    return pl.pallas_call(
        _merge_kernel,
        out_shape=(
            jax.ShapeDtypeStruct((n, D_MODEL), F32),
            jax.ShapeDtypeStruct((n * TOKEN_ROWS, LANES), U32),
            jax.ShapeDtypeStruct((n, ROUTE_LANES), F32),
        ),
        grid=(n // tm,),
        in_specs=[
            tile, mod_spec, mod_spec, mod_spec, mod_spec, mod_spec, vec, vec, full(wgate),
            pl.BlockSpec((1, tm, D_FOURIER), lambda i: (i // tpb, i % tpb, 0)),
            pl.BlockSpec((1, 1, tm, D_LRU), lambda i: (0, i // tpb, i % tpb, 0)),
            pl.BlockSpec((1, 1, tm, D_LRU), lambda i: (1, i // tpb, i % tpb, 0)),
            pl.BlockSpec((tm, D_LRU), lambda i: (i, 0)),
            full(wfo), full(wlo), full(wout), full(wr), full(br),
        ],
        out_specs=(tile, pl.BlockSpec((tm * TOKEN_ROWS, LANES), lambda i: (i, 0)),
                   pl.BlockSpec((tm, ROUTE_LANES), lambda i: (i, 0))),
        compiler_params=_params(("parallel",)),
        name="merge",
    )(x2, *mods, n1, n2, wgate, fm, h, h, yr, wfo, wlo, wout, wr, br)


def _token_rows(t, count=1):
    start = t * TOKEN_ROWS
    if not isinstance(start, int):
        start = pl.multiple_of(start, TOKEN_ROWS)
    return pl.ds(start, count * TOKEN_ROWS)


def _row_copy(src_ref, dst_ref, sem, src_tok, dst_tok):
    return pltpu.make_async_copy(src_ref.at[_token_rows(src_tok)], dst_ref.at[_token_rows(dst_tok)], sem)


def _moe_kernel(n_tokens, be_ref, bw_ref, off_ref, nv_ref, nxt_ref, so_ref, rows_ref,
                h2_hbm, wg_hbm, wu_hbm, wd_hbm, out_hbm,
                win, xbuf, y0, y1, wg_f, wu_f, wd_f, wg_b, wu_b, wd_b, win_sem, w_sem, sct_sem):
    i = pl.program_id(0)
    n_assign = n_tokens * TOP_K
    ys = (y0, y1)
    prev = jnp.maximum(i - 1, 0)

    def scatter_wait(s):
        pltpu.make_async_copy(ys[s], out_hbm.at[_token_rows(0, MOE_BLOCK)], sct_sem.at[s]).wait()

    def weight_copies(e, slot):
        return [pltpu.make_async_copy(w.at[e], f.at[slot], w_sem.at[slot])
                for w, f in ((wg_hbm, wg_f), (wu_hbm, wu_f), (wd_hbm, wd_f))]

    def gather_rows():
        base = off_ref[i]
        for r in range(MOE_BLOCK):
            local = pl.multiple_of(rows_ref[base + r] & LOCAL_MASK, TOKEN_ROWS)
            xbuf[_token_rows(r)] = win[pl.ds(local, TOKEN_ROWS)]

    def issue_scatter_prev(s):
        base = jnp.where(i > 0, off_ref[prev], n_assign)
        for r in range(MOE_BLOCK):
            dst = lax.shift_right_logical(rows_ref[base + r], LOCAL_BITS)
            _row_copy(ys[s], out_hbm, sct_sem.at[s], r, dst).start(priority=r % 2)

    active = nv_ref[i] > 0
    prev_active = (i > 0) & (nv_ref[prev] > 0)
    w_slot = so_ref[i] % 2

    @pl.when(i == 0)
    def _():
        y0[...] = jnp.zeros(y0.shape, U32)
        y1[...] = jnp.zeros(y1.shape, U32)
        pltpu.make_async_copy(y0, out_hbm.at[_token_rows(n_assign, MOE_BLOCK)], sct_sem.at[0]).start()
        for cp in weight_copies(be_ref[0], 0):
            cp.start()

    @pl.when(active & ((i == 0) | (bw_ref[i] != bw_ref[prev])))
    def _():
        cp = pltpu.make_async_copy(h2_hbm.at[_token_rows(bw_ref[i] * MOE_WINDOW, MOE_WINDOW)], win, win_sem)
        cp.start()
        cp.wait()

    @pl.when(active & ((i == 0) | (so_ref[i] != so_ref[prev])))
    def _():
        for cp in weight_copies(be_ref[i], w_slot):
            cp.wait()
        wg_b[...] = wg_f[w_slot].astype(BF16)
        wu_b[...] = wu_f[w_slot].astype(BF16)
        wd_b[...] = wd_f[w_slot].astype(BF16)

        @pl.when(nxt_ref[i] >= 0)
        def _():
            for cp in weight_copies(nxt_ref[i], 1 - w_slot):
                cp.start()

    for s in range(2):
        @pl.when(active & (i % 2 == s))
        def _(s=s):
            scatter_wait(s)
            gather_rows()
            issue_scatter_prev(1 - s)
            xb = _load_token_tiled(xbuf, MOE_BLOCK, BF16)
            hg = jnp.dot(xb, wg_b[...], preferred_element_type=F32)
            hu = jnp.dot(xb, wu_b[...], preferred_element_type=F32)
            hb = (hg * jax.nn.sigmoid(hg) * hu).astype(BF16)
            _store_token_tiled(ys[s], jnp.dot(hb, wd_b[...], preferred_element_type=F32))

        @pl.when(jnp.logical_not(active) & prev_active & (i % 2 == s))
        def _(s=s):
            issue_scatter_prev(1 - s)
            scatter_wait(s)
            scatter_wait(1 - s)


def _moe_call(blk_e, blk_w, blk_off, blk_nv, blk_next_e, blk_seg, rows, h2, w_g, w_u, w_d):
    n_steps = blk_e.shape[0]
    n_tokens = h2.shape[0] // TOKEN_ROWS
    any_spec = pl.BlockSpec(memory_space=pl.ANY)
    block_buf = pltpu.VMEM((MOE_BLOCK * TOKEN_ROWS, LANES), U32)
    grid_spec = pltpu.PrefetchScalarGridSpec(
        num_scalar_prefetch=7,
        grid=(n_steps,),
        in_specs=[any_spec, any_spec, any_spec, any_spec],
        out_specs=any_spec,
        scratch_shapes=[
            pltpu.VMEM((MOE_WINDOW * TOKEN_ROWS, LANES), U32),
            block_buf, block_buf, block_buf,
            pltpu.VMEM((2, D_MODEL, D_EXPERT), F32),
            pltpu.VMEM((2, D_MODEL, D_EXPERT), F32),
            pltpu.VMEM((2, D_EXPERT, D_MODEL), F32),
            pltpu.VMEM((D_MODEL, D_EXPERT), BF16),
            pltpu.VMEM((D_MODEL, D_EXPERT), BF16),
            pltpu.VMEM((D_EXPERT, D_MODEL), BF16),
            pltpu.SemaphoreType.DMA,
            pltpu.SemaphoreType.DMA((2,)),
            pltpu.SemaphoreType.DMA((2,)),
        ],
    )
    return pl.pallas_call(
        functools.partial(_moe_kernel, n_tokens),
        out_shape=jax.ShapeDtypeStruct(((n_tokens * TOP_K + N_DUMMY_ROWS) * TOKEN_ROWS, LANES), U32),
        grid_spec=grid_spec,
        compiler_params=_params(("arbitrary",)),
        name="moe",
    )(blk_e, blk_w, blk_off, blk_nv, blk_next_e, blk_seg, rows, h2, w_g, w_u, w_d)


def _final_kernel(x1_ref, y0_ref, y1_ref, route_ref, g2_ref, fg_ref, o_ref):
    route = route_ref[...]
    rows = route.shape[0]
    moe = (route[:, 2:3] * _load_token_tiled(y0_ref, rows, F32)
           + route[:, 3:4] * _load_token_tiled(y1_ref, rows, F32))
    x2 = x1_ref[...] + g2_ref[0] * moe
    o_ref[...] = x2 * lax.rsqrt(jnp.mean(x2 * x2, axis=-1, keepdims=True) + EPS) * fg_ref[...]


def _final_call(x1, out2, route, g2, fg, batch, seq):
    n = batch * seq
    tm = TOKEN_TILE
    tpb = seq // tm
    tile = pl.BlockSpec((tm, D_MODEL), lambda i: (i, 0))
    return pl.pallas_call(
        _final_kernel,
        out_shape=jax.ShapeDtypeStruct((n, D_MODEL), F32),
        grid=(n // tm,),
        in_specs=[
            tile,
            pl.BlockSpec((tm * TOKEN_ROWS, LANES), lambda i: (i, 0)),
            pl.BlockSpec((tm * TOKEN_ROWS, LANES), lambda i: (i + n // tm, 0)),
            pl.BlockSpec((tm, ROUTE_LANES), lambda i: (i, 0)),
            pl.BlockSpec((1, 1, D_MODEL), lambda i: (i // tpb, 0, 0)),
            pl.BlockSpec((1, D_MODEL), lambda i: (0, 0)),
        ],
        out_specs=tile,
        compiler_params=_params(("parallel",)),
        name="final",
    )(x1, out2, out2, route, g2, fg)


def _channel_dft():
    j = np.arange(FOURIER_GROUP_DIM)
    ang = 2.0 * np.pi * np.outer(j, j) / FOURIER_GROUP_DIM
    return np.concatenate([np.cos(ang), np.sin(ang)], axis=1).astype(np.float32)


def _position_dft(seq):
    rows = seq // GRID_W
    assert GRID_W % rows == 0
    r, c = np.divmod(np.arange(seq), GRID_W)
    phase = (np.outer(r, r) * (GRID_W // rows) + np.outer(c, c)) % GRID_W
    ang = 2.0 * np.pi * phase / GRID_W
    scale = 1.0 / np.sqrt(float(seq) * FOURIER_GROUP_DIM)
    return np.concatenate([np.cos(ang), -np.sin(ang)], axis=1) * scale


def _block_diag(w):
    heads, hd, _ = w.shape
    eye = jnp.eye(heads, dtype=w.dtype)
    return jnp.einsum('hij,hg->higj', w, eye).reshape(heads * hd, heads * hd)


def _gate_weights(w_a, w_x):
    return jnp.stack([0.5 * jnp.concatenate([_block_diag(w_a[d]), _block_diag(w_x[d])], axis=1)
                      for d in range(2)]).astype(BF16)


def _dispatch(eid, n_tokens):
    n_assign = n_tokens * TOP_K
    n_seg = (n_tokens // MOE_WINDOW) * N_EXPERTS
    n_blk = n_assign // MOE_BLOCK + n_seg
    a = jnp.arange(n_assign, dtype=jnp.int32)
    seg_of = (a // (TOP_K * MOE_WINDOW)) * N_EXPERTS + eid.reshape(-1)
    _, order = lax.sort((seg_of, a), num_keys=1)
    segs = jnp.arange(n_seg, dtype=jnp.int32)
    counts = jnp.sum((seg_of[:, None] == segs[None, :]).astype(jnp.int32), axis=0)
    start = jnp.cumsum(counts) - counts
    blocks = (counts + MOE_BLOCK - 1) // MOE_BLOCK
    bend = jnp.cumsum(blocks)
    b = jnp.arange(n_blk + 1, dtype=jnp.int32)
    ended = (b[:, None] >= bend[None, :]).astype(jnp.int32)
    seg = jnp.minimum(jnp.sum(ended, axis=1), n_seg - 1)
    onehot = (seg[:, None] == segs[None, :]).astype(jnp.int32)
    in_seg = (b - jnp.sum(onehot * (bend - blocks)[None, :], axis=1)) * MOE_BLOCK
    blk_off = jnp.sum(onehot * start[None, :], axis=1) + in_seg
    blk_nv = jnp.clip(jnp.sum(onehot * counts[None, :], axis=1) - in_seg, 0, MOE_BLOCK)
    seg_end = jnp.sum(onehot * bend[None, :], axis=1)
    follows = (b[None, :] == seg_end[:, None]).astype(jnp.int32)
    next_e = jnp.sum(follows * (seg % N_EXPERTS)[None, :], axis=1)
    next_nv = jnp.sum(follows * blk_nv[None, :], axis=1)
    blk_next_e = jnp.where(next_nv > 0, next_e, -1)

    tok = lax.shift_right_logical(order, 1)
    dst = (order & 1) * n_tokens + tok
    rows = lax.shift_left(dst, LOCAL_BITS) | ((tok % MOE_WINDOW) * TOKEN_ROWS)
    dummy = lax.shift_left(n_assign + jnp.arange(N_DUMMY_ROWS, dtype=jnp.int32), LOCAL_BITS)
    seg_ordinal = jnp.cumsum((blocks > 0).astype(jnp.int32)) - 1
    blk_seg = jnp.sum(onehot * seg_ordinal[None, :], axis=1)
    assert MOE_WINDOW * TOKEN_ROWS <= LOCAL_MASK + 1 and (n_assign + N_DUMMY_ROWS) << LOCAL_BITS <= 2 ** 32
    return seg % N_EXPERTS, seg // N_EXPERTS, blk_off, blk_nv, blk_next_e, blk_seg, jnp.concatenate([rows, dummy])


def kernel(x, c, ctx, c_ctx, w_mod, b_mod, norm1_g, w_in, conv_w, conv_b, lru_wa, lru_ba, lru_wx, lru_bx,
           lru_lam, w_fourier_out, w_lru_out, w_out, norm2_g, w_group, b_group, w_expert_router,
           b_expert_router, w_gate_e, w_up_e, w_down_e, final_g):
    batch, seq, _ = x.shape
    ctx_len = ctx.shape[1]
    n = batch * seq
    assert w_mod.shape[0] == 1, "single-layer stack only: the context stream is not carried across layers"
    x2 = x.reshape(n, D_MODEL)
    dft_ch = jnp.asarray(_channel_dft().astype(BF16))
    m_pos = jnp.asarray(_position_dft(seq).astype(BF16))

    for l in range(1):
        c_all = jnp.concatenate([c, c_ctx[None], jnp.zeros((MOD_ROWS - batch - 1, D_MODEL), F32)], axis=0)
        mod = _mod_call(c_all, w_mod[l], b_mod[l][None])
        sh1, sc1, g1, sh2, sc2, g2 = [m[:batch, None, :] for m in jnp.split(mod, 6, axis=-1)]
        csh1, csc1 = mod[batch:batch + 1, :D_MODEL], mod[batch:batch + 1, D_MODEL:2 * D_MODEL]

        w_in_b = w_in[l].astype(BF16)
        n1 = norm1_g[l][None]
        wg = _gate_weights(lru_wa[l], lru_wx[l])
        bg = 0.5 * jnp.concatenate([lru_ba[l], lru_bx[l]], axis=-1)[:, None, :]
        lam = lru_lam[l][:, None, :]
        cb = conv_b[l][None]

        uc = _inproj_ctx_call(ctx.reshape(batch * ctx_len, D_MODEL), csh1, csc1, n1,
                              w_in_b[:, D_FOURIER:D_FOURIER + D_LRU])
        hc = _rglru_call(uc.reshape(batch, ctx_len, D_LRU), conv_w[l], cb, wg, bg, lam,
                         jnp.zeros((2, batch, D_LRU), F32))
        h0 = jnp.stack([hc[0, :, -1], hc[1, :, 0]])

        ab, ur, yr = _inproj_call(x2, sh1, sc1, n1, w_in_b[:, :D_FOURIER + 2 * D_LRU], dft_ch, batch, seq)
        h = _rglru_call(ur.reshape(batch, seq, D_LRU), conv_w[l], cb, wg, bg, lam, h0)
        fm = _fourier_call(m_pos, ab)

        w_route = jnp.zeros((D_MODEL, ROUTE_LANES), F32)
        w_route = w_route.at[:, :N_GROUPS].set(w_group[l]).at[:, N_GROUPS:N_GROUPS + N_EXPERTS].set(
            w_expert_router[l])
        b_route = jnp.zeros((1, ROUTE_LANES), F32)
        b_route = b_route.at[0, :N_GROUPS].set(b_group[l]).at[0, N_GROUPS:N_GROUPS + N_EXPERTS].set(
            b_expert_router[l])
        w_route_hi = w_route.astype(BF16)
        w_route_lo = (w_route - w_route_hi.astype(F32)).astype(BF16)
        w_route = jnp.concatenate([w_route_hi, w_route_lo], axis=1)
        x1, h2, route = _merge_call(
            x2, (sh1, sc1, g1, sh2, sc2), n1, norm2_g[l][None],
            (0.5 * w_in[l][:, D_FOURIER + 2 * D_LRU:]).astype(BF16),
            fm, h, yr, w_fourier_out[l].astype(BF16), w_lru_out[l].astype(BF16),
            (0.5 * w_out[l]).astype(BF16),
            w_route, b_route, batch, seq)

        eid = route[:, :TOP_K].astype(jnp.int32)
        out2 = _moe_call(*_dispatch(eid, n), h2, w_gate_e[l], w_up_e[l], w_down_e[l])
        x2 = _final_call(x1, out2, route, g2, final_g[None], batch, seq)
    return x2.reshape(batch, seq, D_MODEL)
```

```python
import functools

import numpy as np
import jax
import jax.numpy as jnp
from jax import lax
from jax.experimental import pallas as pl
from jax.experimental.pallas import tpu as pltpu

F32 = jnp.float32
BF16 = jnp.bfloat16

LANES = 128
SUBLANES = 8
VMEM_LIMIT_BYTES = 56 * 1024 * 1024

D_MODEL = 1024
GRID_W = 64
EPS = 1e-6
N_FOURIER_GROUPS = 4
FOURIER_GROUP_DIM = 128
D_FOURIER = N_FOURIER_GROUPS * FOURIER_GROUP_DIM
N_LRU_HEADS = 8
LRU_HEAD_DIM = 64
D_LRU = N_LRU_HEADS * LRU_HEAD_DIM
CONV_WIDTH = 4
LRU_C = 8.0
N_GROUPS = 4
EXPERTS_PER_GROUP = 8
N_EXPERTS = N_GROUPS * EXPERTS_PER_GROUP
TOP_K = 2
D_EXPERT = 512

MOD_ROWS = 24
MOD_BLOCK_N = 1536
TOKEN_TILE = 512
MERGE_GROUP_ROWS = 128
SCAN_BATCH = SUBLANES
SCAN_CHUNK = 256
SCAN_PITCH = SCAN_CHUNK + SUBLANES
N_SLABS = D_LRU // LANES
FOURIER_TILE = 512
MOE_BLOCK = 256
ROUTE_LANES = LANES
N_DUMMY_ROWS = MOE_BLOCK
MOE_WINDOW = 16384
LOCAL_BITS = 14
LOCAL_MASK = (1 << LOCAL_BITS) - 1


def _params(semantics):
    return pltpu.CompilerParams(dimension_semantics=semantics, vmem_limit_bytes=VMEM_LIMIT_BYTES)


U32 = jnp.uint32
HALF = D_MODEL // 2
TOKEN_ROWS = HALF // LANES
HIGH_HALF_WORD = np.uint32(0xFFFF0000)


def _pack_bf16_pairs(x):
    bits = pltpu.bitcast(x.astype(BF16).astype(F32), U32)
    return (bits[:, :HALF] >> 16) | (bits[:, HALF:] & HIGH_HALF_WORD)


def _unpack_bf16_pairs(u, dtype):
    lo = pltpu.bitcast(u << 16, F32)
    hi = pltpu.bitcast(u & HIGH_HALF_WORD, F32)
    return jnp.concatenate([lo, hi], axis=1).astype(dtype)


def _store_token_tiled(ref, x):
    rows = x.shape[0]
    packed = _pack_bf16_pairs(x)
    for s in range(TOKEN_ROWS):
        ref[pl.ds(s, rows, stride=TOKEN_ROWS), :] = packed[:, s * LANES:(s + 1) * LANES]


def _load_token_tiled(ref, rows, dtype):
    packed = jnp.concatenate([ref[pl.ds(s, rows, stride=TOKEN_ROWS), :] for s in range(TOKEN_ROWS)], axis=1)
    return _unpack_bf16_pairs(packed, dtype)


def _rms_modulate(x, g, shift, scale):
    y = x * lax.rsqrt(jnp.mean(x * x, axis=-1, keepdims=True) + EPS) * g
    return y * (1.0 + scale) + shift


def _mod_kernel(c_ref, w_ref, b_ref, o_ref):
    c = c_ref[...]
    s = c * jax.nn.sigmoid(c)
    o_ref[...] = jnp.dot(s, w_ref[...], preferred_element_type=F32,
                         precision=lax.Precision.HIGHEST) + b_ref[...]


def _mod_call(c_all, w_mod, b_mod):
    n_out = w_mod.shape[1]
    return pl.pallas_call(
        _mod_kernel,
        out_shape=jax.ShapeDtypeStruct((MOD_ROWS, n_out), F32),
        grid=(n_out // MOD_BLOCK_N,),
        in_specs=[
            pl.BlockSpec((MOD_ROWS, D_MODEL), lambda j: (0, 0)),
            pl.BlockSpec((D_MODEL, MOD_BLOCK_N), lambda j: (0, j)),
            pl.BlockSpec((1, MOD_BLOCK_N), lambda j: (0, j)),
        ],
        out_specs=pl.BlockSpec((MOD_ROWS, MOD_BLOCK_N), lambda j: (0, j)),
        compiler_params=_params(("arbitrary",)),
        name="mod",
    )(c_all, w_mod, b_mod)


def _inproj_kernel(x_ref, sh_ref, sc_ref, g_ref, w_ref, dft_ref, ab_ref, ur_ref, yr_ref):
    hx = _rms_modulate(x_ref[...], g_ref[...], sh_ref[0], sc_ref[0]).astype(BF16)
    proj = jnp.dot(hx, w_ref[...], preferred_element_type=F32)
    uf = proj[:, :D_FOURIER].astype(BF16)
    for g in range(N_FOURIER_GROUPS):
        lo, hi = g * FOURIER_GROUP_DIM, (g + 1) * FOURIER_GROUP_DIM
        cs = jnp.dot(uf[:, lo:hi], dft_ref[...], preferred_element_type=F32)
        ab_ref[0, 0, :, lo:hi] = cs[:, :FOURIER_GROUP_DIM].astype(BF16)
        ab_ref[0, 1, :, lo:hi] = cs[:, FOURIER_GROUP_DIM:].astype(BF16)
    ur_ref[...] = proj[:, D_FOURIER:D_FOURIER + D_LRU]
    yr_ref[...] = proj[:, D_FOURIER + D_LRU:].astype(BF16)


def _inproj_call(x2, sh, sc, g, w, dft_ch, batch, seq):
    n = batch * seq
    tpb = seq // TOKEN_TILE
    mod_spec = pl.BlockSpec((1, 1, D_MODEL), lambda i: (i // tpb, 0, 0))
    return pl.pallas_call(
        _inproj_kernel,
        out_shape=(
            jax.ShapeDtypeStruct((batch, 2, seq, D_FOURIER), BF16),
            jax.ShapeDtypeStruct((n, D_LRU), F32),
            jax.ShapeDtypeStruct((n, D_LRU), BF16),
        ),
        grid=(n // TOKEN_TILE,),
        in_specs=[
            pl.BlockSpec((TOKEN_TILE, D_MODEL), lambda i: (i, 0)),
            mod_spec, mod_spec,
            pl.BlockSpec((1, D_MODEL), lambda i: (0, 0)),
            pl.BlockSpec(w.shape, lambda i: (0, 0)),
            pl.BlockSpec(dft_ch.shape, lambda i: (0, 0)),
        ],
        out_specs=(
            pl.BlockSpec((1, 2, TOKEN_TILE, D_FOURIER), lambda i: (i // tpb, 0, i % tpb, 0)),
            pl.BlockSpec((TOKEN_TILE, D_LRU), lambda i: (i, 0)),
            pl.BlockSpec((TOKEN_TILE, D_LRU), lambda i: (i, 0)),
        ),
        compiler_params=_params(("parallel",)),
        name="inproj",
    )(x2, sh, sc, g, w, dft_ch)


def _inproj_ctx_kernel(x_ref, sh_ref, sc_ref, g_ref, w_ref, ur_ref):
    hx = _rms_modulate(x_ref[...], g_ref[...], sh_ref[...], sc_ref[...]).astype(BF16)
    ur_ref[...] = jnp.dot(hx, w_ref[...], preferred_element_type=F32)


def _inproj_ctx_call(ctx2, sh, sc, g, w):
    n = ctx2.shape[0]
    vec = pl.BlockSpec((1, D_MODEL), lambda i: (0, 0))
    return pl.pallas_call(
        _inproj_ctx_kernel,
        out_shape=jax.ShapeDtypeStruct((n, D_LRU), F32),
        grid=(n // TOKEN_TILE,),
        in_specs=[pl.BlockSpec((TOKEN_TILE, D_MODEL), lambda i: (i, 0)), vec, vec, vec,
                  pl.BlockSpec(w.shape, lambda i: (0, 0))],
        out_specs=pl.BlockSpec((TOKEN_TILE, D_LRU), lambda i: (i, 0)),
        compiler_params=_params(("parallel",)),
        name="inproj_ctx",
    )(ctx2, sh, sc, g, w)


def _rglru_kernel(n_chunks, u_ref, up_ref, un_ref, cw_ref, cb_ref, wg_ref, bg_ref, lam_ref, h0_ref,
                  h_ref, a_s, b_s, h_s, state):
    tc, pitch = SCAN_CHUNK, SCAN_PITCH
    d = pl.program_id(0)
    k = pl.program_id(2)
    kk = k + d * (n_chunks - 1 - 2 * k)

    @pl.when(k == 0)
    def _():
        state[...] = h0_ref[0]

    lam = lam_ref[0]
    neg_lam = -lam
    softplus = jnp.maximum(neg_lam, 0.0) + jnp.log1p(jnp.exp(-jnp.abs(neg_lam)))
    rate = (-0.5 * LRU_C * np.log2(np.e).astype(np.float32)) * softplus
    has_prev = kk > 0
    has_next = kk < n_chunks - 1
    row = lax.broadcasted_iota(jnp.int32, (SUBLANES, D_LRU), 0)
    cw = cw_ref[...]

    def patch(rolled, first_rows, fix):
        if first_rows:
            return jnp.concatenate([fix(rolled[:SUBLANES]), rolled[SUBLANES:]], axis=0)
        return jnp.concatenate([rolled[:-SUBLANES], fix(rolled[-SUBLANES:])], axis=0)

    for i in range(SCAN_BATCH):
        u = u_ref[i]
        prev = jnp.where(has_prev, up_ref[i], 0.0)
        nxt = jnp.where(has_next, un_ref[i], 0.0)
        p2, p1, n1 = prev[SUBLANES - 2:SUBLANES - 1], prev[SUBLANES - 1:SUBLANES], nxt[0:1]
        um1 = patch(pltpu.roll(u, 1, 0), True, lambda t: jnp.where(row == 0, p1, t))
        um2 = patch(pltpu.roll(u, 2, 0), True, lambda t: jnp.where(row == 0, p2, jnp.where(row == 1, p1, t)))
        up1 = patch(pltpu.roll(u, tc - 1, 0), False, lambda t: jnp.where(row == SUBLANES - 1, n1, t))
        xc = um2 * cw[0:1] + um1 * cw[1:2] + u * cw[2:3] + up1 * cw[3:4] + cb_ref[...]
        gz = jnp.dot(xc.astype(BF16), wg_ref[0], preferred_element_type=F32) + bg_ref[0]
        t_a = jnp.tanh(gz[:, :D_LRU])
        t_x = jnp.tanh(gz[:, D_LRU:])
        a = jnp.exp2(rate + rate * t_a)
        half_xc = 0.5 * xc
        b = jnp.sqrt(1.0 - a * a) * (half_xc + half_xc * t_x)
        for j in range(N_SLABS):
            a_s[j, pl.ds(i * pitch, tc), :] = a[:, j * LANES:(j + 1) * LANES]
            b_s[j, pl.ds(i * pitch, tc), :] = b[:, j * LANES:(j + 1) * LANES]

    def step(s, h):
        t = s + d * (tc - 1 - 2 * s)
        out = []
        for j in range(N_SLABS):
            rows = pl.ds(t, SCAN_BATCH, stride=pitch)
            hj = a_s[j, rows, :] * h[j] + b_s[j, rows, :]
            h_s[j, rows, :] = hj
            out.append(hj)
        return tuple(out)

    st = state[...]
    h = lax.fori_loop(0, tc, step, tuple(st[:, j * LANES:(j + 1) * LANES] for j in range(N_SLABS)),
                      unroll=4)
    for j in range(N_SLABS):
        state[:, j * LANES:(j + 1) * LANES] = h[j]
    for i in range(SCAN_BATCH):
        for j in range(N_SLABS):
            h_ref[0, i, :, j * LANES:(j + 1) * LANES] = h_s[j, pl.ds(i * pitch, tc), :]


def _rglru_call(u3, conv_w, conv_b, wg, bg, lam, h0):
    batch, seq, _ = u3.shape
    n_chunks = seq // SCAN_CHUNK
    halo_blocks = SCAN_CHUNK // SUBLANES
    last_halo = seq // SUBLANES - 1

    def chunk(d, k):
        return k + d * (n_chunks - 1 - 2 * k)

    scratch = pltpu.VMEM((N_SLABS, SCAN_BATCH * SCAN_PITCH, LANES), F32)
    return pl.pallas_call(
        functools.partial(_rglru_kernel, n_chunks),
        out_shape=jax.ShapeDtypeStruct((2, batch, seq, D_LRU), F32),
        grid=(2, batch // SCAN_BATCH, n_chunks),
        in_specs=[
            pl.BlockSpec((SCAN_BATCH, SCAN_CHUNK, D_LRU), lambda d, g, k: (g, chunk(d, k), 0)),
            pl.BlockSpec((SCAN_BATCH, SUBLANES, D_LRU),
                         lambda d, g, k: (g, jnp.maximum(chunk(d, k) * halo_blocks - 1, 0), 0)),
            pl.BlockSpec((SCAN_BATCH, SUBLANES, D_LRU),
                         lambda d, g, k: (g, jnp.minimum((chunk(d, k) + 1) * halo_blocks, last_halo), 0)),
            pl.BlockSpec((CONV_WIDTH, D_LRU), lambda d, g, k: (0, 0)),
            pl.BlockSpec((1, D_LRU), lambda d, g, k: (0, 0)),
            pl.BlockSpec((1, D_LRU, 2 * D_LRU), lambda d, g, k: (d, 0, 0)),
            pl.BlockSpec((1, 1, 2 * D_LRU), lambda d, g, k: (d, 0, 0)),
            pl.BlockSpec((1, 1, D_LRU), lambda d, g, k: (d, 0, 0)),
            pl.BlockSpec((1, SCAN_BATCH, D_LRU), lambda d, g, k: (d, g, 0)),
        ],
        out_specs=pl.BlockSpec((1, SCAN_BATCH, SCAN_CHUNK, D_LRU), lambda d, g, k: (d, g, chunk(d, k), 0)),
        scratch_shapes=[scratch, scratch, scratch, pltpu.VMEM((SCAN_BATCH, D_LRU), F32)],
        compiler_params=_params(("arbitrary", "arbitrary", "arbitrary")),
        name="rglru",
    )(u3, u3, u3, conv_w, conv_b, wg, bg, lam, h0)


def _fourier_kernel(m_ref, ab_ref, o_ref):
    seq2 = m_ref.shape[1]
    rhs = ab_ref[0].reshape(seq2, D_FOURIER)
    o_ref[0] = jnp.dot(m_ref[...], rhs, preferred_element_type=F32).astype(BF16)


def _fourier_call(m_pos, ab):
    batch, _, seq, _ = ab.shape
    return pl.pallas_call(
        _fourier_kernel,
        out_shape=jax.ShapeDtypeStruct((batch, seq, D_FOURIER), BF16),
        grid=(batch, seq // FOURIER_TILE),
        in_specs=[
            pl.BlockSpec((FOURIER_TILE, 2 * seq), lambda b, m: (m, 0)),
            pl.BlockSpec((1, 2, seq, D_FOURIER), lambda b, m: (b, 0, 0, 0)),
        ],
        out_specs=pl.BlockSpec((1, FOURIER_TILE, D_FOURIER), lambda b, m: (b, m, 0)),
        compiler_params=_params(("parallel", "parallel")),
        name="fourier",
    )(m_pos, ab)


def _route(logits):
    lane = lax.broadcasted_iota(jnp.int32, logits.shape, 1)
    neg = -jnp.inf
    gl = jnp.where(lane < N_GROUPS, logits, neg)
    gmax = jnp.max(gl, axis=1, keepdims=True)
    grp = jnp.min(jnp.where(gl == gmax, lane, ROUTE_LANES), axis=1, keepdims=True)
    p_grp = 1.0 / jnp.sum(jnp.exp(gl - gmax), axis=1, keepdims=True)
    e_lane = lane - N_GROUPS
    in_grp = (e_lane >= 0) & (e_lane < N_EXPERTS) & ((e_lane // EXPERTS_PER_GROUP) == grp)
    el = jnp.where(in_grp, logits, neg)
    t1 = jnp.max(el, axis=1, keepdims=True)
    i1 = jnp.min(jnp.where(el == t1, lane, ROUTE_LANES), axis=1, keepdims=True)
    el2 = jnp.where(lane == i1, neg, el)
    t2 = jnp.max(el2, axis=1, keepdims=True)
    i2 = jnp.min(jnp.where(el2 == t2, lane, ROUTE_LANES), axis=1, keepdims=True)
    e = jnp.exp(t2 - t1)
    w1 = p_grp / (1.0 + e)
    w2 = p_grp * e / (1.0 + e)
    out = jnp.where(lane == 0, (i1 - N_GROUPS).astype(F32), 0.0)
    out = jnp.where(lane == 1, (i2 - N_GROUPS).astype(F32), out)
    out = jnp.where(lane == 2, w1, out)
    return jnp.where(lane == 3, w2, out)


def _merge_kernel(x_ref, sh1_ref, sc1_ref, g1_ref, sh2_ref, sc2_ref, n1_ref, n2_ref, wgate_ref,
                  fm_ref, hf_ref, hb_ref, yr_ref, wfo_ref, wlo_ref, wout_ref, wr_ref, br_ref,
                  x1_ref, h2_ref, route_ref):
    rows = [pl.ds(q * MERGE_GROUP_ROWS, MERGE_GROUP_ROWS) for q in range(x_ref.shape[0] // MERGE_GROUP_ROWS)]
    groups = range(len(rows))

    def dot(a, w_ref):
        return jnp.dot(a, w_ref[...], preferred_element_type=F32)

    x = [x_ref[r, :] for r in rows]
    hx = [_rms_modulate(v, n1_ref[...], sh1_ref[0], sc1_ref[0]).astype(BF16) for v in x]
    lr = []
    for r in rows:
        y = yr_ref[r, :].astype(F32)
        gelu = 0.5 * y * (1.0 + jnp.tanh(np.sqrt(2.0 / np.pi).astype(np.float32) * (y + 0.044715 * (y * y * y))))
        lr.append(((hf_ref[0, 0, r, :] + hb_ref[0, 0, r, :]) * gelu).astype(BF16))
    branch_f = [dot(fm_ref[0, r, :], wfo_ref) for r in rows]
    branch_r = [dot(v, wlo_ref) for v in lr]
    t = [jnp.tanh(v) for v in [dot(v, wgate_ref) for v in hx]]
    mixed = [((branch_f[q] + branch_r[q])
              + (t[q][:, :D_MODEL] * branch_f[q] + t[q][:, D_MODEL:] * branch_r[q])).astype(BF16) for q in groups]
    mix = [dot(v, wout_ref) for v in mixed]
    x1 = [x[q] + g1_ref[0] * mix[q] for q in groups]
    for q in groups:
        x1_ref[rows[q], :] = x1[q]
    h2 = [_rms_modulate(v, n2_ref[...], sh2_ref[0], sc2_ref[0]) for v in x1]
    route = []
    for v in h2:
        hi = v.astype(BF16)
        lo = (v - hi.astype(F32)).astype(BF16)
        parts = dot(hi, wr_ref) + dot(lo, wr_ref)
        route.append(_route(parts[:, :ROUTE_LANES] + parts[:, ROUTE_LANES:] + br_ref[...]))
    _store_token_tiled(h2_ref, jnp.concatenate(h2, axis=0))
    route_ref[...] = jnp.concatenate(route, axis=0)


def _merge_call(x2, mods, n1, n2, wgate, fm, h, yr, wfo, wlo, wout, wr, br, batch, seq):
    n = batch * seq
    tm = TOKEN_TILE
    tpb = seq // tm
    mod_spec = pl.BlockSpec((1, 1, D_MODEL), lambda i: (i // tpb, 0, 0))
    vec = pl.BlockSpec((1, D_MODEL), lambda i: (0, 0))
    tile = pl.BlockSpec((tm, D_MODEL), lambda i: (i, 0))

    def full(a):
        return pl.BlockSpec(a.shape, lambda i: (0,) * a.ndim)

    return pl.pallas_call(
        _merge_kernel,
        out_shape=(
            jax.ShapeDtypeStruct((n, D_MODEL), F32),
            jax.ShapeDtypeStruct((n * TOKEN_ROWS, LANES), U32),
            jax.ShapeDtypeStruct((n, ROUTE_LANES), F32),
        ),
        grid=(n // tm,),
        in_specs=[
            tile, mod_spec, mod_spec, mod_spec, mod_spec, mod_spec, vec, vec, full(wgate),
            pl.BlockSpec((1, tm, D_FOURIER), lambda i: (i // tpb, i % tpb, 0)),
            pl.BlockSpec((1, 1, tm, D_LRU), lambda i: (0, i // tpb, i % tpb, 0)),
            pl.BlockSpec((1, 1, tm, D_LRU), lambda i: (1, i // tpb, i % tpb, 0)),
            pl.BlockSpec((tm, D_LRU), lambda i: (i, 0)),
            full(wfo), full(wlo), full(wout), full(wr), full(br),
        ],
        out_specs=(tile, pl.BlockSpec((tm * TOKEN_ROWS, LANES), lambda i: (i, 0)),
                   pl.BlockSpec((tm, ROUTE_LANES), lambda i: (i, 0))),
        compiler_params=_params(("parallel",)),
        name="merge",
    )(x2, *mods, n1, n2, wgate, fm, h, h, yr, wfo, wlo, wout, wr, br)


def _token_rows(t, count=1):
    start = t * TOKEN_ROWS
    if not isinstance(start, int):
        start = pl.multiple_of(start, TOKEN_ROWS)
    return pl.ds(start, count * TOKEN_ROWS)


def _row_copy(src_ref, dst_ref, sem, src_tok, dst_tok):
    return pltpu.make_async_copy(src_ref.at[_token_rows(src_tok)], dst_ref.at[_token_rows(dst_tok)], sem)


def _moe_kernel(n_tokens, be_ref, bw_ref, off_ref, nv_ref, nxt_ref, so_ref, rows_ref,
                h2_hbm, wg_hbm, wu_hbm, wd_hbm, out_hbm,
                win, xbuf, y0, y1, wg_f, wu_f, wd_f, wg_b, wu_b, wd_b, win_sem, w_sem, sct_sem):
    i = pl.program_id(0)
    n_assign = n_tokens * TOP_K
    ys = (y0, y1)
    prev = jnp.maximum(i - 1, 0)

    def scatter_wait(s):
        pltpu.make_async_copy(ys[s], out_hbm.at[_token_rows(0, MOE_BLOCK)], sct_sem.at[s]).wait()

    def weight_copies(e, slot):
        return [pltpu.make_async_copy(w.at[e], f.at[slot], w_sem.at[slot])
                for w, f in ((wg_hbm, wg_f), (wu_hbm, wu_f), (wd_hbm, wd_f))]

    def gather_rows():
        base = off_ref[i]
        for r in range(MOE_BLOCK):
            xbuf[_token_rows(r)] = win[_token_rows(rows_ref[base + r] & LOCAL_MASK)]

    def issue_scatter_prev(s):
        base = jnp.where(i > 0, off_ref[prev], n_assign)
        for r in range(MOE_BLOCK):
            dst = lax.shift_right_logical(rows_ref[base + r], LOCAL_BITS)
            _row_copy(ys[s], out_hbm, sct_sem.at[s], r, dst).start(priority=r % 2)

    active = nv_ref[i] > 0
    prev_active = (i > 0) & (nv_ref[prev] > 0)
    w_slot = so_ref[i] % 2

    @pl.when(i == 0)
    def _():
        y0[...] = jnp.zeros(y0.shape, U32)
        y1[...] = jnp.zeros(y1.shape, U32)
        pltpu.make_async_copy(y0, out_hbm.at[_token_rows(n_assign, MOE_BLOCK)], sct_sem.at[0]).start()
        for cp in weight_copies(be_ref[0], 0):
            cp.start()

    @pl.when(active & ((i == 0) | (bw_ref[i] != bw_ref[prev])))
    def _():
        cp = pltpu.make_async_copy(h2_hbm.at[_token_rows(bw_ref[i] * MOE_WINDOW, MOE_WINDOW)], win, win_sem)
        cp.start()
        cp.wait()

    @pl.when(active & ((i == 0) | (so_ref[i] != so_ref[prev])))
    def _():
        for cp in weight_copies(be_ref[i], w_slot):
            cp.wait()
        wg_b[...] = wg_f[w_slot].astype(BF16)
        wu_b[...] = wu_f[w_slot].astype(BF16)
        wd_b[...] = wd_f[w_slot].astype(BF16)

        @pl.when(nxt_ref[i] >= 0)
        def _():
            for cp in weight_copies(nxt_ref[i], 1 - w_slot):
                cp.start()

    for s in range(2):
        @pl.when(active & (i % 2 == s))
        def _(s=s):
            scatter_wait(s)
            gather_rows()
            issue_scatter_prev(1 - s)
            xb = _load_token_tiled(xbuf, MOE_BLOCK, BF16)
            hg = jnp.dot(xb, wg_b[...], preferred_element_type=F32)
            hu = jnp.dot(xb, wu_b[...], preferred_element_type=F32)
            hb = (hg * jax.nn.sigmoid(hg) * hu).astype(BF16)
            _store_token_tiled(ys[s], jnp.dot(hb, wd_b[...], preferred_element_type=F32))

        @pl.when(jnp.logical_not(active) & prev_active & (i % 2 == s))
        def _(s=s):
            issue_scatter_prev(1 - s)
            scatter_wait(s)
            scatter_wait(1 - s)


def _moe_call(blk_e, blk_w, blk_off, blk_nv, blk_next_e, blk_seg, rows, h2, w_g, w_u, w_d):
    n_steps = blk_e.shape[0]
    n_tokens = h2.shape[0] // TOKEN_ROWS
    any_spec = pl.BlockSpec(memory_space=pl.ANY)
    block_buf = pltpu.VMEM((MOE_BLOCK * TOKEN_ROWS, LANES), U32)
    grid_spec = pltpu.PrefetchScalarGridSpec(
        num_scalar_prefetch=7,
        grid=(n_steps,),
        in_specs=[any_spec, any_spec, any_spec, any_spec],
        out_specs=any_spec,
        scratch_shapes=[
            pltpu.VMEM((MOE_WINDOW * TOKEN_ROWS, LANES), U32),
            block_buf, block_buf, block_buf,
            pltpu.VMEM((2, D_MODEL, D_EXPERT), F32),
            pltpu.VMEM((2, D_MODEL, D_EXPERT), F32),
            pltpu.VMEM((2, D_EXPERT, D_MODEL), F32),
            pltpu.VMEM((D_MODEL, D_EXPERT), BF16),
            pltpu.VMEM((D_MODEL, D_EXPERT), BF16),
            pltpu.VMEM((D_EXPERT, D_MODEL), BF16),
            pltpu.SemaphoreType.DMA,
            pltpu.SemaphoreType.DMA((2,)),
            pltpu.SemaphoreType.DMA((2,)),
        ],
    )
    return pl.pallas_call(
        functools.partial(_moe_kernel, n_tokens),
        out_shape=jax.ShapeDtypeStruct(((n_tokens * TOP_K + N_DUMMY_ROWS) * TOKEN_ROWS, LANES), U32),
        grid_spec=grid_spec,
        compiler_params=_params(("arbitrary",)),
        name="moe",
    )(blk_e, blk_w, blk_off, blk_nv, blk_next_e, blk_seg, rows, h2, w_g, w_u, w_d)


def _final_kernel(x1_ref, y0_ref, y1_ref, route_ref, g2_ref, fg_ref, o_ref):
    route = route_ref[...]
    rows = route.shape[0]
    moe = (route[:, 2:3] * _load_token_tiled(y0_ref, rows, F32)
           + route[:, 3:4] * _load_token_tiled(y1_ref, rows, F32))
    x2 = x1_ref[...] + g2_ref[0] * moe
    o_ref[...] = x2 * lax.rsqrt(jnp.mean(x2 * x2, axis=-1, keepdims=True) + EPS) * fg_ref[...]


def _final_call(x1, out2, route, g2, fg, batch, seq):
    n = batch * seq
    tm = TOKEN_TILE
    tpb = seq // tm
    tile = pl.BlockSpec((tm, D_MODEL), lambda i: (i, 0))
    return pl.pallas_call(
        _final_kernel,
        out_shape=jax.ShapeDtypeStruct((n, D_MODEL), F32),
        grid=(n // tm,),
        in_specs=[
            tile,
            pl.BlockSpec((tm * TOKEN_ROWS, LANES), lambda i: (i, 0)),
            pl.BlockSpec((tm * TOKEN_ROWS, LANES), lambda i: (i + n // tm, 0)),
            pl.BlockSpec((tm, ROUTE_LANES), lambda i: (i, 0)),
            pl.BlockSpec((1, 1, D_MODEL), lambda i: (i // tpb, 0, 0)),
            pl.BlockSpec((1, D_MODEL), lambda i: (0, 0)),
        ],
        out_specs=tile,
        compiler_params=_params(("parallel",)),
        name="final",
    )(x1, out2, out2, route, g2, fg)


def _channel_dft():
    j = np.arange(FOURIER_GROUP_DIM)
    ang = 2.0 * np.pi * np.outer(j, j) / FOURIER_GROUP_DIM
    return np.concatenate([np.cos(ang), np.sin(ang)], axis=1).astype(np.float32)


def _position_dft(seq):
    rows = seq // GRID_W
    assert GRID_W % rows == 0
    r, c = np.divmod(np.arange(seq), GRID_W)
    phase = (np.outer(r, r) * (GRID_W // rows) + np.outer(c, c)) % GRID_W
    ang = 2.0 * np.pi * phase / GRID_W
    scale = 1.0 / np.sqrt(float(seq) * FOURIER_GROUP_DIM)
    return np.concatenate([np.cos(ang), -np.sin(ang)], axis=1) * scale


def _block_diag(w):
    heads, hd, _ = w.shape
    eye = jnp.eye(heads, dtype=w.dtype)
    return jnp.einsum('hij,hg->higj', w, eye).reshape(heads * hd, heads * hd)


def _gate_weights(w_a, w_x):
    return jnp.stack([0.5 * jnp.concatenate([_block_diag(w_a[d]), _block_diag(w_x[d])], axis=1)
                      for d in range(2)]).astype(BF16)


def _dispatch(eid, n_tokens):
    n_assign = n_tokens * TOP_K
    n_seg = (n_tokens // MOE_WINDOW) * N_EXPERTS
    n_blk = n_assign // MOE_BLOCK + n_seg
    a = jnp.arange(n_assign, dtype=jnp.int32)
    seg_of = (a // (TOP_K * MOE_WINDOW)) * N_EXPERTS + eid.reshape(-1)
    _, order = lax.sort((seg_of, a), num_keys=1)
    segs = jnp.arange(n_seg, dtype=jnp.int32)
    counts = jnp.sum((seg_of[:, None] == segs[None, :]).astype(jnp.int32), axis=0)
    start = jnp.cumsum(counts) - counts
    blocks = (counts + MOE_BLOCK - 1) // MOE_BLOCK
    bend = jnp.cumsum(blocks)
    b = jnp.arange(n_blk + 1, dtype=jnp.int32)
    ended = (b[:, None] >= bend[None, :]).astype(jnp.int32)
    seg = jnp.minimum(jnp.sum(ended, axis=1), n_seg - 1)
    onehot = (seg[:, None] == segs[None, :]).astype(jnp.int32)
    in_seg = (b - jnp.sum(onehot * (bend - blocks)[None, :], axis=1)) * MOE_BLOCK
    blk_off = jnp.sum(onehot * start[None, :], axis=1) + in_seg
    blk_nv = jnp.clip(jnp.sum(onehot * counts[None, :], axis=1) - in_seg, 0, MOE_BLOCK)
    seg_end = jnp.sum(onehot * bend[None, :], axis=1)
    follows = (b[None, :] == seg_end[:, None]).astype(jnp.int32)
    next_e = jnp.sum(follows * (seg % N_EXPERTS)[None, :], axis=1)
    next_nv = jnp.sum(follows * blk_nv[None, :], axis=1)
    blk_next_e = jnp.where(next_nv > 0, next_e, -1)

    tok = lax.shift_right_logical(order, 1)
    dst = (order & 1) * n_tokens + tok
    rows = lax.shift_left(dst, LOCAL_BITS) | (tok % MOE_WINDOW)
    dummy = lax.shift_left(n_assign + jnp.arange(N_DUMMY_ROWS, dtype=jnp.int32), LOCAL_BITS)
    seg_ordinal = jnp.cumsum((blocks > 0).astype(jnp.int32)) - 1
    blk_seg = jnp.sum(onehot * seg_ordinal[None, :], axis=1)
    assert MOE_WINDOW <= LOCAL_MASK + 1 and (n_assign + N_DUMMY_ROWS) << LOCAL_BITS <= 2 ** 32
    return seg % N_EXPERTS, seg // N_EXPERTS, blk_off, blk_nv, blk_next_e, blk_seg, jnp.concatenate([rows, dummy])


def kernel(x, c, ctx, c_ctx, w_mod, b_mod, norm1_g, w_in, conv_w, conv_b, lru_wa, lru_ba, lru_wx, lru_bx,
           lru_lam, w_fourier_out, w_lru_out, w_out, norm2_g, w_group, b_group, w_expert_router,
           b_expert_router, w_gate_e, w_up_e, w_down_e, final_g):
    batch, seq, _ = x.shape
    ctx_len = ctx.shape[1]
    n = batch * seq
    assert w_mod.shape[0] == 1, "single-layer stack only: the context stream is not carried across layers"
    x2 = x.reshape(n, D_MODEL)
    dft_ch = jnp.asarray(_channel_dft().astype(BF16))
    m_pos = jnp.asarray(_position_dft(seq).astype(BF16))

    for l in range(1):
        c_all = jnp.concatenate([c, c_ctx[None], jnp.zeros((MOD_ROWS - batch - 1, D_MODEL), F32)], axis=0)
        mod = _mod_call(c_all, w_mod[l], b_mod[l][None])
        sh1, sc1, g1, sh2, sc2, g2 = [m[:batch, None, :] for m in jnp.split(mod, 6, axis=-1)]
        csh1, csc1 = mod[batch:batch + 1, :D_MODEL], mod[batch:batch + 1, D_MODEL:2 * D_MODEL]

        w_in_b = w_in[l].astype(BF16)
        n1 = norm1_g[l][None]
        wg = _gate_weights(lru_wa[l], lru_wx[l])
        bg = 0.5 * jnp.concatenate([lru_ba[l], lru_bx[l]], axis=-1)[:, None, :]
        lam = lru_lam[l][:, None, :]
        cb = conv_b[l][None]

        uc = _inproj_ctx_call(ctx.reshape(batch * ctx_len, D_MODEL), csh1, csc1, n1,
                              w_in_b[:, D_FOURIER:D_FOURIER + D_LRU])
        hc = _rglru_call(uc.reshape(batch, ctx_len, D_LRU), conv_w[l], cb, wg, bg, lam,
                         jnp.zeros((2, batch, D_LRU), F32))
        h0 = jnp.stack([hc[0, :, -1], hc[1, :, 0]])

        ab, ur, yr = _inproj_call(x2, sh1, sc1, n1, w_in_b[:, :D_FOURIER + 2 * D_LRU], dft_ch, batch, seq)
        h = _rglru_call(ur.reshape(batch, seq, D_LRU), conv_w[l], cb, wg, bg, lam, h0)
        fm = _fourier_call(m_pos, ab)

        w_route = jnp.zeros((D_MODEL, ROUTE_LANES), F32)
        w_route = w_route.at[:, :N_GROUPS].set(w_group[l]).at[:, N_GROUPS:N_GROUPS + N_EXPERTS].set(
            w_expert_router[l])
        b_route = jnp.zeros((1, ROUTE_LANES), F32)
        b_route = b_route.at[0, :N_GROUPS].set(b_group[l]).at[0, N_GROUPS:N_GROUPS + N_EXPERTS].set(
            b_expert_router[l])
        w_route_hi = w_route.astype(BF16)
        w_route_lo = (w_route - w_route_hi.astype(F32)).astype(BF16)
        w_route = jnp.concatenate([w_route_hi, w_route_lo], axis=1)
        x1, h2, route = _merge_call(
            x2, (sh1, sc1, g1, sh2, sc2), n1, norm2_g[l][None],
            (0.5 * w_in[l][:, D_FOURIER + 2 * D_LRU:]).astype(BF16),
            fm, h, yr, w_fourier_out[l].astype(BF16), w_lru_out[l].astype(BF16),
            (0.5 * w_out[l]).astype(BF16),
            w_route, b_route, batch, seq)

        eid = route[:, :TOP_K].astype(jnp.int32)
        out2 = _moe_call(*_dispatch(eid, n), h2, w_gate_e[l], w_up_e[l], w_down_e[l])
        x2 = _final_call(x1, out2, route, g2, final_g[None], batch, seq)
    return x2.reshape(batch, seq, D_MODEL)
```

```python
import functools

import numpy as np
import jax
import jax.numpy as jnp
from jax import lax
from jax.experimental import pallas as pl
from jax.experimental.pallas import tpu as pltpu

F32 = jnp.float32
BF16 = jnp.bfloat16

LANES = 128
SUBLANES = 8
VMEM_LIMIT_BYTES = 56 * 1024 * 1024

D_MODEL = 1024
GRID_W = 64
EPS = 1e-6
N_FOURIER_GROUPS = 4
FOURIER_GROUP_DIM = 128
D_FOURIER = N_FOURIER_GROUPS * FOURIER_GROUP_DIM
N_LRU_HEADS = 8
LRU_HEAD_DIM = 64
D_LRU = N_LRU_HEADS * LRU_HEAD_DIM
CONV_WIDTH = 4
LRU_C = 8.0
N_GROUPS = 4
EXPERTS_PER_GROUP = 8
N_EXPERTS = N_GROUPS * EXPERTS_PER_GROUP
TOP_K = 2
D_EXPERT = 512

MOD_ROWS = 24
MOD_BLOCK_N = 1536
TOKEN_TILE = 512
WIDE_TILE = 1024
MERGE_GROUP_ROWS = 128
SCAN_BATCH = SUBLANES
SCAN_CHUNK = 256
SCAN_PITCH = SCAN_CHUNK + SUBLANES
N_SLABS = D_LRU // LANES
FOURIER_TILE = 1024
MOE_BLOCK = 256
ROUTE_LANES = LANES
N_DUMMY_ROWS = MOE_BLOCK
MOE_WINDOW = 16384
LOCAL_BITS = 14
LOCAL_MASK = (1 << LOCAL_BITS) - 1


def _params(semantics):
    return pltpu.CompilerParams(dimension_semantics=semantics, vmem_limit_bytes=VMEM_LIMIT_BYTES)


U32 = jnp.uint32
HALF = D_MODEL // 2
TOKEN_ROWS = HALF // LANES
HIGH_HALF_WORD = np.uint32(0xFFFF0000)


def _pack_bf16_pairs(x):
    bits = pltpu.bitcast(x.astype(BF16).astype(F32), U32)
    return (bits[:, :HALF] >> 16) | (bits[:, HALF:] & HIGH_HALF_WORD)


def _unpack_bf16_pairs(u, dtype):
    lo = pltpu.bitcast(u << 16, F32)
    hi = pltpu.bitcast(u & HIGH_HALF_WORD, F32)
    return jnp.concatenate([lo, hi], axis=1).astype(dtype)


def _store_token_tiled(ref, x):
    rows = x.shape[0]
    packed = _pack_bf16_pairs(x)
    for s in range(TOKEN_ROWS):
        ref[pl.ds(s, rows, stride=TOKEN_ROWS), :] = packed[:, s * LANES:(s + 1) * LANES]


def _load_token_tiled(ref, rows, dtype):
    packed = jnp.concatenate([ref[pl.ds(s, rows, stride=TOKEN_ROWS), :] for s in range(TOKEN_ROWS)], axis=1)
    return _unpack_bf16_pairs(packed, dtype)


def _rms_modulate(x, g, shift, scale):
    y = x * lax.rsqrt(jnp.mean(x * x, axis=-1, keepdims=True) + EPS) * g
    return y * (1.0 + scale) + shift


def _mod_kernel(c_ref, w_ref, b_ref, o_ref):
    c = c_ref[...]
    s = c * jax.nn.sigmoid(c)
    o_ref[...] = jnp.dot(s, w_ref[...], preferred_element_type=F32,
                         precision=lax.Precision.HIGHEST) + b_ref[...]


def _mod_call(c_all, w_mod, b_mod):
    n_out = w_mod.shape[1]
    return pl.pallas_call(
        _mod_kernel,
        out_shape=jax.ShapeDtypeStruct((MOD_ROWS, n_out), F32),
        grid=(n_out // MOD_BLOCK_N,),
        in_specs=[
            pl.BlockSpec((MOD_ROWS, D_MODEL), lambda j: (0, 0)),
            pl.BlockSpec((D_MODEL, MOD_BLOCK_N), lambda j: (0, j)),
            pl.BlockSpec((1, MOD_BLOCK_N), lambda j: (0, j)),
        ],
        out_specs=pl.BlockSpec((MOD_ROWS, MOD_BLOCK_N), lambda j: (0, j)),
        compiler_params=_params(("arbitrary",)),
        name="mod",
    )(c_all, w_mod, b_mod)


def _inproj_kernel(x_ref, sh_ref, sc_ref, g_ref, w_ref, dft_ref, ab_ref, ur_ref, yr_ref):
    hx = _rms_modulate(x_ref[...], g_ref[...], sh_ref[0], sc_ref[0]).astype(BF16)
    proj = jnp.dot(hx, w_ref[...], preferred_element_type=F32)
    uf = proj[:, :D_FOURIER].astype(BF16)
    for g in range(N_FOURIER_GROUPS):
        lo, hi = g * FOURIER_GROUP_DIM, (g + 1) * FOURIER_GROUP_DIM
        cs = jnp.dot(uf[:, lo:hi], dft_ref[...], preferred_element_type=F32)
        ab_ref[0, 0, :, lo:hi] = cs[:, :FOURIER_GROUP_DIM].astype(BF16)
        ab_ref[0, 1, :, lo:hi] = cs[:, FOURIER_GROUP_DIM:].astype(BF16)
    ur_ref[...] = proj[:, D_FOURIER:D_FOURIER + D_LRU]
    yr_ref[...] = proj[:, D_FOURIER + D_LRU:].astype(BF16)


def _inproj_call(x2, sh, sc, g, w, dft_ch, batch, seq):
    n = batch * seq
    tm = WIDE_TILE
    tpb = seq // tm
    mod_spec = pl.BlockSpec((1, 1, D_MODEL), lambda i: (i // tpb, 0, 0))
    return pl.pallas_call(
        _inproj_kernel,
        out_shape=(
            jax.ShapeDtypeStruct((batch, 2, seq, D_FOURIER), BF16),
            jax.ShapeDtypeStruct((n, D_LRU), F32),
            jax.ShapeDtypeStruct((n, D_LRU), BF16),
        ),
        grid=(n // tm,),
        in_specs=[
            pl.BlockSpec((tm, D_MODEL), lambda i: (i, 0)),
            mod_spec, mod_spec,
            pl.BlockSpec((1, D_MODEL), lambda i: (0, 0)),
            pl.BlockSpec(w.shape, lambda i: (0, 0)),
            pl.BlockSpec(dft_ch.shape, lambda i: (0, 0)),
        ],
        out_specs=(
            pl.BlockSpec((1, 2, tm, D_FOURIER), lambda i: (i // tpb, 0, i % tpb, 0)),
            pl.BlockSpec((tm, D_LRU), lambda i: (i, 0)),
            pl.BlockSpec((tm, D_LRU), lambda i: (i, 0)),
        ),
        compiler_params=_params(("parallel",)),
        name="inproj",
    )(x2, sh, sc, g, w, dft_ch)


def _inproj_ctx_kernel(x_ref, sh_ref, sc_ref, g_ref, w_ref, ur_ref):
    hx = _rms_modulate(x_ref[...], g_ref[...], sh_ref[...], sc_ref[...]).astype(BF16)
    ur_ref[...] = jnp.dot(hx, w_ref[...], preferred_element_type=F32)


def _inproj_ctx_call(ctx2, sh, sc, g, w):
    n = ctx2.shape[0]
    vec = pl.BlockSpec((1, D_MODEL), lambda i: (0, 0))
    return pl.pallas_call(
        _inproj_ctx_kernel,
        out_shape=jax.ShapeDtypeStruct((n, D_LRU), F32),
        grid=(n // TOKEN_TILE,),
        in_specs=[pl.BlockSpec((TOKEN_TILE, D_MODEL), lambda i: (i, 0)), vec, vec, vec,
                  pl.BlockSpec(w.shape, lambda i: (0, 0))],
        out_specs=pl.BlockSpec((TOKEN_TILE, D_LRU), lambda i: (i, 0)),
        compiler_params=_params(("parallel",)),
        name="inproj_ctx",
    )(ctx2, sh, sc, g, w)


def _rglru_kernel(n_chunks, u_ref, up_ref, un_ref, cw_ref, cb_ref, wg_ref, bg_ref, lam_ref, h0_ref,
                  h_ref, a_s, b_s, h_s, state):
    tc, pitch = SCAN_CHUNK, SCAN_PITCH
    d = pl.program_id(0)
    k = pl.program_id(2)
    kk = k + d * (n_chunks - 1 - 2 * k)

    @pl.when(k == 0)
    def _():
        state[...] = h0_ref[0]

    lam = lam_ref[0]
    neg_lam = -lam
    softplus = jnp.maximum(neg_lam, 0.0) + jnp.log1p(jnp.exp(-jnp.abs(neg_lam)))
    rate = (-0.5 * LRU_C * np.log2(np.e).astype(np.float32)) * softplus
    has_prev = kk > 0
    has_next = kk < n_chunks - 1
    row = lax.broadcasted_iota(jnp.int32, (SUBLANES, D_LRU), 0)
    cw = cw_ref[...]

    def patch(rolled, first_rows, fix):
        if first_rows:
            return jnp.concatenate([fix(rolled[:SUBLANES]), rolled[SUBLANES:]], axis=0)
        return jnp.concatenate([rolled[:-SUBLANES], fix(rolled[-SUBLANES:])], axis=0)

    for i in range(SCAN_BATCH):
        u = u_ref[i]
        prev = jnp.where(has_prev, up_ref[i], 0.0)
        nxt = jnp.where(has_next, un_ref[i], 0.0)
        p2, p1, n1 = prev[SUBLANES - 2:SUBLANES - 1], prev[SUBLANES - 1:SUBLANES], nxt[0:1]
        um1 = patch(pltpu.roll(u, 1, 0), True, lambda t: jnp.where(row == 0, p1, t))
        um2 = patch(pltpu.roll(u, 2, 0), True, lambda t: jnp.where(row == 0, p2, jnp.where(row == 1, p1, t)))
        up1 = patch(pltpu.roll(u, tc - 1, 0), False, lambda t: jnp.where(row == SUBLANES - 1, n1, t))
        xc = um2 * cw[0:1] + um1 * cw[1:2] + u * cw[2:3] + up1 * cw[3:4] + cb_ref[...]
        gz = jnp.dot(xc.astype(BF16), wg_ref[0], preferred_element_type=F32) + bg_ref[0]
        t_a = jnp.tanh(gz[:, :D_LRU])
        t_x = jnp.tanh(gz[:, D_LRU:])
        a = jnp.exp2(rate + rate * t_a)
        half_xc = 0.5 * xc
        b = jnp.sqrt(1.0 - a * a) * (half_xc + half_xc * t_x)
        for j in range(N_SLABS):
            a_s[j, pl.ds(i * pitch, tc), :] = a[:, j * LANES:(j + 1) * LANES]
            b_s[j, pl.ds(i * pitch, tc), :] = b[:, j * LANES:(j + 1) * LANES]

    def step(s, h):
        t = s + d * (tc - 1 - 2 * s)
        out = []
        for j in range(N_SLABS):
            rows = pl.ds(t, SCAN_BATCH, stride=pitch)
            hj = a_s[j, rows, :] * h[j] + b_s[j, rows, :]
            h_s[j, rows, :] = hj
            out.append(hj)
        return tuple(out)

    st = state[...]
    h = lax.fori_loop(0, tc, step, tuple(st[:, j * LANES:(j + 1) * LANES] for j in range(N_SLABS)),
                      unroll=4)
    for j in range(N_SLABS):
        state[:, j * LANES:(j + 1) * LANES] = h[j]
    for i in range(SCAN_BATCH):
        for j in range(N_SLABS):
            h_ref[0, i, :, j * LANES:(j + 1) * LANES] = h_s[j, pl.ds(i * pitch, tc), :]


def _rglru_call(u3, conv_w, conv_b, wg, bg, lam, h0):
    batch, seq, _ = u3.shape
    n_chunks = seq // SCAN_CHUNK
    halo_blocks = SCAN_CHUNK // SUBLANES
    last_halo = seq // SUBLANES - 1

    def chunk(d, k):
        return k + d * (n_chunks - 1 - 2 * k)

    scratch = pltpu.VMEM((N_SLABS, SCAN_BATCH * SCAN_PITCH, LANES), F32)
    return pl.pallas_call(
        functools.partial(_rglru_kernel, n_chunks),
        out_shape=jax.ShapeDtypeStruct((2, batch, seq, D_LRU), F32),
        grid=(2, batch // SCAN_BATCH, n_chunks),
        in_specs=[
            pl.BlockSpec((SCAN_BATCH, SCAN_CHUNK, D_LRU), lambda d, g, k: (g, chunk(d, k), 0)),
            pl.BlockSpec((SCAN_BATCH, SUBLANES, D_LRU),
                         lambda d, g, k: (g, jnp.maximum(chunk(d, k) * halo_blocks - 1, 0), 0)),
            pl.BlockSpec((SCAN_BATCH, SUBLANES, D_LRU),
                         lambda d, g, k: (g, jnp.minimum((chunk(d, k) + 1) * halo_blocks, last_halo), 0)),
            pl.BlockSpec((CONV_WIDTH, D_LRU), lambda d, g, k: (0, 0)),
            pl.BlockSpec((1, D_LRU), lambda d, g, k: (0, 0)),
            pl.BlockSpec((1, D_LRU, 2 * D_LRU), lambda d, g, k: (d, 0, 0)),
            pl.BlockSpec((1, 1, 2 * D_LRU), lambda d, g, k: (d, 0, 0)),
            pl.BlockSpec((1, 1, D_LRU), lambda d, g, k: (d, 0, 0)),
            pl.BlockSpec((1, SCAN_BATCH, D_LRU), lambda d, g, k: (d, g, 0)),
        ],
        out_specs=pl.BlockSpec((1, SCAN_BATCH, SCAN_CHUNK, D_LRU), lambda d, g, k: (d, g, chunk(d, k), 0)),
        scratch_shapes=[scratch, scratch, scratch, pltpu.VMEM((SCAN_BATCH, D_LRU), F32)],
        compiler_params=_params(("arbitrary", "arbitrary", "arbitrary")),
        name="rglru",
    )(u3, u3, u3, conv_w, conv_b, wg, bg, lam, h0)


def _fourier_kernel(m_ref, ab_ref, o_ref):
    seq2 = m_ref.shape[1]
    rhs = ab_ref[0].reshape(seq2, D_FOURIER)
    o_ref[0] = jnp.dot(m_ref[...], rhs, preferred_element_type=F32).astype(BF16)


def _fourier_call(m_pos, ab):
    batch, _, seq, _ = ab.shape
    return pl.pallas_call(
        _fourier_kernel,
        out_shape=jax.ShapeDtypeStruct((batch, seq, D_FOURIER), BF16),
        grid=(seq // FOURIER_TILE, batch),
        in_specs=[
            pl.BlockSpec((FOURIER_TILE, 2 * seq), lambda m, b: (m, 0)),
            pl.BlockSpec((1, 2, seq, D_FOURIER), lambda m, b: (b, 0, 0, 0)),
        ],
        out_specs=pl.BlockSpec((1, FOURIER_TILE, D_FOURIER), lambda m, b: (b, m, 0)),
        compiler_params=_params(("parallel", "parallel")),
        name="fourier",
    )(m_pos, ab)


def _route(logits):
    lane = lax.broadcasted_iota(jnp.int32, logits.shape, 1)
    neg = -jnp.inf
    gl = jnp.where(lane < N_GROUPS, logits, neg)
    gmax = jnp.max(gl, axis=1, keepdims=True)
    grp = jnp.min(jnp.where(gl == gmax, lane, ROUTE_LANES), axis=1, keepdims=True)
    p_grp = 1.0 / jnp.sum(jnp.exp(gl - gmax), axis=1, keepdims=True)
    e_lane = lane - N_GROUPS
    in_grp = (e_lane >= 0) & (e_lane < N_EXPERTS) & ((e_lane // EXPERTS_PER_GROUP) == grp)
    el = jnp.where(in_grp, logits, neg)
    t1 = jnp.max(el, axis=1, keepdims=True)
    i1 = jnp.min(jnp.where(el == t1, lane, ROUTE_LANES), axis=1, keepdims=True)
    el2 = jnp.where(lane == i1, neg, el)
    t2 = jnp.max(el2, axis=1, keepdims=True)
    i2 = jnp.min(jnp.where(el2 == t2, lane, ROUTE_LANES), axis=1, keepdims=True)
    e = jnp.exp(t2 - t1)
    w1 = p_grp / (1.0 + e)
    w2 = p_grp * e / (1.0 + e)
    out = jnp.where(lane == 0, (i1 - N_GROUPS).astype(F32), 0.0)
    out = jnp.where(lane == 1, (i2 - N_GROUPS).astype(F32), out)
    out = jnp.where(lane == 2, w1, out)
    return jnp.where(lane == 3, w2, out)


def _merge_kernel(x_ref, sh1_ref, sc1_ref, g1_ref, sh2_ref, sc2_ref, n1_ref, n2_ref, wgate_ref,
                  fm_ref, hf_ref, hb_ref, yr_ref, wfo_ref, wlo_ref, wout_ref, wr_ref, br_ref,
                  x1_ref, h2_ref, route_ref):
    rows = [pl.ds(q * MERGE_GROUP_ROWS, MERGE_GROUP_ROWS) for q in range(x_ref.shape[0] // MERGE_GROUP_ROWS)]
    groups = range(len(rows))

    def dot(a, w_ref):
        return jnp.dot(a, w_ref[...], preferred_element_type=F32)

    x = [x_ref[r, :] for r in rows]
    hx = [_rms_modulate(v, n1_ref[...], sh1_ref[0], sc1_ref[0]).astype(BF16) for v in x]
    lr = []
    for r in rows:
        y = yr_ref[r, :].astype(F32)
        gelu = 0.5 * y * (1.0 + jnp.tanh(np.sqrt(2.0 / np.pi).astype(np.float32) * (y + 0.044715 * (y * y * y))))
        lr.append(((hf_ref[0, 0, r, :] + hb_ref[0, 0, r, :]) * gelu).astype(BF16))
    branch_f = [dot(fm_ref[0, r, :], wfo_ref) for r in rows]
    branch_r = [dot(v, wlo_ref) for v in lr]
    t = [jnp.tanh(v) for v in [dot(v, wgate_ref) for v in hx]]
    mixed = [((branch_f[q] + branch_r[q])
              + (t[q][:, :D_MODEL] * branch_f[q] + t[q][:, D_MODEL:] * branch_r[q])).astype(BF16) for q in groups]
    mix = [dot(v, wout_ref) for v in mixed]
    x1 = [x[q] + g1_ref[0] * mix[q] for q in groups]
    for q in groups:
        x1_ref[rows[q], :] = x1[q]
    h2 = [_rms_modulate(v, n2_ref[...], sh2_ref[0], sc2_ref[0]) for v in x1]
    route = []
    for v in h2:
        hi = v.astype(BF16)
        lo = (v - hi.astype(F32)).astype(BF16)
        parts = dot(hi, wr_ref) + dot(lo, wr_ref)
        route.append(_route(parts[:, :ROUTE_LANES] + parts[:, ROUTE_LANES:] + br_ref[...]))
    _store_token_tiled(h2_ref, jnp.concatenate(h2, axis=0))
    route_ref[...] = jnp.concatenate(route, axis=0)


def _merge_call(x2, mods, n1, n2, wgate, fm, h, yr, wfo, wlo, wout, wr, br, batch, seq):
    n = batch * seq
    tm = TOKEN_TILE
    tpb = seq // tm
    mod_spec = pl.BlockSpec((1, 1, D_MODEL), lambda i: (i // tpb, 0, 0))
    vec = pl.BlockSpec((1, D_MODEL), lambda i: (0, 0))
    tile = pl.BlockSpec((tm, D_MODEL), lambda i: (i, 0))

    def full(a):
        return pl.BlockSpec(a.shape, lambda i: (0,) * a.ndim)

    return pl.pallas_call(
        _merge_kernel,
        out_shape=(
            jax.ShapeDtypeStruct((n, D_MODEL), F32),
            jax.ShapeDtypeStruct((n * TOKEN_ROWS, LANES), U32),
            jax.ShapeDtypeStruct((n, ROUTE_LANES), F32),
        ),
        grid=(n // tm,),
        in_specs=[
            tile, mod_spec, mod_spec, mod_spec, mod_spec, mod_spec, vec, vec, full(wgate),
            pl.BlockSpec((1, tm, D_FOURIER), lambda i: (i // tpb, i % tpb, 0)),
            pl.BlockSpec((1, 1, tm, D_LRU), lambda i: (0, i // tpb, i % tpb, 0)),
            pl.BlockSpec((1, 1, tm, D_LRU), lambda i: (1, i // tpb, i % tpb, 0)),
            pl.BlockSpec((tm, D_LRU), lambda i: (i, 0)),
            full(wfo), full(wlo), full(wout), full(wr), full(br),
        ],
        out_specs=(tile, pl.BlockSpec((tm * TOKEN_ROWS, LANES), lambda i: (i, 0)),
                   pl.BlockSpec((tm, ROUTE_LANES), lambda i: (i, 0))),
        compiler_params=_params(("parallel",)),
        name="merge",
    )(x2, *mods, n1, n2, wgate, fm, h, h, yr, wfo, wlo, wout, wr, br)


def _token_rows(t, count=1):
    start = t * TOKEN_ROWS
    if not isinstance(start, int):
        start = pl.multiple_of(start, TOKEN_ROWS)
    return pl.ds(start, count * TOKEN_ROWS)


def _row_copy(src_ref, dst_ref, sem, src_tok, dst_tok):
    return pltpu.make_async_copy(src_ref.at[_token_rows(src_tok)], dst_ref.at[_token_rows(dst_tok)], sem)


def _moe_kernel(n_tokens, be_ref, bw_ref, off_ref, nv_ref, nxt_ref, so_ref, rows_ref,
                h2_hbm, wg_hbm, wu_hbm, wd_hbm, out_hbm,
                win, xbuf, y0, y1, wg_f, wu_f, wd_f, wg_b, wu_b, wd_b, win_sem, w_sem, sct_sem):
    i = pl.program_id(0)
    n_assign = n_tokens * TOP_K
    ys = (y0, y1)
    prev = jnp.maximum(i - 1, 0)

    def scatter_wait(s):
        pltpu.make_async_copy(ys[s], out_hbm.at[_token_rows(0, MOE_BLOCK)], sct_sem.at[s]).wait()

    def weight_copies(e, slot):
        return [pltpu.make_async_copy(w.at[e], f.at[slot], w_sem.at[slot])
                for w, f in ((wg_hbm, wg_f), (wu_hbm, wu_f), (wd_hbm, wd_f))]

    def gather_rows():
        base = off_ref[i]
        for r in range(MOE_BLOCK):
            xbuf[_token_rows(r)] = win[_token_rows(rows_ref[base + r] & LOCAL_MASK)]

    def issue_scatter_prev(s):
        base = jnp.where(i > 0, off_ref[prev], n_assign)
        for r in range(MOE_BLOCK):
            dst = lax.shift_right_logical(rows_ref[base + r], LOCAL_BITS)
            _row_copy(ys[s], out_hbm, sct_sem.at[s], r, dst).start(priority=r % 2)

    active = nv_ref[i] > 0
    prev_active = (i > 0) & (nv_ref[prev] > 0)
    w_slot = so_ref[i] % 2

    @pl.when(i == 0)
    def _():
        y0[...] = jnp.zeros(y0.shape, U32)
        y1[...] = jnp.zeros(y1.shape, U32)
        pltpu.make_async_copy(y0, out_hbm.at[_token_rows(n_assign, MOE_BLOCK)], sct_sem.at[0]).start()
        for cp in weight_copies(be_ref[0], 0):
            cp.start()

    @pl.when(active & ((i == 0) | (bw_ref[i] != bw_ref[prev])))
    def _():
        cp = pltpu.make_async_copy(h2_hbm.at[_token_rows(bw_ref[i] * MOE_WINDOW, MOE_WINDOW)], win, win_sem)
        cp.start()
        cp.wait()

    @pl.when(active & ((i == 0) | (so_ref[i] != so_ref[prev])))
    def _():
        for cp in weight_copies(be_ref[i], w_slot):
            cp.wait()
        wg_b[...] = wg_f[w_slot].astype(BF16)
        wu_b[...] = wu_f[w_slot].astype(BF16)
        wd_b[...] = wd_f[w_slot].astype(BF16)

        @pl.when(nxt_ref[i] >= 0)
        def _():
            for cp in weight_copies(nxt_ref[i], 1 - w_slot):
                cp.start()

    for s in range(2):
        @pl.when(active & (i % 2 == s))
        def _(s=s):
            scatter_wait(s)
            gather_rows()
            issue_scatter_prev(1 - s)
            xb = _load_token_tiled(xbuf, MOE_BLOCK, BF16)
            hg = jnp.dot(xb, wg_b[...], preferred_element_type=F32)
            hu = jnp.dot(xb, wu_b[...], preferred_element_type=F32)
            hb = (hg * jax.nn.sigmoid(hg) * hu).astype(BF16)
            _store_token_tiled(ys[s], jnp.dot(hb, wd_b[...], preferred_element_type=F32))

        @pl.when(jnp.logical_not(active) & prev_active & (i % 2 == s))
        def _(s=s):
            issue_scatter_prev(1 - s)
            scatter_wait(s)
            scatter_wait(1 - s)


def _moe_call(blk_e, blk_w, blk_off, blk_nv, blk_next_e, blk_seg, rows, h2, w_g, w_u, w_d):
    n_steps = blk_e.shape[0]
    n_tokens = h2.shape[0] // TOKEN_ROWS
    any_spec = pl.BlockSpec(memory_space=pl.ANY)
    block_buf = pltpu.VMEM((MOE_BLOCK * TOKEN_ROWS, LANES), U32)
    grid_spec = pltpu.PrefetchScalarGridSpec(
        num_scalar_prefetch=7,
        grid=(n_steps,),
        in_specs=[any_spec, any_spec, any_spec, any_spec],
        out_specs=any_spec,
        scratch_shapes=[
            pltpu.VMEM((MOE_WINDOW * TOKEN_ROWS, LANES), U32),
            block_buf, block_buf, block_buf,
            pltpu.VMEM((2, D_MODEL, D_EXPERT), F32),
            pltpu.VMEM((2, D_MODEL, D_EXPERT), F32),
            pltpu.VMEM((2, D_EXPERT, D_MODEL), F32),
            pltpu.VMEM((D_MODEL, D_EXPERT), BF16),
            pltpu.VMEM((D_MODEL, D_EXPERT), BF16),
            pltpu.VMEM((D_EXPERT, D_MODEL), BF16),
            pltpu.SemaphoreType.DMA,
            pltpu.SemaphoreType.DMA((2,)),
            pltpu.SemaphoreType.DMA((2,)),
        ],
    )
    return pl.pallas_call(
        functools.partial(_moe_kernel, n_tokens),
        out_shape=jax.ShapeDtypeStruct(((n_tokens * TOP_K + N_DUMMY_ROWS) * TOKEN_ROWS, LANES), U32),
        grid_spec=grid_spec,
        compiler_params=_params(("arbitrary",)),
        name="moe",
    )(blk_e, blk_w, blk_off, blk_nv, blk_next_e, blk_seg, rows, h2, w_g, w_u, w_d)


def _final_kernel(x1_ref, y0_ref, y1_ref, route_ref, g2_ref, fg_ref, o_ref):
    route = route_ref[...]
    rows = route.shape[0]
    moe = (route[:, 2:3] * _load_token_tiled(y0_ref, rows, F32)
           + route[:, 3:4] * _load_token_tiled(y1_ref, rows, F32))
    x2 = x1_ref[...] + g2_ref[0] * moe
    o_ref[...] = x2 * lax.rsqrt(jnp.mean(x2 * x2, axis=-1, keepdims=True) + EPS) * fg_ref[...]


def _final_call(x1, out2, route, g2, fg, batch, seq):
    n = batch * seq
    tm = WIDE_TILE
    tpb = seq // tm
    tile = pl.BlockSpec((tm, D_MODEL), lambda i: (i, 0))
    return pl.pallas_call(
        _final_kernel,
        out_shape=jax.ShapeDtypeStruct((n, D_MODEL), F32),
        grid=(n // tm,),
        in_specs=[
            tile,
            pl.BlockSpec((tm * TOKEN_ROWS, LANES), lambda i: (i, 0)),
            pl.BlockSpec((tm * TOKEN_ROWS, LANES), lambda i: (i + n // tm, 0)),
            pl.BlockSpec((tm, ROUTE_LANES), lambda i: (i, 0)),
            pl.BlockSpec((1, 1, D_MODEL), lambda i: (i // tpb, 0, 0)),
            pl.BlockSpec((1, D_MODEL), lambda i: (0, 0)),
        ],
        out_specs=tile,
        compiler_params=_params(("parallel",)),
        name="final",
    )(x1, out2, out2, route, g2, fg)


def _channel_dft():
    j = np.arange(FOURIER_GROUP_DIM)
    ang = 2.0 * np.pi * np.outer(j, j) / FOURIER_GROUP_DIM
    return np.concatenate([np.cos(ang), np.sin(ang)], axis=1).astype(np.float32)


def _position_dft(seq):
    rows = seq // GRID_W
    assert GRID_W % rows == 0
    r, c = np.divmod(np.arange(seq), GRID_W)
    phase = (np.outer(r, r) * (GRID_W // rows) + np.outer(c, c)) % GRID_W
    ang = 2.0 * np.pi * phase / GRID_W
    scale = 1.0 / np.sqrt(float(seq) * FOURIER_GROUP_DIM)
    return np.concatenate([np.cos(ang), -np.sin(ang)], axis=1) * scale


def _block_diag(w):
    heads, hd, _ = w.shape
    eye = jnp.eye(heads, dtype=w.dtype)
    return jnp.einsum('hij,hg->higj', w, eye).reshape(heads * hd, heads * hd)


def _gate_weights(w_a, w_x):
    return jnp.stack([0.5 * jnp.concatenate([_block_diag(w_a[d]), _block_diag(w_x[d])], axis=1)
                      for d in range(2)]).astype(BF16)


def _dispatch(eid, n_tokens):
    n_assign = n_tokens * TOP_K
    n_seg = (n_tokens // MOE_WINDOW) * N_EXPERTS
    n_blk = n_assign // MOE_BLOCK + n_seg
    a = jnp.arange(n_assign, dtype=jnp.int32)
    seg_of = (a // (TOP_K * MOE_WINDOW)) * N_EXPERTS + eid.reshape(-1)
    _, order = lax.sort((seg_of, a), num_keys=1)
    segs = jnp.arange(n_seg, dtype=jnp.int32)
    counts = jnp.sum((seg_of[:, None] == segs[None, :]).astype(jnp.int32), axis=0)
    start = jnp.cumsum(counts) - counts
    blocks = (counts + MOE_BLOCK - 1) // MOE_BLOCK
    bend = jnp.cumsum(blocks)
    b = jnp.arange(n_blk + 1, dtype=jnp.int32)
    ended = (b[:, None] >= bend[None, :]).astype(jnp.int32)
    seg = jnp.minimum(jnp.sum(ended, axis=1), n_seg - 1)
    onehot = (seg[:, None] == segs[None, :]).astype(jnp.int32)
    in_seg = (b - jnp.sum(onehot * (bend - blocks)[None, :], axis=1)) * MOE_BLOCK
    blk_off = jnp.sum(onehot * start[None, :], axis=1) + in_seg
    blk_nv = jnp.clip(jnp.sum(onehot * counts[None, :], axis=1) - in_seg, 0, MOE_BLOCK)
    seg_end = jnp.sum(onehot * bend[None, :], axis=1)
    follows = (b[None, :] == seg_end[:, None]).astype(jnp.int32)
    next_e = jnp.sum(follows * (seg % N_EXPERTS)[None, :], axis=1)
    next_nv = jnp.sum(follows * blk_nv[None, :], axis=1)
    blk_next_e = jnp.where(next_nv > 0, next_e, -1)

    tok = lax.shift_right_logical(order, 1)
    dst = (order & 1) * n_tokens + tok
    rows = lax.shift_left(dst, LOCAL_BITS) | (tok % MOE_WINDOW)
    dummy = lax.shift_left(n_assign + jnp.arange(N_DUMMY_ROWS, dtype=jnp.int32), LOCAL_BITS)
    seg_ordinal = jnp.cumsum((blocks > 0).astype(jnp.int32)) - 1
    blk_seg = jnp.sum(onehot * seg_ordinal[None, :], axis=1)
    assert MOE_WINDOW <= LOCAL_MASK + 1 and (n_assign + N_DUMMY_ROWS) << LOCAL_BITS <= 2 ** 32
    return seg % N_EXPERTS, seg // N_EXPERTS, blk_off, blk_nv, blk_next_e, blk_seg, jnp.concatenate([rows, dummy])


def kernel(x, c, ctx, c_ctx, w_mod, b_mod, norm1_g, w_in, conv_w, conv_b, lru_wa, lru_ba, lru_wx, lru_bx,
           lru_lam, w_fourier_out, w_lru_out, w_out, norm2_g, w_group, b_group, w_expert_router,
           b_expert_router, w_gate_e, w_up_e, w_down_e, final_g):
    batch, seq, _ = x.shape
    ctx_len = ctx.shape[1]
    n = batch * seq
    assert w_mod.shape[0] == 1, "single-layer stack only: the context stream is not carried across layers"
    x2 = x.reshape(n, D_MODEL)
    dft_ch = jnp.asarray(_channel_dft().astype(BF16))
    m_pos = jnp.asarray(_position_dft(seq).astype(BF16))

    for l in range(1):
        c_all = jnp.concatenate([c, c_ctx[None], jnp.zeros((MOD_ROWS - batch - 1, D_MODEL), F32)], axis=0)
        mod = _mod_call(c_all, w_mod[l], b_mod[l][None])
        sh1, sc1, g1, sh2, sc2, g2 = [m[:batch, None, :] for m in jnp.split(mod, 6, axis=-1)]
        csh1, csc1 = mod[batch:batch + 1, :D_MODEL], mod[batch:batch + 1, D_MODEL:2 * D_MODEL]

        w_in_b = w_in[l].astype(BF16)
        n1 = norm1_g[l][None]
        wg = _gate_weights(lru_wa[l], lru_wx[l])
        bg = 0.5 * jnp.concatenate([lru_ba[l], lru_bx[l]], axis=-1)[:, None, :]
        lam = lru_lam[l][:, None, :]
        cb = conv_b[l][None]

        uc = _inproj_ctx_call(ctx.reshape(batch * ctx_len, D_MODEL), csh1, csc1, n1,
                              w_in_b[:, D_FOURIER:D_FOURIER + D_LRU])
        hc = _rglru_call(uc.reshape(batch, ctx_len, D_LRU), conv_w[l], cb, wg, bg, lam,
                         jnp.zeros((2, batch, D_LRU), F32))
        h0 = jnp.stack([hc[0, :, -1], hc[1, :, 0]])

        ab, ur, yr = _inproj_call(x2, sh1, sc1, n1, w_in_b[:, :D_FOURIER + 2 * D_LRU], dft_ch, batch, seq)
        h = _rglru_call(ur.reshape(batch, seq, D_LRU), conv_w[l], cb, wg, bg, lam, h0)
        fm = _fourier_call(m_pos, ab)

        w_route = jnp.zeros((D_MODEL, ROUTE_LANES), F32)
        w_route = w_route.at[:, :N_GROUPS].set(w_group[l]).at[:, N_GROUPS:N_GROUPS + N_EXPERTS].set(
            w_expert_router[l])
        b_route = jnp.zeros((1, ROUTE_LANES), F32)
        b_route = b_route.at[0, :N_GROUPS].set(b_group[l]).at[0, N_GROUPS:N_GROUPS + N_EXPERTS].set(
            b_expert_router[l])
        w_route_hi = w_route.astype(BF16)
        w_route_lo = (w_route - w_route_hi.astype(F32)).astype(BF16)
        w_route = jnp.concatenate([w_route_hi, w_route_lo], axis=1)
        x1, h2, route = _merge_call(
            x2, (sh1, sc1, g1, sh2, sc2), n1, norm2_g[l][None],
            (0.5 * w_in[l][:, D_FOURIER + 2 * D_LRU:]).astype(BF16),
            fm, h, yr, w_fourier_out[l].astype(BF16), w_lru_out[l].astype(BF16),
            (0.5 * w_out[l]).astype(BF16),
            w_route, b_route, batch, seq)

        eid = route[:, :TOP_K].astype(jnp.int32)
        out2 = _moe_call(*_dispatch(eid, n), h2, w_gate_e[l], w_up_e[l], w_down_e[l])
        x2 = _final_call(x1, out2, route, g2, final_g[None], batch, seq)
    return x2.reshape(batch, seq, D_MODEL)
```

```python
import functools

import numpy as np
import jax
import jax.numpy as jnp
from jax import lax
from jax.experimental import pallas as pl
from jax.experimental.pallas import tpu as pltpu

F32 = jnp.float32
BF16 = jnp.bfloat16

LANES = 128
SUBLANES = 8
VMEM_LIMIT_BYTES = 56 * 1024 * 1024

D_MODEL = 1024
GRID_W = 64
EPS = 1e-6
N_FOURIER_GROUPS = 4
FOURIER_GROUP_DIM = 128
D_FOURIER = N_FOURIER_GROUPS * FOURIER_GROUP_DIM
N_LRU_HEADS = 8
LRU_HEAD_DIM = 64
D_LRU = N_LRU_HEADS * LRU_HEAD_DIM
CONV_WIDTH = 4
LRU_C = 8.0
N_GROUPS = 4
EXPERTS_PER_GROUP = 8
N_EXPERTS = N_GROUPS * EXPERTS_PER_GROUP
TOP_K = 2
D_EXPERT = 512

MOD_ROWS = 24
MOD_BLOCK_N = 1536
TOKEN_TILE = 512
WIDE_TILE = 1024
MERGE_GROUP_ROWS = 128
SCAN_BATCH = SUBLANES
SCAN_CHUNK = 256
SCAN_PITCH = SCAN_CHUNK + SUBLANES
N_SLABS = D_LRU // LANES
FOURIER_TILE = 1024
MOE_BLOCK = 256
ROUTE_LANES = LANES
N_OUT_BUFFERS = 3
N_DUMMY_ROWS = N_OUT_BUFFERS * MOE_BLOCK
MOE_WINDOW = 16384
LOCAL_BITS = 14
LOCAL_MASK = (1 << LOCAL_BITS) - 1


def _params(semantics):
    return pltpu.CompilerParams(dimension_semantics=semantics, vmem_limit_bytes=VMEM_LIMIT_BYTES)


U32 = jnp.uint32
HALF = D_MODEL // 2
TOKEN_ROWS = HALF // LANES
HIGH_HALF_WORD = np.uint32(0xFFFF0000)


def _pack_bf16_pairs(x):
    bits = pltpu.bitcast(x.astype(BF16).astype(F32), U32)
    return (bits[:, :HALF] >> 16) | (bits[:, HALF:] & HIGH_HALF_WORD)


def _unpack_bf16_pairs(u, dtype):
    lo = pltpu.bitcast(u << 16, F32)
    hi = pltpu.bitcast(u & HIGH_HALF_WORD, F32)
    return jnp.concatenate([lo, hi], axis=1).astype(dtype)


def _store_token_tiled(ref, x):
    rows = x.shape[0]
    packed = _pack_bf16_pairs(x)
    for s in range(TOKEN_ROWS):
        ref[pl.ds(s, rows, stride=TOKEN_ROWS), :] = packed[:, s * LANES:(s + 1) * LANES]


def _load_token_tiled(ref, rows, dtype):
    packed = jnp.concatenate([ref[pl.ds(s, rows, stride=TOKEN_ROWS), :] for s in range(TOKEN_ROWS)], axis=1)
    return _unpack_bf16_pairs(packed, dtype)


def _rms_modulate(x, g, shift, scale):
    y = x * lax.rsqrt(jnp.mean(x * x, axis=-1, keepdims=True) + EPS) * g
    return y * (1.0 + scale) + shift


def _mod_kernel(c_ref, w_ref, b_ref, o_ref):
    c = c_ref[...]
    s = c * jax.nn.sigmoid(c)
    o_ref[...] = jnp.dot(s, w_ref[...], preferred_element_type=F32,
                         precision=lax.Precision.HIGHEST) + b_ref[...]


def _mod_call(c_all, w_mod, b_mod):
    n_out = w_mod.shape[1]
    return pl.pallas_call(
        _mod_kernel,
        out_shape=jax.ShapeDtypeStruct((MOD_ROWS, n_out), F32),
        grid=(n_out // MOD_BLOCK_N,),
        in_specs=[
            pl.BlockSpec((MOD_ROWS, D_MODEL), lambda j: (0, 0)),
            pl.BlockSpec((D_MODEL, MOD_BLOCK_N), lambda j: (0, j)),
            pl.BlockSpec((1, MOD_BLOCK_N), lambda j: (0, j)),
        ],
        out_specs=pl.BlockSpec((MOD_ROWS, MOD_BLOCK_N), lambda j: (0, j)),
        compiler_params=_params(("arbitrary",)),
        name="mod",
    )(c_all, w_mod, b_mod)


def _inproj_kernel(x_ref, sh_ref, sc_ref, g_ref, w_ref, dft_ref, ab_ref, ur_ref, yr_ref):
    hx = _rms_modulate(x_ref[...], g_ref[...], sh_ref[0], sc_ref[0]).astype(BF16)
    proj = jnp.dot(hx, w_ref[...], preferred_element_type=F32)
    uf = proj[:, :D_FOURIER].astype(BF16)
    for g in range(N_FOURIER_GROUPS):
        lo, hi = g * FOURIER_GROUP_DIM, (g + 1) * FOURIER_GROUP_DIM
        cs = jnp.dot(uf[:, lo:hi], dft_ref[...], preferred_element_type=F32)
        ab_ref[0, 0, :, lo:hi] = cs[:, :FOURIER_GROUP_DIM].astype(BF16)
        ab_ref[0, 1, :, lo:hi] = cs[:, FOURIER_GROUP_DIM:].astype(BF16)
    ur_ref[...] = proj[:, D_FOURIER:D_FOURIER + D_LRU]
    yr_ref[...] = proj[:, D_FOURIER + D_LRU:].astype(BF16)


def _inproj_call(x2, sh, sc, g, w, dft_ch, batch, seq):
    n = batch * seq
    tm = WIDE_TILE
    tpb = seq // tm
    mod_spec = pl.BlockSpec((1, 1, D_MODEL), lambda i: (i // tpb, 0, 0))
    return pl.pallas_call(
        _inproj_kernel,
        out_shape=(
            jax.ShapeDtypeStruct((batch, 2, seq, D_FOURIER), BF16),
            jax.ShapeDtypeStruct((n, D_LRU), F32),
            jax.ShapeDtypeStruct((n, D_LRU), BF16),
        ),
        grid=(n // tm,),
        in_specs=[
            pl.BlockSpec((tm, D_MODEL), lambda i: (i, 0)),
            mod_spec, mod_spec,
            pl.BlockSpec((1, D_MODEL), lambda i: (0, 0)),
            pl.BlockSpec(w.shape, lambda i: (0, 0)),
            pl.BlockSpec(dft_ch.shape, lambda i: (0, 0)),
        ],
        out_specs=(
            pl.BlockSpec((1, 2, tm, D_FOURIER), lambda i: (i // tpb, 0, i % tpb, 0)),
            pl.BlockSpec((tm, D_LRU), lambda i: (i, 0)),
            pl.BlockSpec((tm, D_LRU), lambda i: (i, 0)),
        ),
        compiler_params=_params(("parallel",)),
        name="inproj",
    )(x2, sh, sc, g, w, dft_ch)


def _inproj_ctx_kernel(x_ref, sh_ref, sc_ref, g_ref, w_ref, ur_ref):
    hx = _rms_modulate(x_ref[...], g_ref[...], sh_ref[...], sc_ref[...]).astype(BF16)
    ur_ref[...] = jnp.dot(hx, w_ref[...], preferred_element_type=F32)


def _inproj_ctx_call(ctx2, sh, sc, g, w):
    n = ctx2.shape[0]
    vec = pl.BlockSpec((1, D_MODEL), lambda i: (0, 0))
    return pl.pallas_call(
        _inproj_ctx_kernel,
        out_shape=jax.ShapeDtypeStruct((n, D_LRU), F32),
        grid=(n // TOKEN_TILE,),
        in_specs=[pl.BlockSpec((TOKEN_TILE, D_MODEL), lambda i: (i, 0)), vec, vec, vec,
                  pl.BlockSpec(w.shape, lambda i: (0, 0))],
        out_specs=pl.BlockSpec((TOKEN_TILE, D_LRU), lambda i: (i, 0)),
        compiler_params=_params(("parallel",)),
        name="inproj_ctx",
    )(ctx2, sh, sc, g, w)


def _rglru_kernel(n_chunks, u_ref, up_ref, un_ref, cw_ref, cb_ref, wg_ref, bg_ref, lam_ref, h0_ref,
                  h_ref, a_s, b_s, h_s, state):
    tc, pitch = SCAN_CHUNK, SCAN_PITCH
    d = pl.program_id(0)
    k = pl.program_id(2)
    kk = k + d * (n_chunks - 1 - 2 * k)

    @pl.when(k == 0)
    def _():
        state[...] = h0_ref[0]

    lam = lam_ref[0]
    neg_lam = -lam
    softplus = jnp.maximum(neg_lam, 0.0) + jnp.log1p(jnp.exp(-jnp.abs(neg_lam)))
    rate = (-0.5 * LRU_C * np.log2(np.e).astype(np.float32)) * softplus
    has_prev = kk > 0
    has_next = kk < n_chunks - 1
    row = lax.broadcasted_iota(jnp.int32, (SUBLANES, D_LRU), 0)
    cw = cw_ref[...]

    def patch(rolled, first_rows, fix):
        if first_rows:
            return jnp.concatenate([fix(rolled[:SUBLANES]), rolled[SUBLANES:]], axis=0)
        return jnp.concatenate([rolled[:-SUBLANES], fix(rolled[-SUBLANES:])], axis=0)

    for i in range(SCAN_BATCH):
        u = u_ref[i]
        prev = jnp.where(has_prev, up_ref[i], 0.0)
        nxt = jnp.where(has_next, un_ref[i], 0.0)
        p2, p1, n1 = prev[SUBLANES - 2:SUBLANES - 1], prev[SUBLANES - 1:SUBLANES], nxt[0:1]
        um1 = patch(pltpu.roll(u, 1, 0), True, lambda t: jnp.where(row == 0, p1, t))
        um2 = patch(pltpu.roll(u, 2, 0), True, lambda t: jnp.where(row == 0, p2, jnp.where(row == 1, p1, t)))
        up1 = patch(pltpu.roll(u, tc - 1, 0), False, lambda t: jnp.where(row == SUBLANES - 1, n1, t))
        xc = um2 * cw[0:1] + um1 * cw[1:2] + u * cw[2:3] + up1 * cw[3:4] + cb_ref[...]
        gz = jnp.dot(xc.astype(BF16), wg_ref[0], preferred_element_type=F32) + bg_ref[0]
        t_a = jnp.tanh(gz[:, :D_LRU])
        t_x = jnp.tanh(gz[:, D_LRU:])
        a = jnp.exp2(rate + rate * t_a)
        half_xc = 0.5 * xc
        b = jnp.sqrt(1.0 - a * a) * (half_xc + half_xc * t_x)
        for j in range(N_SLABS):
            a_s[j, pl.ds(i * pitch, tc), :] = a[:, j * LANES:(j + 1) * LANES]
            b_s[j, pl.ds(i * pitch, tc), :] = b[:, j * LANES:(j + 1) * LANES]

    def step(s, h):
        t = s + d * (tc - 1 - 2 * s)
        out = []
        for j in range(N_SLABS):
            rows = pl.ds(t, SCAN_BATCH, stride=pitch)
            hj = a_s[j, rows, :] * h[j] + b_s[j, rows, :]
            h_s[j, rows, :] = hj
            out.append(hj)
        return tuple(out)

    st = state[...]
    h = lax.fori_loop(0, tc, step, tuple(st[:, j * LANES:(j + 1) * LANES] for j in range(N_SLABS)),
                      unroll=4)
    for j in range(N_SLABS):
        state[:, j * LANES:(j + 1) * LANES] = h[j]
    for i in range(SCAN_BATCH):
        for j in range(N_SLABS):
            h_ref[0, i, :, j * LANES:(j + 1) * LANES] = h_s[j, pl.ds(i * pitch, tc), :]


def _rglru_call(u3, conv_w, conv_b, wg, bg, lam, h0):
    batch, seq, _ = u3.shape
    n_chunks = seq // SCAN_CHUNK
    halo_blocks = SCAN_CHUNK // SUBLANES
    last_halo = seq // SUBLANES - 1

    def chunk(d, k):
        return k + d * (n_chunks - 1 - 2 * k)

    scratch = pltpu.VMEM((N_SLABS, SCAN_BATCH * SCAN_PITCH, LANES), F32)
    return pl.pallas_call(
        functools.partial(_rglru_kernel, n_chunks),
        out_shape=jax.ShapeDtypeStruct((2, batch, seq, D_LRU), F32),
        grid=(2, batch // SCAN_BATCH, n_chunks),
        in_specs=[
            pl.BlockSpec((SCAN_BATCH, SCAN_CHUNK, D_LRU), lambda d, g, k: (g, chunk(d, k), 0)),
            pl.BlockSpec((SCAN_BATCH, SUBLANES, D_LRU),
                         lambda d, g, k: (g, jnp.maximum(chunk(d, k) * halo_blocks - 1, 0), 0)),
            pl.BlockSpec((SCAN_BATCH, SUBLANES, D_LRU),
                         lambda d, g, k: (g, jnp.minimum((chunk(d, k) + 1) * halo_blocks, last_halo), 0)),
            pl.BlockSpec((CONV_WIDTH, D_LRU), lambda d, g, k: (0, 0)),
            pl.BlockSpec((1, D_LRU), lambda d, g, k: (0, 0)),
            pl.BlockSpec((1, D_LRU, 2 * D_LRU), lambda d, g, k: (d, 0, 0)),
            pl.BlockSpec((1, 1, 2 * D_LRU), lambda d, g, k: (d, 0, 0)),
            pl.BlockSpec((1, 1, D_LRU), lambda d, g, k: (d, 0, 0)),
            pl.BlockSpec((1, SCAN_BATCH, D_LRU), lambda d, g, k: (d, g, 0)),
        ],
        out_specs=pl.BlockSpec((1, SCAN_BATCH, SCAN_CHUNK, D_LRU), lambda d, g, k: (d, g, chunk(d, k), 0)),
        scratch_shapes=[scratch, scratch, scratch, pltpu.VMEM((SCAN_BATCH, D_LRU), F32)],
        compiler_params=_params(("arbitrary", "arbitrary", "arbitrary")),
        name="rglru",
    )(u3, u3, u3, conv_w, conv_b, wg, bg, lam, h0)


def _fourier_kernel(m_ref, ab_ref, o_ref):
    seq2 = m_ref.shape[1]
    rhs = ab_ref[0].reshape(seq2, D_FOURIER)
    o_ref[0] = jnp.dot(m_ref[...], rhs, preferred_element_type=F32).astype(BF16)


def _fourier_call(m_pos, ab):
    batch, _, seq, _ = ab.shape
    return pl.pallas_call(
        _fourier_kernel,
        out_shape=jax.ShapeDtypeStruct((batch, seq, D_FOURIER), BF16),
        grid=(seq // FOURIER_TILE, batch),
        in_specs=[
            pl.BlockSpec((FOURIER_TILE, 2 * seq), lambda m, b: (m, 0)),
            pl.BlockSpec((1, 2, seq, D_FOURIER), lambda m, b: (b, 0, 0, 0)),
        ],
        out_specs=pl.BlockSpec((1, FOURIER_TILE, D_FOURIER), lambda m, b: (b, m, 0)),
        compiler_params=_params(("parallel", "parallel")),
        name="fourier",
    )(m_pos, ab)


def _route(logits):
    lane = lax.broadcasted_iota(jnp.int32, logits.shape, 1)
    neg = -jnp.inf
    gl = jnp.where(lane < N_GROUPS, logits, neg)
    gmax = jnp.max(gl, axis=1, keepdims=True)
    grp = jnp.min(jnp.where(gl == gmax, lane, ROUTE_LANES), axis=1, keepdims=True)
    p_grp = 1.0 / jnp.sum(jnp.exp(gl - gmax), axis=1, keepdims=True)
    e_lane = lane - N_GROUPS
    in_grp = (e_lane >= 0) & (e_lane < N_EXPERTS) & ((e_lane // EXPERTS_PER_GROUP) == grp)
    el = jnp.where(in_grp, logits, neg)
    t1 = jnp.max(el, axis=1, keepdims=True)
    i1 = jnp.min(jnp.where(el == t1, lane, ROUTE_LANES), axis=1, keepdims=True)
    el2 = jnp.where(lane == i1, neg, el)
    t2 = jnp.max(el2, axis=1, keepdims=True)
    i2 = jnp.min(jnp.where(el2 == t2, lane, ROUTE_LANES), axis=1, keepdims=True)
    e = jnp.exp(t2 - t1)
    w1 = p_grp / (1.0 + e)
    w2 = p_grp * e / (1.0 + e)
    out = jnp.where(lane == 0, (i1 - N_GROUPS).astype(F32), 0.0)
    out = jnp.where(lane == 1, (i2 - N_GROUPS).astype(F32), out)
    out = jnp.where(lane == 2, w1, out)
    return jnp.where(lane == 3, w2, out)


def _merge_kernel(x_ref, sh1_ref, sc1_ref, g1_ref, sh2_ref, sc2_ref, n1_ref, n2_ref, wgate_ref,
                  fm_ref, hf_ref, hb_ref, yr_ref, wfo_ref, wlo_ref, wout_ref, wr_ref, br_ref,
                  x1_ref, h2_ref, route_ref):
    rows = [pl.ds(q * MERGE_GROUP_ROWS, MERGE_GROUP_ROWS) for q in range(x_ref.shape[0] // MERGE_GROUP_ROWS)]
    groups = range(len(rows))

    def dot(a, w_ref):
        return jnp.dot(a, w_ref[...], preferred_element_type=F32)

    x = [x_ref[r, :] for r in rows]
    hx = [_rms_modulate(v, n1_ref[...], sh1_ref[0], sc1_ref[0]).astype(BF16) for v in x]
    lr = []
    for r in rows:
        y = yr_ref[r, :].astype(F32)
        gelu = 0.5 * y * (1.0 + jnp.tanh(np.sqrt(2.0 / np.pi).astype(np.float32) * (y + 0.044715 * (y * y * y))))
        lr.append(((hf_ref[0, 0, r, :] + hb_ref[0, 0, r, :]) * gelu).astype(BF16))
    branch_f = [dot(fm_ref[0, r, :], wfo_ref) for r in rows]
    branch_r = [dot(v, wlo_ref) for v in lr]
    t = [jnp.tanh(v) for v in [dot(v, wgate_ref) for v in hx]]
    mixed = [((branch_f[q] + branch_r[q])
              + (t[q][:, :D_MODEL] * branch_f[q] + t[q][:, D_MODEL:] * branch_r[q])).astype(BF16) for q in groups]
    mix = [dot(v, wout_ref) for v in mixed]
    x1 = [x[q] + g1_ref[0] * mix[q] for q in groups]
    for q in groups:
        x1_ref[rows[q], :] = x1[q]
    h2 = [_rms_modulate(v, n2_ref[...], sh2_ref[0], sc2_ref[0]) for v in x1]
    route = []
    for v in h2:
        hi = v.astype(BF16)
        lo = (v - hi.astype(F32)).astype(BF16)
        parts = dot(hi, wr_ref) + dot(lo, wr_ref)
        route.append(_route(parts[:, :ROUTE_LANES] + parts[:, ROUTE_LANES:] + br_ref[...]))
    _store_token_tiled(h2_ref, jnp.concatenate(h2, axis=0))
    route_ref[...] = jnp.concatenate(route, axis=0)


def _merge_call(x2, mods, n1, n2, wgate, fm, h, yr, wfo, wlo, wout, wr, br, batch, seq):
    n = batch * seq
    tm = TOKEN_TILE
    tpb = seq // tm
    mod_spec = pl.BlockSpec((1, 1, D_MODEL), lambda i: (i // tpb, 0, 0))
    vec = pl.BlockSpec((1, D_MODEL), lambda i: (0, 0))
    tile = pl.BlockSpec((tm, D_MODEL), lambda i: (i, 0))

    def full(a):
        return pl.BlockSpec(a.shape, lambda i: (0,) * a.ndim)

    return pl.pallas_call(
        _merge_kernel,
        out_shape=(
            jax.ShapeDtypeStruct((n, D_MODEL), F32),
            jax.ShapeDtypeStruct((n * TOKEN_ROWS, LANES), U32),
            jax.ShapeDtypeStruct((n, ROUTE_LANES), F32),
        ),
        grid=(n // tm,),
        in_specs=[
            tile, mod_spec, mod_spec, mod_spec, mod_spec, mod_spec, vec, vec, full(wgate),
            pl.BlockSpec((1, tm, D_FOURIER), lambda i: (i // tpb, i % tpb, 0)),
            pl.BlockSpec((1, 1, tm, D_LRU), lambda i: (0, i // tpb, i % tpb, 0)),
            pl.BlockSpec((1, 1, tm, D_LRU), lambda i: (1, i // tpb, i % tpb, 0)),
            pl.BlockSpec((tm, D_LRU), lambda i: (i, 0)),
            full(wfo), full(wlo), full(wout), full(wr), full(br),
        ],
        out_specs=(tile, pl.BlockSpec((tm * TOKEN_ROWS, LANES), lambda i: (i, 0)),
                   pl.BlockSpec((tm, ROUTE_LANES), lambda i: (i, 0))),
        compiler_params=_params(("parallel",)),
        name="merge",
    )(x2, *mods, n1, n2, wgate, fm, h, h, yr, wfo, wlo, wout, wr, br)


def _token_rows(t, count=1):
    start = t * TOKEN_ROWS
    if not isinstance(start, int):
        start = pl.multiple_of(start, TOKEN_ROWS)
    return pl.ds(start, count * TOKEN_ROWS)


def _row_copy(src_ref, dst_ref, sem, src_tok, dst_tok):
    return pltpu.make_async_copy(src_ref.at[_token_rows(src_tok)], dst_ref.at[_token_rows(dst_tok)], sem)


def _moe_kernel(n_tokens, be_ref, bw_ref, off_ref, nv_ref, nxt_ref, so_ref, rows_ref,
                h2_hbm, wg_hbm, wu_hbm, wd_hbm, out_hbm,
                win, xbuf, y0, y1, y2, wg_f, wu_f, wd_f, wg_b, wu_b, wd_b, win_sem, w_sem, sct_sem):
    i = pl.program_id(0)
    n_assign = n_tokens * TOP_K
    ys = (y0, y1, y2)
    prev = jnp.maximum(i - 1, 0)

    def scatter_wait(s):
        pltpu.make_async_copy(ys[s], out_hbm.at[_token_rows(0, MOE_BLOCK)], sct_sem.at[s]).wait()

    def weight_copies(e, slot):
        return [pltpu.make_async_copy(w.at[e], f.at[slot], w_sem.at[slot])
                for w, f in ((wg_hbm, wg_f), (wu_hbm, wu_f), (wd_hbm, wd_f))]

    def gather_rows():
        base = off_ref[i]
        for r in range(MOE_BLOCK):
            xbuf[_token_rows(r)] = win[_token_rows(rows_ref[base + r] & LOCAL_MASK)]

    def issue_scatter_prev(s):
        base = off_ref[prev]
        nv = jnp.where(i > 0, nv_ref[prev], 0)
        dummy0 = n_assign + s * MOE_BLOCK
        for r in range(MOE_BLOCK):
            dst = lax.shift_right_logical(rows_ref[base + r], LOCAL_BITS)
            _row_copy(ys[s], out_hbm, sct_sem.at[s], r, jnp.where(r < nv, dst, dummy0 + r)).start(priority=r % 2)

    active = nv_ref[i] > 0
    prev_active = (i > 0) & (nv_ref[prev] > 0)
    w_slot = so_ref[i] % 2

    @pl.when(i == 0)
    def _():
        for y in ys:
            y[...] = jnp.zeros(y.shape, U32)
        fills = [pltpu.make_async_copy(ys[q], out_hbm.at[_token_rows(n_assign + q * MOE_BLOCK, MOE_BLOCK)],
                                       sct_sem.at[q]) for q in range(N_OUT_BUFFERS)]
        for cp in fills:
            cp.start()
        fills[-1].wait()
        for cp in weight_copies(be_ref[0], 0):
            cp.start()

    @pl.when(active & ((i == 0) | (bw_ref[i] != bw_ref[prev])))
    def _():
        cp = pltpu.make_async_copy(h2_hbm.at[_token_rows(bw_ref[i] * MOE_WINDOW, MOE_WINDOW)], win, win_sem)
        cp.start()
        cp.wait()

    @pl.when(active & ((i == 0) | (so_ref[i] != so_ref[prev])))
    def _():
        for cp in weight_copies(be_ref[i], w_slot):
            cp.wait()
        wg_b[...] = wg_f[w_slot].astype(BF16)
        wu_b[...] = wu_f[w_slot].astype(BF16)
        wd_b[...] = wd_f[w_slot].astype(BF16)

        @pl.when(nxt_ref[i] >= 0)
        def _():
            for cp in weight_copies(nxt_ref[i], 1 - w_slot):
                cp.start()

    for s in range(N_OUT_BUFFERS):
        prev_s, prev2_s = (s + 2) % N_OUT_BUFFERS, (s + 1) % N_OUT_BUFFERS

        @pl.when(active & (i % N_OUT_BUFFERS == s))
        def _(s=s, prev_s=prev_s):
            scatter_wait(s)
            gather_rows()
            issue_scatter_prev(prev_s)
            xb = _load_token_tiled(xbuf, MOE_BLOCK, BF16)
            hg = jnp.dot(xb, wg_b[...], preferred_element_type=F32)
            hu = jnp.dot(xb, wu_b[...], preferred_element_type=F32)
            hb = (hg * jax.nn.sigmoid(hg) * hu).astype(BF16)
            _store_token_tiled(ys[s], jnp.dot(hb, wd_b[...], preferred_element_type=F32))

        @pl.when(jnp.logical_not(active) & prev_active & (i % N_OUT_BUFFERS == s))
        def _(s=s, prev_s=prev_s, prev2_s=prev2_s):
            issue_scatter_prev(prev_s)
            scatter_wait(s)
            scatter_wait(prev2_s)
            scatter_wait(prev_s)


def _moe_call(blk_e, blk_w, blk_off, blk_nv, blk_next_e, blk_seg, rows, h2, w_g, w_u, w_d):
    n_steps = blk_e.shape[0]
    n_tokens = h2.shape[0] // TOKEN_ROWS
    any_spec = pl.BlockSpec(memory_space=pl.ANY)
    block_buf = pltpu.VMEM((MOE_BLOCK * TOKEN_ROWS, LANES), U32)
    grid_spec = pltpu.PrefetchScalarGridSpec(
        num_scalar_prefetch=7,
        grid=(n_steps,),
        in_specs=[any_spec, any_spec, any_spec, any_spec],
        out_specs=any_spec,
        scratch_shapes=[
            pltpu.VMEM((MOE_WINDOW * TOKEN_ROWS, LANES), U32),
            block_buf, block_buf, block_buf, block_buf,
            pltpu.VMEM((2, D_MODEL, D_EXPERT), F32),
            pltpu.VMEM((2, D_MODEL, D_EXPERT), F32),
            pltpu.VMEM((2, D_EXPERT, D_MODEL), F32),
            pltpu.VMEM((D_MODEL, D_EXPERT), BF16),
            pltpu.VMEM((D_MODEL, D_EXPERT), BF16),
            pltpu.VMEM((D_EXPERT, D_MODEL), BF16),
            pltpu.SemaphoreType.DMA,
            pltpu.SemaphoreType.DMA((2,)),
            pltpu.SemaphoreType.DMA((N_OUT_BUFFERS,)),
        ],
    )
    return pl.pallas_call(
        functools.partial(_moe_kernel, n_tokens),
        out_shape=jax.ShapeDtypeStruct(((n_tokens * TOP_K + N_DUMMY_ROWS) * TOKEN_ROWS, LANES), U32),
        grid_spec=grid_spec,
        compiler_params=_params(("arbitrary",)),
        name="moe",
    )(blk_e, blk_w, blk_off, blk_nv, blk_next_e, blk_seg, rows, h2, w_g, w_u, w_d)


def _final_kernel(x1_ref, y0_ref, y1_ref, route_ref, g2_ref, fg_ref, o_ref):
    route = route_ref[...]
    rows = route.shape[0]
    moe = (route[:, 2:3] * _load_token_tiled(y0_ref, rows, F32)
           + route[:, 3:4] * _load_token_tiled(y1_ref, rows, F32))
    x2 = x1_ref[...] + g2_ref[0] * moe
    o_ref[...] = x2 * lax.rsqrt(jnp.mean(x2 * x2, axis=-1, keepdims=True) + EPS) * fg_ref[...]


def _final_call(x1, out2, route, g2, fg, batch, seq):
    n = batch * seq
    tm = WIDE_TILE
    tpb = seq // tm
    tile = pl.BlockSpec((tm, D_MODEL), lambda i: (i, 0))
    return pl.pallas_call(
        _final_kernel,
        out_shape=jax.ShapeDtypeStruct((n, D_MODEL), F32),
        grid=(n // tm,),
        in_specs=[
            tile,
            pl.BlockSpec((tm * TOKEN_ROWS, LANES), lambda i: (i, 0)),
            pl.BlockSpec((tm * TOKEN_ROWS, LANES), lambda i: (i + n // tm, 0)),
            pl.BlockSpec((tm, ROUTE_LANES), lambda i: (i, 0)),
            pl.BlockSpec((1, 1, D_MODEL), lambda i: (i // tpb, 0, 0)),
            pl.BlockSpec((1, D_MODEL), lambda i: (0, 0)),
        ],
        out_specs=tile,
        compiler_params=_params(("parallel",)),
        name="final",
    )(x1, out2, out2, route, g2, fg)


def _channel_dft():
    j = np.arange(FOURIER_GROUP_DIM)
    ang = 2.0 * np.pi * np.outer(j, j) / FOURIER_GROUP_DIM
    return np.concatenate([np.cos(ang), np.sin(ang)], axis=1).astype(np.float32)


def _position_dft(seq):
    rows = seq // GRID_W
    assert GRID_W % rows == 0
    r, c = np.divmod(np.arange(seq), GRID_W)
    phase = (np.outer(r, r) * (GRID_W // rows) + np.outer(c, c)) % GRID_W
    ang = 2.0 * np.pi * phase / GRID_W
    scale = 1.0 / np.sqrt(float(seq) * FOURIER_GROUP_DIM)
    return np.concatenate([np.cos(ang), -np.sin(ang)], axis=1) * scale


def _block_diag(w):
    heads, hd, _ = w.shape
    eye = jnp.eye(heads, dtype=w.dtype)
    return jnp.einsum('hij,hg->higj', w, eye).reshape(heads * hd, heads * hd)


def _gate_weights(w_a, w_x):
    return jnp.stack([0.5 * jnp.concatenate([_block_diag(w_a[d]), _block_diag(w_x[d])], axis=1)
                      for d in range(2)]).astype(BF16)


def _dispatch(eid, n_tokens):
    n_assign = n_tokens * TOP_K
    n_seg = (n_tokens // MOE_WINDOW) * N_EXPERTS
    n_blk = n_assign // MOE_BLOCK + n_seg
    a = jnp.arange(n_assign, dtype=jnp.int32)
    seg_of = (a // (TOP_K * MOE_WINDOW)) * N_EXPERTS + eid.reshape(-1)
    _, order = lax.sort((seg_of, a), num_keys=1)
    segs = jnp.arange(n_seg, dtype=jnp.int32)
    counts = jnp.sum((seg_of[:, None] == segs[None, :]).astype(jnp.int32), axis=0)
    start = jnp.cumsum(counts) - counts
    blocks = (counts + MOE_BLOCK - 1) // MOE_BLOCK
    bend = jnp.cumsum(blocks)
    b = jnp.arange(n_blk + 1, dtype=jnp.int32)
    ended = (b[:, None] >= bend[None, :]).astype(jnp.int32)
    seg = jnp.minimum(jnp.sum(ended, axis=1), n_seg - 1)
    onehot = (seg[:, None] == segs[None, :]).astype(jnp.int32)
    in_seg = (b - jnp.sum(onehot * (bend - blocks)[None, :], axis=1)) * MOE_BLOCK
    blk_off = jnp.sum(onehot * start[None, :], axis=1) + in_seg
    blk_nv = jnp.clip(jnp.sum(onehot * counts[None, :], axis=1) - in_seg, 0, MOE_BLOCK)
    seg_end = jnp.sum(onehot * bend[None, :], axis=1)
    follows = (b[None, :] == seg_end[:, None]).astype(jnp.int32)
    next_e = jnp.sum(follows * (seg % N_EXPERTS)[None, :], axis=1)
    next_nv = jnp.sum(follows * blk_nv[None, :], axis=1)
    blk_next_e = jnp.where(next_nv > 0, next_e, -1)

    tok = lax.shift_right_logical(order, 1)
    dst = (order & 1) * n_tokens + tok
    rows = lax.shift_left(dst, LOCAL_BITS) | (tok % MOE_WINDOW)
    dummy = lax.shift_left(n_assign + jnp.arange(N_DUMMY_ROWS, dtype=jnp.int32), LOCAL_BITS)
    seg_ordinal = jnp.cumsum((blocks > 0).astype(jnp.int32)) - 1
    blk_seg = jnp.sum(onehot * seg_ordinal[None, :], axis=1)
    assert MOE_WINDOW <= LOCAL_MASK + 1 and (n_assign + N_DUMMY_ROWS) << LOCAL_BITS <= 2 ** 32
    return seg % N_EXPERTS, seg // N_EXPERTS, blk_off, blk_nv, blk_next_e, blk_seg, jnp.concatenate([rows, dummy])


def kernel(x, c, ctx, c_ctx, w_mod, b_mod, norm1_g, w_in, conv_w, conv_b, lru_wa, lru_ba, lru_wx, lru_bx,
           lru_lam, w_fourier_out, w_lru_out, w_out, norm2_g, w_group, b_group, w_expert_router,
           b_expert_router, w_gate_e, w_up_e, w_down_e, final_g):
    batch, seq, _ = x.shape
    ctx_len = ctx.shape[1]
    n = batch * seq
    assert w_mod.shape[0] == 1, "single-layer stack only: the context stream is not carried across layers"
    x2 = x.reshape(n, D_MODEL)
    dft_ch = jnp.asarray(_channel_dft().astype(BF16))
    m_pos = jnp.asarray(_position_dft(seq).astype(BF16))

    for l in range(1):
        c_all = jnp.concatenate([c, c_ctx[None], jnp.zeros((MOD_ROWS - batch - 1, D_MODEL), F32)], axis=0)
        mod = _mod_call(c_all, w_mod[l], b_mod[l][None])
        sh1, sc1, g1, sh2, sc2, g2 = [m[:batch, None, :] for m in jnp.split(mod, 6, axis=-1)]
        csh1, csc1 = mod[batch:batch + 1, :D_MODEL], mod[batch:batch + 1, D_MODEL:2 * D_MODEL]

        w_in_b = w_in[l].astype(BF16)
        n1 = norm1_g[l][None]
        wg = _gate_weights(lru_wa[l], lru_wx[l])
        bg = 0.5 * jnp.concatenate([lru_ba[l], lru_bx[l]], axis=-1)[:, None, :]
        lam = lru_lam[l][:, None, :]
        cb = conv_b[l][None]

        uc = _inproj_ctx_call(ctx.reshape(batch * ctx_len, D_MODEL), csh1, csc1, n1,
                              w_in_b[:, D_FOURIER:D_FOURIER + D_LRU])
        hc = _rglru_call(uc.reshape(batch, ctx_len, D_LRU), conv_w[l], cb, wg, bg, lam,
                         jnp.zeros((2, batch, D_LRU), F32))
        h0 = jnp.stack([hc[0, :, -1], hc[1, :, 0]])

        ab, ur, yr = _inproj_call(x2, sh1, sc1, n1, w_in_b[:, :D_FOURIER + 2 * D_LRU], dft_ch, batch, seq)
        h = _rglru_call(ur.reshape(batch, seq, D_LRU), conv_w[l], cb, wg, bg, lam, h0)
        fm = _fourier_call(m_pos, ab)

        w_route = jnp.zeros((D_MODEL, ROUTE_LANES), F32)
        w_route = w_route.at[:, :N_GROUPS].set(w_group[l]).at[:, N_GROUPS:N_GROUPS + N_EXPERTS].set(
            w_expert_router[l])
        b_route = jnp.zeros((1, ROUTE_LANES), F32)
        b_route = b_route.at[0, :N_GROUPS].set(b_group[l]).at[0, N_GROUPS:N_GROUPS + N_EXPERTS].set(
            b_expert_router[l])
        w_route_hi = w_route.astype(BF16)
        w_route_lo = (w_route - w_route_hi.astype(F32)).astype(BF16)
        w_route = jnp.concatenate([w_route_hi, w_route_lo], axis=1)
        x1, h2, route = _merge_call(
            x2, (sh1, sc1, g1, sh2, sc2), n1, norm2_g[l][None],
            (0.5 * w_in[l][:, D_FOURIER + 2 * D_LRU:]).astype(BF16),
            fm, h, yr, w_fourier_out[l].astype(BF16), w_lru_out[l].astype(BF16),
            (0.5 * w_out[l]).astype(BF16),
            w_route, b_route, batch, seq)

        eid = route[:, :TOP_K].astype(jnp.int32)
        out2 = _moe_call(*_dispatch(eid, n), h2, w_gate_e[l], w_up_e[l], w_down_e[l])
        x2 = _final_call(x1, out2, route, g2, final_g[None], batch, seq)
    return x2.reshape(batch, seq, D_MODEL)
```

```python
import functools

import numpy as np
import jax
import jax.numpy as jnp
from jax import lax
from jax.experimental import pallas as pl
from jax.experimental.pallas import tpu as pltpu

F32 = jnp.float32
BF16 = jnp.bfloat16

LANES = 128
SUBLANES = 8
VMEM_LIMIT_BYTES = 56 * 1024 * 1024

D_MODEL = 1024
GRID_W = 64
EPS = 1e-6
N_FOURIER_GROUPS = 4
FOURIER_GROUP_DIM = 128
D_FOURIER = N_FOURIER_GROUPS * FOURIER_GROUP_DIM
N_LRU_HEADS = 8
LRU_HEAD_DIM = 64
D_LRU = N_LRU_HEADS * LRU_HEAD_DIM
CONV_WIDTH = 4
LRU_C = 8.0
N_GROUPS = 4
EXPERTS_PER_GROUP = 8
N_EXPERTS = N_GROUPS * EXPERTS_PER_GROUP
TOP_K = 2
D_EXPERT = 512

MOD_ROWS = 24
MOD_BLOCK_N = 1536
TOKEN_TILE = 512
WIDE_TILE = 1024
MERGE_GROUP_ROWS = 128
SCAN_BATCH = SUBLANES
SCAN_CHUNK = 256
SCAN_PITCH = SCAN_CHUNK + SUBLANES
N_SLABS = D_LRU // LANES
FOURIER_TILE = 1024
MOE_BLOCK = 128
ROUTE_LANES = LANES
N_OUT_BUFFERS = 3
N_DUMMY_ROWS = N_OUT_BUFFERS * MOE_BLOCK
MOE_WINDOW = 16384
LOCAL_BITS = 14
LOCAL_MASK = (1 << LOCAL_BITS) - 1


def _params(semantics):
    return pltpu.CompilerParams(dimension_semantics=semantics, vmem_limit_bytes=VMEM_LIMIT_BYTES)


U32 = jnp.uint32
HALF = D_MODEL // 2
TOKEN_ROWS = HALF // LANES
HIGH_HALF_WORD = np.uint32(0xFFFF0000)


def _pack_bf16_pairs(x):
    bits = pltpu.bitcast(x.astype(BF16).astype(F32), U32)
    return (bits[:, :HALF] >> 16) | (bits[:, HALF:] & HIGH_HALF_WORD)


def _unpack_bf16_pairs(u, dtype):
    lo = pltpu.bitcast(u << 16, F32)
    hi = pltpu.bitcast(u & HIGH_HALF_WORD, F32)
    return jnp.concatenate([lo, hi], axis=1).astype(dtype)


def _store_token_tiled(ref, x):
    rows = x.shape[0]
    packed = _pack_bf16_pairs(x)
    for s in range(TOKEN_ROWS):
        ref[pl.ds(s, rows, stride=TOKEN_ROWS), :] = packed[:, s * LANES:(s + 1) * LANES]


def _load_token_tiled(ref, rows, dtype):
    packed = jnp.concatenate([ref[pl.ds(s, rows, stride=TOKEN_ROWS), :] for s in range(TOKEN_ROWS)], axis=1)
    return _unpack_bf16_pairs(packed, dtype)


def _rms_modulate(x, g, shift, scale):
    y = x * lax.rsqrt(jnp.mean(x * x, axis=-1, keepdims=True) + EPS) * g
    return y * (1.0 + scale) + shift


def _mod_kernel(c_ref, w_ref, b_ref, o_ref):
    c = c_ref[...]
    s = c * jax.nn.sigmoid(c)
    o_ref[...] = jnp.dot(s, w_ref[...], preferred_element_type=F32,
                         precision=lax.Precision.HIGHEST) + b_ref[...]


def _mod_call(c_all, w_mod, b_mod):
    n_out = w_mod.shape[1]
    return pl.pallas_call(
        _mod_kernel,
        out_shape=jax.ShapeDtypeStruct((MOD_ROWS, n_out), F32),
        grid=(n_out // MOD_BLOCK_N,),
        in_specs=[
            pl.BlockSpec((MOD_ROWS, D_MODEL), lambda j: (0, 0)),
            pl.BlockSpec((D_MODEL, MOD_BLOCK_N), lambda j: (0, j)),
            pl.BlockSpec((1, MOD_BLOCK_N), lambda j: (0, j)),
        ],
        out_specs=pl.BlockSpec((MOD_ROWS, MOD_BLOCK_N), lambda j: (0, j)),
        compiler_params=_params(("arbitrary",)),
        name="mod",
    )(c_all, w_mod, b_mod)


def _inproj_kernel(x_ref, sh_ref, sc_ref, g_ref, w_ref, dft_ref, ab_ref, ur_ref, yr_ref):
    hx = _rms_modulate(x_ref[...], g_ref[...], sh_ref[0], sc_ref[0]).astype(BF16)
    proj = jnp.dot(hx, w_ref[...], preferred_element_type=F32)
    uf = proj[:, :D_FOURIER].astype(BF16)
    for g in range(N_FOURIER_GROUPS):
        lo, hi = g * FOURIER_GROUP_DIM, (g + 1) * FOURIER_GROUP_DIM
        cs = jnp.dot(uf[:, lo:hi], dft_ref[...], preferred_element_type=F32)
        ab_ref[0, 0, :, lo:hi] = cs[:, :FOURIER_GROUP_DIM].astype(BF16)
        ab_ref[0, 1, :, lo:hi] = cs[:, FOURIER_GROUP_DIM:].astype(BF16)
    ur_ref[...] = proj[:, D_FOURIER:D_FOURIER + D_LRU]
    yr_ref[...] = proj[:, D_FOURIER + D_LRU:].astype(BF16)


def _inproj_call(x2, sh, sc, g, w, dft_ch, batch, seq):
    n = batch * seq
    tm = WIDE_TILE
    tpb = seq // tm
    mod_spec = pl.BlockSpec((1, 1, D_MODEL), lambda i: (i // tpb, 0, 0))
    return pl.pallas_call(
        _inproj_kernel,
        out_shape=(
            jax.ShapeDtypeStruct((batch, 2, seq, D_FOURIER), BF16),
            jax.ShapeDtypeStruct((n, D_LRU), F32),
            jax.ShapeDtypeStruct((n, D_LRU), BF16),
        ),
        grid=(n // tm,),
        in_specs=[
            pl.BlockSpec((tm, D_MODEL), lambda i: (i, 0)),
            mod_spec, mod_spec,
            pl.BlockSpec((1, D_MODEL), lambda i: (0, 0)),
            pl.BlockSpec(w.shape, lambda i: (0, 0)),
            pl.BlockSpec(dft_ch.shape, lambda i: (0, 0)),
        ],
        out_specs=(
            pl.BlockSpec((1, 2, tm, D_FOURIER), lambda i: (i // tpb, 0, i % tpb, 0)),
            pl.BlockSpec((tm, D_LRU), lambda i: (i, 0)),
            pl.BlockSpec((tm, D_LRU), lambda i: (i, 0)),
        ),
        compiler_params=_params(("parallel",)),
        name="inproj",
    )(x2, sh, sc, g, w, dft_ch)


def _inproj_ctx_kernel(x_ref, sh_ref, sc_ref, g_ref, w_ref, ur_ref):
    hx = _rms_modulate(x_ref[...], g_ref[...], sh_ref[...], sc_ref[...]).astype(BF16)
    ur_ref[...] = jnp.dot(hx, w_ref[...], preferred_element_type=F32)


def _inproj_ctx_call(ctx2, sh, sc, g, w):
    n = ctx2.shape[0]
    vec = pl.BlockSpec((1, D_MODEL), lambda i: (0, 0))
    return pl.pallas_call(
        _inproj_ctx_kernel,
        out_shape=jax.ShapeDtypeStruct((n, D_LRU), F32),
        grid=(n // TOKEN_TILE,),
        in_specs=[pl.BlockSpec((TOKEN_TILE, D_MODEL), lambda i: (i, 0)), vec, vec, vec,
                  pl.BlockSpec(w.shape, lambda i: (0, 0))],
        out_specs=pl.BlockSpec((TOKEN_TILE, D_LRU), lambda i: (i, 0)),
        compiler_params=_params(("parallel",)),
        name="inproj_ctx",
    )(ctx2, sh, sc, g, w)


def _rglru_kernel(n_chunks, u_ref, up_ref, un_ref, cw_ref, cb_ref, wg_ref, bg_ref, lam_ref, h0_ref,
                  h_ref, a_s, b_s, h_s, state):
    tc, pitch = SCAN_CHUNK, SCAN_PITCH
    d = pl.program_id(0)
    k = pl.program_id(2)
    kk = k + d * (n_chunks - 1 - 2 * k)

    @pl.when(k == 0)
    def _():
        state[...] = h0_ref[0]

    lam = lam_ref[0]
    neg_lam = -lam
    softplus = jnp.maximum(neg_lam, 0.0) + jnp.log1p(jnp.exp(-jnp.abs(neg_lam)))
    rate = (-0.5 * LRU_C * np.log2(np.e).astype(np.float32)) * softplus
    has_prev = kk > 0
    has_next = kk < n_chunks - 1
    row = lax.broadcasted_iota(jnp.int32, (SUBLANES, D_LRU), 0)
    cw = cw_ref[...]

    def patch(rolled, first_rows, fix):
        if first_rows:
            return jnp.concatenate([fix(rolled[:SUBLANES]), rolled[SUBLANES:]], axis=0)
        return jnp.concatenate([rolled[:-SUBLANES], fix(rolled[-SUBLANES:])], axis=0)

    for i in range(SCAN_BATCH):
        u = u_ref[i]
        prev = jnp.where(has_prev, up_ref[i], 0.0)
        nxt = jnp.where(has_next, un_ref[i], 0.0)
        p2, p1, n1 = prev[SUBLANES - 2:SUBLANES - 1], prev[SUBLANES - 1:SUBLANES], nxt[0:1]
        um1 = patch(pltpu.roll(u, 1, 0), True, lambda t: jnp.where(row == 0, p1, t))
        um2 = patch(pltpu.roll(u, 2, 0), True, lambda t: jnp.where(row == 0, p2, jnp.where(row == 1, p1, t)))
        up1 = patch(pltpu.roll(u, tc - 1, 0), False, lambda t: jnp.where(row == SUBLANES - 1, n1, t))
        xc = um2 * cw[0:1] + um1 * cw[1:2] + u * cw[2:3] + up1 * cw[3:4] + cb_ref[...]
        gz = jnp.dot(xc.astype(BF16), wg_ref[0], preferred_element_type=F32) + bg_ref[0]
        t_a = jnp.tanh(gz[:, :D_LRU])
        t_x = jnp.tanh(gz[:, D_LRU:])
        a = jnp.exp2(rate + rate * t_a)
        half_xc = 0.5 * xc
        b = jnp.sqrt(1.0 - a * a) * (half_xc + half_xc * t_x)
        for j in range(N_SLABS):
            a_s[j, pl.ds(i * pitch, tc), :] = a[:, j * LANES:(j + 1) * LANES]
            b_s[j, pl.ds(i * pitch, tc), :] = b[:, j * LANES:(j + 1) * LANES]

    def step(s, h):
        t = s + d * (tc - 1 - 2 * s)
        out = []
        for j in range(N_SLABS):
            rows = pl.ds(t, SCAN_BATCH, stride=pitch)
            hj = a_s[j, rows, :] * h[j] + b_s[j, rows, :]
            h_s[j, rows, :] = hj
            out.append(hj)
        return tuple(out)

    st = state[...]
    h = lax.fori_loop(0, tc, step, tuple(st[:, j * LANES:(j + 1) * LANES] for j in range(N_SLABS)),
                      unroll=4)
    for j in range(N_SLABS):
        state[:, j * LANES:(j + 1) * LANES] = h[j]
    for i in range(SCAN_BATCH):
        for j in range(N_SLABS):
            h_ref[0, i, :, j * LANES:(j + 1) * LANES] = h_s[j, pl.ds(i * pitch, tc), :]


def _rglru_call(u3, conv_w, conv_b, wg, bg, lam, h0):
    batch, seq, _ = u3.shape
    n_chunks = seq // SCAN_CHUNK
    halo_blocks = SCAN_CHUNK // SUBLANES
    last_halo = seq // SUBLANES - 1

    def chunk(d, k):
        return k + d * (n_chunks - 1 - 2 * k)

    scratch = pltpu.VMEM((N_SLABS, SCAN_BATCH * SCAN_PITCH, LANES), F32)
    return pl.pallas_call(
        functools.partial(_rglru_kernel, n_chunks),
        out_shape=jax.ShapeDtypeStruct((2, batch, seq, D_LRU), F32),
        grid=(2, batch // SCAN_BATCH, n_chunks),
        in_specs=[
            pl.BlockSpec((SCAN_BATCH, SCAN_CHUNK, D_LRU), lambda d, g, k: (g, chunk(d, k), 0)),
            pl.BlockSpec((SCAN_BATCH, SUBLANES, D_LRU),
                         lambda d, g, k: (g, jnp.maximum(chunk(d, k) * halo_blocks - 1, 0), 0)),
            pl.BlockSpec((SCAN_BATCH, SUBLANES, D_LRU),
                         lambda d, g, k: (g, jnp.minimum((chunk(d, k) + 1) * halo_blocks, last_halo), 0)),
            pl.BlockSpec((CONV_WIDTH, D_LRU), lambda d, g, k: (0, 0)),
            pl.BlockSpec((1, D_LRU), lambda d, g, k: (0, 0)),
            pl.BlockSpec((1, D_LRU, 2 * D_LRU), lambda d, g, k: (d, 0, 0)),
            pl.BlockSpec((1, 1, 2 * D_LRU), lambda d, g, k: (d, 0, 0)),
            pl.BlockSpec((1, 1, D_LRU), lambda d, g, k: (d, 0, 0)),
            pl.BlockSpec((1, SCAN_BATCH, D_LRU), lambda d, g, k: (d, g, 0)),
        ],
        out_specs=pl.BlockSpec((1, SCAN_BATCH, SCAN_CHUNK, D_LRU), lambda d, g, k: (d, g, chunk(d, k), 0)),
        scratch_shapes=[scratch, scratch, scratch, pltpu.VMEM((SCAN_BATCH, D_LRU), F32)],
        compiler_params=_params(("arbitrary", "arbitrary", "arbitrary")),
        name="rglru",
    )(u3, u3, u3, conv_w, conv_b, wg, bg, lam, h0)


def _fourier_kernel(m_ref, ab_ref, o_ref):
    seq2 = m_ref.shape[1]
    rhs = ab_ref[0].reshape(seq2, D_FOURIER)
    o_ref[0] = jnp.dot(m_ref[...], rhs, preferred_element_type=F32).astype(BF16)


def _fourier_call(m_pos, ab):
    batch, _, seq, _ = ab.shape
    return pl.pallas_call(
        _fourier_kernel,
        out_shape=jax.ShapeDtypeStruct((batch, seq, D_FOURIER), BF16),
        grid=(seq // FOURIER_TILE, batch),
        in_specs=[
            pl.BlockSpec((FOURIER_TILE, 2 * seq), lambda m, b: (m, 0)),
            pl.BlockSpec((1, 2, seq, D_FOURIER), lambda m, b: (b, 0, 0, 0)),
        ],
        out_specs=pl.BlockSpec((1, FOURIER_TILE, D_FOURIER), lambda m, b: (b, m, 0)),
        compiler_params=_params(("parallel", "parallel")),
        name="fourier",
    )(m_pos, ab)


def _route(logits):
    lane = lax.broadcasted_iota(jnp.int32, logits.shape, 1)
    neg = -jnp.inf
    gl = jnp.where(lane < N_GROUPS, logits, neg)
    gmax = jnp.max(gl, axis=1, keepdims=True)
    grp = jnp.min(jnp.where(gl == gmax, lane, ROUTE_LANES), axis=1, keepdims=True)
    p_grp = 1.0 / jnp.sum(jnp.exp(gl - gmax), axis=1, keepdims=True)
    e_lane = lane - N_GROUPS
    in_grp = (e_lane >= 0) & (e_lane < N_EXPERTS) & ((e_lane // EXPERTS_PER_GROUP) == grp)
    el = jnp.where(in_grp, logits, neg)
    t1 = jnp.max(el, axis=1, keepdims=True)
    i1 = jnp.min(jnp.where(el == t1, lane, ROUTE_LANES), axis=1, keepdims=True)
    el2 = jnp.where(lane == i1, neg, el)
    t2 = jnp.max(el2, axis=1, keepdims=True)
    i2 = jnp.min(jnp.where(el2 == t2, lane, ROUTE_LANES), axis=1, keepdims=True)
    e = jnp.exp(t2 - t1)
    w1 = p_grp / (1.0 + e)
    w2 = p_grp * e / (1.0 + e)
    out = jnp.where(lane == 0, (i1 - N_GROUPS).astype(F32), 0.0)
    out = jnp.where(lane == 1, (i2 - N_GROUPS).astype(F32), out)
    out = jnp.where(lane == 2, w1, out)
    return jnp.where(lane == 3, w2, out)


def _merge_kernel(x_ref, sh1_ref, sc1_ref, g1_ref, sh2_ref, sc2_ref, n1_ref, n2_ref, wgate_ref,
                  fm_ref, hf_ref, hb_ref, yr_ref, wfo_ref, wlo_ref, wout_ref, wr_ref, br_ref,
                  x1_ref, h2_ref, route_ref):
    rows = [pl.ds(q * MERGE_GROUP_ROWS, MERGE_GROUP_ROWS) for q in range(x_ref.shape[0] // MERGE_GROUP_ROWS)]
    groups = range(len(rows))

    def dot(a, w_ref):
        return jnp.dot(a, w_ref[...], preferred_element_type=F32)

    x = [x_ref[r, :] for r in rows]
    hx = [_rms_modulate(v, n1_ref[...], sh1_ref[0], sc1_ref[0]).astype(BF16) for v in x]
    lr = []
    for r in rows:
        y = yr_ref[r, :].astype(F32)
        gelu = 0.5 * y * (1.0 + jnp.tanh(np.sqrt(2.0 / np.pi).astype(np.float32) * (y + 0.044715 * (y * y * y))))
        lr.append(((hf_ref[0, 0, r, :] + hb_ref[0, 0, r, :]) * gelu).astype(BF16))
    branch_f = [dot(fm_ref[0, r, :], wfo_ref) for r in rows]
    branch_r = [dot(v, wlo_ref) for v in lr]
    t = [jnp.tanh(v) for v in [dot(v, wgate_ref) for v in hx]]
    mixed = [((branch_f[q] + branch_r[q])
              + (t[q][:, :D_MODEL] * branch_f[q] + t[q][:, D_MODEL:] * branch_r[q])).astype(BF16) for q in groups]
    mix = [dot(v, wout_ref) for v in mixed]
    x1 = [x[q] + g1_ref[0] * mix[q] for q in groups]
    for q in groups:
        x1_ref[rows[q], :] = x1[q]
    h2 = [_rms_modulate(v, n2_ref[...], sh2_ref[0], sc2_ref[0]) for v in x1]
    route = []
    for v in h2:
        hi = v.astype(BF16)
        lo = (v - hi.astype(F32)).astype(BF16)
        parts = dot(hi, wr_ref) + dot(lo, wr_ref)
        route.append(_route(parts[:, :ROUTE_LANES] + parts[:, ROUTE_LANES:] + br_ref[...]))
    _store_token_tiled(h2_ref, jnp.concatenate(h2, axis=0))
    route_ref[...] = jnp.concatenate(route, axis=0)


def _merge_call(x2, mods, n1, n2, wgate, fm, h, yr, wfo, wlo, wout, wr, br, batch, seq):
    n = batch * seq
    tm = TOKEN_TILE
    tpb = seq // tm
    mod_spec = pl.BlockSpec((1, 1, D_MODEL), lambda i: (i // tpb, 0, 0))
    vec = pl.BlockSpec((1, D_MODEL), lambda i: (0, 0))
    tile = pl.BlockSpec((tm, D_MODEL), lambda i: (i, 0))

    def full(a):
        return pl.BlockSpec(a.shape, lambda i: (0,) * a.ndim)

    return pl.pallas_call(
        _merge_kernel,
        out_shape=(
            jax.ShapeDtypeStruct((n, D_MODEL), F32),
            jax.ShapeDtypeStruct((n * TOKEN_ROWS, LANES), U32),
            jax.ShapeDtypeStruct((n, ROUTE_LANES), F32),
        ),
        grid=(n // tm,),
        in_specs=[
            tile, mod_spec, mod_spec, mod_spec, mod_spec, mod_spec, vec, vec, full(wgate),
            pl.BlockSpec((1, tm, D_FOURIER), lambda i: (i // tpb, i % tpb, 0)),
            pl.BlockSpec((1, 1, tm, D_LRU), lambda i: (0, i // tpb, i % tpb, 0)),
            pl.BlockSpec((1, 1, tm, D_LRU), lambda i: (1, i // tpb, i % tpb, 0)),
            pl.BlockSpec((tm, D_LRU), lambda i: (i, 0)),
            full(wfo), full(wlo), full(wout), full(wr), full(br),
        ],
        out_specs=(tile, pl.BlockSpec((tm * TOKEN_ROWS, LANES), lambda i: (i, 0)),
                   pl.BlockSpec((tm, ROUTE_LANES), lambda i: (i, 0))),
        compiler_params=_params(("parallel",)),
        name="merge",
    )(x2, *mods, n1, n2, wgate, fm, h, h, yr, wfo, wlo, wout, wr, br)


def _token_rows(t, count=1):
    start = t * TOKEN_ROWS
    if not isinstance(start, int):
        start = pl.multiple_of(start, TOKEN_ROWS)
    return pl.ds(start, count * TOKEN_ROWS)


def _row_copy(src_ref, dst_ref, sem, src_tok, dst_tok):
    return pltpu.make_async_copy(src_ref.at[_token_rows(src_tok)], dst_ref.at[_token_rows(dst_tok)], sem)


def _moe_kernel(n_tokens, be_ref, bw_ref, off_ref, nv_ref, nxt_ref, so_ref, rows_ref,
                h2_hbm, wg_hbm, wu_hbm, wd_hbm, out_hbm,
                win, xbuf, y0, y1, y2, wg_f, wu_f, wd_f, wg_b, wu_b, wd_b, win_sem, w_sem, sct_sem):
    i = pl.program_id(0)
    n_assign = n_tokens * TOP_K
    ys = (y0, y1, y2)
    prev = jnp.maximum(i - 1, 0)

    def scatter_wait(s):
        pltpu.make_async_copy(ys[s], out_hbm.at[_token_rows(0, MOE_BLOCK)], sct_sem.at[s]).wait()

    def weight_copies(e, slot):
        return [pltpu.make_async_copy(w.at[e], f.at[slot], w_sem.at[slot])
                for w, f in ((wg_hbm, wg_f), (wu_hbm, wu_f), (wd_hbm, wd_f))]

    def gather_rows():
        base = off_ref[i]
        for r in range(MOE_BLOCK):
            xbuf[_token_rows(r)] = win[_token_rows(rows_ref[base + r] & LOCAL_MASK)]

    def issue_scatter_prev(s):
        base = off_ref[prev]
        nv = jnp.where(i > 0, nv_ref[prev], 0)
        dummy0 = n_assign + s * MOE_BLOCK
        for r in range(MOE_BLOCK):
            dst = lax.shift_right_logical(rows_ref[base + r], LOCAL_BITS)
            _row_copy(ys[s], out_hbm, sct_sem.at[s], r, jnp.where(r < nv, dst, dummy0 + r)).start(priority=r % 2)

    active = nv_ref[i] > 0
    prev_active = (i > 0) & (nv_ref[prev] > 0)
    w_slot = so_ref[i] % 2

    @pl.when(i == 0)
    def _():
        for y in ys:
            y[...] = jnp.zeros(y.shape, U32)
        fills = [pltpu.make_async_copy(ys[q], out_hbm.at[_token_rows(n_assign + q * MOE_BLOCK, MOE_BLOCK)],
                                       sct_sem.at[q]) for q in range(N_OUT_BUFFERS)]
        for cp in fills:
            cp.start()
        fills[-1].wait()
        for cp in weight_copies(be_ref[0], 0):
            cp.start()

    @pl.when(active & ((i == 0) | (bw_ref[i] != bw_ref[prev])))
    def _():
        cp = pltpu.make_async_copy(h2_hbm.at[_token_rows(bw_ref[i] * MOE_WINDOW, MOE_WINDOW)], win, win_sem)
        cp.start()
        cp.wait()

    @pl.when(active & ((i == 0) | (so_ref[i] != so_ref[prev])))
    def _():
        for cp in weight_copies(be_ref[i], w_slot):
            cp.wait()
        wg_b[...] = wg_f[w_slot].astype(BF16)
        wu_b[...] = wu_f[w_slot].astype(BF16)
        wd_b[...] = wd_f[w_slot].astype(BF16)

        @pl.when(nxt_ref[i] >= 0)
        def _():
            for cp in weight_copies(nxt_ref[i], 1 - w_slot):
                cp.start()

    for s in range(N_OUT_BUFFERS):
        prev_s, prev2_s = (s + 2) % N_OUT_BUFFERS, (s + 1) % N_OUT_BUFFERS

        @pl.when(active & (i % N_OUT_BUFFERS == s))
        def _(s=s, prev_s=prev_s):
            scatter_wait(s)
            gather_rows()
            issue_scatter_prev(prev_s)
            xb = _load_token_tiled(xbuf, MOE_BLOCK, BF16)
            hg = jnp.dot(xb, wg_b[...], preferred_element_type=F32)
            hu = jnp.dot(xb, wu_b[...], preferred_element_type=F32)
            hb = (hg * jax.nn.sigmoid(hg) * hu).astype(BF16)
            _store_token_tiled(ys[s], jnp.dot(hb, wd_b[...], preferred_element_type=F32))

        @pl.when(jnp.logical_not(active) & prev_active & (i % N_OUT_BUFFERS == s))
        def _(s=s, prev_s=prev_s, prev2_s=prev2_s):
            issue_scatter_prev(prev_s)
            scatter_wait(s)
            scatter_wait(prev2_s)
            scatter_wait(prev_s)


def _moe_call(blk_e, blk_w, blk_off, blk_nv, blk_next_e, blk_seg, rows, h2, w_g, w_u, w_d):
    n_steps = blk_e.shape[0]
    n_tokens = h2.shape[0] // TOKEN_ROWS
    any_spec = pl.BlockSpec(memory_space=pl.ANY)
    block_buf = pltpu.VMEM((MOE_BLOCK * TOKEN_ROWS, LANES), U32)
    grid_spec = pltpu.PrefetchScalarGridSpec(
        num_scalar_prefetch=7,
        grid=(n_steps,),
        in_specs=[any_spec, any_spec, any_spec, any_spec],
        out_specs=any_spec,
        scratch_shapes=[
            pltpu.VMEM((MOE_WINDOW * TOKEN_ROWS, LANES), U32),
            block_buf, block_buf, block_buf, block_buf,
            pltpu.VMEM((2, D_MODEL, D_EXPERT), F32),
            pltpu.VMEM((2, D_MODEL, D_EXPERT), F32),
            pltpu.VMEM((2, D_EXPERT, D_MODEL), F32),
            pltpu.VMEM((D_MODEL, D_EXPERT), BF16),
            pltpu.VMEM((D_MODEL, D_EXPERT), BF16),
            pltpu.VMEM((D_EXPERT, D_MODEL), BF16),
            pltpu.SemaphoreType.DMA,
            pltpu.SemaphoreType.DMA((2,)),
            pltpu.SemaphoreType.DMA((N_OUT_BUFFERS,)),
        ],
    )
    return pl.pallas_call(
        functools.partial(_moe_kernel, n_tokens),
        out_shape=jax.ShapeDtypeStruct(((n_tokens * TOP_K + N_DUMMY_ROWS) * TOKEN_ROWS, LANES), U32),
        grid_spec=grid_spec,
        compiler_params=_params(("arbitrary",)),
        name="moe",
    )(blk_e, blk_w, blk_off, blk_nv, blk_next_e, blk_seg, rows, h2, w_g, w_u, w_d)


def _final_kernel(x1_ref, y0_ref, y1_ref, route_ref, g2_ref, fg_ref, o_ref):
    route = route_ref[...]
    rows = route.shape[0]
    moe = (route[:, 2:3] * _load_token_tiled(y0_ref, rows, F32)
           + route[:, 3:4] * _load_token_tiled(y1_ref, rows, F32))
    x2 = x1_ref[...] + g2_ref[0] * moe
    o_ref[...] = x2 * lax.rsqrt(jnp.mean(x2 * x2, axis=-1, keepdims=True) + EPS) * fg_ref[...]


def _final_call(x1, out2, route, g2, fg, batch, seq):
    n = batch * seq
    tm = WIDE_TILE
    tpb = seq // tm
    tile = pl.BlockSpec((tm, D_MODEL), lambda i: (i, 0))
    return pl.pallas_call(
        _final_kernel,
        out_shape=jax.ShapeDtypeStruct((n, D_MODEL), F32),
        grid=(n // tm,),
        in_specs=[
            tile,
            pl.BlockSpec((tm * TOKEN_ROWS, LANES), lambda i: (i, 0)),
            pl.BlockSpec((tm * TOKEN_ROWS, LANES), lambda i: (i + n // tm, 0)),
            pl.BlockSpec((tm, ROUTE_LANES), lambda i: (i, 0)),
            pl.BlockSpec((1, 1, D_MODEL), lambda i: (i // tpb, 0, 0)),
            pl.BlockSpec((1, D_MODEL), lambda i: (0, 0)),
        ],
        out_specs=tile,
        compiler_params=_params(("parallel",)),
        name="final",
    )(x1, out2, out2, route, g2, fg)


def _channel_dft():
    j = np.arange(FOURIER_GROUP_DIM)
    ang = 2.0 * np.pi * np.outer(j, j) / FOURIER_GROUP_DIM
    return np.concatenate([np.cos(ang), np.sin(ang)], axis=1).astype(np.float32)


def _position_dft(seq):
    rows = seq // GRID_W
    assert GRID_W % rows == 0
    r, c = np.divmod(np.arange(seq), GRID_W)
    phase = (np.outer(r, r) * (GRID_W // rows) + np.outer(c, c)) % GRID_W
    ang = 2.0 * np.pi * phase / GRID_W
    scale = 1.0 / np.sqrt(float(seq) * FOURIER_GROUP_DIM)
    return np.concatenate([np.cos(ang), -np.sin(ang)], axis=1) * scale


def _block_diag(w):
    heads, hd, _ = w.shape
    eye = jnp.eye(heads, dtype=w.dtype)
    return jnp.einsum('hij,hg->higj', w, eye).reshape(heads * hd, heads * hd)


def _gate_weights(w_a, w_x):
    return jnp.stack([0.5 * jnp.concatenate([_block_diag(w_a[d]), _block_diag(w_x[d])], axis=1)
                      for d in range(2)]).astype(BF16)


def _dispatch(eid, n_tokens):
    n_assign = n_tokens * TOP_K
    n_seg = (n_tokens // MOE_WINDOW) * N_EXPERTS
    n_blk = n_assign // MOE_BLOCK + n_seg
    a = jnp.arange(n_assign, dtype=jnp.int32)
    seg_of = (a // (TOP_K * MOE_WINDOW)) * N_EXPERTS + eid.reshape(-1)
    _, order = lax.sort((seg_of, a), num_keys=1)
    segs = jnp.arange(n_seg, dtype=jnp.int32)
    counts = jnp.sum((seg_of[:, None] == segs[None, :]).astype(jnp.int32), axis=0)
    start = jnp.cumsum(counts) - counts
    blocks = (counts + MOE_BLOCK - 1) // MOE_BLOCK
    bend = jnp.cumsum(blocks)
    b = jnp.arange(n_blk + 1, dtype=jnp.int32)
    ended = (b[:, None] >= bend[None, :]).astype(jnp.int32)
    seg = jnp.minimum(jnp.sum(ended, axis=1), n_seg - 1)
    onehot = (seg[:, None] == segs[None, :]).astype(jnp.int32)
    in_seg = (b - jnp.sum(onehot * (bend - blocks)[None, :], axis=1)) * MOE_BLOCK
    blk_off = jnp.sum(onehot * start[None, :], axis=1) + in_seg
    blk_nv = jnp.clip(jnp.sum(onehot * counts[None, :], axis=1) - in_seg, 0, MOE_BLOCK)
    seg_end = jnp.sum(onehot * bend[None, :], axis=1)
    follows = (b[None, :] == seg_end[:, None]).astype(jnp.int32)
    next_e = jnp.sum(follows * (seg % N_EXPERTS)[None, :], axis=1)
    next_nv = jnp.sum(follows * blk_nv[None, :], axis=1)
    blk_next_e = jnp.where(next_nv > 0, next_e, -1)

    tok = lax.shift_right_logical(order, 1)
    dst = (order & 1) * n_tokens + tok
    rows = lax.shift_left(dst, LOCAL_BITS) | (tok % MOE_WINDOW)
    dummy = lax.shift_left(n_assign + jnp.arange(N_DUMMY_ROWS, dtype=jnp.int32), LOCAL_BITS)
    seg_ordinal = jnp.cumsum((blocks > 0).astype(jnp.int32)) - 1
    blk_seg = jnp.sum(onehot * seg_ordinal[None, :], axis=1)
    assert MOE_WINDOW <= LOCAL_MASK + 1 and (n_assign + N_DUMMY_ROWS) << LOCAL_BITS <= 2 ** 32
    return seg % N_EXPERTS, seg // N_EXPERTS, blk_off, blk_nv, blk_next_e, blk_seg, jnp.concatenate([rows, dummy])


def kernel(x, c, ctx, c_ctx, w_mod, b_mod, norm1_g, w_in, conv_w, conv_b, lru_wa, lru_ba, lru_wx, lru_bx,
           lru_lam, w_fourier_out, w_lru_out, w_out, norm2_g, w_group, b_group, w_expert_router,
           b_expert_router, w_gate_e, w_up_e, w_down_e, final_g):
    batch, seq, _ = x.shape
    ctx_len = ctx.shape[1]
    n = batch * seq
    assert w_mod.shape[0] == 1, "single-layer stack only: the context stream is not carried across layers"
    x2 = x.reshape(n, D_MODEL)
    dft_ch = jnp.asarray(_channel_dft().astype(BF16))
    m_pos = jnp.asarray(_position_dft(seq).astype(BF16))

    for l in range(1):
        c_all = jnp.concatenate([c, c_ctx[None], jnp.zeros((MOD_ROWS - batch - 1, D_MODEL), F32)], axis=0)
        mod = _mod_call(c_all, w_mod[l], b_mod[l][None])
        sh1, sc1, g1, sh2, sc2, g2 = [m[:batch, None, :] for m in jnp.split(mod, 6, axis=-1)]
        csh1, csc1 = mod[batch:batch + 1, :D_MODEL], mod[batch:batch + 1, D_MODEL:2 * D_MODEL]

        w_in_b = w_in[l].astype(BF16)
        n1 = norm1_g[l][None]
        wg = _gate_weights(lru_wa[l], lru_wx[l])
        bg = 0.5 * jnp.concatenate([lru_ba[l], lru_bx[l]], axis=-1)[:, None, :]
        lam = lru_lam[l][:, None, :]
        cb = conv_b[l][None]

        uc = _inproj_ctx_call(ctx.reshape(batch * ctx_len, D_MODEL), csh1, csc1, n1,
                              w_in_b[:, D_FOURIER:D_FOURIER + D_LRU])
        hc = _rglru_call(uc.reshape(batch, ctx_len, D_LRU), conv_w[l], cb, wg, bg, lam,
                         jnp.zeros((2, batch, D_LRU), F32))
        h0 = jnp.stack([hc[0, :, -1], hc[1, :, 0]])

        ab, ur, yr = _inproj_call(x2, sh1, sc1, n1, w_in_b[:, :D_FOURIER + 2 * D_LRU], dft_ch, batch, seq)
        h = _rglru_call(ur.reshape(batch, seq, D_LRU), conv_w[l], cb, wg, bg, lam, h0)
        fm = _fourier_call(m_pos, ab)

        w_route = jnp.zeros((D_MODEL, ROUTE_LANES), F32)
        w_route = w_route.at[:, :N_GROUPS].set(w_group[l]).at[:, N_GROUPS:N_GROUPS + N_EXPERTS].set(
            w_expert_router[l])
        b_route = jnp.zeros((1, ROUTE_LANES), F32)
        b_route = b_route.at[0, :N_GROUPS].set(b_group[l]).at[0, N_GROUPS:N_GROUPS + N_EXPERTS].set(
            b_expert_router[l])
        w_route_hi = w_route.astype(BF16)
        w_route_lo = (w_route - w_route_hi.astype(F32)).astype(BF16)
        w_route = jnp.concatenate([w_route_hi, w_route_lo], axis=1)
        x1, h2, route = _merge_call(
            x2, (sh1, sc1, g1, sh2, sc2), n1, norm2_g[l][None],
            (0.5 * w_in[l][:, D_FOURIER + 2 * D_LRU:]).astype(BF16),
            fm, h, yr, w_fourier_out[l].astype(BF16), w_lru_out[l].astype(BF16),
            (0.5 * w_out[l]).astype(BF16),
            w_route, b_route, batch, seq)

        eid = route[:, :TOP_K].astype(jnp.int32)
        out2 = _moe_call(*_dispatch(eid, n), h2, w_gate_e[l], w_up_e[l], w_down_e[l])
        x2 = _final_call(x1, out2, route, g2, final_g[None], batch, seq)
    return x2.reshape(batch, seq, D_MODEL)
```

```python
import functools

import numpy as np
import jax
import jax.numpy as jnp
from jax import lax
from jax.experimental import pallas as pl
from jax.experimental.pallas import tpu as pltpu

F32 = jnp.float32
BF16 = jnp.bfloat16

LANES = 128
SUBLANES = 8
VMEM_LIMIT_BYTES = 56 * 1024 * 1024

D_MODEL = 1024
GRID_W = 64
EPS = 1e-6
N_FOURIER_GROUPS = 4
FOURIER_GROUP_DIM = 128
D_FOURIER = N_FOURIER_GROUPS * FOURIER_GROUP_DIM
N_LRU_HEADS = 8
LRU_HEAD_DIM = 64
D_LRU = N_LRU_HEADS * LRU_HEAD_DIM
CONV_WIDTH = 4
CONV_LEFT = 2
LRU_C = 8.0
N_GROUPS = 4
EXPERTS_PER_GROUP = 8
N_EXPERTS = N_GROUPS * EXPERTS_PER_GROUP
TOP_K = 2
D_EXPERT = 512

MOD_ROWS = 24
MOD_BLOCK_N = 1536
TOKEN_TILE = 512
WIDE_TILE = 1024
MERGE_GROUP_ROWS = 128
SCAN_BATCH = SUBLANES
SCAN_CHUNK = 256
SCAN_GROUP = 32
N_SLABS = D_LRU // LANES
FOURIER_TILE = 1024
MOE_BLOCK = 256
ROUTE_LANES = LANES
N_OUT_BUFFERS = 3
N_DUMMY_ROWS = N_OUT_BUFFERS * MOE_BLOCK
MOE_WINDOW = 16384
LOCAL_BITS = 14
LOCAL_MASK = (1 << LOCAL_BITS) - 1


def _params(semantics):
    return pltpu.CompilerParams(dimension_semantics=semantics, vmem_limit_bytes=VMEM_LIMIT_BYTES)


U32 = jnp.uint32
HALF = D_MODEL // 2
TOKEN_ROWS = HALF // LANES
HIGH_HALF_WORD = np.uint32(0xFFFF0000)


def _pack_bf16_pairs(x):
    bits = pltpu.bitcast(x.astype(BF16).astype(F32), U32)
    return (bits[:, :HALF] >> 16) | (bits[:, HALF:] & HIGH_HALF_WORD)


def _unpack_bf16_pairs(u, dtype):
    lo = pltpu.bitcast(u << 16, F32)
    hi = pltpu.bitcast(u & HIGH_HALF_WORD, F32)
    return jnp.concatenate([lo, hi], axis=1).astype(dtype)


def _store_token_tiled(ref, x):
    rows = x.shape[0]
    packed = _pack_bf16_pairs(x)
    for s in range(TOKEN_ROWS):
        ref[pl.ds(s, rows, stride=TOKEN_ROWS), :] = packed[:, s * LANES:(s + 1) * LANES]


def _load_token_tiled(ref, rows, dtype):
    packed = jnp.concatenate([ref[pl.ds(s, rows, stride=TOKEN_ROWS), :] for s in range(TOKEN_ROWS)], axis=1)
    return _unpack_bf16_pairs(packed, dtype)


def _rms_modulate(x, g, shift, scale):
    y = x * lax.rsqrt(jnp.mean(x * x, axis=-1, keepdims=True) + EPS) * g
    return y * (1.0 + scale) + shift


def _mod_kernel(c_ref, w_ref, b_ref, o_ref):
    c = c_ref[...]
    s = c * jax.nn.sigmoid(c)
    o_ref[...] = jnp.dot(s, w_ref[...], preferred_element_type=F32,
                         precision=lax.Precision.HIGHEST) + b_ref[...]


def _mod_call(c_all, w_mod, b_mod):
    n_out = w_mod.shape[1]
    return pl.pallas_call(
        _mod_kernel,
        out_shape=jax.ShapeDtypeStruct((MOD_ROWS, n_out), F32),
        grid=(n_out // MOD_BLOCK_N,),
        in_specs=[
            pl.BlockSpec((MOD_ROWS, D_MODEL), lambda j: (0, 0)),
            pl.BlockSpec((D_MODEL, MOD_BLOCK_N), lambda j: (0, j)),
            pl.BlockSpec((1, MOD_BLOCK_N), lambda j: (0, j)),
        ],
        out_specs=pl.BlockSpec((MOD_ROWS, MOD_BLOCK_N), lambda j: (0, j)),
        compiler_params=_params(("arbitrary",)),
        name="mod",
    )(c_all, w_mod, b_mod)


def _inproj_kernel(x_ref, sh_ref, sc_ref, g_ref, w_ref, dft_ref, ab_ref, ur_ref, yr_ref):
    hx = _rms_modulate(x_ref[...], g_ref[...], sh_ref[0], sc_ref[0]).astype(BF16)
    proj = jnp.dot(hx, w_ref[...], preferred_element_type=F32)
    uf = proj[:, :D_FOURIER].astype(BF16)
    for g in range(N_FOURIER_GROUPS):
        lo, hi = g * FOURIER_GROUP_DIM, (g + 1) * FOURIER_GROUP_DIM
        cs = jnp.dot(uf[:, lo:hi], dft_ref[...], preferred_element_type=F32)
        ab_ref[0, 0, :, lo:hi] = cs[:, :FOURIER_GROUP_DIM].astype(BF16)
        ab_ref[0, 1, :, lo:hi] = cs[:, FOURIER_GROUP_DIM:].astype(BF16)
    ur_ref[...] = proj[:, D_FOURIER:D_FOURIER + D_LRU]
    yr_ref[...] = proj[:, D_FOURIER + D_LRU:].astype(BF16)


def _inproj_call(x2, sh, sc, g, w, dft_ch, batch, seq):
    n = batch * seq
    tm = WIDE_TILE
    tpb = seq // tm
    mod_spec = pl.BlockSpec((1, 1, D_MODEL), lambda i: (i // tpb, 0, 0))
    return pl.pallas_call(
        _inproj_kernel,
        out_shape=(
            jax.ShapeDtypeStruct((batch, 2, seq, D_FOURIER), BF16),
            jax.ShapeDtypeStruct((n, D_LRU), F32),
            jax.ShapeDtypeStruct((n, D_LRU), BF16),
        ),
        grid=(n // tm,),
        in_specs=[
            pl.BlockSpec((tm, D_MODEL), lambda i: (i, 0)),
            mod_spec, mod_spec,
            pl.BlockSpec((1, D_MODEL), lambda i: (0, 0)),
            pl.BlockSpec(w.shape, lambda i: (0, 0)),
            pl.BlockSpec(dft_ch.shape, lambda i: (0, 0)),
        ],
        out_specs=(
            pl.BlockSpec((1, 2, tm, D_FOURIER), lambda i: (i // tpb, 0, i % tpb, 0)),
            pl.BlockSpec((tm, D_LRU), lambda i: (i, 0)),
            pl.BlockSpec((tm, D_LRU), lambda i: (i, 0)),
        ),
        compiler_params=_params(("parallel",)),
        name="inproj",
    )(x2, sh, sc, g, w, dft_ch)


def _inproj_ctx_kernel(x_ref, sh_ref, sc_ref, g_ref, w_ref, ur_ref):
    hx = _rms_modulate(x_ref[...], g_ref[...], sh_ref[...], sc_ref[...]).astype(BF16)
    ur_ref[...] = jnp.dot(hx, w_ref[...], preferred_element_type=F32)


def _inproj_ctx_call(ctx2, sh, sc, g, w):
    n = ctx2.shape[0]
    vec = pl.BlockSpec((1, D_MODEL), lambda i: (0, 0))
    return pl.pallas_call(
        _inproj_ctx_kernel,
        out_shape=jax.ShapeDtypeStruct((n, D_LRU), F32),
        grid=(n // TOKEN_TILE,),
        in_specs=[pl.BlockSpec((TOKEN_TILE, D_MODEL), lambda i: (i, 0)), vec, vec, vec,
                  pl.BlockSpec(w.shape, lambda i: (0, 0))],
        out_specs=pl.BlockSpec((TOKEN_TILE, D_LRU), lambda i: (i, 0)),
        compiler_params=_params(("parallel",)),
        name="inproj_ctx",
    )(ctx2, sh, sc, g, w)


def _rglru_kernel(n_chunks, u_ref, up_ref, un_ref, cw_ref, cb_ref, wg_ref, bg_ref, lam_ref, h0_ref,
                  h_ref, u_t, a_s, b_s, h_s, state):
    tc, nb = SCAN_CHUNK, SCAN_BATCH
    d = pl.program_id(0)
    k = pl.program_id(2)
    kk = k + d * (n_chunks - 1 - 2 * k)

    @pl.when(k == 0)
    def _():
        state[...] = h0_ref[0]

    lam = lam_ref[0]
    neg_lam = -lam
    softplus = jnp.maximum(neg_lam, 0.0) + jnp.log1p(jnp.exp(-jnp.abs(neg_lam)))
    rate = (-0.5 * LRU_C * np.log2(np.e).astype(np.float32)) * softplus
    has_prev = kk > 0
    has_next = kk < n_chunks - 1
    cw = cw_ref[...]
    slabs = [slice(j * LANES, (j + 1) * LANES) for j in range(N_SLABS)]
    n_right = CONV_WIDTH - 1 - CONV_LEFT

    for i in range(nb):
        u = u_ref[i]
        prev = jnp.where(has_prev, up_ref[i], 0.0)
        nxt = jnp.where(has_next, un_ref[i], 0.0)
        for j, sl in enumerate(slabs):
            for p in range(CONV_LEFT):
                u_t[j, pl.ds(p * nb + i, 1), :] = prev[SUBLANES - CONV_LEFT + p:SUBLANES - CONV_LEFT + p + 1, sl]
            u_t[j, pl.ds(CONV_LEFT * nb + i, tc, stride=nb), :] = u[:, sl]
            for p in range(n_right):
                u_t[j, pl.ds((CONV_LEFT + tc + p) * nb + i, 1), :] = nxt[p:p + 1, sl]

    group = SCAN_GROUP * nb
    for g in range(tc // SCAN_GROUP):
        r0 = g * group

        def tap(k):
            return jnp.concatenate([u_t[j, pl.ds(r0 + k * nb, group), :] for j in range(N_SLABS)], axis=1)

        xc = cb_ref[...] + tap(0) * cw[0:1]
        for k in range(1, CONV_WIDTH):
            xc = xc + tap(k) * cw[k:k + 1]
        gz = jnp.dot(xc.astype(BF16), wg_ref[0], preferred_element_type=F32) + bg_ref[0]
        t_a = jnp.tanh(gz[:, :D_LRU])
        t_x = jnp.tanh(gz[:, D_LRU:])
        a = jnp.exp2(rate + rate * t_a)
        half_xc = 0.5 * xc
        b = jnp.sqrt(1.0 - a * a) * (half_xc + half_xc * t_x)
        for j, sl in enumerate(slabs):
            a_s[j, pl.ds(r0, group), :] = a[:, sl]
            b_s[j, pl.ds(r0, group), :] = b[:, sl]

    def step(s, h):
        t = s + d * (tc - 1 - 2 * s)
        rows = pl.ds(pl.multiple_of(t * nb, nb), nb)
        out = []
        for j in range(N_SLABS):
            hj = a_s[j, rows, :] * h[j] + b_s[j, rows, :]
            h_s[j, rows, :] = hj
            out.append(hj)
        return tuple(out)

    st = state[...]
    h = lax.fori_loop(0, tc, step, tuple(st[:, sl] for sl in slabs), unroll=8)
    for j, sl in enumerate(slabs):
        state[:, sl] = h[j]
    for i in range(nb):
        for j, sl in enumerate(slabs):
            h_ref[0, i, :, sl] = h_s[j, pl.ds(i, tc, stride=nb), :]


def _rglru_call(u3, conv_w, conv_b, wg, bg, lam, h0):
    batch, seq, _ = u3.shape
    n_chunks = seq // SCAN_CHUNK
    halo_blocks = SCAN_CHUNK // SUBLANES
    last_halo = seq // SUBLANES - 1

    def chunk(d, k):
        return k + d * (n_chunks - 1 - 2 * k)

    scratch = pltpu.VMEM((N_SLABS, SCAN_BATCH * SCAN_CHUNK, LANES), F32)
    padded = pltpu.VMEM((N_SLABS, SCAN_BATCH * (SCAN_CHUNK + CONV_WIDTH - 1), LANES), F32)
    return pl.pallas_call(
        functools.partial(_rglru_kernel, n_chunks),
        out_shape=jax.ShapeDtypeStruct((2, batch, seq, D_LRU), F32),
        grid=(2, batch // SCAN_BATCH, n_chunks),
        in_specs=[
            pl.BlockSpec((SCAN_BATCH, SCAN_CHUNK, D_LRU), lambda d, g, k: (g, chunk(d, k), 0)),
            pl.BlockSpec((SCAN_BATCH, SUBLANES, D_LRU),
                         lambda d, g, k: (g, jnp.maximum(chunk(d, k) * halo_blocks - 1, 0), 0)),
            pl.BlockSpec((SCAN_BATCH, SUBLANES, D_LRU),
                         lambda d, g, k: (g, jnp.minimum((chunk(d, k) + 1) * halo_blocks, last_halo), 0)),
            pl.BlockSpec((CONV_WIDTH, D_LRU), lambda d, g, k: (0, 0)),
            pl.BlockSpec((1, D_LRU), lambda d, g, k: (0, 0)),
            pl.BlockSpec((1, D_LRU, 2 * D_LRU), lambda d, g, k: (d, 0, 0)),
            pl.BlockSpec((1, 1, 2 * D_LRU), lambda d, g, k: (d, 0, 0)),
            pl.BlockSpec((1, 1, D_LRU), lambda d, g, k: (d, 0, 0)),
            pl.BlockSpec((1, SCAN_BATCH, D_LRU), lambda d, g, k: (d, g, 0)),
        ],
        out_specs=pl.BlockSpec((1, SCAN_BATCH, SCAN_CHUNK, D_LRU), lambda d, g, k: (d, g, chunk(d, k), 0)),
        scratch_shapes=[padded, scratch, scratch, scratch, pltpu.VMEM((SCAN_BATCH, D_LRU), F32)],
        compiler_params=_params(("arbitrary", "arbitrary", "arbitrary")),
        name="rglru",
    )(u3, u3, u3, conv_w, conv_b, wg, bg, lam, h0)


def _fourier_kernel(m_ref, ab_ref, o_ref):
    seq2 = m_ref.shape[1]
    rhs = ab_ref[0].reshape(seq2, D_FOURIER)
    o_ref[0] = jnp.dot(m_ref[...], rhs, preferred_element_type=F32).astype(BF16)


def _fourier_call(m_pos, ab):
    batch, _, seq, _ = ab.shape
    return pl.pallas_call(
        _fourier_kernel,
        out_shape=jax.ShapeDtypeStruct((batch, seq, D_FOURIER), BF16),
        grid=(seq // FOURIER_TILE, batch),
        in_specs=[
            pl.BlockSpec((FOURIER_TILE, 2 * seq), lambda m, b: (m, 0)),
            pl.BlockSpec((1, 2, seq, D_FOURIER), lambda m, b: (b, 0, 0, 0)),
        ],
        out_specs=pl.BlockSpec((1, FOURIER_TILE, D_FOURIER), lambda m, b: (b, m, 0)),
        compiler_params=_params(("parallel", "parallel")),
        name="fourier",
    )(m_pos, ab)


def _route(logits):
    lane = lax.broadcasted_iota(jnp.int32, logits.shape, 1)
    neg = -jnp.inf
    gl = jnp.where(lane < N_GROUPS, logits, neg)
    gmax = jnp.max(gl, axis=1, keepdims=True)
    grp = jnp.min(jnp.where(gl == gmax, lane, ROUTE_LANES), axis=1, keepdims=True)
    p_grp = 1.0 / jnp.sum(jnp.exp(gl - gmax), axis=1, keepdims=True)
    e_lane = lane - N_GROUPS
    in_grp = (e_lane >= 0) & (e_lane < N_EXPERTS) & ((e_lane // EXPERTS_PER_GROUP) == grp)
    el = jnp.where(in_grp, logits, neg)
    t1 = jnp.max(el, axis=1, keepdims=True)
    i1 = jnp.min(jnp.where(el == t1, lane, ROUTE_LANES), axis=1, keepdims=True)
    el2 = jnp.where(lane == i1, neg, el)
    t2 = jnp.max(el2, axis=1, keepdims=True)
    i2 = jnp.min(jnp.where(el2 == t2, lane, ROUTE_LANES), axis=1, keepdims=True)
    e = jnp.exp(t2 - t1)
    w1 = p_grp / (1.0 + e)
    w2 = p_grp * e / (1.0 + e)
    out = jnp.where(lane == 0, (i1 - N_GROUPS).astype(F32), 0.0)
    out = jnp.where(lane == 1, (i2 - N_GROUPS).astype(F32), out)
    out = jnp.where(lane == 2, w1, out)
    return jnp.where(lane == 3, w2, out)


def _merge_kernel(x_ref, sh1_ref, sc1_ref, g1_ref, sh2_ref, sc2_ref, n1_ref, n2_ref, wgate_ref,
                  fm_ref, hf_ref, hb_ref, yr_ref, wfo_ref, wlo_ref, wout_ref, wr_ref, br_ref,
                  x1_ref, h2_ref, route_ref):
    rows = [pl.ds(q * MERGE_GROUP_ROWS, MERGE_GROUP_ROWS) for q in range(x_ref.shape[0] // MERGE_GROUP_ROWS)]
    groups = range(len(rows))

    def dot(a, w_ref):
        return jnp.dot(a, w_ref[...], preferred_element_type=F32)

    x = [x_ref[r, :] for r in rows]
    hx = [_rms_modulate(v, n1_ref[...], sh1_ref[0], sc1_ref[0]).astype(BF16) for v in x]
    lr = []
    for r in rows:
        y = yr_ref[r, :].astype(F32)
        gelu = 0.5 * y * (1.0 + jnp.tanh(np.sqrt(2.0 / np.pi).astype(np.float32) * (y + 0.044715 * (y * y * y))))
        lr.append(((hf_ref[0, 0, r, :] + hb_ref[0, 0, r, :]) * gelu).astype(BF16))
    branch_f = [dot(fm_ref[0, r, :], wfo_ref) for r in rows]
    branch_r = [dot(v, wlo_ref) for v in lr]
    t = [jnp.tanh(v) for v in [dot(v, wgate_ref) for v in hx]]
    mixed = [((branch_f[q] + branch_r[q])
              + (t[q][:, :D_MODEL] * branch_f[q] + t[q][:, D_MODEL:] * branch_r[q])).astype(BF16) for q in groups]
    mix = [dot(v, wout_ref) for v in mixed]
    x1 = [x[q] + g1_ref[0] * mix[q] for q in groups]
    for q in groups:
        x1_ref[rows[q], :] = x1[q]
    h2 = [_rms_modulate(v, n2_ref[...], sh2_ref[0], sc2_ref[0]) for v in x1]
    route = []
    for v in h2:
        hi = v.astype(BF16)
        lo = (v - hi.astype(F32)).astype(BF16)
        parts = dot(hi, wr_ref) + dot(lo, wr_ref)
        route.append(_route(parts[:, :ROUTE_LANES] + parts[:, ROUTE_LANES:] + br_ref[...]))
    _store_token_tiled(h2_ref, jnp.concatenate(h2, axis=0))
    route_ref[...] = jnp.concatenate(route, axis=0)


def _merge_call(x2, mods, n1, n2, wgate, fm, h, yr, wfo, wlo, wout, wr, br, batch, seq):
    n = batch * seq
    tm = TOKEN_TILE
    tpb = seq // tm
    mod_spec = pl.BlockSpec((1, 1, D_MODEL), lambda i: (i // tpb, 0, 0))
    vec = pl.BlockSpec((1, D_MODEL), lambda i: (0, 0))
    tile = pl.BlockSpec((tm, D_MODEL), lambda i: (i, 0))

    def full(a):
        return pl.BlockSpec(a.shape, lambda i: (0,) * a.ndim)

    return pl.pallas_call(
        _merge_kernel,
        out_shape=(
            jax.ShapeDtypeStruct((n, D_MODEL), F32),
            jax.ShapeDtypeStruct((n * TOKEN_ROWS, LANES), U32),
            jax.ShapeDtypeStruct((n, ROUTE_LANES), F32),
        ),
        grid=(n // tm,),
        in_specs=[
            tile, mod_spec, mod_spec, mod_spec, mod_spec, mod_spec, vec, vec, full(wgate),
            pl.BlockSpec((1, tm, D_FOURIER), lambda i: (i // tpb, i % tpb, 0)),
            pl.BlockSpec((1, 1, tm, D_LRU), lambda i: (0, i // tpb, i % tpb, 0)),
            pl.BlockSpec((1, 1, tm, D_LRU), lambda i: (1, i // tpb, i % tpb, 0)),
            pl.BlockSpec((tm, D_LRU), lambda i: (i, 0)),
            full(wfo), full(wlo), full(wout), full(wr), full(br),
        ],
        out_specs=(tile, pl.BlockSpec((tm * TOKEN_ROWS, LANES), lambda i: (i, 0)),
                   pl.BlockSpec((tm, ROUTE_LANES), lambda i: (i, 0))),
        compiler_params=_params(("parallel",)),
        name="merge",
    )(x2, *mods, n1, n2, wgate, fm, h, h, yr, wfo, wlo, wout, wr, br)


def _token_rows(t, count=1):
    start = t * TOKEN_ROWS
    if not isinstance(start, int):
        start = pl.multiple_of(start, TOKEN_ROWS)
    return pl.ds(start, count * TOKEN_ROWS)


def _row_copy(src_ref, dst_ref, sem, src_tok, dst_tok):
    return pltpu.make_async_copy(src_ref.at[_token_rows(src_tok)], dst_ref.at[_token_rows(dst_tok)], sem)


def _moe_kernel(n_tokens, be_ref, bw_ref, off_ref, nv_ref, nxt_ref, so_ref, rows_ref,
                h2_hbm, wg_hbm, wu_hbm, wd_hbm, out_hbm,
                win, xbuf, y0, y1, y2, wg_f, wu_f, wd_f, wg_b, wu_b, wd_b, win_sem, w_sem, sct_sem):
    i = pl.program_id(0)
    n_assign = n_tokens * TOP_K
    ys = (y0, y1, y2)
    prev = jnp.maximum(i - 1, 0)

    def scatter_wait(s):
        pltpu.make_async_copy(ys[s], out_hbm.at[_token_rows(0, MOE_BLOCK)], sct_sem.at[s]).wait()

    def weight_copies(e, slot):
        return [pltpu.make_async_copy(w.at[e], f.at[slot], w_sem.at[slot])
                for w, f in ((wg_hbm, wg_f), (wu_hbm, wu_f), (wd_hbm, wd_f))]

    def gather_rows():
        base = off_ref[i]
        for r in range(MOE_BLOCK):
            xbuf[_token_rows(r)] = win[_token_rows(rows_ref[base + r] & LOCAL_MASK)]

    def issue_scatter_prev(s):
        base = off_ref[prev]
        nv = jnp.where(i > 0, nv_ref[prev], 0)
        dummy0 = n_assign + s * MOE_BLOCK
        for r in range(MOE_BLOCK):
            dst = lax.shift_right_logical(rows_ref[base + r], LOCAL_BITS)
            _row_copy(ys[s], out_hbm, sct_sem.at[s], r, jnp.where(r < nv, dst, dummy0 + r)).start(priority=r % 2)

    active = nv_ref[i] > 0
    prev_active = (i > 0) & (nv_ref[prev] > 0)
    w_slot = so_ref[i] % 2

    @pl.when(i == 0)
    def _():
        for y in ys:
            y[...] = jnp.zeros(y.shape, U32)
        fills = [pltpu.make_async_copy(ys[q], out_hbm.at[_token_rows(n_assign + q * MOE_BLOCK, MOE_BLOCK)],
                                       sct_sem.at[q]) for q in range(N_OUT_BUFFERS)]
        for cp in fills:
            cp.start()
        fills[-1].wait()
        for cp in weight_copies(be_ref[0], 0):
            cp.start()

    @pl.when(active & ((i == 0) | (bw_ref[i] != bw_ref[prev])))
    def _():
        cp = pltpu.make_async_copy(h2_hbm.at[_token_rows(bw_ref[i] * MOE_WINDOW, MOE_WINDOW)], win, win_sem)
        cp.start()
        cp.wait()

    @pl.when(active & ((i == 0) | (so_ref[i] != so_ref[prev])))
    def _():
        for cp in weight_copies(be_ref[i], w_slot):
            cp.wait()
        wg_b[...] = wg_f[w_slot].astype(BF16)
        wu_b[...] = wu_f[w_slot].astype(BF16)
        wd_b[...] = wd_f[w_slot].astype(BF16)

        @pl.when(nxt_ref[i] >= 0)
        def _():
            for cp in weight_copies(nxt_ref[i], 1 - w_slot):
                cp.start()

    for s in range(N_OUT_BUFFERS):
        prev_s, prev2_s = (s + 2) % N_OUT_BUFFERS, (s + 1) % N_OUT_BUFFERS

        @pl.when(active & (i % N_OUT_BUFFERS == s))
        def _(s=s, prev_s=prev_s):
            scatter_wait(s)
            gather_rows()
            issue_scatter_prev(prev_s)
            xb = _load_token_tiled(xbuf, MOE_BLOCK, BF16)
            hg = jnp.dot(xb, wg_b[...], preferred_element_type=F32)
            hu = jnp.dot(xb, wu_b[...], preferred_element_type=F32)
            hb = (hg * jax.nn.sigmoid(hg) * hu).astype(BF16)
            _store_token_tiled(ys[s], jnp.dot(hb, wd_b[...], preferred_element_type=F32))

        @pl.when(jnp.logical_not(active) & prev_active & (i % N_OUT_BUFFERS == s))
        def _(s=s, prev_s=prev_s, prev2_s=prev2_s):
            issue_scatter_prev(prev_s)
            scatter_wait(s)
            scatter_wait(prev2_s)
            scatter_wait(prev_s)


def _moe_call(blk_e, blk_w, blk_off, blk_nv, blk_next_e, blk_seg, rows, h2, w_g, w_u, w_d):
    n_steps = blk_e.shape[0]
    n_tokens = h2.shape[0] // TOKEN_ROWS
    any_spec = pl.BlockSpec(memory_space=pl.ANY)
    block_buf = pltpu.VMEM((MOE_BLOCK * TOKEN_ROWS, LANES), U32)
    grid_spec = pltpu.PrefetchScalarGridSpec(
        num_scalar_prefetch=7,
        grid=(n_steps,),
        in_specs=[any_spec, any_spec, any_spec, any_spec],
        out_specs=any_spec,
        scratch_shapes=[
            pltpu.VMEM((MOE_WINDOW * TOKEN_ROWS, LANES), U32),
            block_buf, block_buf, block_buf, block_buf,
            pltpu.VMEM((2, D_MODEL, D_EXPERT), F32),
            pltpu.VMEM((2, D_MODEL, D_EXPERT), F32),
            pltpu.VMEM((2, D_EXPERT, D_MODEL), F32),
            pltpu.VMEM((D_MODEL, D_EXPERT), BF16),
            pltpu.VMEM((D_MODEL, D_EXPERT), BF16),
            pltpu.VMEM((D_EXPERT, D_MODEL), BF16),
            pltpu.SemaphoreType.DMA,
            pltpu.SemaphoreType.DMA((2,)),
            pltpu.SemaphoreType.DMA((N_OUT_BUFFERS,)),
        ],
    )
    return pl.pallas_call(
        functools.partial(_moe_kernel, n_tokens),
        out_shape=jax.ShapeDtypeStruct(((n_tokens * TOP_K + N_DUMMY_ROWS) * TOKEN_ROWS, LANES), U32),
        grid_spec=grid_spec,
        compiler_params=_params(("arbitrary",)),
        name="moe",
    )(blk_e, blk_w, blk_off, blk_nv, blk_next_e, blk_seg, rows, h2, w_g, w_u, w_d)


def _final_kernel(x1_ref, y0_ref, y1_ref, route_ref, g2_ref, fg_ref, o_ref):
    route = route_ref[...]
    rows = route.shape[0]
    moe = (route[:, 2:3] * _load_token_tiled(y0_ref, rows, F32)
           + route[:, 3:4] * _load_token_tiled(y1_ref, rows, F32))
    x2 = x1_ref[...] + g2_ref[0] * moe
    o_ref[...] = x2 * lax.rsqrt(jnp.mean(x2 * x2, axis=-1, keepdims=True) + EPS) * fg_ref[...]


def _final_call(x1, out2, route, g2, fg, batch, seq):
    n = batch * seq
    tm = WIDE_TILE
    tpb = seq // tm
    tile = pl.BlockSpec((tm, D_MODEL), lambda i: (i, 0))
    return pl.pallas_call(
        _final_kernel,
        out_shape=jax.ShapeDtypeStruct((n, D_MODEL), F32),
        grid=(n // tm,),
        in_specs=[
            tile,
            pl.BlockSpec((tm * TOKEN_ROWS, LANES), lambda i: (i, 0)),
            pl.BlockSpec((tm * TOKEN_ROWS, LANES), lambda i: (i + n // tm, 0)),
            pl.BlockSpec((tm, ROUTE_LANES), lambda i: (i, 0)),
            pl.BlockSpec((1, 1, D_MODEL), lambda i: (i // tpb, 0, 0)),
            pl.BlockSpec((1, D_MODEL), lambda i: (0, 0)),
        ],
        out_specs=tile,
        compiler_params=_params(("parallel",)),
        name="final",
    )(x1, out2, out2, route, g2, fg)


def _channel_dft():
    j = np.arange(FOURIER_GROUP_DIM)
    ang = 2.0 * np.pi * np.outer(j, j) / FOURIER_GROUP_DIM
    return np.concatenate([np.cos(ang), np.sin(ang)], axis=1).astype(np.float32)


def _position_dft(seq):
    rows = seq // GRID_W
    assert GRID_W % rows == 0
    r, c = np.divmod(np.arange(seq), GRID_W)
    phase = (np.outer(r, r) * (GRID_W // rows) + np.outer(c, c)) % GRID_W
    ang = 2.0 * np.pi * phase / GRID_W
    scale = 1.0 / np.sqrt(float(seq) * FOURIER_GROUP_DIM)
    return np.concatenate([np.cos(ang), -np.sin(ang)], axis=1) * scale


def _block_diag(w):
    heads, hd, _ = w.shape
    eye = jnp.eye(heads, dtype=w.dtype)
    return jnp.einsum('hij,hg->higj', w, eye).reshape(heads * hd, heads * hd)


def _gate_weights(w_a, w_x):
    return jnp.stack([0.5 * jnp.concatenate([_block_diag(w_a[d]), _block_diag(w_x[d])], axis=1)
                      for d in range(2)]).astype(BF16)


def _dispatch(eid, n_tokens):
    n_assign = n_tokens * TOP_K
    n_seg = (n_tokens // MOE_WINDOW) * N_EXPERTS
    n_blk = n_assign // MOE_BLOCK + n_seg
    a = jnp.arange(n_assign, dtype=jnp.int32)
    seg_of = (a // (TOP_K * MOE_WINDOW)) * N_EXPERTS + eid.reshape(-1)
    _, order = lax.sort((seg_of, a), num_keys=1)
    segs = jnp.arange(n_seg, dtype=jnp.int32)
    counts = jnp.sum((seg_of[:, None] == segs[None, :]).astype(jnp.int32), axis=0)
    start = jnp.cumsum(counts) - counts
    blocks = (counts + MOE_BLOCK - 1) // MOE_BLOCK
    bend = jnp.cumsum(blocks)
    b = jnp.arange(n_blk + 1, dtype=jnp.int32)
    ended = (b[:, None] >= bend[None, :]).astype(jnp.int32)
    seg = jnp.minimum(jnp.sum(ended, axis=1), n_seg - 1)
    onehot = (seg[:, None] == segs[None, :]).astype(jnp.int32)
    in_seg = (b - jnp.sum(onehot * (bend - blocks)[None, :], axis=1)) * MOE_BLOCK
    blk_off = jnp.sum(onehot * start[None, :], axis=1) + in_seg
    blk_nv = jnp.clip(jnp.sum(onehot * counts[None, :], axis=1) - in_seg, 0, MOE_BLOCK)
    seg_end = jnp.sum(onehot * bend[None, :], axis=1)
    follows = (b[None, :] == seg_end[:, None]).astype(jnp.int32)
    next_e = jnp.sum(follows * (seg % N_EXPERTS)[None, :], axis=1)
    next_nv = jnp.sum(follows * blk_nv[None, :], axis=1)
    blk_next_e = jnp.where(next_nv > 0, next_e, -1)

    tok = lax.shift_right_logical(order, 1)
    dst = (order & 1) * n_tokens + tok
    rows = lax.shift_left(dst, LOCAL_BITS) | (tok % MOE_WINDOW)
    dummy = lax.shift_left(n_assign + jnp.arange(N_DUMMY_ROWS, dtype=jnp.int32), LOCAL_BITS)
    seg_ordinal = jnp.cumsum((blocks > 0).astype(jnp.int32)) - 1
    blk_seg = jnp.sum(onehot * seg_ordinal[None, :], axis=1)
    assert MOE_WINDOW <= LOCAL_MASK + 1 and (n_assign + N_DUMMY_ROWS) << LOCAL_BITS <= 2 ** 32
    return seg % N_EXPERTS, seg // N_EXPERTS, blk_off, blk_nv, blk_next_e, blk_seg, jnp.concatenate([rows, dummy])


def kernel(x, c, ctx, c_ctx, w_mod, b_mod, norm1_g, w_in, conv_w, conv_b, lru_wa, lru_ba, lru_wx, lru_bx,
           lru_lam, w_fourier_out, w_lru_out, w_out, norm2_g, w_group, b_group, w_expert_router,
           b_expert_router, w_gate_e, w_up_e, w_down_e, final_g):
    batch, seq, _ = x.shape
    ctx_len = ctx.shape[1]
    n = batch * seq
    assert w_mod.shape[0] == 1, "single-layer stack only: the context stream is not carried across layers"
    x2 = x.reshape(n, D_MODEL)
    dft_ch = jnp.asarray(_channel_dft().astype(BF16))
    m_pos = jnp.asarray(_position_dft(seq).astype(BF16))

    for l in range(1):
        c_all = jnp.concatenate([c, c_ctx[None], jnp.zeros((MOD_ROWS - batch - 1, D_MODEL), F32)], axis=0)
        mod = _mod_call(c_all, w_mod[l], b_mod[l][None])
        sh1, sc1, g1, sh2, sc2, g2 = [m[:batch, None, :] for m in jnp.split(mod, 6, axis=-1)]
        csh1, csc1 = mod[batch:batch + 1, :D_MODEL], mod[batch:batch + 1, D_MODEL:2 * D_MODEL]

        w_in_b = w_in[l].astype(BF16)
        n1 = norm1_g[l][None]
        wg = _gate_weights(lru_wa[l], lru_wx[l])
        bg = 0.5 * jnp.concatenate([lru_ba[l], lru_bx[l]], axis=-1)[:, None, :]
        lam = lru_lam[l][:, None, :]
        cb = conv_b[l][None]

        uc = _inproj_ctx_call(ctx.reshape(batch * ctx_len, D_MODEL), csh1, csc1, n1,
                              w_in_b[:, D_FOURIER:D_FOURIER + D_LRU])
        hc = _rglru_call(uc.reshape(batch, ctx_len, D_LRU), conv_w[l], cb, wg, bg, lam,
                         jnp.zeros((2, batch, D_LRU), F32))
        h0 = jnp.stack([hc[0, :, -1], hc[1, :, 0]])

        ab, ur, yr = _inproj_call(x2, sh1, sc1, n1, w_in_b[:, :D_FOURIER + 2 * D_LRU], dft_ch, batch, seq)
        h = _rglru_call(ur.reshape(batch, seq, D_LRU), conv_w[l], cb, wg, bg, lam, h0)
        fm = _fourier_call(m_pos, ab)

        w_route = jnp.zeros((D_MODEL, ROUTE_LANES), F32)
        w_route = w_route.at[:, :N_GROUPS].set(w_group[l]).at[:, N_GROUPS:N_GROUPS + N_EXPERTS].set(
            w_expert_router[l])
        b_route = jnp.zeros((1, ROUTE_LANES), F32)
        b_route = b_route.at[0, :N_GROUPS].set(b_group[l]).at[0, N_GROUPS:N_GROUPS + N_EXPERTS].set(
            b_expert_router[l])
        w_route_hi = w_route.astype(BF16)
        w_route_lo = (w_route - w_route_hi.astype(F32)).astype(BF16)
        w_route = jnp.concatenate([w_route_hi, w_route_lo], axis=1)
        x1, h2, route = _merge_call(
            x2, (sh1, sc1, g1, sh2, sc2), n1, norm2_g[l][None],
            (0.5 * w_in[l][:, D_FOURIER + 2 * D_LRU:]).astype(BF16),
            fm, h, yr, w_fourier_out[l].astype(BF16), w_lru_out[l].astype(BF16),
            (0.5 * w_out[l]).astype(BF16),
            w_route, b_route, batch, seq)

        eid = route[:, :TOP_K].astype(jnp.int32)
        out2 = _moe_call(*_dispatch(eid, n), h2, w_gate_e[l], w_up_e[l], w_down_e[l])
        x2 = _final_call(x1, out2, route, g2, final_g[None], batch, seq)
    return x2.reshape(batch, seq, D_MODEL)
```

```python
import functools

import numpy as np
import jax
import jax.numpy as jnp
from jax import lax
from jax.experimental import pallas as pl
from jax.experimental.pallas import tpu as pltpu

F32 = jnp.float32
BF16 = jnp.bfloat16

LANES = 128
SUBLANES = 8
VMEM_LIMIT_BYTES = 56 * 1024 * 1024

D_MODEL = 1024
GRID_W = 64
EPS = 1e-6
N_FOURIER_GROUPS = 4
FOURIER_GROUP_DIM = 128
D_FOURIER = N_FOURIER_GROUPS * FOURIER_GROUP_DIM
N_LRU_HEADS = 8
LRU_HEAD_DIM = 64
D_LRU = N_LRU_HEADS * LRU_HEAD_DIM
CONV_WIDTH = 4
CONV_LEFT = 2
LRU_C = 8.0
N_GROUPS = 4
EXPERTS_PER_GROUP = 8
N_EXPERTS = N_GROUPS * EXPERTS_PER_GROUP
TOP_K = 2
D_EXPERT = 512

MOD_ROWS = 24
MOD_BLOCK_N = 1536
TOKEN_TILE = 512
WIDE_TILE = 1024
MERGE_GROUP_ROWS = 128
SCAN_BATCH = SUBLANES
SCAN_CHUNK = 256
SCAN_GROUP = 128
N_SLABS = D_LRU // LANES
FOURIER_TILE = 2048
MOE_BLOCK = 256
ROUTE_LANES = LANES
N_OUT_BUFFERS = 3
N_DUMMY_ROWS = N_OUT_BUFFERS * MOE_BLOCK
MOE_WINDOW = 16384
LOCAL_BITS = 14
LOCAL_MASK = (1 << LOCAL_BITS) - 1


def _params(semantics):
    return pltpu.CompilerParams(dimension_semantics=semantics, vmem_limit_bytes=VMEM_LIMIT_BYTES)


U32 = jnp.uint32
HALF = D_MODEL // 2
TOKEN_ROWS = HALF // LANES
HIGH_HALF_WORD = np.uint32(0xFFFF0000)


def _pack_bf16_pairs(x):
    bits = pltpu.bitcast(x.astype(BF16).astype(F32), U32)
    return (bits[:, :HALF] >> 16) | (bits[:, HALF:] & HIGH_HALF_WORD)


def _unpack_bf16_pairs(u, dtype):
    lo = pltpu.bitcast(u << 16, F32)
    hi = pltpu.bitcast(u & HIGH_HALF_WORD, F32)
    return jnp.concatenate([lo, hi], axis=1).astype(dtype)


def _store_token_tiled(ref, x):
    rows = x.shape[0]
    packed = _pack_bf16_pairs(x)
    for s in range(TOKEN_ROWS):
        ref[pl.ds(s, rows, stride=TOKEN_ROWS), :] = packed[:, s * LANES:(s + 1) * LANES]


def _load_token_tiled(ref, rows, dtype):
    packed = jnp.concatenate([ref[pl.ds(s, rows, stride=TOKEN_ROWS), :] for s in range(TOKEN_ROWS)], axis=1)
    return _unpack_bf16_pairs(packed, dtype)


def _rms_modulate(x, g, shift, scale):
    y = x * lax.rsqrt(jnp.mean(x * x, axis=-1, keepdims=True) + EPS) * g
    return y * (1.0 + scale) + shift


def _mod_kernel(c_ref, w_ref, b_ref, o_ref):
    c = c_ref[...]
    s = c * jax.nn.sigmoid(c)
    o_ref[...] = jnp.dot(s, w_ref[...], preferred_element_type=F32,
                         precision=lax.Precision.HIGHEST) + b_ref[...]


def _mod_call(c_all, w_mod, b_mod):
    n_out = w_mod.shape[1]
    return pl.pallas_call(
        _mod_kernel,
        out_shape=jax.ShapeDtypeStruct((MOD_ROWS, n_out), F32),
        grid=(n_out // MOD_BLOCK_N,),
        in_specs=[
            pl.BlockSpec((MOD_ROWS, D_MODEL), lambda j: (0, 0)),
            pl.BlockSpec((D_MODEL, MOD_BLOCK_N), lambda j: (0, j)),
            pl.BlockSpec((1, MOD_BLOCK_N), lambda j: (0, j)),
        ],
        out_specs=pl.BlockSpec((MOD_ROWS, MOD_BLOCK_N), lambda j: (0, j)),
        compiler_params=_params(("arbitrary",)),
        name="mod",
    )(c_all, w_mod, b_mod)


def _inproj_kernel(x_ref, sh_ref, sc_ref, g_ref, w_ref, dft_ref, ab_ref, ur_ref, yr_ref):
    hx = _rms_modulate(x_ref[...], g_ref[...], sh_ref[0], sc_ref[0]).astype(BF16)
    proj = jnp.dot(hx, w_ref[...], preferred_element_type=F32)
    uf = proj[:, :D_FOURIER].astype(BF16)
    for g in range(N_FOURIER_GROUPS):
        lo, hi = g * FOURIER_GROUP_DIM, (g + 1) * FOURIER_GROUP_DIM
        cs = jnp.dot(uf[:, lo:hi], dft_ref[...], preferred_element_type=F32)
        ab_ref[0, 0, :, lo:hi] = cs[:, :FOURIER_GROUP_DIM].astype(BF16)
        ab_ref[0, 1, :, lo:hi] = cs[:, FOURIER_GROUP_DIM:].astype(BF16)
    ur_ref[...] = proj[:, D_FOURIER:D_FOURIER + D_LRU]
    yr_ref[...] = proj[:, D_FOURIER + D_LRU:].astype(BF16)


def _inproj_call(x2, sh, sc, g, w, dft_ch, batch, seq):
    n = batch * seq
    tm = WIDE_TILE
    tpb = seq // tm
    mod_spec = pl.BlockSpec((1, 1, D_MODEL), lambda i: (i // tpb, 0, 0))
    return pl.pallas_call(
        _inproj_kernel,
        out_shape=(
            jax.ShapeDtypeStruct((batch, 2, seq, D_FOURIER), BF16),
            jax.ShapeDtypeStruct((n, D_LRU), F32),
            jax.ShapeDtypeStruct((n, D_LRU), BF16),
        ),
        grid=(n // tm,),
        in_specs=[
            pl.BlockSpec((tm, D_MODEL), lambda i: (i, 0)),
            mod_spec, mod_spec,
            pl.BlockSpec((1, D_MODEL), lambda i: (0, 0)),
            pl.BlockSpec(w.shape, lambda i: (0, 0)),
            pl.BlockSpec(dft_ch.shape, lambda i: (0, 0)),
        ],
        out_specs=(
            pl.BlockSpec((1, 2, tm, D_FOURIER), lambda i: (i // tpb, 0, i % tpb, 0)),
            pl.BlockSpec((tm, D_LRU), lambda i: (i, 0)),
            pl.BlockSpec((tm, D_LRU), lambda i: (i, 0)),
        ),
        compiler_params=_params(("parallel",)),
        name="inproj",
    )(x2, sh, sc, g, w, dft_ch)


def _inproj_ctx_kernel(x_ref, sh_ref, sc_ref, g_ref, w_ref, ur_ref):
    hx = _rms_modulate(x_ref[...], g_ref[...], sh_ref[...], sc_ref[...]).astype(BF16)
    ur_ref[...] = jnp.dot(hx, w_ref[...], preferred_element_type=F32)


def _inproj_ctx_call(ctx2, sh, sc, g, w):
    n = ctx2.shape[0]
    vec = pl.BlockSpec((1, D_MODEL), lambda i: (0, 0))
    return pl.pallas_call(
        _inproj_ctx_kernel,
        out_shape=jax.ShapeDtypeStruct((n, D_LRU), F32),
        grid=(n // TOKEN_TILE,),
        in_specs=[pl.BlockSpec((TOKEN_TILE, D_MODEL), lambda i: (i, 0)), vec, vec, vec,
                  pl.BlockSpec(w.shape, lambda i: (0, 0))],
        out_specs=pl.BlockSpec((TOKEN_TILE, D_LRU), lambda i: (i, 0)),
        compiler_params=_params(("parallel",)),
        name="inproj_ctx",
    )(ctx2, sh, sc, g, w)


def _rglru_kernel(n_chunks, u_ref, up_ref, un_ref, cw_ref, cb_ref, wg_ref, bg_ref, lam_ref, h0_ref,
                  h_ref, u_t, a_s, b_s, h_s, state):
    tc, nb = SCAN_CHUNK, SCAN_BATCH
    d = pl.program_id(0)
    k = pl.program_id(2)
    kk = k + d * (n_chunks - 1 - 2 * k)

    @pl.when(k == 0)
    def _():
        state[...] = h0_ref[0]

    lam = lam_ref[0]
    neg_lam = -lam
    softplus = jnp.maximum(neg_lam, 0.0) + jnp.log1p(jnp.exp(-jnp.abs(neg_lam)))
    rate = (-0.5 * LRU_C * np.log2(np.e).astype(np.float32)) * softplus
    has_prev = kk > 0
    has_next = kk < n_chunks - 1
    cw = cw_ref[...]
    slabs = [slice(j * LANES, (j + 1) * LANES) for j in range(N_SLABS)]
    n_right = CONV_WIDTH - 1 - CONV_LEFT

    for i in range(nb):
        u = u_ref[i]
        prev = jnp.where(has_prev, up_ref[i], 0.0)
        nxt = jnp.where(has_next, un_ref[i], 0.0)
        for j, sl in enumerate(slabs):
            for p in range(CONV_LEFT):
                u_t[j, pl.ds(p * nb + i, 1), :] = prev[SUBLANES - CONV_LEFT + p:SUBLANES - CONV_LEFT + p + 1, sl]
            u_t[j, pl.ds(CONV_LEFT * nb + i, tc, stride=nb), :] = u[:, sl]
            for p in range(n_right):
                u_t[j, pl.ds((CONV_LEFT + tc + p) * nb + i, 1), :] = nxt[p:p + 1, sl]

    group = SCAN_GROUP * nb
    for g in range(tc // SCAN_GROUP):
        r0 = g * group

        def tap(k):
            return jnp.concatenate([u_t[j, pl.ds(r0 + k * nb, group), :] for j in range(N_SLABS)], axis=1)

        xc = cb_ref[...] + tap(0) * cw[0:1]
        for k in range(1, CONV_WIDTH):
            xc = xc + tap(k) * cw[k:k + 1]
        gz = jnp.dot(xc.astype(BF16), wg_ref[0], preferred_element_type=F32) + bg_ref[0]
        t_a = jnp.tanh(gz[:, :D_LRU])
        t_x = jnp.tanh(gz[:, D_LRU:])
        a = jnp.exp2(rate + rate * t_a)
        half_xc = 0.5 * xc
        b = jnp.sqrt(1.0 - a * a) * (half_xc + half_xc * t_x)
        for j, sl in enumerate(slabs):
            a_s[j, pl.ds(r0, group), :] = a[:, sl]
            b_s[j, pl.ds(r0, group), :] = b[:, sl]

    def step(s, h):
        t = s + d * (tc - 1 - 2 * s)
        rows = pl.ds(pl.multiple_of(t * nb, nb), nb)
        out = []
        for j in range(N_SLABS):
            hj = a_s[j, rows, :] * h[j] + b_s[j, rows, :]
            h_s[j, rows, :] = hj
            out.append(hj)
        return tuple(out)

    st = state[...]
    h = lax.fori_loop(0, tc, step, tuple(st[:, sl] for sl in slabs), unroll=8)
    for j, sl in enumerate(slabs):
        state[:, sl] = h[j]
    for i in range(nb):
        for j, sl in enumerate(slabs):
            h_ref[0, i, :, sl] = h_s[j, pl.ds(i, tc, stride=nb), :]


def _rglru_call(u3, conv_w, conv_b, wg, bg, lam, h0):
    batch, seq, _ = u3.shape
    n_chunks = seq // SCAN_CHUNK
    halo_blocks = SCAN_CHUNK // SUBLANES
    last_halo = seq // SUBLANES - 1

    def chunk(d, k):
        return k + d * (n_chunks - 1 - 2 * k)

    scratch = pltpu.VMEM((N_SLABS, SCAN_BATCH * SCAN_CHUNK, LANES), F32)
    padded = pltpu.VMEM((N_SLABS, SCAN_BATCH * (SCAN_CHUNK + CONV_WIDTH - 1), LANES), F32)
    return pl.pallas_call(
        functools.partial(_rglru_kernel, n_chunks),
        out_shape=jax.ShapeDtypeStruct((2, batch, seq, D_LRU), F32),
        grid=(2, batch // SCAN_BATCH, n_chunks),
        in_specs=[
            pl.BlockSpec((SCAN_BATCH, SCAN_CHUNK, D_LRU), lambda d, g, k: (g, chunk(d, k), 0)),
            pl.BlockSpec((SCAN_BATCH, SUBLANES, D_LRU),
                         lambda d, g, k: (g, jnp.maximum(chunk(d, k) * halo_blocks - 1, 0), 0)),
            pl.BlockSpec((SCAN_BATCH, SUBLANES, D_LRU),
                         lambda d, g, k: (g, jnp.minimum((chunk(d, k) + 1) * halo_blocks, last_halo), 0)),
            pl.BlockSpec((CONV_WIDTH, D_LRU), lambda d, g, k: (0, 0)),
            pl.BlockSpec((1, D_LRU), lambda d, g, k: (0, 0)),
            pl.BlockSpec((1, D_LRU, 2 * D_LRU), lambda d, g, k: (d, 0, 0)),
            pl.BlockSpec((1, 1, 2 * D_LRU), lambda d, g, k: (d, 0, 0)),
            pl.BlockSpec((1, 1, D_LRU), lambda d, g, k: (d, 0, 0)),
            pl.BlockSpec((1, SCAN_BATCH, D_LRU), lambda d, g, k: (d, g, 0)),
        ],
        out_specs=pl.BlockSpec((1, SCAN_BATCH, SCAN_CHUNK, D_LRU), lambda d, g, k: (d, g, chunk(d, k), 0)),
        scratch_shapes=[padded, scratch, scratch, scratch, pltpu.VMEM((SCAN_BATCH, D_LRU), F32)],
        compiler_params=_params(("arbitrary", "arbitrary", "arbitrary")),
        name="rglru",
    )(u3, u3, u3, conv_w, conv_b, wg, bg, lam, h0)


def _fourier_kernel(m_ref, ab_ref, o_ref):
    seq2 = m_ref.shape[1]
    rhs = ab_ref[0].reshape(seq2, D_FOURIER)
    o_ref[0] = jnp.dot(m_ref[...], rhs, preferred_element_type=F32).astype(BF16)


def _fourier_call(m_pos, ab):
    batch, _, seq, _ = ab.shape
    return pl.pallas_call(
        _fourier_kernel,
        out_shape=jax.ShapeDtypeStruct((batch, seq, D_FOURIER), BF16),
        grid=(seq // FOURIER_TILE, batch),
        in_specs=[
            pl.BlockSpec((FOURIER_TILE, 2 * seq), lambda m, b: (m, 0)),
            pl.BlockSpec((1, 2, seq, D_FOURIER), lambda m, b: (b, 0, 0, 0)),
        ],
        out_specs=pl.BlockSpec((1, FOURIER_TILE, D_FOURIER), lambda m, b: (b, m, 0)),
        compiler_params=_params(("parallel", "parallel")),
        name="fourier",
    )(m_pos, ab)


def _route(logits):
    lane = lax.broadcasted_iota(jnp.int32, logits.shape, 1)
    neg = -jnp.inf
    gl = jnp.where(lane < N_GROUPS, logits, neg)
    gmax = jnp.max(gl, axis=1, keepdims=True)
    grp = jnp.min(jnp.where(gl == gmax, lane, ROUTE_LANES), axis=1, keepdims=True)
    p_grp = 1.0 / jnp.sum(jnp.exp(gl - gmax), axis=1, keepdims=True)
    e_lane = lane - N_GROUPS
    in_grp = (e_lane >= 0) & (e_lane < N_EXPERTS) & ((e_lane // EXPERTS_PER_GROUP) == grp)
    el = jnp.where(in_grp, logits, neg)
    t1 = jnp.max(el, axis=1, keepdims=True)
    i1 = jnp.min(jnp.where(el == t1, lane, ROUTE_LANES), axis=1, keepdims=True)
    el2 = jnp.where(lane == i1, neg, el)
    t2 = jnp.max(el2, axis=1, keepdims=True)
    i2 = jnp.min(jnp.where(el2 == t2, lane, ROUTE_LANES), axis=1, keepdims=True)
    e = jnp.exp(t2 - t1)
    w1 = p_grp / (1.0 + e)
    w2 = p_grp * e / (1.0 + e)
    out = jnp.where(lane == 0, (i1 - N_GROUPS).astype(F32), 0.0)
    out = jnp.where(lane == 1, (i2 - N_GROUPS).astype(F32), out)
    out = jnp.where(lane == 2, w1, out)
    return jnp.where(lane == 3, w2, out)


def _merge_kernel(x_ref, sh1_ref, sc1_ref, g1_ref, sh2_ref, sc2_ref, n1_ref, n2_ref, wgate_ref,
                  fm_ref, hf_ref, hb_ref, yr_ref, wfo_ref, wlo_ref, wout_ref, wr_ref, br_ref,
                  x1_ref, h2_ref, route_ref):
    rows = [pl.ds(q * MERGE_GROUP_ROWS, MERGE_GROUP_ROWS) for q in range(x_ref.shape[0] // MERGE_GROUP_ROWS)]
    groups = range(len(rows))

    def dot(a, w_ref):
        return jnp.dot(a, w_ref[...], preferred_element_type=F32)

    x = [x_ref[r, :] for r in rows]
    hx = [_rms_modulate(v, n1_ref[...], sh1_ref[0], sc1_ref[0]).astype(BF16) for v in x]
    lr = []
    for r in rows:
        y = yr_ref[r, :].astype(F32)
        gelu = 0.5 * y * (1.0 + jnp.tanh(np.sqrt(2.0 / np.pi).astype(np.float32) * (y + 0.044715 * (y * y * y))))
        lr.append(((hf_ref[0, 0, r, :] + hb_ref[0, 0, r, :]) * gelu).astype(BF16))
    branch_f = [dot(fm_ref[0, r, :], wfo_ref) for r in rows]
    branch_r = [dot(v, wlo_ref) for v in lr]
    t = [jnp.tanh(v) for v in [dot(v, wgate_ref) for v in hx]]
    mixed = [((branch_f[q] + branch_r[q])
              + (t[q][:, :D_MODEL] * branch_f[q] + t[q][:, D_MODEL:] * branch_r[q])).astype(BF16) for q in groups]
    mix = [dot(v, wout_ref) for v in mixed]
    x1 = [x[q] + g1_ref[0] * mix[q] for q in groups]
    for q in groups:
        x1_ref[rows[q], :] = x1[q]
    h2 = [_rms_modulate(v, n2_ref[...], sh2_ref[0], sc2_ref[0]) for v in x1]
    route = []
    for v in h2:
        hi = v.astype(BF16)
        lo = (v - hi.astype(F32)).astype(BF16)
        parts = dot(hi, wr_ref) + dot(lo, wr_ref)
        route.append(_route(parts[:, :ROUTE_LANES] + parts[:, ROUTE_LANES:] + br_ref[...]))
    _store_token_tiled(h2_ref, jnp.concatenate(h2, axis=0))
    route_ref[...] = jnp.concatenate(route, axis=0)


def _merge_call(x2, mods, n1, n2, wgate, fm, h, yr, wfo, wlo, wout, wr, br, batch, seq):
    n = batch * seq
    tm = TOKEN_TILE
    tpb = seq // tm
    mod_spec = pl.BlockSpec((1, 1, D_MODEL), lambda i: (i // tpb, 0, 0))
    vec = pl.BlockSpec((1, D_MODEL), lambda i: (0, 0))
    tile = pl.BlockSpec((tm, D_MODEL), lambda i: (i, 0))

    def full(a):
        return pl.BlockSpec(a.shape, lambda i: (0,) * a.ndim)

    return pl.pallas_call(
        _merge_kernel,
        out_shape=(
            jax.ShapeDtypeStruct((n, D_MODEL), F32),
            jax.ShapeDtypeStruct((n * TOKEN_ROWS, LANES), U32),
            jax.ShapeDtypeStruct((n, ROUTE_LANES), F32),
        ),
        grid=(n // tm,),
        in_specs=[
            tile, mod_spec, mod_spec, mod_spec, mod_spec, mod_spec, vec, vec, full(wgate),
            pl.BlockSpec((1, tm, D_FOURIER), lambda i: (i // tpb, i % tpb, 0)),
            pl.BlockSpec((1, 1, tm, D_LRU), lambda i: (0, i // tpb, i % tpb, 0)),
            pl.BlockSpec((1, 1, tm, D_LRU), lambda i: (1, i // tpb, i % tpb, 0)),
            pl.BlockSpec((tm, D_LRU), lambda i: (i, 0)),
            full(wfo), full(wlo), full(wout), full(wr), full(br),
        ],
        out_specs=(tile, pl.BlockSpec((tm * TOKEN_ROWS, LANES), lambda i: (i, 0)),
                   pl.BlockSpec((tm, ROUTE_LANES), lambda i: (i, 0))),
        compiler_params=_params(("parallel",)),
        name="merge",
    )(x2, *mods, n1, n2, wgate, fm, h, h, yr, wfo, wlo, wout, wr, br)


def _token_rows(t, count=1):
    start = t * TOKEN_ROWS
    if not isinstance(start, int):
        start = pl.multiple_of(start, TOKEN_ROWS)
    return pl.ds(start, count * TOKEN_ROWS)


def _row_copy(src_ref, dst_ref, sem, src_tok, dst_tok):
    return pltpu.make_async_copy(src_ref.at[_token_rows(src_tok)], dst_ref.at[_token_rows(dst_tok)], sem)


def _moe_kernel(n_tokens, be_ref, bw_ref, off_ref, nv_ref, nxt_ref, so_ref, rows_ref,
                h2_hbm, wg_hbm, wu_hbm, wd_hbm, out_hbm,
                win, xbuf, y0, y1, y2, wg_f, wu_f, wd_f, wg_b, wu_b, wd_b, win_sem, w_sem, sct_sem):
    i = pl.program_id(0)
    n_assign = n_tokens * TOP_K
    ys = (y0, y1, y2)
    prev = jnp.maximum(i - 1, 0)

    def scatter_wait(s):
        pltpu.make_async_copy(ys[s], out_hbm.at[_token_rows(0, MOE_BLOCK)], sct_sem.at[s]).wait()

    def weight_copies(e, slot):
        return [pltpu.make_async_copy(w.at[e], f.at[slot], w_sem.at[slot])
                for w, f in ((wg_hbm, wg_f), (wu_hbm, wu_f), (wd_hbm, wd_f))]

    def gather_rows():
        base = off_ref[i]
        for r in range(MOE_BLOCK):
            xbuf[_token_rows(r)] = win[_token_rows(rows_ref[base + r] & LOCAL_MASK)]

    def issue_scatter_prev(s):
        base = off_ref[prev]
        nv = jnp.where(i > 0, nv_ref[prev], 0)
        dummy0 = n_assign + s * MOE_BLOCK
        for r in range(MOE_BLOCK):
            dst = lax.shift_right_logical(rows_ref[base + r], LOCAL_BITS)
            _row_copy(ys[s], out_hbm, sct_sem.at[s], r, jnp.where(r < nv, dst, dummy0 + r)).start(priority=r % 2)

    active = nv_ref[i] > 0
    prev_active = (i > 0) & (nv_ref[prev] > 0)
    w_slot = so_ref[i] % 2

    @pl.when(i == 0)
    def _():
        for y in ys:
            y[...] = jnp.zeros(y.shape, U32)
        fills = [pltpu.make_async_copy(ys[q], out_hbm.at[_token_rows(n_assign + q * MOE_BLOCK, MOE_BLOCK)],
                                       sct_sem.at[q]) for q in range(N_OUT_BUFFERS)]
        for cp in fills:
            cp.start()
        fills[-1].wait()
        for cp in weight_copies(be_ref[0], 0):
            cp.start()

    @pl.when(active & ((i == 0) | (bw_ref[i] != bw_ref[prev])))
    def _():
        cp = pltpu.make_async_copy(h2_hbm.at[_token_rows(bw_ref[i] * MOE_WINDOW, MOE_WINDOW)], win, win_sem)
        cp.start()
        cp.wait()

    @pl.when(active & ((i == 0) | (so_ref[i] != so_ref[prev])))
    def _():
        for cp in weight_copies(be_ref[i], w_slot):
            cp.wait()
        wg_b[...] = wg_f[w_slot].astype(BF16)
        wu_b[...] = wu_f[w_slot].astype(BF16)
        wd_b[...] = wd_f[w_slot].astype(BF16)

        @pl.when(nxt_ref[i] >= 0)
        def _():
            for cp in weight_copies(nxt_ref[i], 1 - w_slot):
                cp.start()

    for s in range(N_OUT_BUFFERS):
        prev_s, prev2_s = (s + 2) % N_OUT_BUFFERS, (s + 1) % N_OUT_BUFFERS

        @pl.when(active & (i % N_OUT_BUFFERS == s))
        def _(s=s, prev_s=prev_s):
            scatter_wait(s)
            gather_rows()
            issue_scatter_prev(prev_s)
            xb = _load_token_tiled(xbuf, MOE_BLOCK, BF16)
            hg = jnp.dot(xb, wg_b[...], preferred_element_type=F32)
            hu = jnp.dot(xb, wu_b[...], preferred_element_type=F32)
            hb = (hg * jax.nn.sigmoid(hg) * hu).astype(BF16)
            _store_token_tiled(ys[s], jnp.dot(hb, wd_b[...], preferred_element_type=F32))

        @pl.when(jnp.logical_not(active) & prev_active & (i % N_OUT_BUFFERS == s))
        def _(s=s, prev_s=prev_s, prev2_s=prev2_s):
            issue_scatter_prev(prev_s)
            scatter_wait(s)
            scatter_wait(prev2_s)
            scatter_wait(prev_s)


def _moe_call(blk_e, blk_w, blk_off, blk_nv, blk_next_e, blk_seg, rows, h2, w_g, w_u, w_d):
    n_steps = blk_e.shape[0]
    n_tokens = h2.shape[0] // TOKEN_ROWS
    any_spec = pl.BlockSpec(memory_space=pl.ANY)
    block_buf = pltpu.VMEM((MOE_BLOCK * TOKEN_ROWS, LANES), U32)
    grid_spec = pltpu.PrefetchScalarGridSpec(
        num_scalar_prefetch=7,
        grid=(n_steps,),
        in_specs=[any_spec, any_spec, any_spec, any_spec],
        out_specs=any_spec,
        scratch_shapes=[
            pltpu.VMEM((MOE_WINDOW * TOKEN_ROWS, LANES), U32),
            block_buf, block_buf, block_buf, block_buf,
            pltpu.VMEM((2, D_MODEL, D_EXPERT), F32),
            pltpu.VMEM((2, D_MODEL, D_EXPERT), F32),
            pltpu.VMEM((2, D_EXPERT, D_MODEL), F32),
            pltpu.VMEM((D_MODEL, D_EXPERT), BF16),
            pltpu.VMEM((D_MODEL, D_EXPERT), BF16),
            pltpu.VMEM((D_EXPERT, D_MODEL), BF16),
            pltpu.SemaphoreType.DMA,
            pltpu.SemaphoreType.DMA((2,)),
            pltpu.SemaphoreType.DMA((N_OUT_BUFFERS,)),
        ],
    )
    return pl.pallas_call(
        functools.partial(_moe_kernel, n_tokens),
        out_shape=jax.ShapeDtypeStruct(((n_tokens * TOP_K + N_DUMMY_ROWS) * TOKEN_ROWS, LANES), U32),
        grid_spec=grid_spec,
        compiler_params=_params(("arbitrary",)),
        name="moe",
    )(blk_e, blk_w, blk_off, blk_nv, blk_next_e, blk_seg, rows, h2, w_g, w_u, w_d)


def _final_kernel(x1_ref, y0_ref, y1_ref, route_ref, g2_ref, fg_ref, o_ref):
    route = route_ref[...]
    rows = route.shape[0]
    moe = (route[:, 2:3] * _load_token_tiled(y0_ref, rows, F32)
           + route[:, 3:4] * _load_token_tiled(y1_ref, rows, F32))
    x2 = x1_ref[...] + g2_ref[0] * moe
    o_ref[...] = x2 * lax.rsqrt(jnp.mean(x2 * x2, axis=-1, keepdims=True) + EPS) * fg_ref[...]


def _final_call(x1, out2, route, g2, fg, batch, seq):
    n = batch * seq
    tm = WIDE_TILE
    tpb = seq // tm
    tile = pl.BlockSpec((tm, D_MODEL), lambda i: (i, 0))
    return pl.pallas_call(
        _final_kernel,
        out_shape=jax.ShapeDtypeStruct((n, D_MODEL), F32),
        grid=(n // tm,),
        in_specs=[
            tile,
            pl.BlockSpec((tm * TOKEN_ROWS, LANES), lambda i: (i, 0)),
            pl.BlockSpec((tm * TOKEN_ROWS, LANES), lambda i: (i + n // tm, 0)),
            pl.BlockSpec((tm, ROUTE_LANES), lambda i: (i, 0)),
            pl.BlockSpec((1, 1, D_MODEL), lambda i: (i // tpb, 0, 0)),
            pl.BlockSpec((1, D_MODEL), lambda i: (0, 0)),
        ],
        out_specs=tile,
        compiler_params=_params(("parallel",)),
        name="final",
    )(x1, out2, out2, route, g2, fg)


def _channel_dft():
    j = np.arange(FOURIER_GROUP_DIM)
    ang = 2.0 * np.pi * np.outer(j, j) / FOURIER_GROUP_DIM
    return np.concatenate([np.cos(ang), np.sin(ang)], axis=1).astype(np.float32)


def _position_dft(seq):
    rows = seq // GRID_W
    assert GRID_W % rows == 0
    r, c = np.divmod(np.arange(seq), GRID_W)
    phase = (np.outer(r, r) * (GRID_W // rows) + np.outer(c, c)) % GRID_W
    ang = 2.0 * np.pi * phase / GRID_W
    scale = 1.0 / np.sqrt(float(seq) * FOURIER_GROUP_DIM)
    return np.concatenate([np.cos(ang), -np.sin(ang)], axis=1) * scale


def _block_diag(w):
    heads, hd, _ = w.shape
    eye = jnp.eye(heads, dtype=w.dtype)
    return jnp.einsum('hij,hg->higj', w, eye).reshape(heads * hd, heads * hd)


def _gate_weights(w_a, w_x):
    return jnp.stack([0.5 * jnp.concatenate([_block_diag(w_a[d]), _block_diag(w_x[d])], axis=1)
                      for d in range(2)]).astype(BF16)


def _dispatch(eid, n_tokens):
    n_assign = n_tokens * TOP_K
    n_seg = (n_tokens // MOE_WINDOW) * N_EXPERTS
    n_blk = n_assign // MOE_BLOCK + n_seg
    a = jnp.arange(n_assign, dtype=jnp.int32)
    seg_of = (a // (TOP_K * MOE_WINDOW)) * N_EXPERTS + eid.reshape(-1)
    a_bits = (n_assign - 1).bit_length()
    assert n_seg << a_bits < 2 ** 31
    order = lax.sort(lax.shift_left(seg_of, a_bits) | a) & ((1 << a_bits) - 1)
    segs = jnp.arange(n_seg, dtype=jnp.int32)
    counts = jnp.sum((seg_of[:, None] == segs[None, :]).astype(jnp.int32), axis=0)
    start = jnp.cumsum(counts) - counts
    blocks = (counts + MOE_BLOCK - 1) // MOE_BLOCK
    bend = jnp.cumsum(blocks)
    b = jnp.arange(n_blk + 1, dtype=jnp.int32)
    ended = (b[:, None] >= bend[None, :]).astype(jnp.int32)
    seg = jnp.minimum(jnp.sum(ended, axis=1), n_seg - 1)
    onehot = (seg[:, None] == segs[None, :]).astype(jnp.int32)
    in_seg = (b - jnp.sum(onehot * (bend - blocks)[None, :], axis=1)) * MOE_BLOCK
    blk_off = jnp.sum(onehot * start[None, :], axis=1) + in_seg
    blk_nv = jnp.clip(jnp.sum(onehot * counts[None, :], axis=1) - in_seg, 0, MOE_BLOCK)
    seg_end = jnp.sum(onehot * bend[None, :], axis=1)
    follows = (b[None, :] == seg_end[:, None]).astype(jnp.int32)
    next_e = jnp.sum(follows * (seg % N_EXPERTS)[None, :], axis=1)
    next_nv = jnp.sum(follows * blk_nv[None, :], axis=1)
    blk_next_e = jnp.where(next_nv > 0, next_e, -1)

    tok = lax.shift_right_logical(order, 1)
    dst = (order & 1) * n_tokens + tok
    rows = lax.shift_left(dst, LOCAL_BITS) | (tok % MOE_WINDOW)
    dummy = lax.shift_left(n_assign + jnp.arange(N_DUMMY_ROWS, dtype=jnp.int32), LOCAL_BITS)
    seg_ordinal = jnp.cumsum((blocks > 0).astype(jnp.int32)) - 1
    blk_seg = jnp.sum(onehot * seg_ordinal[None, :], axis=1)
    assert MOE_WINDOW <= LOCAL_MASK + 1 and (n_assign + N_DUMMY_ROWS) << LOCAL_BITS <= 2 ** 32
    return seg % N_EXPERTS, seg // N_EXPERTS, blk_off, blk_nv, blk_next_e, blk_seg, jnp.concatenate([rows, dummy])


def kernel(x, c, ctx, c_ctx, w_mod, b_mod, norm1_g, w_in, conv_w, conv_b, lru_wa, lru_ba, lru_wx, lru_bx,
           lru_lam, w_fourier_out, w_lru_out, w_out, norm2_g, w_group, b_group, w_expert_router,
           b_expert_router, w_gate_e, w_up_e, w_down_e, final_g):
    batch, seq, _ = x.shape
    ctx_len = ctx.shape[1]
    n = batch * seq
    assert w_mod.shape[0] == 1, "single-layer stack only: the context stream is not carried across layers"
    x2 = x.reshape(n, D_MODEL)
    dft_ch = jnp.asarray(_channel_dft().astype(BF16))
    m_pos = jnp.asarray(_position_dft(seq).astype(BF16))

    for l in range(1):
        c_all = jnp.concatenate([c, c_ctx[None], jnp.zeros((MOD_ROWS - batch - 1, D_MODEL), F32)], axis=0)
        mod = _mod_call(c_all, w_mod[l], b_mod[l][None])
        sh1, sc1, g1, sh2, sc2, g2 = [m[:batch, None, :] for m in jnp.split(mod, 6, axis=-1)]
        csh1, csc1 = mod[batch:batch + 1, :D_MODEL], mod[batch:batch + 1, D_MODEL:2 * D_MODEL]

        w_in_b = w_in[l].astype(BF16)
        n1 = norm1_g[l][None]
        wg = _gate_weights(lru_wa[l], lru_wx[l])
        bg = 0.5 * jnp.concatenate([lru_ba[l], lru_bx[l]], axis=-1)[:, None, :]
        lam = lru_lam[l][:, None, :]
        cb = conv_b[l][None]

        uc = _inproj_ctx_call(ctx.reshape(batch * ctx_len, D_MODEL), csh1, csc1, n1,
                              w_in_b[:, D_FOURIER:D_FOURIER + D_LRU])
        hc = _rglru_call(uc.reshape(batch, ctx_len, D_LRU), conv_w[l], cb, wg, bg, lam,
                         jnp.zeros((2, batch, D_LRU), F32))
        h0 = jnp.stack([hc[0, :, -1], hc[1, :, 0]])

        ab, ur, yr = _inproj_call(x2, sh1, sc1, n1, w_in_b[:, :D_FOURIER + 2 * D_LRU], dft_ch, batch, seq)
        h = _rglru_call(ur.reshape(batch, seq, D_LRU), conv_w[l], cb, wg, bg, lam, h0)
        fm = _fourier_call(m_pos, ab)

        w_route = jnp.zeros((D_MODEL, ROUTE_LANES), F32)
        w_route = w_route.at[:, :N_GROUPS].set(w_group[l]).at[:, N_GROUPS:N_GROUPS + N_EXPERTS].set(
            w_expert_router[l])
        b_route = jnp.zeros((1, ROUTE_LANES), F32)
        b_route = b_route.at[0, :N_GROUPS].set(b_group[l]).at[0, N_GROUPS:N_GROUPS + N_EXPERTS].set(
            b_expert_router[l])
        w_route_hi = w_route.astype(BF16)
        w_route_lo = (w_route - w_route_hi.astype(F32)).astype(BF16)
        w_route = jnp.concatenate([w_route_hi, w_route_lo], axis=1)
        x1, h2, route = _merge_call(
            x2, (sh1, sc1, g1, sh2, sc2), n1, norm2_g[l][None],
            (0.5 * w_in[l][:, D_FOURIER + 2 * D_LRU:]).astype(BF16),
            fm, h, yr, w_fourier_out[l].astype(BF16), w_lru_out[l].astype(BF16),
            (0.5 * w_out[l]).astype(BF16),
            w_route, b_route, batch, seq)

        eid = route[:, :TOP_K].astype(jnp.int32)
        out2 = _moe_call(*_dispatch(eid, n), h2, w_gate_e[l], w_up_e[l], w_down_e[l])
        x2 = _final_call(x1, out2, route, g2, final_g[None], batch, seq)
    return x2.reshape(batch, seq, D_MODEL)
```

```python
import functools

import numpy as np
import jax
import jax.numpy as jnp
from jax import lax
from jax.experimental import pallas as pl
from jax.experimental.pallas import tpu as pltpu

F32 = jnp.float32
BF16 = jnp.bfloat16

LANES = 128
SUBLANES = 8
VMEM_LIMIT_BYTES = 56 * 1024 * 1024

D_MODEL = 1024
GRID_W = 64
EPS = 1e-6
N_FOURIER_GROUPS = 4
FOURIER_GROUP_DIM = 128
D_FOURIER = N_FOURIER_GROUPS * FOURIER_GROUP_DIM
N_LRU_HEADS = 8
LRU_HEAD_DIM = 64
D_LRU = N_LRU_HEADS * LRU_HEAD_DIM
CONV_WIDTH = 4
CONV_LEFT = 2
LRU_C = 8.0
N_GROUPS = 4
EXPERTS_PER_GROUP = 8
N_EXPERTS = N_GROUPS * EXPERTS_PER_GROUP
TOP_K = 2
D_EXPERT = 512

MOD_ROWS = 24
MOD_BLOCK_N = 1536
TOKEN_TILE = 512
WIDE_TILE = 1024
MERGE_GROUP_ROWS = 128
SCAN_BATCH = SUBLANES
SCAN_CHUNK = 256
SCAN_GROUP = 128
N_SLABS = D_LRU // LANES
FOURIER_TILE = 2048
MOE_BLOCK = 256
ROUTE_LANES = LANES
N_OUT_BUFFERS = 3
N_DUMMY_ROWS = N_OUT_BUFFERS * MOE_BLOCK
MOE_WINDOW = 16384
LOCAL_BITS = 14
LOCAL_MASK = (1 << LOCAL_BITS) - 1


def _params(semantics):
    return pltpu.CompilerParams(dimension_semantics=semantics, vmem_limit_bytes=VMEM_LIMIT_BYTES)


U32 = jnp.uint32
HALF = D_MODEL // 2
TOKEN_ROWS = HALF // LANES
HIGH_HALF_WORD = np.uint32(0xFFFF0000)


def _pack_bf16_pairs(x):
    bits = pltpu.bitcast(x.astype(BF16).astype(F32), U32)
    return (bits[:, :HALF] >> 16) | (bits[:, HALF:] & HIGH_HALF_WORD)


def _unpack_bf16_pairs(u, dtype):
    lo = pltpu.bitcast(u << 16, F32)
    hi = pltpu.bitcast(u & HIGH_HALF_WORD, F32)
    return jnp.concatenate([lo, hi], axis=1).astype(dtype)


def _store_token_tiled(ref, x):
    rows = x.shape[0]
    packed = _pack_bf16_pairs(x)
    for s in range(TOKEN_ROWS):
        ref[pl.ds(s, rows, stride=TOKEN_ROWS), :] = packed[:, s * LANES:(s + 1) * LANES]


def _load_token_tiled(ref, rows, dtype):
    packed = jnp.concatenate([ref[pl.ds(s, rows, stride=TOKEN_ROWS), :] for s in range(TOKEN_ROWS)], axis=1)
    return _unpack_bf16_pairs(packed, dtype)


def _rms_modulate(x, g, shift, scale):
    y = x * lax.rsqrt(jnp.mean(x * x, axis=-1, keepdims=True) + EPS) * g
    return y * (1.0 + scale) + shift


def _mod_kernel(c_ref, w_ref, b_ref, o_ref):
    c = c_ref[...]
    s = c * jax.nn.sigmoid(c)
    o_ref[...] = jnp.dot(s, w_ref[...], preferred_element_type=F32,
                         precision=lax.Precision.HIGHEST) + b_ref[...]


def _mod_call(c_all, w_mod, b_mod):
    n_out = w_mod.shape[1]
    return pl.pallas_call(
        _mod_kernel,
        out_shape=jax.ShapeDtypeStruct((MOD_ROWS, n_out), F32),
        grid=(n_out // MOD_BLOCK_N,),
        in_specs=[
            pl.BlockSpec((MOD_ROWS, D_MODEL), lambda j: (0, 0)),
            pl.BlockSpec((D_MODEL, MOD_BLOCK_N), lambda j: (0, j)),
            pl.BlockSpec((1, MOD_BLOCK_N), lambda j: (0, j)),
        ],
        out_specs=pl.BlockSpec((MOD_ROWS, MOD_BLOCK_N), lambda j: (0, j)),
        compiler_params=_params(("arbitrary",)),
        name="mod",
    )(c_all, w_mod, b_mod)


def _inproj_kernel(x_ref, sh_ref, sc_ref, g_ref, w_ref, dft_ref, ab_ref, ur_ref, yr_ref):
    hx = _rms_modulate(x_ref[...], g_ref[...], sh_ref[0], sc_ref[0]).astype(BF16)
    proj = jnp.dot(hx, w_ref[...], preferred_element_type=F32)
    uf = proj[:, :D_FOURIER].astype(BF16)
    for g in range(N_FOURIER_GROUPS):
        lo, hi = g * FOURIER_GROUP_DIM, (g + 1) * FOURIER_GROUP_DIM
        cs = jnp.dot(uf[:, lo:hi], dft_ref[...], preferred_element_type=F32)
        ab_ref[0, 0, :, lo:hi] = cs[:, :FOURIER_GROUP_DIM].astype(BF16)
        ab_ref[0, 1, :, lo:hi] = cs[:, FOURIER_GROUP_DIM:].astype(BF16)
    ur_ref[...] = proj[:, D_FOURIER:D_FOURIER + D_LRU]
    yr_ref[...] = proj[:, D_FOURIER + D_LRU:].astype(BF16)


def _inproj_call(x2, sh, sc, g, w, dft_ch, batch, seq):
    n = batch * seq
    tm = WIDE_TILE
    tpb = seq // tm
    mod_spec = pl.BlockSpec((1, 1, D_MODEL), lambda i: (i // tpb, 0, 0))
    return pl.pallas_call(
        _inproj_kernel,
        out_shape=(
            jax.ShapeDtypeStruct((batch, 2, seq, D_FOURIER), BF16),
            jax.ShapeDtypeStruct((n, D_LRU), F32),
            jax.ShapeDtypeStruct((n, D_LRU), BF16),
        ),
        grid=(n // tm,),
        in_specs=[
            pl.BlockSpec((tm, D_MODEL), lambda i: (i, 0)),
            mod_spec, mod_spec,
            pl.BlockSpec((1, D_MODEL), lambda i: (0, 0)),
            pl.BlockSpec(w.shape, lambda i: (0, 0)),
            pl.BlockSpec(dft_ch.shape, lambda i: (0, 0)),
        ],
        out_specs=(
            pl.BlockSpec((1, 2, tm, D_FOURIER), lambda i: (i // tpb, 0, i % tpb, 0)),
            pl.BlockSpec((tm, D_LRU), lambda i: (i, 0)),
            pl.BlockSpec((tm, D_LRU), lambda i: (i, 0)),
        ),
        compiler_params=_params(("parallel",)),
        name="inproj",
    )(x2, sh, sc, g, w, dft_ch)


def _inproj_ctx_kernel(x_ref, sh_ref, sc_ref, g_ref, w_ref, ur_ref):
    hx = _rms_modulate(x_ref[...], g_ref[...], sh_ref[...], sc_ref[...]).astype(BF16)
    ur_ref[...] = jnp.dot(hx, w_ref[...], preferred_element_type=F32)


def _inproj_ctx_call(ctx2, sh, sc, g, w):
    n = ctx2.shape[0]
    vec = pl.BlockSpec((1, D_MODEL), lambda i: (0, 0))
    return pl.pallas_call(
        _inproj_ctx_kernel,
        out_shape=jax.ShapeDtypeStruct((n, D_LRU), F32),
        grid=(n // TOKEN_TILE,),
        in_specs=[pl.BlockSpec((TOKEN_TILE, D_MODEL), lambda i: (i, 0)), vec, vec, vec,
                  pl.BlockSpec(w.shape, lambda i: (0, 0))],
        out_specs=pl.BlockSpec((TOKEN_TILE, D_LRU), lambda i: (i, 0)),
        compiler_params=_params(("parallel",)),
        name="inproj_ctx",
    )(ctx2, sh, sc, g, w)


def _rglru_kernel(n_chunks, u_ref, up_ref, un_ref, cw_ref, cb_ref, wg_ref, bg_ref, lam_ref, h0_ref,
                  h_ref, u_t, a_s, b_s, h_s, state):
    tc, nb = SCAN_CHUNK, SCAN_BATCH
    d = pl.program_id(0)
    k = pl.program_id(2)
    kk = k + d * (n_chunks - 1 - 2 * k)

    @pl.when(k == 0)
    def _():
        state[...] = h0_ref[0]

    lam = lam_ref[0]
    neg_lam = -lam
    softplus = jnp.maximum(neg_lam, 0.0) + jnp.log1p(jnp.exp(-jnp.abs(neg_lam)))
    rate = (-0.5 * LRU_C * np.log2(np.e).astype(np.float32)) * softplus
    has_prev = kk > 0
    has_next = kk < n_chunks - 1
    cw = cw_ref[...]
    slabs = [slice(j * LANES, (j + 1) * LANES) for j in range(N_SLABS)]
    n_right = CONV_WIDTH - 1 - CONV_LEFT

    for i in range(nb):
        u = u_ref[i]
        prev = jnp.where(has_prev, up_ref[i], 0.0)
        nxt = jnp.where(has_next, un_ref[i], 0.0)
        for j, sl in enumerate(slabs):
            for p in range(CONV_LEFT):
                u_t[j, pl.ds(p * nb + i, 1), :] = prev[SUBLANES - CONV_LEFT + p:SUBLANES - CONV_LEFT + p + 1, sl]
            u_t[j, pl.ds(CONV_LEFT * nb + i, tc, stride=nb), :] = u[:, sl]
            for p in range(n_right):
                u_t[j, pl.ds((CONV_LEFT + tc + p) * nb + i, 1), :] = nxt[p:p + 1, sl]

    group = SCAN_GROUP * nb
    for g in range(tc // SCAN_GROUP):
        r0 = g * group

        def tap(k):
            return jnp.concatenate([u_t[j, pl.ds(r0 + k * nb, group), :] for j in range(N_SLABS)], axis=1)

        xc = cb_ref[...] + tap(0) * cw[0:1]
        for k in range(1, CONV_WIDTH):
            xc = xc + tap(k) * cw[k:k + 1]
        gz = jnp.dot(xc.astype(BF16), wg_ref[0], preferred_element_type=F32) + bg_ref[0]
        t_a = jnp.tanh(gz[:, :D_LRU])
        t_x = jnp.tanh(gz[:, D_LRU:])
        a = jnp.exp2(rate + rate * t_a)
        half_xc = 0.5 * xc
        b = jnp.sqrt(1.0 - a * a) * (half_xc + half_xc * t_x)
        for j, sl in enumerate(slabs):
            a_s[j, pl.ds(r0, group), :] = a[:, sl]
            b_s[j, pl.ds(r0, group), :] = b[:, sl]

    def step(s, h):
        t = s + d * (tc - 1 - 2 * s)
        rows = pl.ds(pl.multiple_of(t * nb, nb), nb)
        out = []
        for j in range(N_SLABS):
            hj = a_s[j, rows, :] * h[j] + b_s[j, rows, :]
            h_s[j, rows, :] = hj
            out.append(hj)
        return tuple(out)

    st = state[...]
    h = lax.fori_loop(0, tc, step, tuple(st[:, sl] for sl in slabs), unroll=8)
    for j, sl in enumerate(slabs):
        state[:, sl] = h[j]
    for i in range(nb):
        for j, sl in enumerate(slabs):
            h_ref[0, i, :, sl] = h_s[j, pl.ds(i, tc, stride=nb), :]


def _rglru_call(u3, conv_w, conv_b, wg, bg, lam, h0):
    batch, seq, _ = u3.shape
    n_chunks = seq // SCAN_CHUNK
    halo_blocks = SCAN_CHUNK // SUBLANES
    last_halo = seq // SUBLANES - 1

    def chunk(d, k):
        return k + d * (n_chunks - 1 - 2 * k)

    scratch = pltpu.VMEM((N_SLABS, SCAN_BATCH * SCAN_CHUNK, LANES), F32)
    padded = pltpu.VMEM((N_SLABS, SCAN_BATCH * (SCAN_CHUNK + CONV_WIDTH - 1), LANES), F32)
    return pl.pallas_call(
        functools.partial(_rglru_kernel, n_chunks),
        out_shape=jax.ShapeDtypeStruct((2, batch, seq, D_LRU), F32),
        grid=(2, batch // SCAN_BATCH, n_chunks),
        in_specs=[
            pl.BlockSpec((SCAN_BATCH, SCAN_CHUNK, D_LRU), lambda d, g, k: (g, chunk(d, k), 0)),
            pl.BlockSpec((SCAN_BATCH, SUBLANES, D_LRU),
                         lambda d, g, k: (g, jnp.maximum(chunk(d, k) * halo_blocks - 1, 0), 0)),
            pl.BlockSpec((SCAN_BATCH, SUBLANES, D_LRU),
                         lambda d, g, k: (g, jnp.minimum((chunk(d, k) + 1) * halo_blocks, last_halo), 0)),
            pl.BlockSpec((CONV_WIDTH, D_LRU), lambda d, g, k: (0, 0)),
            pl.BlockSpec((1, D_LRU), lambda d, g, k: (0, 0)),
            pl.BlockSpec((1, D_LRU, 2 * D_LRU), lambda d, g, k: (d, 0, 0)),
            pl.BlockSpec((1, 1, 2 * D_LRU), lambda d, g, k: (d, 0, 0)),
            pl.BlockSpec((1, 1, D_LRU), lambda d, g, k: (d, 0, 0)),
            pl.BlockSpec((1, SCAN_BATCH, D_LRU), lambda d, g, k: (d, g, 0)),
        ],
        out_specs=pl.BlockSpec((1, SCAN_BATCH, SCAN_CHUNK, D_LRU), lambda d, g, k: (d, g, chunk(d, k), 0)),
        scratch_shapes=[padded, scratch, scratch, scratch, pltpu.VMEM((SCAN_BATCH, D_LRU), F32)],
        compiler_params=_params(("arbitrary", "arbitrary", "arbitrary")),
        name="rglru",
    )(u3, u3, u3, conv_w, conv_b, wg, bg, lam, h0)


def _fourier_kernel(m_ref, ab_ref, o_ref):
    seq2 = m_ref.shape[1]
    rhs = ab_ref[0].reshape(seq2, D_FOURIER)
    o_ref[0] = jnp.dot(m_ref[...], rhs, preferred_element_type=F32).astype(BF16)


def _fourier_call(m_pos, ab):
    batch, _, seq, _ = ab.shape
    return pl.pallas_call(
        _fourier_kernel,
        out_shape=jax.ShapeDtypeStruct((batch, seq, D_FOURIER), BF16),
        grid=(seq // FOURIER_TILE, batch),
        in_specs=[
            pl.BlockSpec((FOURIER_TILE, 2 * seq), lambda m, b: (m, 0)),
            pl.BlockSpec((1, 2, seq, D_FOURIER), lambda m, b: (b, 0, 0, 0)),
        ],
        out_specs=pl.BlockSpec((1, FOURIER_TILE, D_FOURIER), lambda m, b: (b, m, 0)),
        compiler_params=_params(("parallel", "parallel")),
        name="fourier",
    )(m_pos, ab)


def _route(logits):
    lane = lax.broadcasted_iota(jnp.int32, logits.shape, 1)
    neg = -jnp.inf
    gl = jnp.where(lane < N_GROUPS, logits, neg)
    gmax = jnp.max(gl, axis=1, keepdims=True)
    grp = jnp.min(jnp.where(gl == gmax, lane, ROUTE_LANES), axis=1, keepdims=True)
    p_grp = 1.0 / jnp.sum(jnp.exp(gl - gmax), axis=1, keepdims=True)
    e_lane = lane - N_GROUPS
    in_grp = (e_lane >= 0) & (e_lane < N_EXPERTS) & ((e_lane // EXPERTS_PER_GROUP) == grp)
    el = jnp.where(in_grp, logits, neg)
    t1 = jnp.max(el, axis=1, keepdims=True)
    i1 = jnp.min(jnp.where(el == t1, lane, ROUTE_LANES), axis=1, keepdims=True)
    el2 = jnp.where(lane == i1, neg, el)
    t2 = jnp.max(el2, axis=1, keepdims=True)
    i2 = jnp.min(jnp.where(el2 == t2, lane, ROUTE_LANES), axis=1, keepdims=True)
    e = jnp.exp(t2 - t1)
    w1 = p_grp / (1.0 + e)
    w2 = p_grp * e / (1.0 + e)
    out = jnp.where(lane == 0, (i1 - N_GROUPS).astype(F32), 0.0)
    out = jnp.where(lane == 1, (i2 - N_GROUPS).astype(F32), out)
    out = jnp.where(lane == 2, w1, out)
    return jnp.where(lane == 3, w2, out)


def _merge_kernel(x_ref, sh1_ref, sc1_ref, g1_ref, sh2_ref, sc2_ref, n1_ref, n2_ref, wgate_ref,
                  fm_ref, hf_ref, hb_ref, yr_ref, wfo_ref, wlo_ref, wout_ref, wr_ref, br_ref,
                  x1_ref, h2_ref, route_ref):
    rows = [pl.ds(q * MERGE_GROUP_ROWS, MERGE_GROUP_ROWS) for q in range(x_ref.shape[0] // MERGE_GROUP_ROWS)]
    groups = range(len(rows))

    def dot(a, w_ref):
        return jnp.dot(a, w_ref[...], preferred_element_type=F32)

    x = [x_ref[r, :] for r in rows]
    hx = [_rms_modulate(v, n1_ref[...], sh1_ref[0], sc1_ref[0]).astype(BF16) for v in x]
    lr = []
    for r in rows:
        y = yr_ref[r, :].astype(F32)
        gelu = 0.5 * y * (1.0 + jnp.tanh(np.sqrt(2.0 / np.pi).astype(np.float32) * (y + 0.044715 * (y * y * y))))
        lr.append(((hf_ref[0, 0, r, :] + hb_ref[0, 0, r, :]) * gelu).astype(BF16))
    branch_f = [dot(fm_ref[0, r, :], wfo_ref) for r in rows]
    branch_r = [dot(v, wlo_ref) for v in lr]
    t = [jnp.tanh(v) for v in [dot(v, wgate_ref) for v in hx]]
    mixed = [((branch_f[q] + branch_r[q])
              + (t[q][:, :D_MODEL] * branch_f[q] + t[q][:, D_MODEL:] * branch_r[q])).astype(BF16) for q in groups]
    mix = [dot(v, wout_ref) for v in mixed]
    x1 = [x[q] + g1_ref[0] * mix[q] for q in groups]
    for q in groups:
        x1_ref[rows[q], :] = x1[q]
    h2 = [_rms_modulate(v, n2_ref[...], sh2_ref[0], sc2_ref[0]) for v in x1]
    route = []
    for v in h2:
        hi = v.astype(BF16)
        lo = (v - hi.astype(F32)).astype(BF16)
        parts = dot(hi, wr_ref) + dot(lo, wr_ref)
        route.append(_route(parts[:, :ROUTE_LANES] + parts[:, ROUTE_LANES:] + br_ref[...]))
    _store_token_tiled(h2_ref, jnp.concatenate(h2, axis=0))
    route_ref[...] = jnp.concatenate(route, axis=0)


def _merge_call(x2, mods, n1, n2, wgate, fm, h, yr, wfo, wlo, wout, wr, br, batch, seq):
    n = batch * seq
    tm = TOKEN_TILE
    tpb = seq // tm
    mod_spec = pl.BlockSpec((1, 1, D_MODEL), lambda i: (i // tpb, 0, 0))
    vec = pl.BlockSpec((1, D_MODEL), lambda i: (0, 0))
    tile = pl.BlockSpec((tm, D_MODEL), lambda i: (i, 0))

    def full(a):
        return pl.BlockSpec(a.shape, lambda i: (0,) * a.ndim)

    return pl.pallas_call(
        _merge_kernel,
        out_shape=(
            jax.ShapeDtypeStruct((n, D_MODEL), F32),
            jax.ShapeDtypeStruct((n * TOKEN_ROWS, LANES), U32),
            jax.ShapeDtypeStruct((n, ROUTE_LANES), F32),
        ),
        grid=(n // tm,),
        in_specs=[
            tile, mod_spec, mod_spec, mod_spec, mod_spec, mod_spec, vec, vec, full(wgate),
            pl.BlockSpec((1, tm, D_FOURIER), lambda i: (i // tpb, i % tpb, 0)),
            pl.BlockSpec((1, 1, tm, D_LRU), lambda i: (0, i // tpb, i % tpb, 0)),
            pl.BlockSpec((1, 1, tm, D_LRU), lambda i: (1, i // tpb, i % tpb, 0)),
            pl.BlockSpec((tm, D_LRU), lambda i: (i, 0)),
            full(wfo), full(wlo), full(wout), full(wr), full(br),
        ],
        out_specs=(tile, pl.BlockSpec((tm * TOKEN_ROWS, LANES), lambda i: (i, 0)),
                   pl.BlockSpec((tm, ROUTE_LANES), lambda i: (i, 0))),
        compiler_params=_params(("parallel",)),
        name="merge",
    )(x2, *mods, n1, n2, wgate, fm, h, h, yr, wfo, wlo, wout, wr, br)


def _token_rows(t, count=1):
    start = t * TOKEN_ROWS
    if not isinstance(start, int):
        start = pl.multiple_of(start, TOKEN_ROWS)
    return pl.ds(start, count * TOKEN_ROWS)


def _row_copy(src_ref, dst_ref, sem, src_tok, dst_tok):
    return pltpu.make_async_copy(src_ref.at[_token_rows(src_tok)], dst_ref.at[_token_rows(dst_tok)], sem)


def _moe_kernel(n_tokens, be_ref, bw_ref, off_ref, nv_ref, nxt_ref, so_ref, rows_ref,
                h2_hbm, wg_hbm, wu_hbm, wd_hbm, out_hbm,
                win, xbuf, y0, y1, y2, wg_f, wu_f, wd_f, wg_b, wu_b, wd_b, win_sem, w_sem, sct_sem):
    i = pl.program_id(0)
    n_assign = n_tokens * TOP_K
    ys = (y0, y1, y2)
    prev = jnp.maximum(i - 1, 0)

    def scatter_wait(s):
        pltpu.make_async_copy(ys[s], out_hbm.at[_token_rows(0, MOE_BLOCK)], sct_sem.at[s]).wait()

    def weight_copies(e, slot):
        return [pltpu.make_async_copy(w.at[e], f.at[slot], w_sem.at[slot])
                for w, f in ((wg_hbm, wg_f), (wu_hbm, wu_f), (wd_hbm, wd_f))]

    def gather_rows():
        base = off_ref[i]
        for r in range(MOE_BLOCK):
            xbuf[_token_rows(r)] = win[_token_rows(rows_ref[base + r] & LOCAL_MASK)]

    def issue_scatter_prev(s):
        base = off_ref[prev]
        nv = jnp.where(i > 0, nv_ref[prev], 0)
        dummy0 = n_assign + s * MOE_BLOCK
        for r in range(MOE_BLOCK):
            dst = lax.shift_right_logical(rows_ref[base + r], LOCAL_BITS)
            _row_copy(ys[s], out_hbm, sct_sem.at[s], r, jnp.where(r < nv, dst, dummy0 + r)).start(priority=r % 2)

    active = nv_ref[i] > 0
    prev_active = (i > 0) & (nv_ref[prev] > 0)
    w_slot = so_ref[i] % 2

    @pl.when(i == 0)
    def _():
        for y in ys:
            y[...] = jnp.zeros(y.shape, U32)
        fills = [pltpu.make_async_copy(ys[q], out_hbm.at[_token_rows(n_assign + q * MOE_BLOCK, MOE_BLOCK)],
                                       sct_sem.at[q]) for q in range(N_OUT_BUFFERS)]
        for cp in fills:
            cp.start()
        fills[-1].wait()
        for cp in weight_copies(be_ref[0], 0):
            cp.start()

    @pl.when(active & ((i == 0) | (bw_ref[i] != bw_ref[prev])))
    def _():
        cp = pltpu.make_async_copy(h2_hbm.at[_token_rows(bw_ref[i] * MOE_WINDOW, MOE_WINDOW)], win, win_sem)
        cp.start()
        cp.wait()

    @pl.when(active & ((i == 0) | (so_ref[i] != so_ref[prev])))
    def _():
        for cp in weight_copies(be_ref[i], w_slot):
            cp.wait()
        wg_b[...] = wg_f[w_slot].astype(BF16)
        wu_b[...] = wu_f[w_slot].astype(BF16)
        wd_b[...] = wd_f[w_slot].astype(BF16)

        @pl.when(nxt_ref[i] >= 0)
        def _():
            for cp in weight_copies(nxt_ref[i], 1 - w_slot):
                cp.start()

    for s in range(N_OUT_BUFFERS):
        prev_s, prev2_s = (s + 2) % N_OUT_BUFFERS, (s + 1) % N_OUT_BUFFERS

        @pl.when(active & (i % N_OUT_BUFFERS == s))
        def _(s=s, prev_s=prev_s):
            scatter_wait(s)
            gather_rows()
            issue_scatter_prev(prev_s)
            xb = _load_token_tiled(xbuf, MOE_BLOCK, BF16)
            hg = jnp.dot(xb, wg_b[...], preferred_element_type=F32)
            hu = jnp.dot(xb, wu_b[...], preferred_element_type=F32)
            hb = (hg * jax.nn.sigmoid(hg) * hu).astype(BF16)
            _store_token_tiled(ys[s], jnp.dot(hb, wd_b[...], preferred_element_type=F32))

        @pl.when(jnp.logical_not(active) & prev_active & (i % N_OUT_BUFFERS == s))
        def _(s=s, prev_s=prev_s, prev2_s=prev2_s):
            issue_scatter_prev(prev_s)
            scatter_wait(s)
            scatter_wait(prev2_s)
            scatter_wait(prev_s)


def _moe_call(blk_e, blk_w, blk_off, blk_nv, blk_next_e, blk_seg, rows, h2, w_g, w_u, w_d):
    n_steps = blk_e.shape[0]
    n_tokens = h2.shape[0] // TOKEN_ROWS
    any_spec = pl.BlockSpec(memory_space=pl.ANY)
    block_buf = pltpu.VMEM((MOE_BLOCK * TOKEN_ROWS, LANES), U32)
    grid_spec = pltpu.PrefetchScalarGridSpec(
        num_scalar_prefetch=7,
        grid=(n_steps,),
        in_specs=[any_spec, any_spec, any_spec, any_spec],
        out_specs=any_spec,
        scratch_shapes=[
            pltpu.VMEM((MOE_WINDOW * TOKEN_ROWS, LANES), U32),
            block_buf, block_buf, block_buf, block_buf,
            pltpu.VMEM((2, D_MODEL, D_EXPERT), F32),
            pltpu.VMEM((2, D_MODEL, D_EXPERT), F32),
            pltpu.VMEM((2, D_EXPERT, D_MODEL), F32),
            pltpu.VMEM((D_MODEL, D_EXPERT), BF16),
            pltpu.VMEM((D_MODEL, D_EXPERT), BF16),
            pltpu.VMEM((D_EXPERT, D_MODEL), BF16),
            pltpu.SemaphoreType.DMA,
            pltpu.SemaphoreType.DMA((2,)),
            pltpu.SemaphoreType.DMA((N_OUT_BUFFERS,)),
        ],
    )
    return pl.pallas_call(
        functools.partial(_moe_kernel, n_tokens),
        out_shape=jax.ShapeDtypeStruct(((n_tokens * TOP_K + N_DUMMY_ROWS) * TOKEN_ROWS, LANES), U32),
        grid_spec=grid_spec,
        compiler_params=_params(("arbitrary",)),
        name="moe",
    )(blk_e, blk_w, blk_off, blk_nv, blk_next_e, blk_seg, rows, h2, w_g, w_u, w_d)


def _final_kernel(x1_ref, y0_ref, y1_ref, route_ref, g2_ref, fg_ref, o_ref):
    route = route_ref[...]
    rows = route.shape[0]
    moe = (route[:, 2:3] * _load_token_tiled(y0_ref, rows, F32)
           + route[:, 3:4] * _load_token_tiled(y1_ref, rows, F32))
    x2 = x1_ref[...] + g2_ref[0] * moe
    o_ref[...] = x2 * lax.rsqrt(jnp.mean(x2 * x2, axis=-1, keepdims=True) + EPS) * fg_ref[...]


def _final_call(x1, out2, route, g2, fg, batch, seq):
    n = batch * seq
    tm = WIDE_TILE
    tpb = seq // tm
    tile = pl.BlockSpec((tm, D_MODEL), lambda i: (i, 0))
    return pl.pallas_call(
        _final_kernel,
        out_shape=jax.ShapeDtypeStruct((n, D_MODEL), F32),
        grid=(n // tm,),
        in_specs=[
            tile,
            pl.BlockSpec((tm * TOKEN_ROWS, LANES), lambda i: (i, 0)),
            pl.BlockSpec((tm * TOKEN_ROWS, LANES), lambda i: (i + n // tm, 0)),
            pl.BlockSpec((tm, ROUTE_LANES), lambda i: (i, 0)),
            pl.BlockSpec((1, 1, D_MODEL), lambda i: (i // tpb, 0, 0)),
            pl.BlockSpec((1, D_MODEL), lambda i: (0, 0)),
        ],
        out_specs=tile,
        compiler_params=_params(("parallel",)),
        name="final",
    )(x1, out2, out2, route, g2, fg)


def _channel_dft():
    j = np.arange(FOURIER_GROUP_DIM)
    ang = 2.0 * np.pi * np.outer(j, j) / FOURIER_GROUP_DIM
    return np.concatenate([np.cos(ang), np.sin(ang)], axis=1).astype(np.float32)


def _position_dft(seq):
    rows = seq // GRID_W
    assert GRID_W % rows == 0
    r, c = np.divmod(np.arange(seq), GRID_W)
    phase = (np.outer(r, r) * (GRID_W // rows) + np.outer(c, c)) % GRID_W
    ang = 2.0 * np.pi * phase / GRID_W
    scale = 1.0 / np.sqrt(float(seq) * FOURIER_GROUP_DIM)
    return np.concatenate([np.cos(ang), -np.sin(ang)], axis=1) * scale


def _block_diag(w):
    heads, hd, _ = w.shape
    eye = jnp.eye(heads, dtype=w.dtype)
    return jnp.einsum('hij,hg->higj', w, eye).reshape(heads * hd, heads * hd)


def _gate_weights(w_a, w_x):
    return jnp.stack([0.5 * jnp.concatenate([_block_diag(w_a[d]), _block_diag(w_x[d])], axis=1)
                      for d in range(2)]).astype(BF16)


def _dispatch(eid, n_tokens):
    n_assign = n_tokens * TOP_K
    n_seg = (n_tokens // MOE_WINDOW) * N_EXPERTS
    n_blk = n_assign // MOE_BLOCK + n_seg
    a = jnp.arange(n_assign, dtype=jnp.int32)
    seg_of = ((a % n_tokens) // MOE_WINDOW) * N_EXPERTS + eid.reshape(-1)
    a_bits = (n_assign - 1).bit_length()
    assert n_seg << a_bits < 2 ** 31
    order = lax.sort(lax.shift_left(seg_of, a_bits) | a) & ((1 << a_bits) - 1)
    segs = jnp.arange(n_seg, dtype=jnp.int32)
    counts = jnp.sum((seg_of[:, None] == segs[None, :]).astype(jnp.int32), axis=0)
    start = jnp.cumsum(counts) - counts
    blocks = (counts + MOE_BLOCK - 1) // MOE_BLOCK
    bend = jnp.cumsum(blocks)
    b = jnp.arange(n_blk + 1, dtype=jnp.int32)
    ended = (b[:, None] >= bend[None, :]).astype(jnp.int32)
    seg = jnp.minimum(jnp.sum(ended, axis=1), n_seg - 1)
    onehot = (seg[:, None] == segs[None, :]).astype(jnp.int32)
    in_seg = (b - jnp.sum(onehot * (bend - blocks)[None, :], axis=1)) * MOE_BLOCK
    blk_off = jnp.sum(onehot * start[None, :], axis=1) + in_seg
    blk_nv = jnp.clip(jnp.sum(onehot * counts[None, :], axis=1) - in_seg, 0, MOE_BLOCK)
    seg_end = jnp.sum(onehot * bend[None, :], axis=1)
    follows = (b[None, :] == seg_end[:, None]).astype(jnp.int32)
    next_e = jnp.sum(follows * (seg % N_EXPERTS)[None, :], axis=1)
    next_nv = jnp.sum(follows * blk_nv[None, :], axis=1)
    blk_next_e = jnp.where(next_nv > 0, next_e, -1)

    rows = lax.shift_left(order, LOCAL_BITS) | ((order % n_tokens) % MOE_WINDOW)
    dummy = lax.shift_left(n_assign + jnp.arange(N_DUMMY_ROWS, dtype=jnp.int32), LOCAL_BITS)
    seg_ordinal = jnp.cumsum((blocks > 0).astype(jnp.int32)) - 1
    blk_seg = jnp.sum(onehot * seg_ordinal[None, :], axis=1)
    assert MOE_WINDOW <= LOCAL_MASK + 1 and (n_assign + N_DUMMY_ROWS) << LOCAL_BITS <= 2 ** 32
    return seg % N_EXPERTS, seg // N_EXPERTS, blk_off, blk_nv, blk_next_e, blk_seg, jnp.concatenate([rows, dummy])


def kernel(x, c, ctx, c_ctx, w_mod, b_mod, norm1_g, w_in, conv_w, conv_b, lru_wa, lru_ba, lru_wx, lru_bx,
           lru_lam, w_fourier_out, w_lru_out, w_out, norm2_g, w_group, b_group, w_expert_router,
           b_expert_router, w_gate_e, w_up_e, w_down_e, final_g):
    batch, seq, _ = x.shape
    ctx_len = ctx.shape[1]
    n = batch * seq
    assert w_mod.shape[0] == 1, "single-layer stack only: the context stream is not carried across layers"
    x2 = x.reshape(n, D_MODEL)
    dft_ch = jnp.asarray(_channel_dft().astype(BF16))
    m_pos = jnp.asarray(_position_dft(seq).astype(BF16))

    for l in range(1):
        c_all = jnp.concatenate([c, c_ctx[None], jnp.zeros((MOD_ROWS - batch - 1, D_MODEL), F32)], axis=0)
        mod = _mod_call(c_all, w_mod[l], b_mod[l][None])
        sh1, sc1, g1, sh2, sc2, g2 = [m[:batch, None, :] for m in jnp.split(mod, 6, axis=-1)]
        csh1, csc1 = mod[batch:batch + 1, :D_MODEL], mod[batch:batch + 1, D_MODEL:2 * D_MODEL]

        w_in_b = w_in[l].astype(BF16)
        n1 = norm1_g[l][None]
        wg = _gate_weights(lru_wa[l], lru_wx[l])
        bg = 0.5 * jnp.concatenate([lru_ba[l], lru_bx[l]], axis=-1)[:, None, :]
        lam = lru_lam[l][:, None, :]
        cb = conv_b[l][None]

        uc = _inproj_ctx_call(ctx.reshape(batch * ctx_len, D_MODEL), csh1, csc1, n1,
                              w_in_b[:, D_FOURIER:D_FOURIER + D_LRU])
        hc = _rglru_call(uc.reshape(batch, ctx_len, D_LRU), conv_w[l], cb, wg, bg, lam,
                         jnp.zeros((2, batch, D_LRU), F32))
        h0 = jnp.stack([hc[0, :, -1], hc[1, :, 0]])

        ab, ur, yr = _inproj_call(x2, sh1, sc1, n1, w_in_b[:, :D_FOURIER + 2 * D_LRU], dft_ch, batch, seq)
        h = _rglru_call(ur.reshape(batch, seq, D_LRU), conv_w[l], cb, wg, bg, lam, h0)
        fm = _fourier_call(m_pos, ab)

        w_route = jnp.zeros((D_MODEL, ROUTE_LANES), F32)
        w_route = w_route.at[:, :N_GROUPS].set(w_group[l]).at[:, N_GROUPS:N_GROUPS + N_EXPERTS].set(
            w_expert_router[l])
        b_route = jnp.zeros((1, ROUTE_LANES), F32)
        b_route = b_route.at[0, :N_GROUPS].set(b_group[l]).at[0, N_GROUPS:N_GROUPS + N_EXPERTS].set(
            b_expert_router[l])
        w_route_hi = w_route.astype(BF16)
        w_route_lo = (w_route - w_route_hi.astype(F32)).astype(BF16)
        w_route = jnp.concatenate([w_route_hi, w_route_lo], axis=1)
        x1, h2, route = _merge_call(
            x2, (sh1, sc1, g1, sh2, sc2), n1, norm2_g[l][None],
            (0.5 * w_in[l][:, D_FOURIER + 2 * D_LRU:]).astype(BF16),
            fm, h, yr, w_fourier_out[l].astype(BF16), w_lru_out[l].astype(BF16),
            (0.5 * w_out[l]).astype(BF16),
            w_route, b_route, batch, seq)

        eid = jnp.stack([route[:, k] for k in range(TOP_K)]).astype(jnp.int32)
        out2 = _moe_call(*_dispatch(eid, n), h2, w_gate_e[l], w_up_e[l], w_down_e[l])
        x2 = _final_call(x1, out2, route, g2, final_g[None], batch, seq)
    return x2.reshape(batch, seq, D_MODEL)
```

```python
import functools

import numpy as np
import jax
import jax.numpy as jnp
from jax import lax
from jax.experimental import pallas as pl
from jax.experimental.pallas import tpu as pltpu

F32 = jnp.float32
BF16 = jnp.bfloat16

LANES = 128
SUBLANES = 8
VMEM_LIMIT_BYTES = 56 * 1024 * 1024

D_MODEL = 1024
GRID_W = 64
EPS = 1e-6
N_FOURIER_GROUPS = 4
FOURIER_GROUP_DIM = 128
D_FOURIER = N_FOURIER_GROUPS * FOURIER_GROUP_DIM
N_LRU_HEADS = 8
LRU_HEAD_DIM = 64
D_LRU = N_LRU_HEADS * LRU_HEAD_DIM
CONV_WIDTH = 4
CONV_LEFT = 2
LRU_C = 8.0
N_GROUPS = 4
EXPERTS_PER_GROUP = 8
N_EXPERTS = N_GROUPS * EXPERTS_PER_GROUP
TOP_K = 2
D_EXPERT = 512

MOD_ROWS = 24
MOD_BLOCK_N = 1536
TOKEN_TILE = 512
WIDE_TILE = 1024
MERGE_GROUP_ROWS = 128
SCAN_BATCH = SUBLANES
SCAN_CHUNK = 256
SCAN_GROUP = 128
N_SLABS = D_LRU // LANES
FOURIER_TILE = 512
MOE_BLOCK = 256
ROUTE_LANES = LANES
N_OUT_BUFFERS = 3
N_DUMMY_ROWS = N_OUT_BUFFERS * MOE_BLOCK
MOE_WINDOW = 16384
LOCAL_BITS = 14
LOCAL_MASK = (1 << LOCAL_BITS) - 1


def _params(semantics):
    return pltpu.CompilerParams(dimension_semantics=semantics, vmem_limit_bytes=VMEM_LIMIT_BYTES)


U32 = jnp.uint32
HALF = D_MODEL // 2
TOKEN_ROWS = HALF // LANES
HIGH_HALF_WORD = np.uint32(0xFFFF0000)


def _pack_bf16_pairs(x):
    bits = pltpu.bitcast(x.astype(BF16).astype(F32), U32)
    return (bits[:, :HALF] >> 16) | (bits[:, HALF:] & HIGH_HALF_WORD)


def _unpack_bf16_pairs(u, dtype):
    lo = pltpu.bitcast(u << 16, F32)
    hi = pltpu.bitcast(u & HIGH_HALF_WORD, F32)
    return jnp.concatenate([lo, hi], axis=1).astype(dtype)


def _store_token_tiled(ref, x):
    rows = x.shape[0]
    packed = _pack_bf16_pairs(x)
    for s in range(TOKEN_ROWS):
        ref[pl.ds(s, rows, stride=TOKEN_ROWS), :] = packed[:, s * LANES:(s + 1) * LANES]


def _load_token_tiled(ref, rows, dtype):
    packed = jnp.concatenate([ref[pl.ds(s, rows, stride=TOKEN_ROWS), :] for s in range(TOKEN_ROWS)], axis=1)
    return _unpack_bf16_pairs(packed, dtype)


def _rms_modulate(x, g, shift, scale):
    y = x * lax.rsqrt(jnp.mean(x * x, axis=-1, keepdims=True) + EPS) * g
    return y * (1.0 + scale) + shift


def _mod_kernel(c_ref, w_ref, b_ref, o_ref):
    c = c_ref[...]
    s = c * jax.nn.sigmoid(c)
    o_ref[...] = jnp.dot(s, w_ref[...], preferred_element_type=F32,
                         precision=lax.Precision.HIGHEST) + b_ref[...]


def _mod_call(c_all, w_mod, b_mod):
    n_out = w_mod.shape[1]
    return pl.pallas_call(
        _mod_kernel,
        out_shape=jax.ShapeDtypeStruct((MOD_ROWS, n_out), F32),
        grid=(n_out // MOD_BLOCK_N,),
        in_specs=[
            pl.BlockSpec((MOD_ROWS, D_MODEL), lambda j: (0, 0)),
            pl.BlockSpec((D_MODEL, MOD_BLOCK_N), lambda j: (0, j)),
            pl.BlockSpec((1, MOD_BLOCK_N), lambda j: (0, j)),
        ],
        out_specs=pl.BlockSpec((MOD_ROWS, MOD_BLOCK_N), lambda j: (0, j)),
        compiler_params=_params(("arbitrary",)),
        name="mod",
    )(c_all, w_mod, b_mod)


def _inproj_kernel(x_ref, sh_ref, sc_ref, g_ref, w_ref, dft_ref, ab_ref, ur_ref, yr_ref):
    hx = _rms_modulate(x_ref[...], g_ref[...], sh_ref[0], sc_ref[0]).astype(BF16)
    proj = jnp.dot(hx, w_ref[...], preferred_element_type=F32)
    uf = proj[:, :D_FOURIER].astype(BF16)
    for g in range(N_FOURIER_GROUPS):
        lo, hi = g * FOURIER_GROUP_DIM, (g + 1) * FOURIER_GROUP_DIM
        cs = jnp.dot(uf[:, lo:hi], dft_ref[...], preferred_element_type=F32)
        ab_ref[0, 0, :, lo:hi] = cs[:, :FOURIER_GROUP_DIM].astype(BF16)
        ab_ref[0, 1, :, lo:hi] = cs[:, FOURIER_GROUP_DIM:].astype(BF16)
    ur_ref[...] = proj[:, D_FOURIER:D_FOURIER + D_LRU]
    yr_ref[...] = proj[:, D_FOURIER + D_LRU:].astype(BF16)


def _inproj_call(x2, sh, sc, g, w, dft_ch, batch, seq):
    n = batch * seq
    tm = WIDE_TILE
    tpb = seq // tm
    mod_spec = pl.BlockSpec((1, 1, D_MODEL), lambda i: (i // tpb, 0, 0))
    return pl.pallas_call(
        _inproj_kernel,
        out_shape=(
            jax.ShapeDtypeStruct((batch, 2, seq, D_FOURIER), BF16),
            jax.ShapeDtypeStruct((n, D_LRU), F32),
            jax.ShapeDtypeStruct((n, D_LRU), BF16),
        ),
        grid=(n // tm,),
        in_specs=[
            pl.BlockSpec((tm, D_MODEL), lambda i: (i, 0)),
            mod_spec, mod_spec,
            pl.BlockSpec((1, D_MODEL), lambda i: (0, 0)),
            pl.BlockSpec(w.shape, lambda i: (0, 0)),
            pl.BlockSpec(dft_ch.shape, lambda i: (0, 0)),
        ],
        out_specs=(
            pl.BlockSpec((1, 2, tm, D_FOURIER), lambda i: (i // tpb, 0, i % tpb, 0)),
            pl.BlockSpec((tm, D_LRU), lambda i: (i, 0)),
            pl.BlockSpec((tm, D_LRU), lambda i: (i, 0)),
        ),
        compiler_params=_params(("parallel",)),
        name="inproj",
    )(x2, sh, sc, g, w, dft_ch)


def _inproj_ctx_kernel(x_ref, sh_ref, sc_ref, g_ref, w_ref, ur_ref):
    hx = _rms_modulate(x_ref[...], g_ref[...], sh_ref[...], sc_ref[...]).astype(BF16)
    ur_ref[...] = jnp.dot(hx, w_ref[...], preferred_element_type=F32)


def _inproj_ctx_call(ctx2, sh, sc, g, w):
    n = ctx2.shape[0]
    vec = pl.BlockSpec((1, D_MODEL), lambda i: (0, 0))
    return pl.pallas_call(
        _inproj_ctx_kernel,
        out_shape=jax.ShapeDtypeStruct((n, D_LRU), F32),
        grid=(n // TOKEN_TILE,),
        in_specs=[pl.BlockSpec((TOKEN_TILE, D_MODEL), lambda i: (i, 0)), vec, vec, vec,
                  pl.BlockSpec(w.shape, lambda i: (0, 0))],
        out_specs=pl.BlockSpec((TOKEN_TILE, D_LRU), lambda i: (i, 0)),
        compiler_params=_params(("parallel",)),
        name="inproj_ctx",
    )(ctx2, sh, sc, g, w)


def _rglru_kernel(n_chunks, u_ref, up_ref, un_ref, cw_ref, cb_ref, wg_ref, bg_ref, lam_ref, h0_ref,
                  h_ref, u_t, a_s, b_s, h_s, state):
    tc, nb = SCAN_CHUNK, SCAN_BATCH
    d = pl.program_id(0)
    k = pl.program_id(2)
    kk = k + d * (n_chunks - 1 - 2 * k)

    @pl.when(k == 0)
    def _():
        state[...] = h0_ref[0]

    lam = lam_ref[0]
    neg_lam = -lam
    softplus = jnp.maximum(neg_lam, 0.0) + jnp.log1p(jnp.exp(-jnp.abs(neg_lam)))
    rate = (-0.5 * LRU_C * np.log2(np.e).astype(np.float32)) * softplus
    has_prev = kk > 0
    has_next = kk < n_chunks - 1
    cw = cw_ref[...]
    slabs = [slice(j * LANES, (j + 1) * LANES) for j in range(N_SLABS)]
    n_right = CONV_WIDTH - 1 - CONV_LEFT

    for i in range(nb):
        u = u_ref[i]
        prev = jnp.where(has_prev, up_ref[i], 0.0)
        nxt = jnp.where(has_next, un_ref[i], 0.0)
        for j, sl in enumerate(slabs):
            for p in range(CONV_LEFT):
                u_t[j, pl.ds(p * nb + i, 1), :] = prev[SUBLANES - CONV_LEFT + p:SUBLANES - CONV_LEFT + p + 1, sl]
            u_t[j, pl.ds(CONV_LEFT * nb + i, tc, stride=nb), :] = u[:, sl]
            for p in range(n_right):
                u_t[j, pl.ds((CONV_LEFT + tc + p) * nb + i, 1), :] = nxt[p:p + 1, sl]

    group = SCAN_GROUP * nb
    for g in range(tc // SCAN_GROUP):
        r0 = g * group

        def tap(k):
            return jnp.concatenate([u_t[j, pl.ds(r0 + k * nb, group), :] for j in range(N_SLABS)], axis=1)

        xc = cb_ref[...] + tap(0) * cw[0:1]
        for k in range(1, CONV_WIDTH):
            xc = xc + tap(k) * cw[k:k + 1]
        gz = jnp.dot(xc.astype(BF16), wg_ref[0], preferred_element_type=F32) + bg_ref[0]
        t_a = jnp.tanh(gz[:, :D_LRU])
        t_x = jnp.tanh(gz[:, D_LRU:])
        a = jnp.exp2(rate + rate * t_a)
        half_xc = 0.5 * xc
        b = jnp.sqrt(1.0 - a * a) * (half_xc + half_xc * t_x)
        for j, sl in enumerate(slabs):
            a_s[j, pl.ds(r0, group), :] = a[:, sl]
            b_s[j, pl.ds(r0, group), :] = b[:, sl]

    def step(s, h):
        t = s + d * (tc - 1 - 2 * s)
        rows = pl.ds(pl.multiple_of(t * nb, nb), nb)
        out = []
        for j in range(N_SLABS):
            hj = a_s[j, rows, :] * h[j] + b_s[j, rows, :]
            h_s[j, rows, :] = hj
            out.append(hj)
        return tuple(out)

    st = state[...]
    h = lax.fori_loop(0, tc, step, tuple(st[:, sl] for sl in slabs), unroll=8)
    for j, sl in enumerate(slabs):
        state[:, sl] = h[j]
    for i in range(nb):
        for j, sl in enumerate(slabs):
            h_ref[0, i, :, sl] = h_s[j, pl.ds(i, tc, stride=nb), :]


def _rglru_call(u3, conv_w, conv_b, wg, bg, lam, h0):
    batch, seq, _ = u3.shape
    n_chunks = seq // SCAN_CHUNK
    halo_blocks = SCAN_CHUNK // SUBLANES
    last_halo = seq // SUBLANES - 1

    def chunk(d, k):
        return k + d * (n_chunks - 1 - 2 * k)

    scratch = pltpu.VMEM((N_SLABS, SCAN_BATCH * SCAN_CHUNK, LANES), F32)
    padded = pltpu.VMEM((N_SLABS, SCAN_BATCH * (SCAN_CHUNK + CONV_WIDTH - 1), LANES), F32)
    return pl.pallas_call(
        functools.partial(_rglru_kernel, n_chunks),
        out_shape=jax.ShapeDtypeStruct((2, batch, seq, D_LRU), F32),
        grid=(2, batch // SCAN_BATCH, n_chunks),
        in_specs=[
            pl.BlockSpec((SCAN_BATCH, SCAN_CHUNK, D_LRU), lambda d, g, k: (g, chunk(d, k), 0)),
            pl.BlockSpec((SCAN_BATCH, SUBLANES, D_LRU),
                         lambda d, g, k: (g, jnp.maximum(chunk(d, k) * halo_blocks - 1, 0), 0)),
            pl.BlockSpec((SCAN_BATCH, SUBLANES, D_LRU),
                         lambda d, g, k: (g, jnp.minimum((chunk(d, k) + 1) * halo_blocks, last_halo), 0)),
            pl.BlockSpec((CONV_WIDTH, D_LRU), lambda d, g, k: (0, 0)),
            pl.BlockSpec((1, D_LRU), lambda d, g, k: (0, 0)),
            pl.BlockSpec((1, D_LRU, 2 * D_LRU), lambda d, g, k: (d, 0, 0)),
            pl.BlockSpec((1, 1, 2 * D_LRU), lambda d, g, k: (d, 0, 0)),
            pl.BlockSpec((1, 1, D_LRU), lambda d, g, k: (d, 0, 0)),
            pl.BlockSpec((1, SCAN_BATCH, D_LRU), lambda d, g, k: (d, g, 0)),
        ],
        out_specs=pl.BlockSpec((1, SCAN_BATCH, SCAN_CHUNK, D_LRU), lambda d, g, k: (d, g, chunk(d, k), 0)),
        scratch_shapes=[padded, scratch, scratch, scratch, pltpu.VMEM((SCAN_BATCH, D_LRU), F32)],
        compiler_params=_params(("arbitrary", "arbitrary", "arbitrary")),
        name="rglru",
    )(u3, u3, u3, conv_w, conv_b, wg, bg, lam, h0)


def _fourier_kernel(m_ref, ab_ref, o_ref):
    seq2 = m_ref.shape[1]
    rhs = ab_ref[0].reshape(seq2, D_FOURIER)
    o_ref[0] = jnp.dot(m_ref[...], rhs, preferred_element_type=F32).astype(BF16)


def _fourier_call(m_pos, ab):
    batch, _, seq, _ = ab.shape
    return pl.pallas_call(
        _fourier_kernel,
        out_shape=jax.ShapeDtypeStruct((batch, seq, D_FOURIER), BF16),
        grid=(seq // FOURIER_TILE, batch),
        in_specs=[
            pl.BlockSpec((FOURIER_TILE, 2 * seq), lambda m, b: (m, 0)),
            pl.BlockSpec((1, 2, seq, D_FOURIER), lambda m, b: (b, 0, 0, 0)),
        ],
        out_specs=pl.BlockSpec((1, FOURIER_TILE, D_FOURIER), lambda m, b: (b, m, 0)),
        compiler_params=_params(("parallel", "parallel")),
        name="fourier",
    )(m_pos, ab)


def _route(logits):
    lane = lax.broadcasted_iota(jnp.int32, logits.shape, 1)
    neg = -jnp.inf
    gl = jnp.where(lane < N_GROUPS, logits, neg)
    gmax = jnp.max(gl, axis=1, keepdims=True)
    grp = jnp.min(jnp.where(gl == gmax, lane, ROUTE_LANES), axis=1, keepdims=True)
    p_grp = 1.0 / jnp.sum(jnp.exp(gl - gmax), axis=1, keepdims=True)
    e_lane = lane - N_GROUPS
    in_grp = (e_lane >= 0) & (e_lane < N_EXPERTS) & ((e_lane // EXPERTS_PER_GROUP) == grp)
    el = jnp.where(in_grp, logits, neg)
    t1 = jnp.max(el, axis=1, keepdims=True)
    i1 = jnp.min(jnp.where(el == t1, lane, ROUTE_LANES), axis=1, keepdims=True)
    el2 = jnp.where(lane == i1, neg, el)
    t2 = jnp.max(el2, axis=1, keepdims=True)
    i2 = jnp.min(jnp.where(el2 == t2, lane, ROUTE_LANES), axis=1, keepdims=True)
    e = jnp.exp(t2 - t1)
    w1 = p_grp / (1.0 + e)
    w2 = p_grp * e / (1.0 + e)
    out = jnp.where(lane == 0, (i1 - N_GROUPS).astype(F32), 0.0)
    out = jnp.where(lane == 1, (i2 - N_GROUPS).astype(F32), out)
    out = jnp.where(lane == 2, w1, out)
    return jnp.where(lane == 3, w2, out)


def _merge_kernel(x_ref, sh1_ref, sc1_ref, g1_ref, sh2_ref, sc2_ref, n1_ref, n2_ref, wgate_ref,
                  fm_ref, hf_ref, hb_ref, yr_ref, wfo_ref, wlo_ref, wout_ref, wr_ref, br_ref,
                  x1_ref, h2_ref, route_ref):
    rows = [pl.ds(q * MERGE_GROUP_ROWS, MERGE_GROUP_ROWS) for q in range(x_ref.shape[0] // MERGE_GROUP_ROWS)]
    groups = range(len(rows))

    def dot(a, w_ref):
        return jnp.dot(a, w_ref[...], preferred_element_type=F32)

    x = [x_ref[r, :] for r in rows]
    hx = [_rms_modulate(v, n1_ref[...], sh1_ref[0], sc1_ref[0]).astype(BF16) for v in x]
    lr = []
    for r in rows:
        y = yr_ref[r, :].astype(F32)
        gelu = 0.5 * y * (1.0 + jnp.tanh(np.sqrt(2.0 / np.pi).astype(np.float32) * (y + 0.044715 * (y * y * y))))
        lr.append(((hf_ref[0, 0, r, :] + hb_ref[0, 0, r, :]) * gelu).astype(BF16))
    branch_f = [dot(fm_ref[0, r, :], wfo_ref) for r in rows]
    branch_r = [dot(v, wlo_ref) for v in lr]
    t = [jnp.tanh(v) for v in [dot(v, wgate_ref) for v in hx]]
    mixed = [((branch_f[q] + branch_r[q])
              + (t[q][:, :D_MODEL] * branch_f[q] + t[q][:, D_MODEL:] * branch_r[q])).astype(BF16) for q in groups]
    mix = [dot(v, wout_ref) for v in mixed]
    x1 = [x[q] + g1_ref[0] * mix[q] for q in groups]
    for q in groups:
        x1_ref[rows[q], :] = x1[q]
    h2 = [_rms_modulate(v, n2_ref[...], sh2_ref[0], sc2_ref[0]) for v in x1]
    route = []
    for v in h2:
        hi = v.astype(BF16)
        lo = (v - hi.astype(F32)).astype(BF16)
        parts = dot(hi, wr_ref) + dot(lo, wr_ref)
        route.append(_route(parts[:, :ROUTE_LANES] + parts[:, ROUTE_LANES:] + br_ref[...]))
    _store_token_tiled(h2_ref, jnp.concatenate(h2, axis=0))
    route_ref[...] = jnp.concatenate(route, axis=0)


def _merge_call(x2, mods, n1, n2, wgate, fm, h, yr, wfo, wlo, wout, wr, br, batch, seq):
    n = batch * seq
    tm = TOKEN_TILE
    tpb = seq // tm
    mod_spec = pl.BlockSpec((1, 1, D_MODEL), lambda i: (i // tpb, 0, 0))
    vec = pl.BlockSpec((1, D_MODEL), lambda i: (0, 0))
    tile = pl.BlockSpec((tm, D_MODEL), lambda i: (i, 0))

    def full(a):
        return pl.BlockSpec(a.shape, lambda i: (0,) * a.ndim)

    return pl.pallas_call(
        _merge_kernel,
        out_shape=(
            jax.ShapeDtypeStruct((n, D_MODEL), F32),
            jax.ShapeDtypeStruct((n * TOKEN_ROWS, LANES), U32),
            jax.ShapeDtypeStruct((n, ROUTE_LANES), F32),
        ),
        grid=(n // tm,),
        in_specs=[
            tile, mod_spec, mod_spec, mod_spec, mod_spec, mod_spec, vec, vec, full(wgate),
            pl.BlockSpec((1, tm, D_FOURIER), lambda i: (i // tpb, i % tpb, 0)),
            pl.BlockSpec((1, 1, tm, D_LRU), lambda i: (0, i // tpb, i % tpb, 0)),
            pl.BlockSpec((1, 1, tm, D_LRU), lambda i: (1, i // tpb, i % tpb, 0)),
            pl.BlockSpec((tm, D_LRU), lambda i: (i, 0)),
            full(wfo), full(wlo), full(wout), full(wr), full(br),
        ],
        out_specs=(tile, pl.BlockSpec((tm * TOKEN_ROWS, LANES), lambda i: (i, 0)),
                   pl.BlockSpec((tm, ROUTE_LANES), lambda i: (i, 0))),
        compiler_params=_params(("parallel",)),
        name="merge",
    )(x2, *mods, n1, n2, wgate, fm, h, h, yr, wfo, wlo, wout, wr, br)


def _token_rows(t, count=1):
    start = t * TOKEN_ROWS
    if not isinstance(start, int):
        start = pl.multiple_of(start, TOKEN_ROWS)
    return pl.ds(start, count * TOKEN_ROWS)


def _row_copy(src_ref, dst_ref, sem, src_tok, dst_tok):
    return pltpu.make_async_copy(src_ref.at[_token_rows(src_tok)], dst_ref.at[_token_rows(dst_tok)], sem)


def _moe_kernel(n_tokens, be_ref, bw_ref, off_ref, nv_ref, nxt_ref, so_ref, rows_ref,
                h2_hbm, wg_hbm, wu_hbm, wd_hbm, out_hbm,
                win, xbuf, y0, y1, y2, wg_f, wu_f, wd_f, wg_b, wu_b, wd_b, win_sem, w_sem, sct_sem):
    i = pl.program_id(0)
    n_assign = n_tokens * TOP_K
    ys = (y0, y1, y2)
    prev = jnp.maximum(i - 1, 0)

    def scatter_wait(s):
        pltpu.make_async_copy(ys[s], out_hbm.at[_token_rows(0, MOE_BLOCK)], sct_sem.at[s]).wait()

    def weight_copies(e, slot):
        return [pltpu.make_async_copy(w.at[e], f.at[slot], w_sem.at[slot])
                for w, f in ((wg_hbm, wg_f), (wu_hbm, wu_f), (wd_hbm, wd_f))]

    def gather_rows():
        base = off_ref[i]
        for r in range(MOE_BLOCK):
            xbuf[_token_rows(r)] = win[_token_rows(rows_ref[base + r] & LOCAL_MASK)]

    def issue_scatter_prev(s):
        base = off_ref[prev]
        nv = jnp.where(i > 0, nv_ref[prev], 0)
        dummy0 = n_assign + s * MOE_BLOCK
        for r in range(MOE_BLOCK):
            dst = lax.shift_right_logical(rows_ref[base + r], LOCAL_BITS)
            _row_copy(ys[s], out_hbm, sct_sem.at[s], r, jnp.where(r < nv, dst, dummy0 + r)).start(priority=r % 2)

    active = nv_ref[i] > 0
    prev_active = (i > 0) & (nv_ref[prev] > 0)
    w_slot = so_ref[i] % 2

    @pl.when(i == 0)
    def _():
        for y in ys:
            y[...] = jnp.zeros(y.shape, U32)
        fills = [pltpu.make_async_copy(ys[q], out_hbm.at[_token_rows(n_assign + q * MOE_BLOCK, MOE_BLOCK)],
                                       sct_sem.at[q]) for q in range(N_OUT_BUFFERS)]
        for cp in fills:
            cp.start()
        fills[-1].wait()
        for cp in weight_copies(be_ref[0], 0):
            cp.start()

    @pl.when(active & ((i == 0) | (bw_ref[i] != bw_ref[prev])))
    def _():
        cp = pltpu.make_async_copy(h2_hbm.at[_token_rows(bw_ref[i] * MOE_WINDOW, MOE_WINDOW)], win, win_sem)
        cp.start()
        cp.wait()

    @pl.when(active & ((i == 0) | (so_ref[i] != so_ref[prev])))
    def _():
        for cp in weight_copies(be_ref[i], w_slot):
            cp.wait()
        wg_b[...] = wg_f[w_slot].astype(BF16)
        wu_b[...] = wu_f[w_slot].astype(BF16)
        wd_b[...] = wd_f[w_slot].astype(BF16)

        @pl.when(nxt_ref[i] >= 0)
        def _():
            for cp in weight_copies(nxt_ref[i], 1 - w_slot):
                cp.start()

    for s in range(N_OUT_BUFFERS):
        prev_s, prev2_s = (s + 2) % N_OUT_BUFFERS, (s + 1) % N_OUT_BUFFERS

        @pl.when(active & (i % N_OUT_BUFFERS == s))
        def _(s=s, prev_s=prev_s):
            scatter_wait(s)
            gather_rows()
            issue_scatter_prev(prev_s)
            xb = _load_token_tiled(xbuf, MOE_BLOCK, BF16)
            hg = jnp.dot(xb, wg_b[...], preferred_element_type=F32)
            hu = jnp.dot(xb, wu_b[...], preferred_element_type=F32)
            hb = (hg * jax.nn.sigmoid(hg) * hu).astype(BF16)
            _store_token_tiled(ys[s], jnp.dot(hb, wd_b[...], preferred_element_type=F32))

        @pl.when(jnp.logical_not(active) & prev_active & (i % N_OUT_BUFFERS == s))
        def _(s=s, prev_s=prev_s, prev2_s=prev2_s):
            issue_scatter_prev(prev_s)
            scatter_wait(s)
            scatter_wait(prev2_s)
            scatter_wait(prev_s)


def _moe_call(blk_e, blk_w, blk_off, blk_nv, blk_next_e, blk_seg, rows, h2, w_g, w_u, w_d):
    n_steps = blk_e.shape[0]
    n_tokens = h2.shape[0] // TOKEN_ROWS
    any_spec = pl.BlockSpec(memory_space=pl.ANY)
    block_buf = pltpu.VMEM((MOE_BLOCK * TOKEN_ROWS, LANES), U32)
    grid_spec = pltpu.PrefetchScalarGridSpec(
        num_scalar_prefetch=7,
        grid=(n_steps,),
        in_specs=[any_spec, any_spec, any_spec, any_spec],
        out_specs=any_spec,
        scratch_shapes=[
            pltpu.VMEM((MOE_WINDOW * TOKEN_ROWS, LANES), U32),
            block_buf, block_buf, block_buf, block_buf,
            pltpu.VMEM((2, D_MODEL, D_EXPERT), F32),
            pltpu.VMEM((2, D_MODEL, D_EXPERT), F32),
            pltpu.VMEM((2, D_EXPERT, D_MODEL), F32),
            pltpu.VMEM((D_MODEL, D_EXPERT), BF16),
            pltpu.VMEM((D_MODEL, D_EXPERT), BF16),
            pltpu.VMEM((D_EXPERT, D_MODEL), BF16),
            pltpu.SemaphoreType.DMA,
            pltpu.SemaphoreType.DMA((2,)),
            pltpu.SemaphoreType.DMA((N_OUT_BUFFERS,)),
        ],
    )
    return pl.pallas_call(
        functools.partial(_moe_kernel, n_tokens),
        out_shape=jax.ShapeDtypeStruct(((n_tokens * TOP_K + N_DUMMY_ROWS) * TOKEN_ROWS, LANES), U32),
        grid_spec=grid_spec,
        compiler_params=_params(("arbitrary",)),
        name="moe",
    )(blk_e, blk_w, blk_off, blk_nv, blk_next_e, blk_seg, rows, h2, w_g, w_u, w_d)


def _final_kernel(x1_ref, y0_ref, y1_ref, route_ref, g2_ref, fg_ref, o_ref):
    route = route_ref[...]
    rows = route.shape[0]
    moe = (route[:, 2:3] * _load_token_tiled(y0_ref, rows, F32)
           + route[:, 3:4] * _load_token_tiled(y1_ref, rows, F32))
    x2 = x1_ref[...] + g2_ref[0] * moe
    o_ref[...] = x2 * lax.rsqrt(jnp.mean(x2 * x2, axis=-1, keepdims=True) + EPS) * fg_ref[...]


def _final_call(x1, out2, route, g2, fg, batch, seq):
    n = batch * seq
    tm = WIDE_TILE
    tpb = seq // tm
    tile = pl.BlockSpec((tm, D_MODEL), lambda i: (i, 0))
    return pl.pallas_call(
        _final_kernel,
        out_shape=jax.ShapeDtypeStruct((n, D_MODEL), F32),
        grid=(n // tm,),
        in_specs=[
            tile,
            pl.BlockSpec((tm * TOKEN_ROWS, LANES), lambda i: (i, 0)),
            pl.BlockSpec((tm * TOKEN_ROWS, LANES), lambda i: (i + n // tm, 0)),
            pl.BlockSpec((tm, ROUTE_LANES), lambda i: (i, 0)),
            pl.BlockSpec((1, 1, D_MODEL), lambda i: (i // tpb, 0, 0)),
            pl.BlockSpec((1, D_MODEL), lambda i: (0, 0)),
        ],
        out_specs=tile,
        compiler_params=_params(("parallel",)),
        name="final",
    )(x1, out2, out2, route, g2, fg)


def _channel_dft():
    j = np.arange(FOURIER_GROUP_DIM)
    ang = 2.0 * np.pi * np.outer(j, j) / FOURIER_GROUP_DIM
    return np.concatenate([np.cos(ang), np.sin(ang)], axis=1).astype(np.float32)


def _position_dft(seq):
    rows = seq // GRID_W
    assert GRID_W % rows == 0
    r, c = np.divmod(np.arange(seq), GRID_W)
    phase = (np.outer(r, r) * (GRID_W // rows) + np.outer(c, c)) % GRID_W
    ang = 2.0 * np.pi * phase / GRID_W
    scale = 1.0 / np.sqrt(float(seq) * FOURIER_GROUP_DIM)
    return np.concatenate([np.cos(ang), -np.sin(ang)], axis=1) * scale


def _block_diag(w):
    heads, hd, _ = w.shape
    eye = jnp.eye(heads, dtype=w.dtype)
    return jnp.einsum('hij,hg->higj', w, eye).reshape(heads * hd, heads * hd)


def _gate_weights(w_a, w_x):
    return jnp.stack([0.5 * jnp.concatenate([_block_diag(w_a[d]), _block_diag(w_x[d])], axis=1)
                      for d in range(2)]).astype(BF16)


def _dispatch(eid, n_tokens):
    n_assign = n_tokens * TOP_K
    n_seg = (n_tokens // MOE_WINDOW) * N_EXPERTS
    n_blk = n_assign // MOE_BLOCK + n_seg
    a = jnp.arange(n_assign, dtype=jnp.int32)
    seg_of = ((a % n_tokens) // MOE_WINDOW) * N_EXPERTS + eid.reshape(-1)
    a_bits = (n_assign - 1).bit_length()
    assert n_seg << a_bits < 2 ** 31
    order = lax.sort(lax.shift_left(seg_of, a_bits) | a) & ((1 << a_bits) - 1)
    segs = jnp.arange(n_seg, dtype=jnp.int32)
    counts = jnp.sum((seg_of[:, None] == segs[None, :]).astype(jnp.int32), axis=0)
    start = jnp.cumsum(counts) - counts
    blocks = (counts + MOE_BLOCK - 1) // MOE_BLOCK
    bend = jnp.cumsum(blocks)
    b = jnp.arange(n_blk + 1, dtype=jnp.int32)
    ended = (b[:, None] >= bend[None, :]).astype(jnp.int32)
    seg = jnp.minimum(jnp.sum(ended, axis=1), n_seg - 1)
    onehot = (seg[:, None] == segs[None, :]).astype(jnp.int32)
    in_seg = (b - jnp.sum(onehot * (bend - blocks)[None, :], axis=1)) * MOE_BLOCK
    blk_off = jnp.sum(onehot * start[None, :], axis=1) + in_seg
    blk_nv = jnp.clip(jnp.sum(onehot * counts[None, :], axis=1) - in_seg, 0, MOE_BLOCK)
    seg_end = jnp.sum(onehot * bend[None, :], axis=1)
    follows = (b[None, :] == seg_end[:, None]).astype(jnp.int32)
    next_e = jnp.sum(follows * (seg % N_EXPERTS)[None, :], axis=1)
    next_nv = jnp.sum(follows * blk_nv[None, :], axis=1)
    blk_next_e = jnp.where(next_nv > 0, next_e, -1)

    rows = lax.shift_left(order, LOCAL_BITS) | ((order % n_tokens) % MOE_WINDOW)
    dummy = lax.shift_left(n_assign + jnp.arange(N_DUMMY_ROWS, dtype=jnp.int32), LOCAL_BITS)
    seg_ordinal = jnp.cumsum((blocks > 0).astype(jnp.int32)) - 1
    blk_seg = jnp.sum(onehot * seg_ordinal[None, :], axis=1)
    assert MOE_WINDOW <= LOCAL_MASK + 1 and (n_assign + N_DUMMY_ROWS) << LOCAL_BITS <= 2 ** 32
    return seg % N_EXPERTS, seg // N_EXPERTS, blk_off, blk_nv, blk_next_e, blk_seg, jnp.concatenate([rows, dummy])


def kernel(x, c, ctx, c_ctx, w_mod, b_mod, norm1_g, w_in, conv_w, conv_b, lru_wa, lru_ba, lru_wx, lru_bx,
           lru_lam, w_fourier_out, w_lru_out, w_out, norm2_g, w_group, b_group, w_expert_router,
           b_expert_router, w_gate_e, w_up_e, w_down_e, final_g):
    batch, seq, _ = x.shape
    ctx_len = ctx.shape[1]
    n = batch * seq
    assert w_mod.shape[0] == 1, "single-layer stack only: the context stream is not carried across layers"
    x2 = x.reshape(n, D_MODEL)
    dft_ch = jnp.asarray(_channel_dft().astype(BF16))
    m_pos = jnp.asarray(_position_dft(seq).astype(BF16))

    for l in range(1):
        c_all = jnp.concatenate([c, c_ctx[None], jnp.zeros((MOD_ROWS - batch - 1, D_MODEL), F32)], axis=0)
        mod = _mod_call(c_all, w_mod[l], b_mod[l][None])
        sh1, sc1, g1, sh2, sc2, g2 = [m[:batch, None, :] for m in jnp.split(mod, 6, axis=-1)]
        csh1, csc1 = mod[batch:batch + 1, :D_MODEL], mod[batch:batch + 1, D_MODEL:2 * D_MODEL]

        w_in_b = w_in[l].astype(BF16)
        n1 = norm1_g[l][None]
        wg = _gate_weights(lru_wa[l], lru_wx[l])
        bg = 0.5 * jnp.concatenate([lru_ba[l], lru_bx[l]], axis=-1)[:, None, :]
        lam = lru_lam[l][:, None, :]
        cb = conv_b[l][None]

        uc = _inproj_ctx_call(ctx.reshape(batch * ctx_len, D_MODEL), csh1, csc1, n1,
                              w_in_b[:, D_FOURIER:D_FOURIER + D_LRU])
        hc = _rglru_call(uc.reshape(batch, ctx_len, D_LRU), conv_w[l], cb, wg, bg, lam,
                         jnp.zeros((2, batch, D_LRU), F32))
        h0 = jnp.stack([hc[0, :, -1], hc[1, :, 0]])

        ab, ur, yr = _inproj_call(x2, sh1, sc1, n1, w_in_b[:, :D_FOURIER + 2 * D_LRU], dft_ch, batch, seq)
        h = _rglru_call(ur.reshape(batch, seq, D_LRU), conv_w[l], cb, wg, bg, lam, h0)
        fm = _fourier_call(m_pos, ab)

        w_route = jnp.zeros((D_MODEL, ROUTE_LANES), F32)
        w_route = w_route.at[:, :N_GROUPS].set(w_group[l]).at[:, N_GROUPS:N_GROUPS + N_EXPERTS].set(
            w_expert_router[l])
        b_route = jnp.zeros((1, ROUTE_LANES), F32)
        b_route = b_route.at[0, :N_GROUPS].set(b_group[l]).at[0, N_GROUPS:N_GROUPS + N_EXPERTS].set(
            b_expert_router[l])
        w_route_hi = w_route.astype(BF16)
        w_route_lo = (w_route - w_route_hi.astype(F32)).astype(BF16)
        w_route = jnp.concatenate([w_route_hi, w_route_lo], axis=1)
        x1, h2, route = _merge_call(
            x2, (sh1, sc1, g1, sh2, sc2), n1, norm2_g[l][None],
            (0.5 * w_in[l][:, D_FOURIER + 2 * D_LRU:]).astype(BF16),
            fm, h, yr, w_fourier_out[l].astype(BF16), w_lru_out[l].astype(BF16),
            (0.5 * w_out[l]).astype(BF16),
            w_route, b_route, batch, seq)

        eid = jnp.stack([route[:, k] for k in range(TOP_K)]).astype(jnp.int32)
        out2 = _moe_call(*_dispatch(eid, n), h2, w_gate_e[l], w_up_e[l], w_down_e[l])
        x2 = _final_call(x1, out2, route, g2, final_g[None], batch, seq)
    return x2.reshape(batch, seq, D_MODEL)
```
